```python
import jax, jax.numpy as jnp
from jax import lax
import numpy as np

D_MODEL = 1024
BATCH = 8
SEQ = 8192
DEPTH = 2

CTX_LEN = 256
GRID_W = 64
HEAD_DIM = 64
ROPE_THETA = 10000.0
NORM_EPS = 1e-6
A_HEADS = 8
A_KV_HEADS = 2
B_HEADS = 8
B_KV_HEADS = 2
WINDOW = 128
BLOCK = 128
AB_Q_COLS = (A_HEADS + B_HEADS) * HEAD_DIM
AB_KV_COLS = 2 * (A_KV_HEADS + B_KV_HEADS) * HEAD_DIM
AB_IN = AB_Q_COLS + AB_KV_COLS
AB_OUT = (A_HEADS + B_HEADS) * HEAD_DIM
C_HEADS = 12
C_WIDTH = C_HEADS * HEAD_DIM
DECAY_LORA = 64
ICLR_LORA = 64
GATE_LORA = 128
LNX_EPS = 64e-5
C_SPLITS = (C_WIDTH, C_WIDTH, C_WIDTH, DECAY_LORA, DECAY_LORA, ICLR_LORA, ICLR_LORA, GATE_LORA)
C_IN = sum(C_SPLITS)
D_GROUPS = 4
D_GROUP_DIM = 64
D_WIDTH = D_GROUPS * D_GROUP_DIM
POOL_WINDOWS = (2, 4, 8, 16)
CD_IN = C_IN + D_WIDTH
CD_OUT = C_WIDTH + D_WIDTH
FFN_HIDDEN = -(-8 * D_MODEL // (3 * 256)) * 256

kernel_name = "hybrid_prefix_dit_block"


def rms_norm(x, gain, eps=NORM_EPS):
    xf = x.astype(jnp.float32)
    y = xf * lax.rsqrt(jnp.mean(xf * xf, axis=-1, keepdims=True) + eps)
    return (y * gain.astype(jnp.float32)).astype(x.dtype)


def modulate(h, shift, scale):
    return h * (1.0 + scale) + shift


def swiglu(h, w_in, w_out):
    gate, up = jnp.split(h @ w_in, 2, axis=-1)
    return (jax.nn.silu(gate) * up) @ w_out


def axial_rope_tables(n):
    rows = n // GRID_W
    row = jnp.repeat(jnp.arange(rows, dtype=jnp.float32), GRID_W)
    col = jnp.tile(jnp.arange(GRID_W, dtype=jnp.float32), rows)
    n_freq = HEAD_DIM // 4
    inv = ROPE_THETA ** (-jnp.arange(n_freq, dtype=jnp.float32) / n_freq)
    ang = jnp.concatenate([row[:, None] * inv[None, :], col[:, None] * inv[None, :]], axis=-1)
    return jnp.cos(ang), jnp.sin(ang)


def apply_rope(x, cos, sin):
    half = x.shape[-1] // 2
    xf = x.astype(jnp.float32)
    x1, x2 = xf[..., :half], xf[..., half:]
    c = cos[None, :, None, :]
    s = sin[None, :, None, :]
    return jnp.concatenate([x1 * c - x2 * s, x2 * c + x1 * s], axis=-1).astype(x.dtype)


def _heads(z, h):
    return z.reshape(z.shape[0], z.shape[1], h, HEAD_DIM)


def gqa_attend(q, k, v, mask=None, sink=None):
    b, m, hq, d = q.shape
    hkv = k.shape[2]
    g = hq // hkv
    qg = q.reshape(b, m, hkv, g, d)
    s = jnp.einsum("bmkgd,bskd->bkgms", qg, k, preferred_element_type=jnp.float32) * (d ** -0.5)
    if mask is not None:
        s = jnp.where(mask, s, -jnp.inf)
    if sink is not None:
        sk = jnp.broadcast_to(sink.astype(jnp.float32).reshape(1, hkv, g, 1, 1), s.shape[:-1] + (1,))
        s = jnp.concatenate([s, sk], axis=-1)
    p = jax.nn.softmax(s, axis=-1)
    if sink is not None:
        p = p[..., :-1]
    o = jnp.einsum("bkgms,bskd->bmkgd", p.astype(v.dtype), v)
    return o.reshape(b, m, hq * d)


def window_attention(q, k, v, kc, vc, sink):
    b, n, hq, d = q.shape
    n_ctx = kc.shape[1]
    span = BLOCK + 2 * WINDOW
    kp = jnp.pad(k, ((0, 0), (WINDOW, WINDOW), (0, 0), (0, 0)))
    vp = jnp.pad(v, ((0, 0), (WINDOW, WINDOW), (0, 0), (0, 0)))
    r_idx = jnp.arange(BLOCK)
    s_idx = jnp.arange(span)
    ctx_mask = jnp.ones((BLOCK, n_ctx), dtype=bool)

    def one_block(i):
        start = i * BLOCK
        qb = lax.dynamic_slice_in_dim(q, start, BLOCK, axis=1)
        kb = jnp.concatenate([lax.dynamic_slice_in_dim(kp, start, span, axis=1), kc], axis=1)
        vb = jnp.concatenate([lax.dynamic_slice_in_dim(vp, start, span, axis=1), vc], axis=1)
        qpos = start + r_idx
        kpos = start - WINDOW + s_idx
        local = (jnp.abs(qpos[:, None] - kpos[None, :]) <= WINDOW) & ((kpos >= 0) & (kpos < n))[None, :]
        mask = jnp.concatenate([local, ctx_mask], axis=1)
        return gqa_attend(qb, kb, vb, mask=mask, sink=sink)

    out = lax.map(one_block, jnp.arange(n // BLOCK))
    return jnp.moveaxis(out, 0, 1).reshape(b, n, hq * d)


def global_attention(q, k, v, kc, vc):
    b, n, hq, d = q.shape
    kk = jnp.concatenate([k, kc], axis=1)
    vv = jnp.concatenate([v, vc], axis=1)

    def one_block(i):
        qb = lax.dynamic_slice_in_dim(q, i * BLOCK, BLOCK, axis=1)
        return gqa_attend(qb, kk, vv)

    out = lax.map(one_block, jnp.arange(n // BLOCK))
    return jnp.moveaxis(out, 0, 1).reshape(b, n, hq * d)


def _split_q(p):
    aq, bq = jnp.split(p, [A_HEADS * HEAD_DIM], axis=-1)
    return _heads(aq, A_HEADS), _heads(bq, B_HEADS)


def _split_kv(p):
    a = A_KV_HEADS * HEAD_DIM
    bw = B_KV_HEADS * HEAD_DIM
    ak, av, bk, bv = jnp.split(p, [a, 2 * a, 2 * a + bw], axis=-1)
    return _heads(ak, A_KV_HEADS), _heads(av, A_KV_HEADS), _heads(bk, B_KV_HEADS), _heads(bv, B_KV_HEADS)


def mixer_ab(hl, hc, w_in, q_gain, k_gain, sink, w_out, cos, sin, ctx_out):
    pl = hl @ w_in
    aq, bq = _split_q(pl[..., :AB_Q_COLS])
    ak, av, bk, bv = _split_kv(pl[..., AB_Q_COLS:])
    aq = apply_rope(aq, cos, sin)
    ak = apply_rope(ak, cos, sin)
    bq = apply_rope(rms_norm(bq, q_gain), cos, sin)
    bk = apply_rope(rms_norm(bk, k_gain), cos, sin)
    pc = hc @ (w_in if ctx_out else w_in[:, AB_Q_COLS:])
    ak_c, av_c, bk_c, bv_c = _split_kv(pc[..., -AB_KV_COLS:])
    bk_c = rms_norm(bk_c, k_gain)
    ya = window_attention(aq, ak, av, ak_c, av_c, sink)
    yb = global_attention(bq, bk, bv, bk_c, bv_c)
    out_l = jnp.concatenate([ya, yb], axis=-1) @ w_out
    if not ctx_out:
        return out_l, None
    aq_c, bq_c = _split_q(pc[..., :AB_Q_COLS])
    ya_c = gqa_attend(aq_c, ak_c, av_c, sink=sink)
    yb_c = gqa_attend(rms_norm(bq_c, q_gain), bk_c, bv_c)
    out_c = jnp.concatenate([ya_c, yb_c], axis=-1) @ w_out
    return out_l, out_c


def token_shift_centred(u):
    up = jnp.pad(u, ((0, 0), (1, 1), (0, 0)))
    return 0.5 * (up[:, :-2] + up[:, 2:])


def rwkv_prepare(pc, mu, w0, w2, a0, a2, g2, k_k, k_a):
    b, t, _ = pc.shape
    pc = pc.astype(jnp.float32)
    pc = pc + (token_shift_centred(pc) - pc) * mu
    r, k, v, xw_f, xw_b, xa_f, xa_b, xg = jnp.split(pc, np.cumsum(C_SPLITS)[:-1], axis=-1)
    g = jax.nn.sigmoid(xg) @ g2
    kk = _heads(k * k_k, C_HEADS)
    kk = kk / jnp.maximum(jnp.sqrt(jnp.sum(kk * kk, axis=-1, keepdims=True)), 1e-12)
    dirs = []
    for d, (xw, xa) in enumerate(((xw_f, xa_f), (xw_b, xa_b))):
        w_log = -jax.nn.softplus(-(w0[d] + jnp.tanh(xw) @ w2[d])) - 0.5
        a = jax.nn.sigmoid(a0[d] + xa @ a2[d])
        k_d = k * (1.0 + (a - 1.0) * k_a)
        decay = jnp.exp(-jnp.exp(w_log))
        dirs.append((_heads(decay, C_HEADS), _heads(k_d, C_HEADS), kk * _heads(a, C_HEADS)))
    return _heads(r, C_HEADS), _heads(v, C_HEADS), kk, g, dirs


def rwkv7_scan(s0, decay, k, v, kk, bvec, r, reverse):
    xs = (decay, k, v, kk, bvec) + (() if r is None else (r,))
    xs = tuple(jnp.moveaxis(z, 1, 0) for z in xs)

    def step(S, inp):
        w_t, k_t, v_t, kk_t, b_t = inp[:5]
        sa = jnp.einsum("bhvk,bhk->bhv", S, -kk_t)
        S = S * w_t[:, :, None, :] + sa[..., None] * b_t[:, :, None, :] + v_t[..., None] * k_t[:, :, None, :]
        y = None if r is None else jnp.einsum("bhvk,bhk->bhv", S, inp[5])
        return S, y

    s_fin, ys = lax.scan(step, s0, xs, reverse=reverse)
    return s_fin, (None if r is None else jnp.moveaxis(ys, 0, 1))


def rwkv_bidirectional(prep, s0_f, s0_b, with_output):
    r, v, kk, g, dirs = prep
    outs = []
    finals = []
    for (decay, k_d, b_d), s0, rev in zip(dirs, (s0_f, s0_b), (False, True)):
        s_fin, y = rwkv7_scan(s0, decay, k_d, v, kk, b_d, r if with_output else None, rev)
        finals.append(s_fin)
        outs.append(y)
    y = outs[0] + outs[1] if with_output else None
    return y, finals[0], finals[1]


def rwkv_output(y, prep, lnx_w, lnx_b, r_k):
    r, v, kk, g, dirs = prep
    b, t = y.shape[:2]
    mean = jnp.mean(y, axis=-1, keepdims=True)
    var = jnp.mean(jnp.square(y - mean), axis=-1, keepdims=True)
    yn = ((y - mean) * lax.rsqrt(var + LNX_EPS)).reshape(b, t, C_WIDTH) * lnx_w + lnx_b
    bonus = sum(jnp.sum(r * k_d * r_k, axis=-1, keepdims=True) * v for (_, k_d, _) in dirs)
    return (yn + bonus.reshape(b, t, C_WIDTH)) * g


def multiscale_pool(u, pool_w, pool_scale):
    b, t, _ = u.shape
    uf = u.astype(jnp.float32).reshape(b, t, D_GROUPS, D_GROUP_DIM)
    cs = jnp.pad(jnp.cumsum(uf, axis=1), ((0, 0), (1, 0), (0, 0), (0, 0)))
    pos = jnp.arange(t)
    outs = []
    for gi, w in enumerate(POOL_WINDOWS):
        lo = jnp.clip(pos - w // 2, 0, t)
        hi = jnp.clip(pos + (w - 1 - w // 2) + 1, 0, t)
        csg = cs[:, :, gi]
        mean = (csg[:, hi] - csg[:, lo]) / (hi - lo).astype(jnp.float32)[None, :, None]
        outs.append(mean - uf[:, :, gi])
    pooled = jnp.stack(outs, axis=2)
    y = jnp.einsum("btgi,gio->btgo", pooled, pool_w.astype(jnp.float32)).reshape(b, t, D_WIDTH)
    return (y * pool_scale).astype(u.dtype)


def mixer_cd(hl, hc, w_in, mu, w0, w2, a0, a2, g2, k_k, k_a, r_k, lnx_w, lnx_b,
             pool_w, pool_scale, w_out, ctx_out):
    rw = (mu, w0, w2, a0, a2, g2, k_k, k_a)
    b = hl.shape[0]
    pl = hl @ w_in
    pc = hc @ (w_in if ctx_out else w_in[:, :C_IN])
    prep_c = rwkv_prepare(pc[..., :C_IN], *rw)
    zeros = jnp.zeros((b, C_HEADS, HEAD_DIM, HEAD_DIM), jnp.float32)
    yc, s_f, s_b = rwkv_bidirectional(prep_c, zeros, zeros, ctx_out)
    prep_l = rwkv_prepare(pl[..., :C_IN], *rw)
    yl, _, _ = rwkv_bidirectional(prep_l, s_f, s_b, True)
    out_l = jnp.concatenate([rwkv_output(yl, prep_l, lnx_w, lnx_b, r_k).astype(hl.dtype),
                             multiscale_pool(pl[..., C_IN:], pool_w, pool_scale)], axis=-1) @ w_out
    if not ctx_out:
        return out_l, None
    out_c = jnp.concatenate([rwkv_output(yc, prep_c, lnx_w, lnx_b, r_k).astype(hc.dtype),
                             multiscale_pool(pc[..., C_IN:], pool_w, pool_scale)], axis=-1) @ w_out
    return out_l, out_c


def setup_inputs(seed: int = 0) -> dict:
    key = jax.random.key(seed)
    keys = iter(jax.random.split(key, 32))
    f32 = jnp.float32
    n_even = (DEPTH + 1) // 2
    n_odd = DEPTH // 2
    d = D_MODEL

    def nrm(shape, scale):
        return scale * jax.random.normal(next(keys), shape, f32)

    def uni(shape, lo, hi):
        return jax.random.uniform(next(keys), shape, f32, lo, hi)

    return {
        "x": nrm((BATCH, SEQ, d), 1.0),
        "c": nrm((BATCH, d), 1.0),
        "ctx": nrm((BATCH, CTX_LEN, d), 1.0),
        "c_ctx": nrm((d,), 1.0),
        "norm_gain": 1.0 + nrm((DEPTH, 2, d), 0.02),
        "ada_w": nrm((DEPTH, d, 6 * d), 0.5 * d ** -0.5),
        "ada_b": nrm((DEPTH, 6 * d), 0.02),
        "ffn_w_in": nrm((DEPTH, d, 2 * FFN_HIDDEN), d ** -0.5),
        "ffn_w_out": nrm((DEPTH, FFN_HIDDEN, d), FFN_HIDDEN ** -0.5),
        "final_gain": 1.0 + nrm((d,), 0.02),
        "ab_w_in": nrm((n_even, d, AB_IN), d ** -0.5),
        "ab_q_gain": 1.0 + nrm((n_even, HEAD_DIM), 0.02),
        "ab_k_gain": 1.0 + nrm((n_even, HEAD_DIM), 0.02),
        "ab_sink": nrm((n_even, A_HEADS), 0.5),
        "ab_w_out": nrm((n_even, AB_OUT, d), AB_OUT ** -0.5),
        "cd_w_in": nrm((n_odd, d, CD_IN), d ** -0.5),
        "cd_mu": uni((n_odd, C_IN), 0.0, 1.0),
        "cd_w0": uni((n_odd, 2, C_WIDTH), -5.0, 0.0),
        "cd_w2": nrm((n_odd, 2, DECAY_LORA, C_WIDTH), 0.1),
        "cd_a0": nrm((n_odd, 2, C_WIDTH), 0.1),
        "cd_a2": nrm((n_odd, 2, ICLR_LORA, C_WIDTH), 0.1),
        "cd_g2": nrm((n_odd, GATE_LORA, C_WIDTH), GATE_LORA ** -0.5),
        "cd_k_k": 0.85 + nrm((n_odd, C_WIDTH), 0.02),
        "cd_k_a": 1.0 + nrm((n_odd, C_WIDTH), 0.02),
        "cd_r_k": nrm((n_odd, C_HEADS, HEAD_DIM), 0.1),
        "cd_lnx_w": 1.0 + nrm((n_odd, C_WIDTH), 0.02),
        "cd_lnx_b": nrm((n_odd, C_WIDTH), 0.02),
        "cd_pool_w": nrm((n_odd, D_GROUPS, D_GROUP_DIM, D_GROUP_DIM), D_GROUP_DIM ** -0.5),
        "cd_pool_scale": 1.0 + nrm((n_odd, D_WIDTH), 0.02),
        "cd_w_out": nrm((n_odd, CD_OUT, d), CD_OUT ** -0.5),
    }


def reference(x, c, ctx, c_ctx, norm_gain, ada_w, ada_b, ffn_w_in, ffn_w_out, final_gain,
              ab_w_in, ab_q_gain, ab_k_gain, ab_sink, ab_w_out,
              cd_w_in, cd_mu, cd_w0, cd_w2, cd_a0, cd_a2, cd_g2, cd_k_k, cd_k_a, cd_r_k,
              cd_lnx_w, cd_lnx_b, cd_pool_w, cd_pool_scale, cd_w_out):
    n = x.shape[1]
    cos, sin = axial_rope_tables(n)
    xl, xc = x, ctx
    for i in range(DEPTH):
        ctx_out = i < DEPTH - 1
        j = i // 2
        mod_l = jnp.split((jax.nn.silu(c) @ ada_w[i] + ada_b[i])[:, None, :], 6, axis=-1)
        mod_c = jnp.split((jax.nn.silu(c_ctx) @ ada_w[i] + ada_b[i])[None, None, :], 6, axis=-1)
        hl = modulate(rms_norm(xl, norm_gain[i, 0]), mod_l[0], mod_l[1])
        hc = modulate(rms_norm(xc, norm_gain[i, 0]), mod_c[0], mod_c[1])
        if i % 2 == 0:
            yl, yc = mixer_ab(hl, hc, ab_w_in[j], ab_q_gain[j], ab_k_gain[j], ab_sink[j], ab_w_out[j],
                              cos, sin, ctx_out)
        else:
            yl, yc = mixer_cd(hl, hc, cd_w_in[j], cd_mu[j], cd_w0[j], cd_w2[j], cd_a0[j], cd_a2[j],
                              cd_g2[j], cd_k_k[j], cd_k_a[j], cd_r_k[j], cd_lnx_w[j], cd_lnx_b[j],
                              cd_pool_w[j], cd_pool_scale[j], cd_w_out[j], ctx_out)
        xl = xl + mod_l[2] * yl
        xl = xl + mod_l[5] * swiglu(modulate(rms_norm(xl, norm_gain[i, 1]), mod_l[3], mod_l[4]),
                                    ffn_w_in[i], ffn_w_out[i])
        if ctx_out:
            xc = xc + mod_c[2] * yc
            xc = xc + mod_c[5] * swiglu(modulate(rms_norm(xc, norm_gain[i, 1]), mod_c[3], mod_c[4]),
                                        ffn_w_in[i], ffn_w_out[i])
    return rms_norm(xl, final_gain)
```

```python
import functools

import numpy as np
import jax
import jax.numpy as jnp
from jax import lax
from jax.experimental import pallas as pl
from jax.experimental.pallas import tpu as pltpu

F32 = jnp.float32
BF16 = jnp.bfloat16
HIGHEST = lax.Precision.HIGHEST

D_MODEL = 1024
GRID_W = 64
HEAD_DIM = 64
ROPE_THETA = 10000.0
NORM_EPS = 1e-6
WINDOW = 128
AB_Q_COLS = 1024
C_WIDTH = 768
C_IN = 2688
D_WIDTH = 256
CD_IN = C_IN + D_WIDTH
LNX_EPS = 64e-5
FFN_HIDDEN = 2816
POOL_WINDOWS = (2, 4, 8, 16)
DECAY_LORA_PAD = 64

LANES = 128
ROW_TILE = 256
HALO = 8
CHUNK = 64
SCAN_BLOCK = 256
VMEM_LIMIT = 56 * 1024 * 1024


def _cparams(*sem):
    return pltpu.CompilerParams(dimension_semantics=sem, vmem_limit_bytes=VMEM_LIMIT)


def _dot(a, b):
    return jnp.dot(a, b, preferred_element_type=F32)


def _dot32(a, b):
    return jnp.dot(a, b, preferred_element_type=F32, precision=HIGHEST)


def _sigmoid(x):
    return 1.0 / (1.0 + jnp.exp(-x))


def _norm_mod(x, gain, shift, scale):
    ms = jnp.mean(x * x, axis=-1, keepdims=True)
    return (x * lax.rsqrt(ms + NORM_EPS) * gain) * (1.0 + scale) + shift


def _mods_kernel(c_ref, w_ref, b_ref, o_ref):
    c = c_ref[...]
    o_ref[0] = _dot32(c * _sigmoid(c), w_ref[0]) + b_ref[0]


def _mods(cs, ada_w, ada_b):
    depth, d, n6 = ada_w.shape
    tn = 768
    rows = cs.shape[0]
    return pl.pallas_call(
        _mods_kernel,
        grid=(depth, n6 // tn),
        in_specs=[pl.BlockSpec((rows, d), lambda l, j: (0, 0)),
                  pl.BlockSpec((1, d, tn), lambda l, j: (l, 0, j)),
                  pl.BlockSpec((1, 1, tn), lambda l, j: (l, 0, j))],
        out_specs=pl.BlockSpec((1, rows, tn), lambda l, j: (l, 0, j)),
        out_shape=jax.ShapeDtypeStruct((depth, rows, n6), F32),
        compiler_params=_cparams("arbitrary", "arbitrary"),
        name="mods",
    )(cs, ada_w, ada_b.reshape(depth, 1, n6))


def _proj0_kernel(x_ref, mod_ref, gain_ref, w_ref, cos_ref, sin_ref, qg_ref, kg_ref, seg_ref,
                  q_ref, kv_ref):
    h = _norm_mod(x_ref[0], gain_ref[...], mod_ref[0, 0:1, :], mod_ref[0, 1:2, :])
    p = _dot(h.astype(BF16), w_ref[...])
    tm = p.shape[0]
    cos = cos_ref[...]
    sin = sin_ref[...]
    lane = lax.broadcasted_iota(jnp.int32, (tm, LANES), 1)
    first_half = (lane & (HEAD_DIM - 1)) < HEAD_DIM // 2
    seg = seg_ref[...]

    def rope(z):
        partner = jnp.where(first_half, pltpu.roll(z, LANES - HEAD_DIM // 2, 1),
                            pltpu.roll(z, HEAD_DIM // 2, 1))
        return z * cos + partner * sin

    def head_norm(z, g):
        ms = _dot32(z * z, seg) * (1.0 / HEAD_DIM)
        return z * lax.rsqrt(ms + NORM_EPS) * g

    scale = HEAD_DIM ** -0.5
    for blk in range(8):
        z = p[:, blk * LANES:(blk + 1) * LANES]
        if blk >= 4:
            z = head_norm(z, qg_ref[...])
        q_ref[0, :, blk * LANES:(blk + 1) * LANES] = (rope(z) * scale).astype(BF16)
    for blk in range(8):
        z = p[:, AB_Q_COLS + blk * LANES:AB_Q_COLS + (blk + 1) * LANES]
        if blk in (4, 5):
            z = head_norm(z, kg_ref[...])
        if blk in (0, 1, 4, 5):
            z = rope(z)
        kv_ref[0, :, blk * LANES:(blk + 1) * LANES] = z.astype(BF16)


def _proj0(x, mod, gain, w, cos, sin, qg, kg, seg):
    b, n, d = x.shape
    tm = ROW_TILE
    nw = w.shape[1]
    const = lambda bi, i: (0, 0)
    return pl.pallas_call(
        _proj0_kernel,
        grid=(b, n // tm),
        in_specs=[pl.BlockSpec((1, tm, d), lambda bi, i: (bi, i, 0)),
                  pl.BlockSpec((1, 6, d), lambda bi, i: (bi, 0, 0)),
                  pl.BlockSpec((1, d), const),
                  pl.BlockSpec((d, nw), const),
                  pl.BlockSpec((tm, LANES), lambda bi, i: (i, 0)),
                  pl.BlockSpec((tm, LANES), lambda bi, i: (i, 0)),
                  pl.BlockSpec((1, LANES), const),
                  pl.BlockSpec((1, LANES), const),
                  pl.BlockSpec((LANES, LANES), const)],
        out_specs=[pl.BlockSpec((1, tm, 1024), lambda bi, i: (bi, i, 0)),
                   pl.BlockSpec((1, tm, 1024), lambda bi, i: (bi, i, 0))],
        out_shape=[jax.ShapeDtypeStruct((b, n, 1024), BF16),
                   jax.ShapeDtypeStruct((b, n, 1024), BF16)],
        compiler_params=_cparams("parallel", "parallel"),
        name="proj0",
    )(x, mod, gain, w, cos, sin, qg, kg, seg)


def _masked_q(q_ref, h, low):
    q2 = q_ref[0, :, (h // 2) * LANES:(h // 2 + 1) * LANES]
    keep = low if h % 2 == 0 else jnp.logical_not(low)
    return jnp.where(keep, q2, jnp.zeros_like(q2))


def _attn_b_kernel(q_ref, kc_ref, vc_ref, *rest, n_kt, has_latent):
    if has_latent:
        k_ref, v_ref, o_ref, m_sc, l_sc, acc_sc = rest
    else:
        o_ref, m_sc, l_sc, acc_sc = rest
    kt = pl.program_id(3)
    tq = q_ref.shape[1]
    low = lax.broadcasted_iota(jnp.int32, (tq, LANES), 1) < HEAD_DIM

    def update(k, v):
        reps = k.shape[0] // LANES
        for h in range(4):
            s = lax.dot_general(_masked_q(q_ref, h, low), k, (((1,), (1,)), ((), ())),
                                preferred_element_type=F32)
            m_prev = m_sc[h]
            m_new = jnp.maximum(m_prev, jnp.max(s, axis=1, keepdims=True))
            p = jnp.exp(s - jnp.concatenate([m_new] * reps, axis=1))
            alpha = jnp.exp(m_prev - m_new)
            l_sc[h] = alpha * l_sc[h] + jnp.sum(p, axis=1, keepdims=True)
            acc_sc[h] = alpha * acc_sc[h] + _dot(p.astype(BF16), v)
            m_sc[h] = m_new

    @pl.when(kt == 0)
    def _():
        m_sc[...] = jnp.full(m_sc.shape, -jnp.inf, F32)
        l_sc[...] = jnp.zeros(l_sc.shape, F32)
        acc_sc[...] = jnp.zeros(acc_sc.shape, F32)
        update(kc_ref[0], vc_ref[0])

    if has_latent:
        update(k_ref[0], v_ref[0])

    @pl.when(kt == n_kt - 1)
    def _():
        for pr in range(2):
            o0 = acc_sc[2 * pr] / l_sc[2 * pr]
            o1 = acc_sc[2 * pr + 1] / l_sc[2 * pr + 1]
            o_ref[0, :, pr * LANES:(pr + 1) * LANES] = jnp.where(low, o0, o1).astype(BF16)


def _attn_b(q, kv_c, kv=None, tk=512):
    b, n, _ = q.shape
    nc = kv_c.shape[1]
    tq = ROW_TILE
    has_latent = kv is not None
    n_kt = kv.shape[1] // tk if has_latent else 1
    in_specs = [pl.BlockSpec((1, tq, 2 * LANES), lambda bi, g, i, j: (bi, i, 2 + g)),
                pl.BlockSpec((1, nc, LANES), lambda bi, g, i, j: (bi, 0, 4 + g)),
                pl.BlockSpec((1, nc, LANES), lambda bi, g, i, j: (bi, 0, 6 + g))]
    args = [q, kv_c, kv_c]
    if has_latent:
        in_specs += [pl.BlockSpec((1, tk, LANES), lambda bi, g, i, j: (bi, j, 4 + g)),
                     pl.BlockSpec((1, tk, LANES), lambda bi, g, i, j: (bi, j, 6 + g))]
        args += [kv, kv]
    return pl.pallas_call(
        functools.partial(_attn_b_kernel, n_kt=n_kt, has_latent=has_latent),
        grid=(b, 2, n // tq, n_kt),
        in_specs=in_specs,
        out_specs=pl.BlockSpec((1, tq, 2 * LANES), lambda bi, g, i, j: (bi, i, g)),
        out_shape=jax.ShapeDtypeStruct((b, n, 512), BF16),
        scratch_shapes=[pltpu.VMEM((4, tq, LANES), F32), pltpu.VMEM((4, tq, LANES), F32),
                        pltpu.VMEM((4, tq, LANES), F32)],
        compiler_params=_cparams("parallel", "parallel", "parallel", "arbitrary"),
        name="attn_b",
    )(*args)


def _attn_a_kernel(sink_ref, q_ref, kc_ref, vc_ref, *rest, has_local):
    if has_local:
        kp_ref, kcur_ref, kn_ref, vp_ref, vcur_ref, vn_ref, o_ref, kbuf, vbuf = rest
    else:
        o_ref, kbuf, vbuf = rest
    g = pl.program_id(1)
    i = pl.program_id(2)
    n_i = pl.num_programs(2)
    tq = q_ref.shape[1]
    nc = kc_ref.shape[1]
    kbuf[0:nc, :] = kc_ref[0]
    vbuf[0:nc, :] = vc_ref[0]
    nk = nc
    if has_local:
        kbuf[nc:nc + WINDOW, :] = kp_ref[0]
        kbuf[nc + WINDOW:nc + WINDOW + tq, :] = kcur_ref[0]
        kbuf[nc + WINDOW + tq:nc + 2 * WINDOW + tq, :] = kn_ref[0]
        vbuf[nc:nc + WINDOW, :] = vp_ref[0]
        vbuf[nc + WINDOW:nc + WINDOW + tq, :] = vcur_ref[0]
        vbuf[nc + WINDOW + tq:nc + 2 * WINDOW + tq, :] = vn_ref[0]
        nk = nc + 2 * WINDOW + tq
        row = lax.broadcasted_iota(jnp.int32, (tq, nk), 0)
        col = lax.broadcasted_iota(jnp.int32, (tq, nk), 1) - nc
        lo_c = jnp.where(i == 0, WINDOW, 0)
        hi_c = jnp.where(i == n_i - 1, WINDOW + tq, 2 * WINDOW + tq)
        valid = (col < 0) | ((col >= row) & (col <= row + 2 * WINDOW) & (col >= lo_c) & (col < hi_c))
    k = kbuf[...]
    v = vbuf[...]
    low = lax.broadcasted_iota(jnp.int32, (tq, LANES), 1) < HEAD_DIM
    outs = []
    for h in range(4):
        s = lax.dot_general(_masked_q(q_ref, h, low), k, (((1,), (1,)), ((), ())),
                            preferred_element_type=F32)
        if has_local:
            s = jnp.where(valid, s, -jnp.inf)
        sink = sink_ref[g * 4 + h]
        m = jnp.maximum(jnp.max(s, axis=1, keepdims=True), sink)
        p = jnp.exp(s - m)
        l = jnp.sum(p, axis=1, keepdims=True) + jnp.exp(sink - m)
        outs.append(_dot(p.astype(BF16), v) / l)
    for pr in range(2):
        o_ref[0, :, pr * LANES:(pr + 1) * LANES] = jnp.where(low, outs[2 * pr], outs[2 * pr + 1]).astype(BF16)


def _attn_a(q, kv_c, sink, kv=None):
    b, n, _ = q.shape
    nc = kv_c.shape[1]
    tq = ROW_TILE
    has_local = kv is not None
    in_specs = [pl.BlockSpec(memory_space=pltpu.SMEM),
                pl.BlockSpec((1, tq, 2 * LANES), lambda bi, g, i: (bi, i, g)),
                pl.BlockSpec((1, nc, LANES), lambda bi, g, i: (bi, 0, g)),
                pl.BlockSpec((1, nc, LANES), lambda bi, g, i: (bi, 0, 2 + g))]
    args = [sink, q, kv_c, kv_c]
    nk = nc
    if has_local:
        per = tq // WINDOW
        last = n // WINDOW - 1
        prev_i = lambda i: jnp.maximum(i * per - 1, 0)
        next_i = lambda i: jnp.minimum((i + 1) * per, last)
        for off in (0, 2):
            in_specs += [pl.BlockSpec((1, WINDOW, LANES), lambda bi, g, i, off=off: (bi, prev_i(i), off + g)),
                         pl.BlockSpec((1, tq, LANES), lambda bi, g, i, off=off: (bi, i, off + g)),
                         pl.BlockSpec((1, WINDOW, LANES), lambda bi, g, i, off=off: (bi, next_i(i), off + g))]
            args += [kv, kv, kv]
        nk = nc + 2 * WINDOW + tq
    return pl.pallas_call(
        functools.partial(_attn_a_kernel, has_local=has_local),
        grid=(b, 2, n // tq),
        in_specs=in_specs,
        out_specs=pl.BlockSpec((1, tq, 2 * LANES), lambda bi, g, i: (bi, i, g)),
        out_shape=jax.ShapeDtypeStruct((b, n, 512), BF16),
        scratch_shapes=[pltpu.VMEM((nk, LANES), BF16), pltpu.VMEM((nk, LANES), BF16)],
        compiler_params=_cparams("parallel", "parallel", "parallel"),
        name="attn_a",
    )(*args)


def _out0_kernel(ya_ref, yb_ref, x_ref, mod_ref, w_ref, o_ref):
    half = ya_ref.shape[2]
    y = _dot(ya_ref[0], w_ref[0:half, :]) + _dot(yb_ref[0], w_ref[half:2 * half, :])
    o_ref[0] = x_ref[0] + mod_ref[0, 2:3, :] * y


def _out0(ya, yb, x, mod, w):
    b, n, d = x.shape
    tm = ROW_TILE
    row = lambda bi, i: (bi, i, 0)
    return pl.pallas_call(
        _out0_kernel,
        grid=(b, n // tm),
        in_specs=[pl.BlockSpec((1, tm, ya.shape[2]), row),
                  pl.BlockSpec((1, tm, yb.shape[2]), row),
                  pl.BlockSpec((1, tm, d), row),
                  pl.BlockSpec((1, 6, d), lambda bi, i: (bi, 0, 0)),
                  pl.BlockSpec(w.shape, lambda bi, i: (0, 0))],
        out_specs=pl.BlockSpec((1, tm, d), row),
        out_shape=jax.ShapeDtypeStruct((b, n, d), F32),
        compiler_params=_cparams("parallel", "parallel"),
        name="out0",
    )(ya, yb, x, mod, w)


def _ffn_kernel(x_ref, mod_ref, gain_ref, w1_ref, w2_ref, fg_ref, o_ref, *, final):
    x = x_ref[0]
    h = _norm_mod(x, gain_ref[...], mod_ref[0, 3:4, :], mod_ref[0, 4:5, :]).astype(BF16)
    hid = w2_ref.shape[0]
    ch = hid // 2
    acc = None
    for j in range(2):
        gate = _dot(h, w1_ref[:, j * ch:(j + 1) * ch])
        up = _dot(h, w1_ref[:, hid + j * ch:hid + (j + 1) * ch])
        act = (gate * _sigmoid(gate) * up).astype(BF16)
        part = _dot(act, w2_ref[j * ch:(j + 1) * ch, :])
        acc = part if acc is None else acc + part
    y = x + mod_ref[0, 5:6, :] * acc
    if final:
        ms = jnp.mean(y * y, axis=-1, keepdims=True)
        y = y * lax.rsqrt(ms + NORM_EPS) * fg_ref[...]
    o_ref[0] = y


def _ffn(x, mod, gain, w1, w2, fgain, final):
    b, n, d = x.shape
    tm = ROW_TILE
    row = lambda bi, i: (bi, i, 0)
    const = lambda bi, i: (0, 0)
    return pl.pallas_call(
        functools.partial(_ffn_kernel, final=final),
        grid=(b, n // tm),
        in_specs=[pl.BlockSpec((1, tm, d), row),
                  pl.BlockSpec((1, 6, d), lambda bi, i: (bi, 0, 0)),
                  pl.BlockSpec((1, d), const),
                  pl.BlockSpec(w1.shape, const, pipeline_mode=pl.Buffered(1)),
                  pl.BlockSpec(w2.shape, const, pipeline_mode=pl.Buffered(1)),
                  pl.BlockSpec((1, d), const)],
        out_specs=pl.BlockSpec((1, tm, d), row),
        out_shape=jax.ShapeDtypeStruct((b, n, d), F32),
        compiler_params=_cparams("parallel", "parallel"),
        name="ffn",
    )(x, mod, gain, w1, w2, fgain)


def _proj1_kernel(x_ref, mod_ref, gain_ref, w_ref, o_ref):
    h = _norm_mod(x_ref[0], gain_ref[...], mod_ref[0, 0:1, :], mod_ref[0, 1:2, :])
    o_ref[0] = _dot(h.astype(BF16), w_ref[...])


def _proj1(x, mod, gain, w):
    b, n, d = x.shape
    tm = ROW_TILE
    nw = w.shape[1]
    return pl.pallas_call(
        _proj1_kernel,
        grid=(b, n // tm),
        in_specs=[pl.BlockSpec((1, tm, d), lambda bi, i: (bi, i, 0)),
                  pl.BlockSpec((1, 6, d), lambda bi, i: (bi, 0, 0)),
                  pl.BlockSpec((1, d), lambda bi, i: (0, 0)),
                  pl.BlockSpec((d, nw), lambda bi, i: (0, 0))],
        out_specs=pl.BlockSpec((1, tm, nw), lambda bi, i: (bi, i, 0)),
        out_shape=jax.ShapeDtypeStruct((b, n, nw), F32),
        compiler_params=_cparams("parallel", "parallel"),
        name="proj1",
    )(x, mod, gain, w)


def _prep_kernel(pc_ref, prev_ref, next_ref, mu_ref, w0_ref, w2_ref, a0_ref, a2_ref, g2_ref,
                 kk_ref, ka_ref, rk_ref, seg_ref, *rest, with_pool):
    if with_pool:
        pw_ref, ps_ref, r_o, v_o, kk_o, g_o, bonus_o, lw_o, kd_o, b_o, d_o, ext = rest
    else:
        r_o, v_o, kk_o, g_o, bonus_o, lw_o, kd_o, b_o, ext = rest
    i = pl.program_id(1)
    nt = pl.num_programs(1)
    tm = pc_ref.shape[1]
    cw = C_WIDTH
    ext[HALO:HALO + tm, :] = pc_ref[0]
    ext[0:HALO, :] = jnp.where(i > 0, prev_ref[0], 0.0)
    ext[HALO + tm:2 * HALO + tm, :] = jnp.where(i < nt - 1, next_ref[0], 0.0)

    def mixed(lo, hi):
        cur = ext[HALO:HALO + tm, lo:hi]
        nb = 0.5 * (ext[HALO - 1:HALO - 1 + tm, lo:hi] + ext[HALO + 1:HALO + 1 + tm, lo:hi])
        return cur + (nb - cur) * mu_ref[:, lo:hi]

    lora = mixed(3 * cw, C_IN)
    tw = jnp.tanh(lora[:, 0:LANES])
    xa = lora[:, LANES:2 * LANES]
    g_o[0] = _dot32(_sigmoid(lora[:, 2 * LANES:3 * LANES]), g2_ref[...])
    seg = seg_ref[...]
    for d in range(2):
        z = w0_ref[d] + _dot32(tw, w2_ref[d])
        w_log = -(jnp.maximum(-z, 0.0) + jnp.log(1.0 + jnp.exp(-jnp.abs(z)))) - 0.5
        lw_o[d, 0] = -jnp.exp(w_log)
        kd_o[d, 0] = _sigmoid(a0_ref[d] + _dot32(xa, a2_ref[d]))
    for pb in range(cw // LANES):
        sl = slice(pb * LANES, (pb + 1) * LANES)
        r = mixed(pb * LANES, (pb + 1) * LANES)
        k = mixed(cw + pb * LANES, cw + (pb + 1) * LANES)
        v = mixed(2 * cw + pb * LANES, 2 * cw + (pb + 1) * LANES)
        kk = k * kk_ref[:, sl]
        nrm = jnp.sqrt(_dot32(kk * kk, seg))
        kk = kk / jnp.maximum(nrm, 1e-12)
        ksum = jnp.zeros_like(k)
        for d in range(2):
            a = kd_o[d, 0, :, sl]
            k_d = k * (1.0 + (a - 1.0) * ka_ref[:, sl])
            kd_o[d, 0, :, sl] = k_d
            b_o[d, 0, :, sl] = kk * a
            ksum = ksum + k_d
        r_o[0, :, sl] = r
        v_o[0, :, sl] = v
        kk_o[0, :, sl] = kk
        bonus_o[0, :, sl] = _dot32(r * rk_ref[:, sl] * ksum, seg) * v
    if with_pool:
        n_tok = nt * tm
        pos = i * tm + lax.broadcasted_iota(jnp.int32, (tm, LANES), 0)
        group1 = lax.broadcasted_iota(jnp.int32, (tm, LANES), 1) >= D_WIDTH // 4
        pooled = []
        for half in range(2):
            lo, hi = C_IN + half * LANES, C_IN + (half + 1) * LANES
            w_small, w_big = POOL_WINDOWS[2 * half], POOL_WINDOWS[2 * half + 1]
            s_small = None
            s_big = None
            for off in range(-(w_big // 2), w_big - w_big // 2):
                piece = ext[HALO + off:HALO + off + tm, lo:hi]
                s_big = piece if s_big is None else s_big + piece
                if -(w_small // 2) <= off < w_small - w_small // 2:
                    s_small = piece if s_small is None else s_small + piece

            def count(w):
                lo_p = jnp.clip(pos - w // 2, 0, n_tok)
                hi_p = jnp.clip(pos + (w - w // 2), 0, n_tok)
                return (hi_p - lo_p).astype(F32)

            mean = jnp.where(group1, s_big / count(w_big), s_small / count(w_small))
            pooled.append((mean - ext[HALO:HALO + tm, lo:hi]).astype(BF16))
        pm = jnp.concatenate(pooled, axis=1)
        d_o[0] = _dot(pm, pw_ref[...]) * ps_ref[...]


def _prep(pc, wts, with_pool):
    b, n, cin = pc.shape
    tm = ROW_TILE
    cw = C_WIDTH
    per = tm // HALO
    last = n // HALO - 1
    row = lambda bi, i: (bi, i, 0)
    drow = lambda bi, i: (0, bi, i, 0)

    def full(a):
        return pl.BlockSpec(a.shape, lambda bi, i, nd=a.ndim: (0,) * nd)

    in_specs = [pl.BlockSpec((1, tm, cin), row),
                pl.BlockSpec((1, HALO, cin), lambda bi, i: (bi, jnp.maximum(i * per - 1, 0), 0)),
                pl.BlockSpec((1, HALO, cin), lambda bi, i: (bi, jnp.minimum((i + 1) * per, last), 0))]
    in_specs += [full(a) for a in wts]
    out_specs = [pl.BlockSpec((1, tm, cw), row)] * 5 + [pl.BlockSpec((2, 1, tm, cw), drow)] * 3
    out_shape = [jax.ShapeDtypeStruct((b, n, cw), F32)] * 5 + [jax.ShapeDtypeStruct((2, b, n, cw), F32)] * 3
    if with_pool:
        out_specs.append(pl.BlockSpec((1, tm, D_WIDTH), row))
        out_shape.append(jax.ShapeDtypeStruct((b, n, D_WIDTH), F32))
    return pl.pallas_call(
        functools.partial(_prep_kernel, with_pool=with_pool),
        grid=(b, n // tm),
        in_specs=in_specs,
        out_specs=out_specs,
        out_shape=out_shape,
        scratch_shapes=[pltpu.VMEM((tm + 2 * HALO, cin), F32)],
        compiler_params=_cparams("parallel", "parallel"),
        name="prep",
    )(pc, pc, pc, *wts)


def _scan_masks(rev):
    n = 2 * CHUNK
    t = np.arange(n)[:, None]
    u = np.arange(n)[None, :]
    same = (t // CHUNK) == (u // CHUNK)
    before = (u > t) if rev else (u < t)
    masks = [same & before, same & (before | (u == t))]
    s = 1
    while s < CHUNK:
        blk = (t // (2 * s)) == (u // (2 * s))
        t_late = (t % (2 * s) < s) if rev else (t % (2 * s) >= s)
        u_early = (u % (2 * s) >= s) if rev else (u % (2 * s) < s)
        masks.append(blk & t_late & u_early)
        s *= 2
    tc = np.arange(CHUNK)[:, None]
    uc = np.arange(CHUNK)[None, :]
    cum = (uc >= tc) if rev else (uc <= tc)
    return np.stack(masks).astype(np.float32), cum.astype(np.float32)


def _scan_kernel(r_ref, v_ref, kk_ref, lw_ref, kd_ref, b_ref, h0_ref, msk_ref, cum_ref, *rest,
                 rev, with_y):
    if with_y:
        y_ref, hT_ref, h_sc = rest
    else:
        hT_ref, h_sc = rest
    tb = pl.program_id(2)
    n_tb = pl.num_programs(2)
    n_chunks = r_ref.shape[1] // CHUNK
    n2 = 2 * CHUNK

    @pl.when(tb == 0)
    def _():
        h_sc[...] = h0_ref[0, 0]

    low = lax.broadcasted_iota(jnp.int32, (CHUNK, LANES), 1) < HEAD_DIM
    rid = lax.broadcasted_iota(jnp.int32, (n2, n2), 0)
    cid = lax.broadcasted_iota(jnp.int32, (n2, n2), 1)
    eye = (rid == cid).astype(F32)
    n_levels = msk_ref.shape[0] - 2

    def stack(z):
        return jnp.concatenate([jnp.where(low, z, 0.0), jnp.where(low, 0.0, z)], axis=0)

    order = range(n_chunks - 1, -1, -1) if rev else range(n_chunks)
    for c in order:
        rows = slice(c * CHUNK, (c + 1) * CHUNK)
        lw = lw_ref[0, 0, rows, :]
        kk = kk_ref[0, rows, :]
        v_st = stack(v_ref[0, rows, :])
        cs = _dot32(cum_ref[...], lw)
        ctot = jnp.sum(lw, axis=0, keepdims=True)
        e_up = jnp.exp(-cs)
        e_end = jnp.exp(ctot - cs)
        a_st = stack(-kk * jnp.exp(cs - lw))
        b_st = stack(b_ref[0, 0, rows, :] * e_up)
        k_st = stack(kd_ref[0, 0, rows, :] * e_up)
        bh_st = stack(b_ref[0, 0, rows, :] * e_end)
        kh_st = stack(kd_ref[0, 0, rows, :] * e_end)
        if with_y:
            lhs = jnp.concatenate([a_st, stack(r_ref[0, rows, :] * jnp.exp(cs))], axis=0)
        else:
            lhs = a_st
        gram = lax.dot_general(lhs, jnp.concatenate([b_st, k_st], axis=0), (((1,), (1,)), ((), ())),
                               preferred_element_type=F32, precision=HIGHEST)
        a_ab = jnp.where(msk_ref[0] > 0, gram[0:n2, 0:n2], 0.0)
        a_ak = jnp.where(msk_ref[0] > 0, gram[0:n2, n2:2 * n2], 0.0)
        t_inv = eye + a_ab * msk_ref[2]
        for lvl in range(1, n_levels):
            t_inv = t_inv + _dot32(t_inv, _dot32(a_ab * msk_ref[2 + lvl], t_inv))
        at = _dot32(t_inv, a_st)
        u0 = _dot32(t_inv, _dot32(a_ak, v_st))
        h = h_sc[...]
        if with_y:
            a_rb = jnp.where(msk_ref[1] > 0, gram[n2:2 * n2, 0:n2], 0.0)
            a_rk = jnp.where(msk_ref[1] > 0, gram[n2:2 * n2, n2:2 * n2], 0.0)
            rt = lhs[n2:2 * n2] + _dot32(a_rb, at)
            y_st = _dot32(rt, h) + _dot32(a_rb, u0) + _dot32(a_rk, v_st)
            y_ref[0, rows, :] = y_st[0:CHUNK] + y_st[CHUNK:n2]
        bh_t = bh_st.T
        m = eye * jnp.exp(ctot) + _dot32(bh_t, at)
        cmat = _dot32(bh_t, u0) + _dot32(kh_st.T, v_st)
        h_sc[...] = _dot32(m, h) + cmat

    @pl.when(tb == n_tb - 1)
    def _():
        hT_ref[0, 0] = h_sc[...]


def _scan(prep, d, h0, rev, with_y):
    r, v, kk, lw, kd, bb = prep
    b, n, cw = r.shape
    tb = min(SCAN_BLOCK, n)
    n_tb = n // tb
    n_pairs = cw // LANES
    msk, cum = _scan_masks(rev)
    tmap = (lambda t: n_tb - 1 - t) if rev else (lambda t: t)
    shared = pl.BlockSpec((1, tb, LANES), lambda bi, p, t: (bi, tmap(t), p))
    perdir = pl.BlockSpec((1, 1, tb, LANES), lambda bi, p, t: (d, bi, tmap(t), p))
    state = pl.BlockSpec((1, 1, LANES, LANES), lambda bi, p, t: (bi, p, 0, 0))
    out_specs = [state]
    out_shape = [jax.ShapeDtypeStruct((b, n_pairs, LANES, LANES), F32)]
    if with_y:
        out_specs = [shared] + out_specs
        out_shape = [jax.ShapeDtypeStruct((b, n, cw), F32)] + out_shape
    res = pl.pallas_call(
        functools.partial(_scan_kernel, rev=rev, with_y=with_y),
        grid=(b, n_pairs, n_tb),
        in_specs=[shared, shared, shared, perdir, perdir, perdir, state,
                  pl.BlockSpec(msk.shape, lambda bi, p, t: (0, 0, 0)),
                  pl.BlockSpec(cum.shape, lambda bi, p, t: (0, 0))],
        out_specs=out_specs,
        out_shape=out_shape,
        scratch_shapes=[pltpu.VMEM((LANES, LANES), F32)],
        compiler_params=_cparams("parallel", "parallel", "arbitrary"),
        name="scan_rev" if rev else "scan_fwd",
    )(r, v, kk, lw, kd, bb, h0, jnp.asarray(msk), jnp.asarray(cum))
    return (res[0], res[1]) if with_y else (None, res[0])


def _out1_kernel(yf_ref, yb_ref, bonus_ref, g_ref, dp_ref, x_ref, mod_ref, lw_ref, lb_ref, seg_ref,
                 w_ref, o_ref):
    seg = seg_ref[...]
    cw = yf_ref.shape[2]
    acc = _dot(dp_ref[0].astype(BF16), w_ref[cw:cw + D_WIDTH, :])
    for pb in range(cw // LANES):
        sl = slice(pb * LANES, (pb + 1) * LANES)
        y = yf_ref[0, :, sl] + yb_ref[0, :, sl]
        mean = _dot32(y, seg) * (1.0 / HEAD_DIM)
        dev = y - mean
        var = _dot32(dev * dev, seg) * (1.0 / HEAD_DIM)
        yn = dev * lax.rsqrt(var + LNX_EPS) * lw_ref[:, sl] + lb_ref[:, sl]
        z = (yn + bonus_ref[0, :, sl]) * g_ref[0, :, sl]
        acc = acc + _dot(z.astype(BF16), w_ref[sl, :])
    o_ref[0] = x_ref[0] + mod_ref[0, 2:3, :] * acc


def _out1(yf, yb, bonus, g, dp, x, mod, lnx_w, lnx_b, seg, w):
    b, n, d = x.shape
    tm = ROW_TILE
    cw = yf.shape[2]
    row = lambda bi, i: (bi, i, 0)
    const = lambda bi, i: (0, 0)
    return pl.pallas_call(
        _out1_kernel,
        grid=(b, n // tm),
        in_specs=[pl.BlockSpec((1, tm, cw), row)] * 4
        + [pl.BlockSpec((1, tm, D_WIDTH), row),
           pl.BlockSpec((1, tm, d), row),
           pl.BlockSpec((1, 6, d), lambda bi, i: (bi, 0, 0)),
           pl.BlockSpec((1, cw), const), pl.BlockSpec((1, cw), const),
           pl.BlockSpec((LANES, LANES), const),
           pl.BlockSpec(w.shape, const)],
        out_specs=pl.BlockSpec((1, tm, d), row),
        out_shape=jax.ShapeDtypeStruct((b, n, d), F32),
        compiler_params=_cparams("parallel", "parallel"),
        name="out1",
    )(yf, yb, bonus, g, dp, x, mod, lnx_w, lnx_b, seg, w)


def _rope_tables(n):
    rows = n // GRID_W
    row = jnp.repeat(jnp.arange(rows, dtype=F32), GRID_W)
    col = jnp.tile(jnp.arange(GRID_W, dtype=F32), rows)
    n_freq = HEAD_DIM // 4
    inv = ROPE_THETA ** (-jnp.arange(n_freq, dtype=F32) / n_freq)
    ang = jnp.concatenate([row[:, None] * inv[None, :], col[:, None] * inv[None, :]], axis=-1)
    cos, sin = jnp.cos(ang), jnp.sin(ang)
    cos_t = jnp.tile(cos, (1, LANES // cos.shape[1]))
    sin_t = jnp.tile(jnp.concatenate([-sin, sin], axis=-1), (1, LANES // HEAD_DIM))
    return cos_t, sin_t


def _kv_dup_columns():
    cols = []
    for section in range(4):
        for head in range(2):
            base = AB_Q_COLS + section * 2 * HEAD_DIM + head * HEAD_DIM
            cols += list(range(base, base + HEAD_DIM)) * 2
    return np.concatenate([np.arange(AB_Q_COLS), np.asarray(cols)])


def kernel(x, c, ctx, c_ctx, norm_gain, ada_w, ada_b, ffn_w_in, ffn_w_out, final_gain, ab_w_in, ab_q_gain, ab_k_gain, ab_sink, ab_w_out, cd_w_in, cd_mu, cd_w0, cd_w2, cd_a0, cd_a2, cd_g2, cd_k_k, cd_k_a, cd_r_k, cd_lnx_w, cd_lnx_b, cd_pool_w, cd_pool_scale, cd_w_out):
    b, n, d = x.shape
    nc = ctx.shape[1]
    pad = (-(b + 1)) % 8
    cs = jnp.concatenate([c, c_ctx[None, :], jnp.zeros((pad, d), F32)], axis=0)
    mods = _mods(cs, ada_w, ada_b)
    seg = jnp.asarray(np.kron(np.eye(2), np.ones((HEAD_DIM, HEAD_DIM))).astype(np.float32))
    fgain = final_gain.reshape(1, d)

    def layer_mods(i):
        ml = mods[i, :b].reshape(b, 6, d)
        mc = jnp.broadcast_to(mods[i, b].reshape(1, 6, d), (b, 6, d))
        return ml, mc

    ml, mc = layer_mods(0)
    w0 = ab_w_in[0][:, _kv_dup_columns()].astype(BF16)
    gain = norm_gain[0, 0].reshape(1, d)
    qg = jnp.tile(ab_q_gain[0], 2).reshape(1, LANES)
    kg = jnp.tile(ab_k_gain[0], 2).reshape(1, LANES)
    cos_l, sin_l = _rope_tables(n)
    cos_c, sin_c = jnp.ones((nc, LANES), F32), jnp.zeros((nc, LANES), F32)
    q_l, kv_l = _proj0(x, ml, gain, w0, cos_l, sin_l, qg, kg, seg)
    q_c, kv_c = _proj0(ctx, mc, gain, w0, cos_c, sin_c, qg, kg, seg)
    sink = ab_sink[0]
    w_out0 = ab_w_out[0].astype(BF16)
    gain2 = norm_gain[0, 1].reshape(1, d)
    w1 = ffn_w_in[0].astype(BF16)
    w2 = ffn_w_out[0].astype(BF16)
    xl = _out0(_attn_a(q_l, kv_c, sink, kv_l), _attn_b(q_l, kv_c, kv_l), x, ml, w_out0)
    xc = _out0(_attn_a(q_c, kv_c, sink), _attn_b(q_c, kv_c), ctx, mc, w_out0)
    xl = _ffn(xl, ml, gain2, w1, w2, fgain, False)
    xc = _ffn(xc, mc, gain2, w1, w2, fgain, False)

    ml, mc = layer_mods(1)
    gain = norm_gain[1, 0].reshape(1, d)
    w_in1 = cd_w_in[0].astype(BF16)
    pc_l = _proj1(xl, ml, gain, w_in1)
    pc_c = _proj1(xc, mc, gain, w_in1[:, :C_IN])
    zeros = jnp.zeros((DECAY_LORA_PAD, C_WIDTH), F32)
    w2x = jnp.stack([jnp.concatenate([cd_w2[0, 0], zeros]), jnp.concatenate([zeros, cd_w2[0, 1]])])
    a2x = jnp.stack([jnp.concatenate([cd_a2[0, 0], zeros]), jnp.concatenate([zeros, cd_a2[0, 1]])])
    wts = [cd_mu[0].reshape(1, C_IN), cd_w0[0].reshape(2, 1, C_WIDTH), w2x,
           cd_a0[0].reshape(2, 1, C_WIDTH), a2x, cd_g2[0],
           cd_k_k[0].reshape(1, C_WIDTH), cd_k_a[0].reshape(1, C_WIDTH),
           cd_r_k[0].reshape(1, C_WIDTH), seg]
    pool_w = jax.scipy.linalg.block_diag(*[cd_pool_w[0, g] for g in range(4)]).astype(BF16)
    pool_wts = [pool_w, cd_pool_scale[0].reshape(1, D_WIDTH)]
    r_c, v_c, kk_c, _, _, lw_c, kd_c, b_c = _prep(pc_c, wts, False)
    r_l, v_l, kk_l, g_l, bonus_l, lw_l, kd_l, b_l, dp_l = _prep(pc_l, wts + pool_wts, True)
    h_zero = jnp.zeros((b, C_WIDTH // LANES, LANES, LANES), F32)
    prep_c = (r_c, v_c, kk_c, lw_c, kd_c, b_c)
    prep_l = (r_l, v_l, kk_l, lw_l, kd_l, b_l)
    _, h_f = _scan(prep_c, 0, h_zero, False, False)
    _, h_b = _scan(prep_c, 1, h_zero, True, False)
    y_f, _ = _scan(prep_l, 0, h_f, False, True)
    y_b, _ = _scan(prep_l, 1, h_b, True, True)
    xl = _out1(y_f, y_b, bonus_l, g_l, dp_l, xl, ml, cd_lnx_w[0].reshape(1, C_WIDTH),
               cd_lnx_b[0].reshape(1, C_WIDTH), seg, cd_w_out[0].astype(BF16))
    return _ffn(xl, ml, norm_gain[1, 1].reshape(1, d), ffn_w_in[1].astype(BF16),
                ffn_w_out[1].astype(BF16), fgain, True)
```

```python
import functools

import numpy as np
import jax
import jax.numpy as jnp
from jax import lax
from jax.experimental import pallas as pl
from jax.experimental.pallas import tpu as pltpu

F32 = jnp.float32
BF16 = jnp.bfloat16
HIGHEST = lax.Precision.HIGHEST

D_MODEL = 1024
GRID_W = 64
HEAD_DIM = 64
ROPE_THETA = 10000.0
NORM_EPS = 1e-6
WINDOW = 128
AB_Q_COLS = 1024
C_WIDTH = 768
C_IN = 2688
D_WIDTH = 256
CD_IN = C_IN + D_WIDTH
LNX_EPS = 64e-5
FFN_HIDDEN = 2816
POOL_WINDOWS = (2, 4, 8, 16)
DECAY_LORA_PAD = 64

LANES = 128
ROW_TILE = 256
HALO = 8
CHUNK = 64
SCAN_BLOCK = 512
VMEM_LIMIT = 56 * 1024 * 1024


def _cparams(*sem):
    return pltpu.CompilerParams(dimension_semantics=sem, vmem_limit_bytes=VMEM_LIMIT)


def _dot(a, b):
    return jnp.dot(a, b, preferred_element_type=F32)


def _dot32(a, b):
    return jnp.dot(a, b, preferred_element_type=F32, precision=HIGHEST)


def _sigmoid(x):
    return 1.0 / (1.0 + jnp.exp(-x))


def _norm_mod(x, gain, shift, scale):
    ms = jnp.mean(x * x, axis=-1, keepdims=True)
    return (x * lax.rsqrt(ms + NORM_EPS) * gain) * (1.0 + scale) + shift


def _mods_kernel(c_ref, w_ref, b_ref, o_ref):
    c = c_ref[...]
    o_ref[0] = _dot32(c * _sigmoid(c), w_ref[0]) + b_ref[0]


def _mods(cs, ada_w, ada_b):
    depth, d, n6 = ada_w.shape
    tn = 768
    rows = cs.shape[0]
    return pl.pallas_call(
        _mods_kernel,
        grid=(depth, n6 // tn),
        in_specs=[pl.BlockSpec((rows, d), lambda l, j: (0, 0)),
                  pl.BlockSpec((1, d, tn), lambda l, j: (l, 0, j)),
                  pl.BlockSpec((1, 1, tn), lambda l, j: (l, 0, j))],
        out_specs=pl.BlockSpec((1, rows, tn), lambda l, j: (l, 0, j)),
        out_shape=jax.ShapeDtypeStruct((depth, rows, n6), F32),
        compiler_params=_cparams("arbitrary", "arbitrary"),
        name="mods",
    )(cs, ada_w, ada_b.reshape(depth, 1, n6))


def _proj0_kernel(x_ref, mod_ref, gain_ref, w_ref, cos_ref, sin_ref, qg_ref, kg_ref, seg_ref,
                  q_ref, kv_ref):
    h = _norm_mod(x_ref[0], gain_ref[...], mod_ref[0, 0:1, :], mod_ref[0, 1:2, :])
    p = _dot(h.astype(BF16), w_ref[...])
    tm = p.shape[0]
    cos = cos_ref[...]
    sin = sin_ref[...]
    lane = lax.broadcasted_iota(jnp.int32, (tm, LANES), 1)
    first_half = (lane & (HEAD_DIM - 1)) < HEAD_DIM // 2
    seg = seg_ref[...]

    def rope(z):
        partner = jnp.where(first_half, pltpu.roll(z, LANES - HEAD_DIM // 2, 1),
                            pltpu.roll(z, HEAD_DIM // 2, 1))
        return z * cos + partner * sin

    def head_norm(z, g):
        ms = _dot32(z * z, seg) * (1.0 / HEAD_DIM)
        return z * lax.rsqrt(ms + NORM_EPS) * g

    scale = HEAD_DIM ** -0.5
    for blk in range(8):
        z = p[:, blk * LANES:(blk + 1) * LANES]
        if blk >= 4:
            z = head_norm(z, qg_ref[...])
        q_ref[0, :, blk * LANES:(blk + 1) * LANES] = (rope(z) * scale).astype(BF16)
    for blk in range(8):
        z = p[:, AB_Q_COLS + blk * LANES:AB_Q_COLS + (blk + 1) * LANES]
        if blk in (4, 5):
            z = head_norm(z, kg_ref[...])
        if blk in (0, 1, 4, 5):
            z = rope(z)
        kv_ref[0, :, blk * LANES:(blk + 1) * LANES] = z.astype(BF16)


def _proj0(x, mod, gain, w, cos, sin, qg, kg, seg):
    b, n, d = x.shape
    tm = ROW_TILE
    nw = w.shape[1]
    const = lambda bi, i: (0, 0)
    return pl.pallas_call(
        _proj0_kernel,
        grid=(b, n // tm),
        in_specs=[pl.BlockSpec((1, tm, d), lambda bi, i: (bi, i, 0)),
                  pl.BlockSpec((1, 6, d), lambda bi, i: (bi, 0, 0)),
                  pl.BlockSpec((1, d), const),
                  pl.BlockSpec((d, nw), const),
                  pl.BlockSpec((tm, LANES), lambda bi, i: (i, 0)),
                  pl.BlockSpec((tm, LANES), lambda bi, i: (i, 0)),
                  pl.BlockSpec((1, LANES), const),
                  pl.BlockSpec((1, LANES), const),
                  pl.BlockSpec((LANES, LANES), const)],
        out_specs=[pl.BlockSpec((1, tm, 1024), lambda bi, i: (bi, i, 0)),
                   pl.BlockSpec((1, tm, 1024), lambda bi, i: (bi, i, 0))],
        out_shape=[jax.ShapeDtypeStruct((b, n, 1024), BF16),
                   jax.ShapeDtypeStruct((b, n, 1024), BF16)],
        compiler_params=_cparams("parallel", "parallel"),
        name="proj0",
    )(x, mod, gain, w, cos, sin, qg, kg, seg)


def _masked_q(q_ref, h, low):
    q2 = q_ref[0, :, (h // 2) * LANES:(h // 2 + 1) * LANES]
    keep = low if h % 2 == 0 else jnp.logical_not(low)
    return jnp.where(keep, q2, jnp.zeros_like(q2))


def _attn_b_kernel(q_ref, kc_ref, vc_ref, *rest, n_kt, has_latent):
    if has_latent:
        k_ref, v_ref, o_ref, m_sc, l_sc, acc_sc = rest
    else:
        o_ref, m_sc, l_sc, acc_sc = rest
    kt = pl.program_id(3)
    tq = q_ref.shape[1]
    low = lax.broadcasted_iota(jnp.int32, (tq, LANES), 1) < HEAD_DIM

    def update(k, v):
        reps = k.shape[0] // LANES
        for h in range(4):
            s = lax.dot_general(_masked_q(q_ref, h, low), k, (((1,), (1,)), ((), ())),
                                preferred_element_type=F32)
            m_prev = m_sc[h]
            m_new = jnp.maximum(m_prev, jnp.max(s, axis=1, keepdims=True))
            p = jnp.exp(s - jnp.concatenate([m_new] * reps, axis=1))
            alpha = jnp.exp(m_prev - m_new)
            l_sc[h] = alpha * l_sc[h] + jnp.sum(p, axis=1, keepdims=True)
            acc_sc[h] = alpha * acc_sc[h] + _dot(p.astype(BF16), v)
            m_sc[h] = m_new

    @pl.when(kt == 0)
    def _():
        m_sc[...] = jnp.full(m_sc.shape, -jnp.inf, F32)
        l_sc[...] = jnp.zeros(l_sc.shape, F32)
        acc_sc[...] = jnp.zeros(acc_sc.shape, F32)
        update(kc_ref[0], vc_ref[0])

    if has_latent:
        update(k_ref[0], v_ref[0])

    @pl.when(kt == n_kt - 1)
    def _():
        for pr in range(2):
            o0 = acc_sc[2 * pr] / l_sc[2 * pr]
            o1 = acc_sc[2 * pr + 1] / l_sc[2 * pr + 1]
            o_ref[0, :, pr * LANES:(pr + 1) * LANES] = jnp.where(low, o0, o1).astype(BF16)


def _attn_b(q, kv_c, kv=None, tk=512):
    b, n, _ = q.shape
    nc = kv_c.shape[1]
    tq = ROW_TILE
    has_latent = kv is not None
    n_kt = kv.shape[1] // tk if has_latent else 1
    in_specs = [pl.BlockSpec((1, tq, 2 * LANES), lambda bi, g, i, j: (bi, i, 2 + g)),
                pl.BlockSpec((1, nc, LANES), lambda bi, g, i, j: (bi, 0, 4 + g)),
                pl.BlockSpec((1, nc, LANES), lambda bi, g, i, j: (bi, 0, 6 + g))]
    args = [q, kv_c, kv_c]
    if has_latent:
        in_specs += [pl.BlockSpec((1, tk, LANES), lambda bi, g, i, j: (bi, j, 4 + g)),
                     pl.BlockSpec((1, tk, LANES), lambda bi, g, i, j: (bi, j, 6 + g))]
        args += [kv, kv]
    return pl.pallas_call(
        functools.partial(_attn_b_kernel, n_kt=n_kt, has_latent=has_latent),
        grid=(b, 2, n // tq, n_kt),
        in_specs=in_specs,
        out_specs=pl.BlockSpec((1, tq, 2 * LANES), lambda bi, g, i, j: (bi, i, g)),
        out_shape=jax.ShapeDtypeStruct((b, n, 512), BF16),
        scratch_shapes=[pltpu.VMEM((4, tq, LANES), F32), pltpu.VMEM((4, tq, LANES), F32),
                        pltpu.VMEM((4, tq, LANES), F32)],
        compiler_params=_cparams("parallel", "parallel", "parallel", "arbitrary"),
        name="attn_b",
    )(*args)


def _attn_a_kernel(sink_ref, q_ref, kc_ref, vc_ref, *rest, has_local):
    if has_local:
        kp_ref, kcur_ref, kn_ref, vp_ref, vcur_ref, vn_ref, o_ref, kbuf, vbuf = rest
    else:
        o_ref, kbuf, vbuf = rest
    g = pl.program_id(1)
    i = pl.program_id(2)
    n_i = pl.num_programs(2)
    tq = q_ref.shape[1]
    nc = kc_ref.shape[1]
    kbuf[0:nc, :] = kc_ref[0]
    vbuf[0:nc, :] = vc_ref[0]
    nk = nc
    if has_local:
        kbuf[nc:nc + WINDOW, :] = kp_ref[0]
        kbuf[nc + WINDOW:nc + WINDOW + tq, :] = kcur_ref[0]
        kbuf[nc + WINDOW + tq:nc + 2 * WINDOW + tq, :] = kn_ref[0]
        vbuf[nc:nc + WINDOW, :] = vp_ref[0]
        vbuf[nc + WINDOW:nc + WINDOW + tq, :] = vcur_ref[0]
        vbuf[nc + WINDOW + tq:nc + 2 * WINDOW + tq, :] = vn_ref[0]
        nk = nc + 2 * WINDOW + tq
        row = lax.broadcasted_iota(jnp.int32, (tq, nk), 0)
        col = lax.broadcasted_iota(jnp.int32, (tq, nk), 1) - nc
        lo_c = jnp.where(i == 0, WINDOW, 0)
        hi_c = jnp.where(i == n_i - 1, WINDOW + tq, 2 * WINDOW + tq)
        valid = (col < 0) | ((col >= row) & (col <= row + 2 * WINDOW) & (col >= lo_c) & (col < hi_c))
    k = kbuf[...]
    v = vbuf[...]
    low = lax.broadcasted_iota(jnp.int32, (tq, LANES), 1) < HEAD_DIM
    outs = []
    for h in range(4):
        s = lax.dot_general(_masked_q(q_ref, h, low), k, (((1,), (1,)), ((), ())),
                            preferred_element_type=F32)
        if has_local:
            s = jnp.where(valid, s, -jnp.inf)
        sink = sink_ref[g * 4 + h]
        m = jnp.maximum(jnp.max(s, axis=1, keepdims=True), sink)
        p = jnp.exp(s - m)
        l = jnp.sum(p, axis=1, keepdims=True) + jnp.exp(sink - m)
        outs.append(_dot(p.astype(BF16), v) / l)
    for pr in range(2):
        o_ref[0, :, pr * LANES:(pr + 1) * LANES] = jnp.where(low, outs[2 * pr], outs[2 * pr + 1]).astype(BF16)


def _attn_a(q, kv_c, sink, kv=None):
    b, n, _ = q.shape
    nc = kv_c.shape[1]
    tq = ROW_TILE
    has_local = kv is not None
    in_specs = [pl.BlockSpec(memory_space=pltpu.SMEM),
                pl.BlockSpec((1, tq, 2 * LANES), lambda bi, g, i: (bi, i, g)),
                pl.BlockSpec((1, nc, LANES), lambda bi, g, i: (bi, 0, g)),
                pl.BlockSpec((1, nc, LANES), lambda bi, g, i: (bi, 0, 2 + g))]
    args = [sink, q, kv_c, kv_c]
    nk = nc
    if has_local:
        per = tq // WINDOW
        last = n // WINDOW - 1
        prev_i = lambda i: jnp.maximum(i * per - 1, 0)
        next_i = lambda i: jnp.minimum((i + 1) * per, last)
        for off in (0, 2):
            in_specs += [pl.BlockSpec((1, WINDOW, LANES), lambda bi, g, i, off=off: (bi, prev_i(i), off + g)),
                         pl.BlockSpec((1, tq, LANES), lambda bi, g, i, off=off: (bi, i, off + g)),
                         pl.BlockSpec((1, WINDOW, LANES), lambda bi, g, i, off=off: (bi, next_i(i), off + g))]
            args += [kv, kv, kv]
        nk = nc + 2 * WINDOW + tq
    return pl.pallas_call(
        functools.partial(_attn_a_kernel, has_local=has_local),
        grid=(b, 2, n // tq),
        in_specs=in_specs,
        out_specs=pl.BlockSpec((1, tq, 2 * LANES), lambda bi, g, i: (bi, i, g)),
        out_shape=jax.ShapeDtypeStruct((b, n, 512), BF16),
        scratch_shapes=[pltpu.VMEM((nk, LANES), BF16), pltpu.VMEM((nk, LANES), BF16)],
        compiler_params=_cparams("parallel", "parallel", "parallel"),
        name="attn_a",
    )(*args)


def _out0_kernel(ya_ref, yb_ref, x_ref, mod_ref, w_ref, o_ref):
    half = ya_ref.shape[2]
    y = _dot(ya_ref[0], w_ref[0:half, :]) + _dot(yb_ref[0], w_ref[half:2 * half, :])
    o_ref[0] = x_ref[0] + mod_ref[0, 2:3, :] * y


def _out0(ya, yb, x, mod, w):
    b, n, d = x.shape
    tm = ROW_TILE
    row = lambda bi, i: (bi, i, 0)
    return pl.pallas_call(
        _out0_kernel,
        grid=(b, n // tm),
        in_specs=[pl.BlockSpec((1, tm, ya.shape[2]), row),
                  pl.BlockSpec((1, tm, yb.shape[2]), row),
                  pl.BlockSpec((1, tm, d), row),
                  pl.BlockSpec((1, 6, d), lambda bi, i: (bi, 0, 0)),
                  pl.BlockSpec(w.shape, lambda bi, i: (0, 0))],
        out_specs=pl.BlockSpec((1, tm, d), row),
        out_shape=jax.ShapeDtypeStruct((b, n, d), F32),
        compiler_params=_cparams("parallel", "parallel"),
        name="out0",
    )(ya, yb, x, mod, w)


def _ffn_kernel(x_ref, mod_ref, gain_ref, w1_ref, w2_ref, fg_ref, o_ref, *, final):
    x = x_ref[0]
    h = _norm_mod(x, gain_ref[...], mod_ref[0, 3:4, :], mod_ref[0, 4:5, :]).astype(BF16)
    hid = w2_ref.shape[0]
    ch = hid // 2
    acc = None
    for j in range(2):
        gate = _dot(h, w1_ref[:, j * ch:(j + 1) * ch])
        up = _dot(h, w1_ref[:, hid + j * ch:hid + (j + 1) * ch])
        act = (gate * _sigmoid(gate) * up).astype(BF16)
        part = _dot(act, w2_ref[j * ch:(j + 1) * ch, :])
        acc = part if acc is None else acc + part
    y = x + mod_ref[0, 5:6, :] * acc
    if final:
        ms = jnp.mean(y * y, axis=-1, keepdims=True)
        y = y * lax.rsqrt(ms + NORM_EPS) * fg_ref[...]
    o_ref[0] = y


def _ffn(x, mod, gain, w1, w2, fgain, final):
    b, n, d = x.shape
    tm = ROW_TILE
    row = lambda bi, i: (bi, i, 0)
    const = lambda bi, i: (0, 0)
    return pl.pallas_call(
        functools.partial(_ffn_kernel, final=final),
        grid=(b, n // tm),
        in_specs=[pl.BlockSpec((1, tm, d), row),
                  pl.BlockSpec((1, 6, d), lambda bi, i: (bi, 0, 0)),
                  pl.BlockSpec((1, d), const),
                  pl.BlockSpec(w1.shape, const, pipeline_mode=pl.Buffered(1)),
                  pl.BlockSpec(w2.shape, const, pipeline_mode=pl.Buffered(1)),
                  pl.BlockSpec((1, d), const)],
        out_specs=pl.BlockSpec((1, tm, d), row),
        out_shape=jax.ShapeDtypeStruct((b, n, d), F32),
        compiler_params=_cparams("parallel", "parallel"),
        name="ffn",
    )(x, mod, gain, w1, w2, fgain)


def _proj1_kernel(x_ref, mod_ref, gain_ref, w_ref, o_ref):
    h = _norm_mod(x_ref[0], gain_ref[...], mod_ref[0, 0:1, :], mod_ref[0, 1:2, :])
    o_ref[0] = _dot(h.astype(BF16), w_ref[...])


def _proj1(x, mod, gain, w):
    b, n, d = x.shape
    tm = ROW_TILE
    nw = w.shape[1]
    return pl.pallas_call(
        _proj1_kernel,
        grid=(b, n // tm),
        in_specs=[pl.BlockSpec((1, tm, d), lambda bi, i: (bi, i, 0)),
                  pl.BlockSpec((1, 6, d), lambda bi, i: (bi, 0, 0)),
                  pl.BlockSpec((1, d), lambda bi, i: (0, 0)),
                  pl.BlockSpec((d, nw), lambda bi, i: (0, 0))],
        out_specs=pl.BlockSpec((1, tm, nw), lambda bi, i: (bi, i, 0)),
        out_shape=jax.ShapeDtypeStruct((b, n, nw), F32),
        compiler_params=_cparams("parallel", "parallel"),
        name="proj1",
    )(x, mod, gain, w)


def _prep_kernel(pc_ref, prev_ref, next_ref, mu_ref, w0_ref, w2_ref, a0_ref, a2_ref, g2_ref,
                 kk_ref, ka_ref, rk_ref, seg_ref, tri_ref, *rest, latent):
    if latent:
        (pw_ref, ps_ref, v_o, at_o, rt_o, bt_o, kt_o, bh_o, kh_o, wl_o, g_o, bonus_o, d_o,
         ext, tmp) = rest
    else:
        v_o, at_o, rt_o, bt_o, kt_o, bh_o, kh_o, wl_o, ext, tmp = rest
    i = pl.program_id(1)
    nt = pl.num_programs(1)
    tm = pc_ref.shape[1]
    cw = C_WIDTH
    ext[HALO:HALO + tm, :] = pc_ref[0]
    ext[0:HALO, :] = jnp.where(i > 0, prev_ref[0], 0.0)
    ext[HALO + tm:2 * HALO + tm, :] = jnp.where(i < nt - 1, next_ref[0], 0.0)

    def mixed(lo, hi):
        cur = ext[HALO:HALO + tm, lo:hi]
        nb = 0.5 * (ext[HALO - 1:HALO - 1 + tm, lo:hi] + ext[HALO + 1:HALO + 1 + tm, lo:hi])
        return cur + (nb - cur) * mu_ref[:, lo:hi]

    lora = mixed(3 * cw, C_IN)
    tw = jnp.tanh(lora[:, 0:LANES])
    xa = lora[:, LANES:2 * LANES]
    if latent:
        g_o[0] = _dot32(_sigmoid(lora[:, 2 * LANES:3 * LANES]), g2_ref[...])
    for d in range(2):
        z = w0_ref[d] + _dot32(tw, w2_ref[d])
        w_log = -(jnp.maximum(-z, 0.0) + jnp.log(1.0 + jnp.exp(-jnp.abs(z)))) - 0.5
        tmp[d] = -jnp.exp(w_log)
        tmp[2 + d] = _sigmoid(a0_ref[d] + _dot32(xa, a2_ref[d]))
    seg = seg_ref[...]
    n_chunks = tm // CHUNK
    for pb in range(cw // LANES):
        sl = slice(pb * LANES, (pb + 1) * LANES)
        r = mixed(pb * LANES, (pb + 1) * LANES)
        k = mixed(cw + pb * LANES, cw + (pb + 1) * LANES)
        v = mixed(2 * cw + pb * LANES, 2 * cw + (pb + 1) * LANES)
        kk = k * kk_ref[:, sl]
        kk = kk / jnp.maximum(jnp.sqrt(_dot32(kk * kk, seg)), 1e-12)
        ksum = jnp.zeros_like(k)
        for d in range(2):
            lw = tmp[d, :, sl]
            a = tmp[2 + d, :, sl]
            k_d = k * (1.0 + (a - 1.0) * ka_ref[:, sl])
            bb = kk * a
            ksum = ksum + k_d
            cs = _dot32(tri_ref[d], lw)
            tot = []
            for j in range(n_chunks):
                last = j * CHUNK if d == 1 else (j + 1) * CHUNK - 1
                ctot = cs[last:last + 1, :]
                wl_o[d, 0, j, :, sl] = jnp.exp(ctot)
                tot.append(jnp.broadcast_to(ctot, (CHUNK, LANES)))
            e_up = jnp.exp(-cs)
            e_end = jnp.exp(jnp.concatenate(tot, axis=0) - cs)
            at_o[d, 0, :, sl] = (-kk * jnp.exp(cs - lw)).astype(BF16)
            rt_o[d, 0, :, sl] = (r * jnp.exp(cs)).astype(BF16)
            bt_o[d, 0, :, sl] = (bb * e_up).astype(BF16)
            kt_o[d, 0, :, sl] = (k_d * e_up).astype(BF16)
            bh_o[d, 0, :, sl] = (bb * e_end).astype(BF16)
            kh_o[d, 0, :, sl] = (k_d * e_end).astype(BF16)
        v_o[0, :, sl] = v.astype(BF16)
        if latent:
            bonus_o[0, :, sl] = _dot32(r * rk_ref[:, sl] * ksum, seg) * v
    if latent:
        n_tok = nt * tm
        pos = i * tm + lax.broadcasted_iota(jnp.int32, (tm, LANES), 0)
        group1 = lax.broadcasted_iota(jnp.int32, (tm, LANES), 1) >= D_WIDTH // 4
        pooled = []
        for half in range(2):
            lo, hi = C_IN + half * LANES, C_IN + (half + 1) * LANES
            w_small, w_big = POOL_WINDOWS[2 * half], POOL_WINDOWS[2 * half + 1]
            s_small = None
            s_big = None
            for off in range(-(w_big // 2), w_big - w_big // 2):
                piece = ext[HALO + off:HALO + off + tm, lo:hi]
                s_big = piece if s_big is None else s_big + piece
                if -(w_small // 2) <= off < w_small - w_small // 2:
                    s_small = piece if s_small is None else s_small + piece

            def count(w):
                lo_p = jnp.clip(pos - w // 2, 0, n_tok)
                hi_p = jnp.clip(pos + (w - w // 2), 0, n_tok)
                return (hi_p - lo_p).astype(F32)

            mean = jnp.where(group1, s_big / count(w_big), s_small / count(w_small))
            pooled.append((mean - ext[HALO:HALO + tm, lo:hi]).astype(BF16))
        pm = jnp.concatenate(pooled, axis=1)
        d_o[0] = _dot(pm, pw_ref[...]) * ps_ref[...]


def _chunk_cumsum_matrices(tm):
    t = np.arange(tm)[:, None]
    u = np.arange(tm)[None, :]
    same = (t // CHUNK) == (u // CHUNK)
    return np.stack([same & (u <= t), same & (u >= t)]).astype(np.float32)


def _prep(pc, wts, latent):
    b, n, cin = pc.shape
    tm = ROW_TILE
    cw = C_WIDTH
    per = tm // HALO
    last = n // HALO - 1
    n_chunks = tm // CHUNK
    row = lambda bi, i: (bi, i, 0)
    drow = lambda bi, i: (0, bi, i, 0)
    wts = list(wts)
    wts = wts[:10] + [jnp.asarray(_chunk_cumsum_matrices(tm))] + (wts[10:] if latent else [])

    def full(a):
        return pl.BlockSpec(a.shape, lambda bi, i, nd=a.ndim: (0,) * nd)

    in_specs = [pl.BlockSpec((1, tm, cin), row),
                pl.BlockSpec((1, HALO, cin), lambda bi, i: (bi, jnp.maximum(i * per - 1, 0), 0)),
                pl.BlockSpec((1, HALO, cin), lambda bi, i: (bi, jnp.minimum((i + 1) * per, last), 0))]
    in_specs += [full(a) for a in wts]
    out_specs = [pl.BlockSpec((1, tm, cw), row)] + [pl.BlockSpec((2, 1, tm, cw), drow)] * 6
    out_shape = [jax.ShapeDtypeStruct((b, n, cw), BF16)] + [jax.ShapeDtypeStruct((2, b, n, cw), BF16)] * 6
    out_specs.append(pl.BlockSpec((2, 1, n_chunks, 1, cw), lambda bi, i: (0, bi, i, 0, 0)))
    out_shape.append(jax.ShapeDtypeStruct((2, b, n // CHUNK, 1, cw), F32))
    if latent:
        out_specs += [pl.BlockSpec((1, tm, cw), row)] * 2 + [pl.BlockSpec((1, tm, D_WIDTH), row)]
        out_shape += [jax.ShapeDtypeStruct((b, n, cw), F32)] * 2 + [jax.ShapeDtypeStruct((b, n, D_WIDTH), F32)]
    return pl.pallas_call(
        functools.partial(_prep_kernel, latent=latent),
        grid=(b, n // tm),
        in_specs=in_specs,
        out_specs=out_specs,
        out_shape=out_shape,
        scratch_shapes=[pltpu.VMEM((tm + 2 * HALO, cin), F32), pltpu.VMEM((4, tm, cw), F32)],
        compiler_params=_cparams("parallel", "parallel"),
        name="prep",
    )(pc, pc, pc, *wts)


def _scan_masks(rev):
    n = 2 * CHUNK
    t = np.arange(n)[:, None]
    u = np.arange(n)[None, :]
    same = (t // CHUNK) == (u // CHUNK)
    before = (u > t) if rev else (u < t)
    masks = [same & before, same & (before | (u == t))]
    s = 1
    while s < CHUNK:
        blk = (t // (2 * s)) == (u // (2 * s))
        t_late = (t % (2 * s) < s) if rev else (t % (2 * s) >= s)
        u_early = (u % (2 * s) >= s) if rev else (u % (2 * s) < s)
        masks.append(blk & t_late & u_early)
        s *= 2
    return np.stack(masks).astype(np.float32)


def _scan_kernel(v_ref, at_ref, rt_ref, bt_ref, kt_ref, bh_ref, kh_ref, wl_ref, h0_ref, msk_ref, *rest,
                 rev, with_y):
    if with_y:
        y_ref, hT_ref, g_sc = rest
    else:
        hT_ref, g_sc = rest
    tb = pl.program_id(2)
    n_tb = pl.num_programs(2)
    n_chunks = v_ref.shape[1] // CHUNK
    n2 = 2 * CHUNK

    @pl.when(tb == 0)
    def _():
        g_sc[...] = h0_ref[0, 0]

    low = lax.broadcasted_iota(jnp.int32, (CHUNK, LANES), 1) < HEAD_DIM
    rid = lax.broadcasted_iota(jnp.int32, (n2, n2), 0)
    cid = lax.broadcasted_iota(jnp.int32, (n2, n2), 1)
    eye = (rid == cid).astype(F32)
    n_levels = msk_ref.shape[0] - 2
    strict = msk_ref[0] > 0
    incl = msk_ref[1] > 0
    nt_dims = (((1,), (1,)), ((), ()))
    tn_dims = (((0,), (0,)), ((), ()))

    def stack(z):
        zero = jnp.zeros_like(z)
        return jnp.concatenate([jnp.where(low, z, zero), jnp.where(low, zero, z)], axis=0)

    order = list(range(n_chunks - 1, -1, -1) if rev else range(n_chunks))
    ch = {}
    for c in order:
        rows = slice(c * CHUNK, (c + 1) * CHUNK)
        q = ch[c] = {"rows": rows}
        q["v"] = stack(v_ref[0, rows, :])
        q["a"] = stack(at_ref[0, 0, rows, :])
        rhs = jnp.concatenate([stack(bt_ref[0, 0, rows, :]), stack(kt_ref[0, 0, rows, :])], axis=0)
        if with_y:
            q["r"] = stack(rt_ref[0, 0, rows, :])
            lhs = jnp.concatenate([q["a"], q["r"]], axis=0)
        else:
            lhs = q["a"]
        gram = lax.dot_general(lhs, rhs, nt_dims, preferred_element_type=F32)
        q["ab"] = jnp.where(strict, gram[0:n2, 0:n2], 0.0)
        q["akv"] = _dot(jnp.where(strict, gram[0:n2, n2:2 * n2], 0.0).astype(BF16), q["v"])
        if with_y:
            q["rb"] = jnp.where(incl, gram[n2:2 * n2, 0:n2], 0.0).astype(BF16)
            q["rkv"] = _dot(jnp.where(incl, gram[n2:2 * n2, n2:2 * n2], 0.0).astype(BF16), q["v"])
        q["t"] = eye + q["ab"] * msk_ref[2]
    for lvl in range(1, n_levels):
        for c in order:
            q = ch[c]
            q["x"] = _dot((q["ab"] * msk_ref[2 + lvl]).astype(BF16), q["t"].astype(BF16))
        for c in order:
            q = ch[c]
            q["t"] = q["t"] + _dot(q["t"].astype(BF16), q["x"].astype(BF16))
    for c in order:
        q = ch[c]
        rows = q["rows"]
        au = _dot(q["t"].astype(BF16), jnp.concatenate([q["a"], q["akv"].astype(BF16)], axis=1))
        q["au"] = au.astype(BF16)
        q["mc"] = lax.dot_general(q["au"], stack(bh_ref[0, 0, rows, :]), tn_dims,
                                  preferred_element_type=F32)
        q["mt"] = q["mc"][0:LANES].astype(BF16)
        q["ct"] = q["mc"][LANES:2 * LANES] + lax.dot_general(
            q["v"], stack(kh_ref[0, 0, rows, :]), tn_dims, preferred_element_type=F32)
        if with_y:
            rbau = _dot(q["rb"], q["au"])
            q["rt"] = (q["r"].astype(F32) + rbau[:, 0:LANES]).astype(BF16)
            q["y0"] = rbau[:, LANES:2 * LANES] + q["rkv"]
    g = g_sc[...]
    for c in order:
        q = ch[c]
        g_b = g.astype(BF16)
        if with_y:
            y_st = lax.dot_general(q["rt"], g_b, nt_dims, preferred_element_type=F32) + q["y0"]
            y_ref[0, q["rows"], :] = y_st[0:CHUNK] + y_st[CHUNK:n2]
        g = g * wl_ref[0, 0, c] + _dot(g_b, q["mt"]) + q["ct"]
    g_sc[...] = g

    @pl.when(tb == n_tb - 1)
    def _():
        hT_ref[0, 0] = g_sc[...]


def _scan(prep, d, h0, rev, with_y):
    v, at, rt, bt, kt, bh, kh, wl = prep
    b, n, cw = v.shape
    tb = min(SCAN_BLOCK, n)
    n_tb = n // tb
    n_pairs = cw // LANES
    msk = _scan_masks(rev)
    tmap = (lambda t: n_tb - 1 - t) if rev else (lambda t: t)
    shared = pl.BlockSpec((1, tb, LANES), lambda bi, p, t: (bi, tmap(t), p))
    perdir = pl.BlockSpec((1, 1, tb, LANES), lambda bi, p, t: (d, bi, tmap(t), p))
    decay = pl.BlockSpec((1, 1, tb // CHUNK, 1, LANES), lambda bi, p, t: (d, bi, tmap(t), 0, p))
    state = pl.BlockSpec((1, 1, LANES, LANES), lambda bi, p, t: (bi, p, 0, 0))
    out_specs = [state]
    out_shape = [jax.ShapeDtypeStruct((b, n_pairs, LANES, LANES), F32)]
    if with_y:
        out_specs = [shared] + out_specs
        out_shape = [jax.ShapeDtypeStruct((b, n, cw), F32)] + out_shape
    res = pl.pallas_call(
        functools.partial(_scan_kernel, rev=rev, with_y=with_y),
        grid=(b, n_pairs, n_tb),
        in_specs=[shared, perdir, perdir, perdir, perdir, perdir, perdir, decay, state,
                  pl.BlockSpec(msk.shape, lambda bi, p, t: (0, 0, 0))],
        out_specs=out_specs,
        out_shape=out_shape,
        scratch_shapes=[pltpu.VMEM((LANES, LANES), F32)],
        compiler_params=_cparams("parallel", "parallel", "arbitrary"),
        name="scan_rev" if rev else "scan_fwd",
    )(v, at, rt, bt, kt, bh, kh, wl, h0, jnp.asarray(msk))
    return (res[0], res[1]) if with_y else (None, res[0])


def _out1_kernel(yf_ref, yb_ref, bonus_ref, g_ref, dp_ref, x_ref, mod_ref, lw_ref, lb_ref, seg_ref,
                 w_ref, o_ref):
    seg = seg_ref[...]
    cw = yf_ref.shape[2]
    acc = _dot(dp_ref[0].astype(BF16), w_ref[cw:cw + D_WIDTH, :])
    for pb in range(cw // LANES):
        sl = slice(pb * LANES, (pb + 1) * LANES)
        y = yf_ref[0, :, sl] + yb_ref[0, :, sl]
        mean = _dot32(y, seg) * (1.0 / HEAD_DIM)
        dev = y - mean
        var = _dot32(dev * dev, seg) * (1.0 / HEAD_DIM)
        yn = dev * lax.rsqrt(var + LNX_EPS) * lw_ref[:, sl] + lb_ref[:, sl]
        z = (yn + bonus_ref[0, :, sl]) * g_ref[0, :, sl]
        acc = acc + _dot(z.astype(BF16), w_ref[sl, :])
    o_ref[0] = x_ref[0] + mod_ref[0, 2:3, :] * acc


def _out1(yf, yb, bonus, g, dp, x, mod, lnx_w, lnx_b, seg, w):
    b, n, d = x.shape
    tm = ROW_TILE
    cw = yf.shape[2]
    row = lambda bi, i: (bi, i, 0)
    const = lambda bi, i: (0, 0)
    return pl.pallas_call(
        _out1_kernel,
        grid=(b, n // tm),
        in_specs=[pl.BlockSpec((1, tm, cw), row)] * 4
        + [pl.BlockSpec((1, tm, D_WIDTH), row),
           pl.BlockSpec((1, tm, d), row),
           pl.BlockSpec((1, 6, d), lambda bi, i: (bi, 0, 0)),
           pl.BlockSpec((1, cw), const), pl.BlockSpec((1, cw), const),
           pl.BlockSpec((LANES, LANES), const),
           pl.BlockSpec(w.shape, const)],
        out_specs=pl.BlockSpec((1, tm, d), row),
        out_shape=jax.ShapeDtypeStruct((b, n, d), F32),
        compiler_params=_cparams("parallel", "parallel"),
        name="out1",
    )(yf, yb, bonus, g, dp, x, mod, lnx_w, lnx_b, seg, w)


def _rope_tables(n):
    rows = n // GRID_W
    row = jnp.repeat(jnp.arange(rows, dtype=F32), GRID_W)
    col = jnp.tile(jnp.arange(GRID_W, dtype=F32), rows)
    n_freq = HEAD_DIM // 4
    inv = ROPE_THETA ** (-jnp.arange(n_freq, dtype=F32) / n_freq)
    ang = jnp.concatenate([row[:, None] * inv[None, :], col[:, None] * inv[None, :]], axis=-1)
    cos, sin = jnp.cos(ang), jnp.sin(ang)
    cos_t = jnp.tile(cos, (1, LANES // cos.shape[1]))
    sin_t = jnp.tile(jnp.concatenate([-sin, sin], axis=-1), (1, LANES // HEAD_DIM))
    return cos_t, sin_t


def _kv_dup_columns():
    cols = []
    for section in range(4):
        for head in range(2):
            base = AB_Q_COLS + section * 2 * HEAD_DIM + head * HEAD_DIM
            cols += list(range(base, base + HEAD_DIM)) * 2
    return np.concatenate([np.arange(AB_Q_COLS), np.asarray(cols)])


def kernel(x, c, ctx, c_ctx, norm_gain, ada_w, ada_b, ffn_w_in, ffn_w_out, final_gain, ab_w_in, ab_q_gain, ab_k_gain, ab_sink, ab_w_out, cd_w_in, cd_mu, cd_w0, cd_w2, cd_a0, cd_a2, cd_g2, cd_k_k, cd_k_a, cd_r_k, cd_lnx_w, cd_lnx_b, cd_pool_w, cd_pool_scale, cd_w_out):
    b, n, d = x.shape
    nc = ctx.shape[1]
    pad = (-(b + 1)) % 8
    cs = jnp.concatenate([c, c_ctx[None, :], jnp.zeros((pad, d), F32)], axis=0)
    mods = _mods(cs, ada_w, ada_b)
    seg = jnp.asarray(np.kron(np.eye(2), np.ones((HEAD_DIM, HEAD_DIM))).astype(np.float32))
    fgain = final_gain.reshape(1, d)

    def layer_mods(i):
        ml = mods[i, :b].reshape(b, 6, d)
        mc = jnp.broadcast_to(mods[i, b].reshape(1, 6, d), (b, 6, d))
        return ml, mc

    ml, mc = layer_mods(0)
    w0 = ab_w_in[0][:, _kv_dup_columns()].astype(BF16)
    gain = norm_gain[0, 0].reshape(1, d)
    qg = jnp.tile(ab_q_gain[0], 2).reshape(1, LANES)
    kg = jnp.tile(ab_k_gain[0], 2).reshape(1, LANES)
    cos_l, sin_l = _rope_tables(n)
    cos_c, sin_c = jnp.ones((nc, LANES), F32), jnp.zeros((nc, LANES), F32)
    q_l, kv_l = _proj0(x, ml, gain, w0, cos_l, sin_l, qg, kg, seg)
    q_c, kv_c = _proj0(ctx, mc, gain, w0, cos_c, sin_c, qg, kg, seg)
    sink = ab_sink[0]
    w_out0 = ab_w_out[0].astype(BF16)
    gain2 = norm_gain[0, 1].reshape(1, d)
    w1 = ffn_w_in[0].astype(BF16)
    w2 = ffn_w_out[0].astype(BF16)
    xl = _out0(_attn_a(q_l, kv_c, sink, kv_l), _attn_b(q_l, kv_c, kv_l), x, ml, w_out0)
    xc = _out0(_attn_a(q_c, kv_c, sink), _attn_b(q_c, kv_c), ctx, mc, w_out0)
    xl = _ffn(xl, ml, gain2, w1, w2, fgain, False)
    xc = _ffn(xc, mc, gain2, w1, w2, fgain, False)

    ml, mc = layer_mods(1)
    gain = norm_gain[1, 0].reshape(1, d)
    w_in1 = cd_w_in[0].astype(BF16)
    pc_l = _proj1(xl, ml, gain, w_in1)
    pc_c = _proj1(xc, mc, gain, w_in1[:, :C_IN])
    zeros = jnp.zeros((DECAY_LORA_PAD, C_WIDTH), F32)
    w2x = jnp.stack([jnp.concatenate([cd_w2[0, 0], zeros]), jnp.concatenate([zeros, cd_w2[0, 1]])])
    a2x = jnp.stack([jnp.concatenate([cd_a2[0, 0], zeros]), jnp.concatenate([zeros, cd_a2[0, 1]])])
    wts = [cd_mu[0].reshape(1, C_IN), cd_w0[0].reshape(2, 1, C_WIDTH), w2x,
           cd_a0[0].reshape(2, 1, C_WIDTH), a2x, cd_g2[0],
           cd_k_k[0].reshape(1, C_WIDTH), cd_k_a[0].reshape(1, C_WIDTH),
           cd_r_k[0].reshape(1, C_WIDTH), seg]
    pool_w = jax.scipy.linalg.block_diag(*[cd_pool_w[0, g] for g in range(4)]).astype(BF16)
    pool_wts = [pool_w, cd_pool_scale[0].reshape(1, D_WIDTH)]
    prep_c = _prep(pc_c, wts, False)
    *prep_l, g_l, bonus_l, dp_l = _prep(pc_l, wts + pool_wts, True)
    h_zero = jnp.zeros((b, C_WIDTH // LANES, LANES, LANES), F32)
    _, h_f = _scan(prep_c, 0, h_zero, False, False)
    _, h_b = _scan(prep_c, 1, h_zero, True, False)
    y_f, _ = _scan(prep_l, 0, h_f, False, True)
    y_b, _ = _scan(prep_l, 1, h_b, True, True)
    xl = _out1(y_f, y_b, bonus_l, g_l, dp_l, xl, ml, cd_lnx_w[0].reshape(1, C_WIDTH),
               cd_lnx_b[0].reshape(1, C_WIDTH), seg, cd_w_out[0].astype(BF16))
    return _ffn(xl, ml, norm_gain[1, 1].reshape(1, d), ffn_w_in[1].astype(BF16),
                ffn_w_out[1].astype(BF16), fgain, True)
```

```python
import functools

import numpy as np
import jax
import jax.numpy as jnp
from jax import lax
from jax.experimental import pallas as pl
from jax.experimental.pallas import tpu as pltpu

F32 = jnp.float32
BF16 = jnp.bfloat16
HIGHEST = lax.Precision.HIGHEST
LOG2_E = 1.4426950408889634

D_MODEL = 1024
GRID_W = 64
HEAD_DIM = 64
ROPE_THETA = 10000.0
NORM_EPS = 1e-6
WINDOW = 128
AB_Q_COLS = 1024
C_WIDTH = 768
C_IN = 2688
D_WIDTH = 256
CD_IN = C_IN + D_WIDTH
LNX_EPS = 64e-5
FFN_HIDDEN = 2816
POOL_WINDOWS = (2, 4, 8, 16)
DECAY_LORA_PAD = 64

LANES = 128
ROW_TILE = 256
HALO = 8
CHUNK = 64
SCAN_BLOCK = 512
SCAN_PAIRS = 3
VMEM_LIMIT = 56 * 1024 * 1024


def _cparams(*sem):
    return pltpu.CompilerParams(dimension_semantics=sem, vmem_limit_bytes=VMEM_LIMIT)


def _dot(a, b):
    return jnp.dot(a, b, preferred_element_type=F32)


def _dot32(a, b):
    return jnp.dot(a, b, preferred_element_type=F32, precision=HIGHEST)


def _sigmoid(x):
    return 1.0 / (1.0 + jnp.exp(-x))


def _norm_mod(x, gain, shift, scale):
    ms = jnp.mean(x * x, axis=-1, keepdims=True)
    return (x * lax.rsqrt(ms + NORM_EPS) * gain) * (1.0 + scale) + shift


def _mods_kernel(c_ref, w_ref, b_ref, o_ref):
    c = c_ref[...]
    o_ref[0] = _dot32(c * _sigmoid(c), w_ref[0]) + b_ref[0]


def _mods(cs, ada_w, ada_b):
    depth, d, n6 = ada_w.shape
    tn = 768
    rows = cs.shape[0]
    return pl.pallas_call(
        _mods_kernel,
        grid=(depth, n6 // tn),
        in_specs=[pl.BlockSpec((rows, d), lambda l, j: (0, 0)),
                  pl.BlockSpec((1, d, tn), lambda l, j: (l, 0, j)),
                  pl.BlockSpec((1, 1, tn), lambda l, j: (l, 0, j))],
        out_specs=pl.BlockSpec((1, rows, tn), lambda l, j: (l, 0, j)),
        out_shape=jax.ShapeDtypeStruct((depth, rows, n6), F32),
        compiler_params=_cparams("arbitrary", "arbitrary"),
        name="mods",
    )(cs, ada_w, ada_b.reshape(depth, 1, n6))


def _proj0_kernel(x_ref, mod_ref, gain_ref, w_ref, cos_ref, sin_ref, qg_ref, kg_ref, seg_ref,
                  q_ref, kv_ref):
    h = _norm_mod(x_ref[0], gain_ref[...], mod_ref[0, 0:1, :], mod_ref[0, 1:2, :])
    p = _dot(h.astype(BF16), w_ref[...])
    tm = p.shape[0]
    cos = cos_ref[...]
    sin = sin_ref[...]
    lane = lax.broadcasted_iota(jnp.int32, (tm, LANES), 1)
    first_half = (lane & (HEAD_DIM - 1)) < HEAD_DIM // 2
    seg = seg_ref[...]

    def rope(z):
        partner = jnp.where(first_half, pltpu.roll(z, LANES - HEAD_DIM // 2, 1),
                            pltpu.roll(z, HEAD_DIM // 2, 1))
        return z * cos + partner * sin

    def head_norm(z, g):
        ms = _dot32(z * z, seg) * (1.0 / HEAD_DIM)
        return z * lax.rsqrt(ms + NORM_EPS) * g

    scale = HEAD_DIM ** -0.5
    for blk in range(8):
        z = p[:, blk * LANES:(blk + 1) * LANES]
        if blk >= 4:
            z = head_norm(z, qg_ref[...])
        blk_scale = scale * LOG2_E if blk >= 4 else scale
        q_ref[0, :, blk * LANES:(blk + 1) * LANES] = (rope(z) * blk_scale).astype(BF16)
    for blk in range(8):
        z = p[:, AB_Q_COLS + blk * LANES:AB_Q_COLS + (blk + 1) * LANES]
        if blk in (4, 5):
            z = head_norm(z, kg_ref[...])
        if blk in (0, 1, 4, 5):
            z = rope(z)
        kv_ref[0, :, blk * LANES:(blk + 1) * LANES] = z.astype(BF16)


def _proj0(x, mod, gain, w, cos, sin, qg, kg, seg):
    b, n, d = x.shape
    tm = ROW_TILE
    nw = w.shape[1]
    const = lambda bi, i: (0, 0)
    return pl.pallas_call(
        _proj0_kernel,
        grid=(b, n // tm),
        in_specs=[pl.BlockSpec((1, tm, d), lambda bi, i: (bi, i, 0)),
                  pl.BlockSpec((1, 6, d), lambda bi, i: (bi, 0, 0)),
                  pl.BlockSpec((1, d), const),
                  pl.BlockSpec((d, nw), const),
                  pl.BlockSpec((tm, LANES), lambda bi, i: (i, 0)),
                  pl.BlockSpec((tm, LANES), lambda bi, i: (i, 0)),
                  pl.BlockSpec((1, LANES), const),
                  pl.BlockSpec((1, LANES), const),
                  pl.BlockSpec((LANES, LANES), const)],
        out_specs=[pl.BlockSpec((1, tm, 1024), lambda bi, i: (bi, i, 0)),
                   pl.BlockSpec((1, tm, 1024), lambda bi, i: (bi, i, 0))],
        out_shape=[jax.ShapeDtypeStruct((b, n, 1024), BF16),
                   jax.ShapeDtypeStruct((b, n, 1024), BF16)],
        compiler_params=_cparams("parallel", "parallel"),
        name="proj0",
    )(x, mod, gain, w, cos, sin, qg, kg, seg)


def _masked_q(q_ref, h, low):
    q2 = q_ref[0, :, (h // 2) * LANES:(h // 2 + 1) * LANES]
    keep = low if h % 2 == 0 else jnp.logical_not(low)
    return jnp.where(keep, q2, jnp.zeros_like(q2))


def _attn_b_kernel(q_ref, kc_ref, vc_ref, *rest, n_kt, has_latent):
    if has_latent:
        k_ref, v_ref, o_ref, m_sc, acc_sc = rest
    else:
        o_ref, m_sc, acc_sc = rest
    kt = pl.program_id(3)
    tq = q_ref.shape[1]
    low = lax.broadcasted_iota(jnp.int32, (tq, LANES), 1) < HEAD_DIM

    def update(k, v):
        tk = k.shape[0]
        v1 = jnp.where(lax.broadcasted_iota(jnp.int32, (tk, LANES), 1) < HEAD_DIM, v, jnp.ones_like(v))
        ss = [lax.dot_general(_masked_q(q_ref, h, low), k, (((1,), (1,)), ((), ())),
                              preferred_element_type=F32) for h in range(4)]
        for h in range(4):
            m_prev = m_sc[h]
            m_new = jnp.maximum(m_prev, jnp.max(ss[h], axis=1, keepdims=True))
            p = jnp.exp2(ss[h] - jnp.concatenate([m_new] * (tk // LANES), axis=1)).astype(BF16)
            acc_sc[h] = jnp.exp2(m_prev - m_new) * acc_sc[h] + _dot(p, v1)
            m_sc[h] = m_new

    @pl.when(kt == 0)
    def _():
        m_sc[...] = jnp.full(m_sc.shape, -jnp.inf, F32)
        acc_sc[...] = jnp.zeros(acc_sc.shape, F32)
        update(kc_ref[0], vc_ref[0])

    if has_latent:
        update(k_ref[0], v_ref[0])

    @pl.when(kt == n_kt - 1)
    def _():
        for pr in range(2):
            a0 = acc_sc[2 * pr]
            a1 = acc_sc[2 * pr + 1]
            o0 = a0 / pltpu.roll(a0, HEAD_DIM, 1)
            o1 = pltpu.roll(a1, HEAD_DIM, 1) / a1
            o_ref[0, :, pr * LANES:(pr + 1) * LANES] = jnp.where(low, o0, o1).astype(BF16)


def _attn_b(q, kv_c, kv=None, tk=1024):
    b, n, _ = q.shape
    nc = kv_c.shape[1]
    tq = ROW_TILE
    has_latent = kv is not None
    n_kt = kv.shape[1] // tk if has_latent else 1
    in_specs = [pl.BlockSpec((1, tq, 2 * LANES), lambda bi, g, i, j: (bi, i, 2 + g)),
                pl.BlockSpec((1, nc, LANES), lambda bi, g, i, j: (bi, 0, 4 + g)),
                pl.BlockSpec((1, nc, LANES), lambda bi, g, i, j: (bi, 0, 6 + g))]
    args = [q, kv_c, kv_c]
    if has_latent:
        in_specs += [pl.BlockSpec((1, tk, LANES), lambda bi, g, i, j: (bi, j, 4 + g)),
                     pl.BlockSpec((1, tk, LANES), lambda bi, g, i, j: (bi, j, 6 + g))]
        args += [kv, kv]
    return pl.pallas_call(
        functools.partial(_attn_b_kernel, n_kt=n_kt, has_latent=has_latent),
        grid=(b, 2, n // tq, n_kt),
        in_specs=in_specs,
        out_specs=pl.BlockSpec((1, tq, 2 * LANES), lambda bi, g, i, j: (bi, i, g)),
        out_shape=jax.ShapeDtypeStruct((b, n, 512), BF16),
        scratch_shapes=[pltpu.VMEM((4, tq, LANES), F32), pltpu.VMEM((4, tq, LANES), F32)],
        compiler_params=_cparams("parallel", "parallel", "parallel", "arbitrary"),
        name="attn_b",
    )(*args)


def _attn_a_kernel(sink_ref, q_ref, kc_ref, vc_ref, *rest, has_local):
    if has_local:
        kp_ref, kcur_ref, kn_ref, vp_ref, vcur_ref, vn_ref, o_ref, kbuf, vbuf = rest
    else:
        o_ref, kbuf, vbuf = rest
    g = pl.program_id(1)
    i = pl.program_id(2)
    n_i = pl.num_programs(2)
    tq = q_ref.shape[1]
    nc = kc_ref.shape[1]
    kbuf[0:nc, :] = kc_ref[0]
    vbuf[0:nc, :] = vc_ref[0]
    nk = nc
    if has_local:
        kbuf[nc:nc + WINDOW, :] = kp_ref[0]
        kbuf[nc + WINDOW:nc + WINDOW + tq, :] = kcur_ref[0]
        kbuf[nc + WINDOW + tq:nc + 2 * WINDOW + tq, :] = kn_ref[0]
        vbuf[nc:nc + WINDOW, :] = vp_ref[0]
        vbuf[nc + WINDOW:nc + WINDOW + tq, :] = vcur_ref[0]
        vbuf[nc + WINDOW + tq:nc + 2 * WINDOW + tq, :] = vn_ref[0]
        nk = nc + 2 * WINDOW + tq
        row = lax.broadcasted_iota(jnp.int32, (tq, nk), 0)
        col = lax.broadcasted_iota(jnp.int32, (tq, nk), 1) - nc
        lo_c = jnp.where(i == 0, WINDOW, 0)
        hi_c = jnp.where(i == n_i - 1, WINDOW + tq, 2 * WINDOW + tq)
        valid = (col < 0) | ((col >= row) & (col <= row + 2 * WINDOW) & (col >= lo_c) & (col < hi_c))
    k = kbuf[...]
    v = vbuf[...]
    low = lax.broadcasted_iota(jnp.int32, (tq, LANES), 1) < HEAD_DIM
    outs = []
    for h in range(4):
        s = lax.dot_general(_masked_q(q_ref, h, low), k, (((1,), (1,)), ((), ())),
                            preferred_element_type=F32)
        if has_local:
            s = jnp.where(valid, s, -jnp.inf)
        sink = sink_ref[g * 4 + h]
        m = jnp.maximum(jnp.max(s, axis=1, keepdims=True), sink)
        p = jnp.exp(s - m)
        l = jnp.sum(p, axis=1, keepdims=True) + jnp.exp(sink - m)
        outs.append(_dot(p.astype(BF16), v) / l)
    for pr in range(2):
        o_ref[0, :, pr * LANES:(pr + 1) * LANES] = jnp.where(low, outs[2 * pr], outs[2 * pr + 1]).astype(BF16)


def _attn_a(q, kv_c, sink, kv=None):
    b, n, _ = q.shape
    nc = kv_c.shape[1]
    tq = ROW_TILE
    has_local = kv is not None
    in_specs = [pl.BlockSpec(memory_space=pltpu.SMEM),
                pl.BlockSpec((1, tq, 2 * LANES), lambda bi, g, i: (bi, i, g)),
                pl.BlockSpec((1, nc, LANES), lambda bi, g, i: (bi, 0, g)),
                pl.BlockSpec((1, nc, LANES), lambda bi, g, i: (bi, 0, 2 + g))]
    args = [sink, q, kv_c, kv_c]
    nk = nc
    if has_local:
        per = tq // WINDOW
        last = n // WINDOW - 1
        prev_i = lambda i: jnp.maximum(i * per - 1, 0)
        next_i = lambda i: jnp.minimum((i + 1) * per, last)
        for off in (0, 2):
            in_specs += [pl.BlockSpec((1, WINDOW, LANES), lambda bi, g, i, off=off: (bi, prev_i(i), off + g)),
                         pl.BlockSpec((1, tq, LANES), lambda bi, g, i, off=off: (bi, i, off + g)),
                         pl.BlockSpec((1, WINDOW, LANES), lambda bi, g, i, off=off: (bi, next_i(i), off + g))]
            args += [kv, kv, kv]
        nk = nc + 2 * WINDOW + tq
    return pl.pallas_call(
        functools.partial(_attn_a_kernel, has_local=has_local),
        grid=(b, 2, n // tq),
        in_specs=in_specs,
        out_specs=pl.BlockSpec((1, tq, 2 * LANES), lambda bi, g, i: (bi, i, g)),
        out_shape=jax.ShapeDtypeStruct((b, n, 512), BF16),
        scratch_shapes=[pltpu.VMEM((nk, LANES), BF16), pltpu.VMEM((nk, LANES), BF16)],
        compiler_params=_cparams("parallel", "parallel", "parallel"),
        name="attn_a",
    )(*args)


def _out0_kernel(ya_ref, yb_ref, x_ref, mod_ref, w_ref, o_ref):
    half = ya_ref.shape[2]
    y = _dot(ya_ref[0], w_ref[0:half, :]) + _dot(yb_ref[0], w_ref[half:2 * half, :])
    o_ref[0] = x_ref[0] + mod_ref[0, 2:3, :] * y


def _out0(ya, yb, x, mod, w):
    b, n, d = x.shape
    tm = ROW_TILE
    row = lambda bi, i: (bi, i, 0)
    return pl.pallas_call(
        _out0_kernel,
        grid=(b, n // tm),
        in_specs=[pl.BlockSpec((1, tm, ya.shape[2]), row),
                  pl.BlockSpec((1, tm, yb.shape[2]), row),
                  pl.BlockSpec((1, tm, d), row),
                  pl.BlockSpec((1, 6, d), lambda bi, i: (bi, 0, 0)),
                  pl.BlockSpec(w.shape, lambda bi, i: (0, 0))],
        out_specs=pl.BlockSpec((1, tm, d), row),
        out_shape=jax.ShapeDtypeStruct((b, n, d), F32),
        compiler_params=_cparams("parallel", "parallel"),
        name="out0",
    )(ya, yb, x, mod, w)


def _ffn_kernel(x_ref, mod_ref, gain_ref, w1_ref, w2_ref, fg_ref, o_ref, *, final):
    x = x_ref[0]
    h = _norm_mod(x, gain_ref[...], mod_ref[0, 3:4, :], mod_ref[0, 4:5, :]).astype(BF16)
    hid = w2_ref.shape[0]
    ch = hid // 2
    acc = None
    for j in range(2):
        gate = _dot(h, w1_ref[:, j * ch:(j + 1) * ch])
        up = _dot(h, w1_ref[:, hid + j * ch:hid + (j + 1) * ch])
        act = (gate * _sigmoid(gate) * up).astype(BF16)
        part = _dot(act, w2_ref[j * ch:(j + 1) * ch, :])
        acc = part if acc is None else acc + part
    y = x + mod_ref[0, 5:6, :] * acc
    if final:
        ms = jnp.mean(y * y, axis=-1, keepdims=True)
        y = y * lax.rsqrt(ms + NORM_EPS) * fg_ref[...]
    o_ref[0] = y


def _ffn(x, mod, gain, w1, w2, fgain, final):
    b, n, d = x.shape
    tm = ROW_TILE
    row = lambda bi, i: (bi, i, 0)
    const = lambda bi, i: (0, 0)
    return pl.pallas_call(
        functools.partial(_ffn_kernel, final=final),
        grid=(b, n // tm),
        in_specs=[pl.BlockSpec((1, tm, d), row),
                  pl.BlockSpec((1, 6, d), lambda bi, i: (bi, 0, 0)),
                  pl.BlockSpec((1, d), const),
                  pl.BlockSpec(w1.shape, const, pipeline_mode=pl.Buffered(1)),
                  pl.BlockSpec(w2.shape, const, pipeline_mode=pl.Buffered(1)),
                  pl.BlockSpec((1, d), const)],
        out_specs=pl.BlockSpec((1, tm, d), row),
        out_shape=jax.ShapeDtypeStruct((b, n, d), F32),
        compiler_params=_cparams("parallel", "parallel"),
        name="ffn",
    )(x, mod, gain, w1, w2, fgain)


def _proj1_kernel(x_ref, mod_ref, gain_ref, w_ref, o_ref):
    h = _norm_mod(x_ref[0], gain_ref[...], mod_ref[0, 0:1, :], mod_ref[0, 1:2, :])
    o_ref[0] = _dot(h.astype(BF16), w_ref[...])


def _proj1(x, mod, gain, w):
    b, n, d = x.shape
    tm = ROW_TILE
    nw = w.shape[1]
    return pl.pallas_call(
        _proj1_kernel,
        grid=(b, n // tm),
        in_specs=[pl.BlockSpec((1, tm, d), lambda bi, i: (bi, i, 0)),
                  pl.BlockSpec((1, 6, d), lambda bi, i: (bi, 0, 0)),
                  pl.BlockSpec((1, d), lambda bi, i: (0, 0)),
                  pl.BlockSpec((d, nw), lambda bi, i: (0, 0))],
        out_specs=pl.BlockSpec((1, tm, nw), lambda bi, i: (bi, i, 0)),
        out_shape=jax.ShapeDtypeStruct((b, n, nw), F32),
        compiler_params=_cparams("parallel", "parallel"),
        name="proj1",
    )(x, mod, gain, w)


def _prep_kernel(pc_ref, prev_ref, next_ref, mu_ref, w0_ref, w2_ref, a0_ref, a2_ref, g2_ref,
                 kk_ref, ka_ref, rk_ref, seg_ref, tri_ref, *rest, latent):
    if latent:
        (pw_ref, ps_ref, v_o, at_o, rt_o, bt_o, kt_o, bh_o, kh_o, wl_o, g_o, bonus_o, d_o,
         ext, tmp) = rest
    else:
        v_o, at_o, rt_o, bt_o, kt_o, bh_o, kh_o, wl_o, ext, tmp = rest
    i = pl.program_id(1)
    nt = pl.num_programs(1)
    tm = pc_ref.shape[1]
    cw = C_WIDTH
    ext[HALO:HALO + tm, :] = pc_ref[0]
    ext[0:HALO, :] = jnp.where(i > 0, prev_ref[0], 0.0)
    ext[HALO + tm:2 * HALO + tm, :] = jnp.where(i < nt - 1, next_ref[0], 0.0)

    def mixed(lo, hi):
        cur = ext[HALO:HALO + tm, lo:hi]
        nb = 0.5 * (ext[HALO - 1:HALO - 1 + tm, lo:hi] + ext[HALO + 1:HALO + 1 + tm, lo:hi])
        return cur + (nb - cur) * mu_ref[:, lo:hi]

    lora = mixed(3 * cw, C_IN)
    tw = jnp.tanh(lora[:, 0:LANES])
    xa = lora[:, LANES:2 * LANES]
    if latent:
        g_o[0] = _dot32(_sigmoid(lora[:, 2 * LANES:3 * LANES]), g2_ref[...])
    for d in range(2):
        z = w0_ref[d] + _dot32(tw, w2_ref[d])
        w_log = -(jnp.maximum(-z, 0.0) + jnp.log(1.0 + jnp.exp(-jnp.abs(z)))) - 0.5
        tmp[d] = -jnp.exp(w_log)
        tmp[2 + d] = _sigmoid(a0_ref[d] + _dot32(xa, a2_ref[d]))
    seg = seg_ref[...]
    n_chunks = tm // CHUNK
    for pb in range(cw // LANES):
        sl = slice(pb * LANES, (pb + 1) * LANES)
        r = mixed(pb * LANES, (pb + 1) * LANES)
        k = mixed(cw + pb * LANES, cw + (pb + 1) * LANES)
        v = mixed(2 * cw + pb * LANES, 2 * cw + (pb + 1) * LANES)
        kk = k * kk_ref[:, sl]
        kk = kk / jnp.maximum(jnp.sqrt(_dot32(kk * kk, seg)), 1e-12)
        ksum = jnp.zeros_like(k)
        for d in range(2):
            lw = tmp[d, :, sl]
            a = tmp[2 + d, :, sl]
            k_d = k * (1.0 + (a - 1.0) * ka_ref[:, sl])
            bb = kk * a
            ksum = ksum + k_d
            cs = _dot32(tri_ref[d], lw)
            tot = []
            for j in range(n_chunks):
                last = j * CHUNK if d == 1 else (j + 1) * CHUNK - 1
                ctot = cs[last:last + 1, :]
                wl_o[d, 0, j, :, sl] = jnp.exp(ctot)
                tot.append(jnp.broadcast_to(ctot, (CHUNK, LANES)))
            e_up = jnp.exp(-cs)
            e_end = jnp.exp(jnp.concatenate(tot, axis=0) - cs)
            at_o[d, 0, :, sl] = (-kk * jnp.exp(cs - lw)).astype(BF16)
            rt_o[d, 0, :, sl] = (r * jnp.exp(cs)).astype(BF16)
            bt_o[d, 0, :, sl] = (bb * e_up).astype(BF16)
            kt_o[d, 0, :, sl] = (k_d * e_up).astype(BF16)
            bh_o[d, 0, :, sl] = (bb * e_end).astype(BF16)
            kh_o[d, 0, :, sl] = (k_d * e_end).astype(BF16)
        v_o[0, :, sl] = v.astype(BF16)
        if latent:
            bonus_o[0, :, sl] = _dot32(r * rk_ref[:, sl] * ksum, seg) * v
    if latent:
        n_tok = nt * tm
        pos = i * tm + lax.broadcasted_iota(jnp.int32, (tm, LANES), 0)
        group1 = lax.broadcasted_iota(jnp.int32, (tm, LANES), 1) >= D_WIDTH // 4
        pooled = []
        for half in range(2):
            lo, hi = C_IN + half * LANES, C_IN + (half + 1) * LANES
            w_small, w_big = POOL_WINDOWS[2 * half], POOL_WINDOWS[2 * half + 1]
            s_small = None
            s_big = None
            for off in range(-(w_big // 2), w_big - w_big // 2):
                piece = ext[HALO + off:HALO + off + tm, lo:hi]
                s_big = piece if s_big is None else s_big + piece
                if -(w_small // 2) <= off < w_small - w_small // 2:
                    s_small = piece if s_small is None else s_small + piece

            def count(w):
                lo_p = jnp.clip(pos - w // 2, 0, n_tok)
                hi_p = jnp.clip(pos + (w - w // 2), 0, n_tok)
                return (hi_p - lo_p).astype(F32)

            mean = jnp.where(group1, s_big / count(w_big), s_small / count(w_small))
            pooled.append((mean - ext[HALO:HALO + tm, lo:hi]).astype(BF16))
        pm = jnp.concatenate(pooled, axis=1)
        d_o[0] = _dot(pm, pw_ref[...]) * ps_ref[...]


def _chunk_cumsum_matrices(tm):
    t = np.arange(tm)[:, None]
    u = np.arange(tm)[None, :]
    same = (t // CHUNK) == (u // CHUNK)
    return np.stack([same & (u <= t), same & (u >= t)]).astype(np.float32)


def _prep(pc, wts, latent):
    b, n, cin = pc.shape
    tm = ROW_TILE
    cw = C_WIDTH
    per = tm // HALO
    last = n // HALO - 1
    n_chunks = tm // CHUNK
    row = lambda bi, i: (bi, i, 0)
    drow = lambda bi, i: (0, bi, i, 0)
    wts = list(wts)
    wts = wts[:10] + [jnp.asarray(_chunk_cumsum_matrices(tm))] + (wts[10:] if latent else [])

    def full(a):
        return pl.BlockSpec(a.shape, lambda bi, i, nd=a.ndim: (0,) * nd)

    in_specs = [pl.BlockSpec((1, tm, cin), row),
                pl.BlockSpec((1, HALO, cin), lambda bi, i: (bi, jnp.maximum(i * per - 1, 0), 0)),
                pl.BlockSpec((1, HALO, cin), lambda bi, i: (bi, jnp.minimum((i + 1) * per, last), 0))]
    in_specs += [full(a) for a in wts]
    out_specs = [pl.BlockSpec((1, tm, cw), row)] + [pl.BlockSpec((2, 1, tm, cw), drow)] * 6
    out_shape = [jax.ShapeDtypeStruct((b, n, cw), BF16)] + [jax.ShapeDtypeStruct((2, b, n, cw), BF16)] * 6
    out_specs.append(pl.BlockSpec((2, 1, n_chunks, 1, cw), lambda bi, i: (0, bi, i, 0, 0)))
    out_shape.append(jax.ShapeDtypeStruct((2, b, n // CHUNK, 1, cw), F32))
    if latent:
        out_specs += [pl.BlockSpec((1, tm, cw), row)] * 2 + [pl.BlockSpec((1, tm, D_WIDTH), row)]
        out_shape += [jax.ShapeDtypeStruct((b, n, cw), F32)] * 2 + [jax.ShapeDtypeStruct((b, n, D_WIDTH), F32)]
    return pl.pallas_call(
        functools.partial(_prep_kernel, latent=latent),
        grid=(b, n // tm),
        in_specs=in_specs,
        out_specs=out_specs,
        out_shape=out_shape,
        scratch_shapes=[pltpu.VMEM((tm + 2 * HALO, cin), F32), pltpu.VMEM((4, tm, cw), F32)],
        compiler_params=_cparams("parallel", "parallel"),
        name="prep",
    )(pc, pc, pc, *wts)


def _scan_masks(rev):
    n = 2 * CHUNK
    t = np.arange(n)[:, None]
    u = np.arange(n)[None, :]
    same = (t // CHUNK) == (u // CHUNK)
    before = (u > t) if rev else (u < t)
    masks = [same & before, same & (before | (u == t))]
    s = 1
    while s < CHUNK:
        blk = (t // (2 * s)) == (u // (2 * s))
        t_late = (t % (2 * s) < s) if rev else (t % (2 * s) >= s)
        u_early = (u % (2 * s) >= s) if rev else (u % (2 * s) < s)
        masks.append(blk & t_late & u_early)
        s *= 2
    return np.stack(masks).astype(np.float32)


def _scan_kernel(v_ref, at_ref, rt_ref, bt_ref, kt_ref, bh_ref, kh_ref, wl_ref, h0_ref, msk_ref, *rest,
                 rev, with_y):
    if with_y:
        y_ref, hT_ref, g_sc = rest
    else:
        hT_ref, g_sc = rest
    tb = pl.program_id(2)
    n_tb = pl.num_programs(2)
    n_chunks = v_ref.shape[1] // CHUNK
    n2 = 2 * CHUNK

    @pl.when(tb == 0)
    def _():
        g_sc[...] = h0_ref[0]

    low = lax.broadcasted_iota(jnp.int32, (CHUNK, LANES), 1) < HEAD_DIM
    rid = lax.broadcasted_iota(jnp.int32, (n2, n2), 0)
    cid = lax.broadcasted_iota(jnp.int32, (n2, n2), 1)
    eye = (rid == cid).astype(F32)
    n_levels = msk_ref.shape[0] - 2
    strict = msk_ref[0] > 0
    incl = msk_ref[1] > 0
    nt_dims = (((1,), (1,)), ((), ()))
    tn_dims = (((0,), (0,)), ((), ()))

    def stack(z):
        zero = jnp.zeros_like(z)
        return jnp.concatenate([jnp.where(low, z, zero), jnp.where(low, zero, z)], axis=0)

    order = list(range(n_chunks - 1, -1, -1) if rev else range(n_chunks))
    n_pairs = v_ref.shape[2] // LANES
    chains = [(pp, c) for c in order for pp in range(n_pairs)]
    ch = {}
    for key in chains:
        pp, c = key
        rows = slice(c * CHUNK, (c + 1) * CHUNK)
        lanes = slice(pp * LANES, (pp + 1) * LANES)
        q = ch[key] = {"rows": rows, "lanes": lanes}
        q["v"] = stack(v_ref[0, rows, lanes])
        q["a"] = stack(at_ref[0, 0, rows, lanes])
        rhs = jnp.concatenate([stack(bt_ref[0, 0, rows, lanes]), stack(kt_ref[0, 0, rows, lanes])], axis=0)
        if with_y:
            q["r"] = stack(rt_ref[0, 0, rows, lanes])
            lhs = jnp.concatenate([q["a"], q["r"]], axis=0)
        else:
            lhs = q["a"]
        gram = lax.dot_general(lhs, rhs, nt_dims, preferred_element_type=F32)
        q["ab"] = jnp.where(strict, gram[0:n2, 0:n2], 0.0)
        q["akv"] = _dot(jnp.where(strict, gram[0:n2, n2:2 * n2], 0.0).astype(BF16), q["v"])
        if with_y:
            q["rb"] = jnp.where(incl, gram[n2:2 * n2, 0:n2], 0.0).astype(BF16)
            q["rkv"] = _dot(jnp.where(incl, gram[n2:2 * n2, n2:2 * n2], 0.0).astype(BF16), q["v"])
        q["t"] = eye + q["ab"] * msk_ref[2]
    for lvl in range(1, n_levels):
        for key in chains:
            q = ch[key]
            q["x"] = _dot((q["ab"] * msk_ref[2 + lvl]).astype(BF16), q["t"].astype(BF16))
        for key in chains:
            q = ch[key]
            q["t"] = q["t"] + _dot(q["t"].astype(BF16), q["x"].astype(BF16))
    for key in chains:
        q = ch[key]
        rows, lanes = q["rows"], q["lanes"]
        au = _dot(q["t"].astype(BF16), jnp.concatenate([q["a"], q["akv"].astype(BF16)], axis=1))
        q["au"] = au.astype(BF16)
        q["mc"] = lax.dot_general(q["au"], stack(bh_ref[0, 0, rows, lanes]), tn_dims,
                                  preferred_element_type=F32)
        q["mt"] = q["mc"][0:LANES].astype(BF16)
        q["ct"] = q["mc"][LANES:2 * LANES] + lax.dot_general(
            q["v"], stack(kh_ref[0, 0, rows, lanes]), tn_dims, preferred_element_type=F32)
        if with_y:
            rbau = _dot(q["rb"], q["au"])
            q["rt"] = (q["r"].astype(F32) + rbau[:, 0:LANES]).astype(BF16)
            q["y0"] = rbau[:, LANES:2 * LANES] + q["rkv"]
    g = [g_sc[pp] for pp in range(n_pairs)]
    for key in chains:
        pp, c = key
        q = ch[key]
        g_b = g[pp].astype(BF16)
        if with_y:
            y_st = lax.dot_general(q["rt"], g_b, nt_dims, preferred_element_type=F32) + q["y0"]
            y_ref[0, q["rows"], q["lanes"]] = y_st[0:CHUNK] + y_st[CHUNK:n2]
        g[pp] = g[pp] * wl_ref[0, 0, c, :, q["lanes"]] + _dot(g_b, q["mt"]) + q["ct"]
    for pp in range(n_pairs):
        g_sc[pp] = g[pp]

    @pl.when(tb == n_tb - 1)
    def _():
        hT_ref[0] = g_sc[...]


def _scan(prep, d, h0, rev, with_y):
    v, at, rt, bt, kt, bh, kh, wl = prep
    b, n, cw = v.shape
    tb = min(SCAN_BLOCK, n)
    n_tb = n // tb
    n_pairs = cw // LANES
    sp = SCAN_PAIRS
    wide = sp * LANES
    msk = _scan_masks(rev)
    tmap = (lambda t: n_tb - 1 - t) if rev else (lambda t: t)
    shared = pl.BlockSpec((1, tb, wide), lambda bi, p, t: (bi, tmap(t), p))
    perdir = pl.BlockSpec((1, 1, tb, wide), lambda bi, p, t: (d, bi, tmap(t), p))
    decay = pl.BlockSpec((1, 1, tb // CHUNK, 1, wide), lambda bi, p, t: (d, bi, tmap(t), 0, p))
    state = pl.BlockSpec((1, sp, LANES, LANES), lambda bi, p, t: (bi, p, 0, 0))
    out_specs = [state]
    out_shape = [jax.ShapeDtypeStruct((b, n_pairs, LANES, LANES), F32)]
    if with_y:
        out_specs = [shared] + out_specs
        out_shape = [jax.ShapeDtypeStruct((b, n, cw), F32)] + out_shape
    res = pl.pallas_call(
        functools.partial(_scan_kernel, rev=rev, with_y=with_y),
        grid=(b, n_pairs // sp, n_tb),
        in_specs=[shared, perdir, perdir, perdir, perdir, perdir, perdir, decay, state,
                  pl.BlockSpec(msk.shape, lambda bi, p, t: (0, 0, 0))],
        out_specs=out_specs,
        out_shape=out_shape,
        scratch_shapes=[pltpu.VMEM((sp, LANES, LANES), F32)],
        compiler_params=_cparams("parallel", "parallel", "arbitrary"),
        name="scan_rev" if rev else "scan_fwd",
    )(v, at, rt, bt, kt, bh, kh, wl, h0, jnp.asarray(msk))
    return (res[0], res[1]) if with_y else (None, res[0])


def _out1_kernel(yf_ref, yb_ref, bonus_ref, g_ref, dp_ref, x_ref, mod_ref, lw_ref, lb_ref, seg_ref,
                 w_ref, o_ref):
    seg = seg_ref[...]
    cw = yf_ref.shape[2]
    acc = _dot(dp_ref[0].astype(BF16), w_ref[cw:cw + D_WIDTH, :])
    for pb in range(cw // LANES):
        sl = slice(pb * LANES, (pb + 1) * LANES)
        y = yf_ref[0, :, sl] + yb_ref[0, :, sl]
        mean = _dot32(y, seg) * (1.0 / HEAD_DIM)
        dev = y - mean
        var = _dot32(dev * dev, seg) * (1.0 / HEAD_DIM)
        yn = dev * lax.rsqrt(var + LNX_EPS) * lw_ref[:, sl] + lb_ref[:, sl]
        z = (yn + bonus_ref[0, :, sl]) * g_ref[0, :, sl]
        acc = acc + _dot(z.astype(BF16), w_ref[sl, :])
    o_ref[0] = x_ref[0] + mod_ref[0, 2:3, :] * acc


def _out1(yf, yb, bonus, g, dp, x, mod, lnx_w, lnx_b, seg, w):
    b, n, d = x.shape
    tm = ROW_TILE
    cw = yf.shape[2]
    row = lambda bi, i: (bi, i, 0)
    const = lambda bi, i: (0, 0)
    return pl.pallas_call(
        _out1_kernel,
        grid=(b, n // tm),
        in_specs=[pl.BlockSpec((1, tm, cw), row)] * 4
        + [pl.BlockSpec((1, tm, D_WIDTH), row),
           pl.BlockSpec((1, tm, d), row),
           pl.BlockSpec((1, 6, d), lambda bi, i: (bi, 0, 0)),
           pl.BlockSpec((1, cw), const), pl.BlockSpec((1, cw), const),
           pl.BlockSpec((LANES, LANES), const),
           pl.BlockSpec(w.shape, const)],
        out_specs=pl.BlockSpec((1, tm, d), row),
        out_shape=jax.ShapeDtypeStruct((b, n, d), F32),
        compiler_params=_cparams("parallel", "parallel"),
        name="out1",
    )(yf, yb, bonus, g, dp, x, mod, lnx_w, lnx_b, seg, w)


def _rope_tables(n):
    rows = n // GRID_W
    row = jnp.repeat(jnp.arange(rows, dtype=F32), GRID_W)
    col = jnp.tile(jnp.arange(GRID_W, dtype=F32), rows)
    n_freq = HEAD_DIM // 4
    inv = ROPE_THETA ** (-jnp.arange(n_freq, dtype=F32) / n_freq)
    ang = jnp.concatenate([row[:, None] * inv[None, :], col[:, None] * inv[None, :]], axis=-1)
    cos, sin = jnp.cos(ang), jnp.sin(ang)
    cos_t = jnp.tile(cos, (1, LANES // cos.shape[1]))
    sin_t = jnp.tile(jnp.concatenate([-sin, sin], axis=-1), (1, LANES // HEAD_DIM))
    return cos_t, sin_t


def _kv_dup_columns():
    cols = []
    for section in range(4):
        for head in range(2):
            base = AB_Q_COLS + section * 2 * HEAD_DIM + head * HEAD_DIM
            cols += list(range(base, base + HEAD_DIM)) * 2
    return np.concatenate([np.arange(AB_Q_COLS), np.asarray(cols)])


def kernel(x, c, ctx, c_ctx, norm_gain, ada_w, ada_b, ffn_w_in, ffn_w_out, final_gain, ab_w_in, ab_q_gain, ab_k_gain, ab_sink, ab_w_out, cd_w_in, cd_mu, cd_w0, cd_w2, cd_a0, cd_a2, cd_g2, cd_k_k, cd_k_a, cd_r_k, cd_lnx_w, cd_lnx_b, cd_pool_w, cd_pool_scale, cd_w_out):
    b, n, d = x.shape
    nc = ctx.shape[1]
    pad = (-(b + 1)) % 8
    cs = jnp.concatenate([c, c_ctx[None, :], jnp.zeros((pad, d), F32)], axis=0)
    mods = _mods(cs, ada_w, ada_b)
    seg = jnp.asarray(np.kron(np.eye(2), np.ones((HEAD_DIM, HEAD_DIM))).astype(np.float32))
    fgain = final_gain.reshape(1, d)

    def layer_mods(i):
        ml = mods[i, :b].reshape(b, 6, d)
        mc = jnp.broadcast_to(mods[i, b].reshape(1, 6, d), (b, 6, d))
        return ml, mc

    ml, mc = layer_mods(0)
    w0 = ab_w_in[0][:, _kv_dup_columns()].astype(BF16)
    gain = norm_gain[0, 0].reshape(1, d)
    qg = jnp.tile(ab_q_gain[0], 2).reshape(1, LANES)
    kg = jnp.tile(ab_k_gain[0], 2).reshape(1, LANES)
    cos_l, sin_l = _rope_tables(n)
    cos_c, sin_c = jnp.ones((nc, LANES), F32), jnp.zeros((nc, LANES), F32)
    q_l, kv_l = _proj0(x, ml, gain, w0, cos_l, sin_l, qg, kg, seg)
    q_c, kv_c = _proj0(ctx, mc, gain, w0, cos_c, sin_c, qg, kg, seg)
    sink = ab_sink[0]
    w_out0 = ab_w_out[0].astype(BF16)
    gain2 = norm_gain[0, 1].reshape(1, d)
    w1 = ffn_w_in[0].astype(BF16)
    w2 = ffn_w_out[0].astype(BF16)
    xl = _out0(_attn_a(q_l, kv_c, sink, kv_l), _attn_b(q_l, kv_c, kv_l), x, ml, w_out0)
    xc = _out0(_attn_a(q_c, kv_c, sink), _attn_b(q_c, kv_c), ctx, mc, w_out0)
    xl = _ffn(xl, ml, gain2, w1, w2, fgain, False)
    xc = _ffn(xc, mc, gain2, w1, w2, fgain, False)

    ml, mc = layer_mods(1)
    gain = norm_gain[1, 0].reshape(1, d)
    w_in1 = cd_w_in[0].astype(BF16)
    pc_l = _proj1(xl, ml, gain, w_in1)
    pc_c = _proj1(xc, mc, gain, w_in1[:, :C_IN])
    zeros = jnp.zeros((DECAY_LORA_PAD, C_WIDTH), F32)
    w2x = jnp.stack([jnp.concatenate([cd_w2[0, 0], zeros]), jnp.concatenate([zeros, cd_w2[0, 1]])])
    a2x = jnp.stack([jnp.concatenate([cd_a2[0, 0], zeros]), jnp.concatenate([zeros, cd_a2[0, 1]])])
    wts = [cd_mu[0].reshape(1, C_IN), cd_w0[0].reshape(2, 1, C_WIDTH), w2x,
           cd_a0[0].reshape(2, 1, C_WIDTH), a2x, cd_g2[0],
           cd_k_k[0].reshape(1, C_WIDTH), cd_k_a[0].reshape(1, C_WIDTH),
           cd_r_k[0].reshape(1, C_WIDTH), seg]
    pool_w = jax.scipy.linalg.block_diag(*[cd_pool_w[0, g] for g in range(4)]).astype(BF16)
    pool_wts = [pool_w, cd_pool_scale[0].reshape(1, D_WIDTH)]
    prep_c = _prep(pc_c, wts, False)
    *prep_l, g_l, bonus_l, dp_l = _prep(pc_l, wts + pool_wts, True)
    h_zero = jnp.zeros((b, C_WIDTH // LANES, LANES, LANES), F32)
    _, h_f = _scan(prep_c, 0, h_zero, False, False)
    _, h_b = _scan(prep_c, 1, h_zero, True, False)
    y_f, _ = _scan(prep_l, 0, h_f, False, True)
    y_b, _ = _scan(prep_l, 1, h_b, True, True)
    xl = _out1(y_f, y_b, bonus_l, g_l, dp_l, xl, ml, cd_lnx_w[0].reshape(1, C_WIDTH),
               cd_lnx_b[0].reshape(1, C_WIDTH), seg, cd_w_out[0].astype(BF16))
    return _ffn(xl, ml, norm_gain[1, 1].reshape(1, d), ffn_w_in[1].astype(BF16),
                ffn_w_out[1].astype(BF16), fgain, True)
```

```python
import functools

import numpy as np
import jax
import jax.numpy as jnp
from jax import lax
from jax.experimental import pallas as pl
from jax.experimental.pallas import tpu as pltpu

F32 = jnp.float32
BF16 = jnp.bfloat16
HIGHEST = lax.Precision.HIGHEST
LOG2_E = 1.4426950408889634

D_MODEL = 1024
GRID_W = 64
HEAD_DIM = 64
ROPE_THETA = 10000.0
NORM_EPS = 1e-6
WINDOW = 128
AB_Q_COLS = 1024
C_WIDTH = 768
C_IN = 2688
D_WIDTH = 256
CD_IN = C_IN + D_WIDTH
LNX_EPS = 64e-5
FFN_HIDDEN = 2816
POOL_WINDOWS = (2, 4, 8, 16)
DECAY_LORA_PAD = 64

LANES = 128
ROW_TILE = 256
HALO = 8
CHUNK = 64
SCAN_BLOCK = 512
SCAN_PAIRS = 3
VMEM_LIMIT = 56 * 1024 * 1024


def _cparams(*sem):
    return pltpu.CompilerParams(dimension_semantics=sem, vmem_limit_bytes=VMEM_LIMIT)


def _dot(a, b):
    return jnp.dot(a, b, preferred_element_type=F32)


def _dot32(a, b):
    return jnp.dot(a, b, preferred_element_type=F32, precision=HIGHEST)


def _split(a, terms):
    pieces = []
    for _ in range(terms - 1):
        hi = a.astype(BF16)
        pieces.append(hi)
        a = a - hi.astype(F32)
    pieces.append(a.astype(BF16))
    return pieces


def _seg_sum(z, seg):
    hi, lo = _split(z, 2)
    return _dot(hi, seg) + _dot(lo, seg)


def _sigmoid(x):
    return 1.0 / (1.0 + jnp.exp(-x))


def _norm_mod(x, gain, shift, scale):
    ms = jnp.mean(x * x, axis=-1, keepdims=True)
    return (x * lax.rsqrt(ms + NORM_EPS) * gain) * (1.0 + scale) + shift


def _mods_kernel(c_ref, w_ref, b_ref, o_ref):
    c = c_ref[...]
    o_ref[0] = _dot32(c * _sigmoid(c), w_ref[0]) + b_ref[0]


def _mods(cs, ada_w, ada_b):
    depth, d, n6 = ada_w.shape
    tn = 768
    rows = cs.shape[0]
    return pl.pallas_call(
        _mods_kernel,
        grid=(depth, n6 // tn),
        in_specs=[pl.BlockSpec((rows, d), lambda l, j: (0, 0)),
                  pl.BlockSpec((1, d, tn), lambda l, j: (l, 0, j)),
                  pl.BlockSpec((1, 1, tn), lambda l, j: (l, 0, j))],
        out_specs=pl.BlockSpec((1, rows, tn), lambda l, j: (l, 0, j)),
        out_shape=jax.ShapeDtypeStruct((depth, rows, n6), F32),
        compiler_params=_cparams("arbitrary", "arbitrary"),
        name="mods",
    )(cs, ada_w, ada_b.reshape(depth, 1, n6))


def _proj0_kernel(x_ref, mod_ref, gain_ref, w_ref, cos_ref, sin_ref, qg_ref, kg_ref, seg_ref,
                  q_ref, kv_ref):
    h = _norm_mod(x_ref[0], gain_ref[...], mod_ref[0, 0:1, :], mod_ref[0, 1:2, :])
    p = _dot(h.astype(BF16), w_ref[...])
    tm = p.shape[0]
    cos = cos_ref[...]
    sin = sin_ref[...]
    lane = lax.broadcasted_iota(jnp.int32, (tm, LANES), 1)
    first_half = (lane & (HEAD_DIM - 1)) < HEAD_DIM // 2
    seg = seg_ref[...]

    def rope(z):
        partner = jnp.where(first_half, pltpu.roll(z, LANES - HEAD_DIM // 2, 1),
                            pltpu.roll(z, HEAD_DIM // 2, 1))
        return z * cos + partner * sin

    def head_norm(z, g):
        ms = _seg_sum(z * z, seg) * (1.0 / HEAD_DIM)
        return z * lax.rsqrt(ms + NORM_EPS) * g

    scale = HEAD_DIM ** -0.5 * LOG2_E
    for blk in range(8):
        z = p[:, blk * LANES:(blk + 1) * LANES]
        if blk >= 4:
            z = head_norm(z, qg_ref[...])
        q_ref[0, :, blk * LANES:(blk + 1) * LANES] = (rope(z) * scale).astype(BF16)
    for blk in range(8):
        z = p[:, AB_Q_COLS + blk * LANES:AB_Q_COLS + (blk + 1) * LANES]
        if blk in (4, 5):
            z = head_norm(z, kg_ref[...])
        if blk in (0, 1, 4, 5):
            z = rope(z)
        kv_ref[0, :, blk * LANES:(blk + 1) * LANES] = z.astype(BF16)


def _proj0(x, mod, gain, w, cos, sin, qg, kg, seg):
    b, n, d = x.shape
    tm = ROW_TILE
    nw = w.shape[1]
    const = lambda bi, i: (0, 0)
    return pl.pallas_call(
        _proj0_kernel,
        grid=(b, n // tm),
        in_specs=[pl.BlockSpec((1, tm, d), lambda bi, i: (bi, i, 0)),
                  pl.BlockSpec((1, 6, d), lambda bi, i: (bi, 0, 0)),
                  pl.BlockSpec((1, d), const),
                  pl.BlockSpec((d, nw), const),
                  pl.BlockSpec((tm, LANES), lambda bi, i: (i, 0)),
                  pl.BlockSpec((tm, LANES), lambda bi, i: (i, 0)),
                  pl.BlockSpec((1, LANES), const),
                  pl.BlockSpec((1, LANES), const),
                  pl.BlockSpec((LANES, LANES), const)],
        out_specs=[pl.BlockSpec((1, tm, 1024), lambda bi, i: (bi, i, 0)),
                   pl.BlockSpec((1, tm, 1024), lambda bi, i: (bi, i, 0))],
        out_shape=[jax.ShapeDtypeStruct((b, n, 1024), BF16),
                   jax.ShapeDtypeStruct((b, n, 1024), BF16)],
        compiler_params=_cparams("parallel", "parallel"),
        name="proj0",
    )(x, mod, gain, w, cos, sin, qg, kg, seg)


def _masked_q(q_ref, h, low):
    q2 = q_ref[0, :, (h // 2) * LANES:(h // 2 + 1) * LANES]
    keep = low if h % 2 == 0 else jnp.logical_not(low)
    return jnp.where(keep, q2, jnp.zeros_like(q2))


def _attn_b_kernel(q_ref, kc_ref, vc_ref, *rest, n_kt, has_latent):
    if has_latent:
        k_ref, v_ref, o_ref, m_sc, acc_sc = rest
    else:
        o_ref, m_sc, acc_sc = rest
    kt = pl.program_id(3)
    tq = q_ref.shape[1]
    low = lax.broadcasted_iota(jnp.int32, (tq, LANES), 1) < HEAD_DIM

    def update(k, v):
        tk = k.shape[0]
        v1 = jnp.where(lax.broadcasted_iota(jnp.int32, (tk, LANES), 1) < HEAD_DIM, v, jnp.ones_like(v))
        ss = [lax.dot_general(_masked_q(q_ref, h, low), k, (((1,), (1,)), ((), ())),
                              preferred_element_type=F32) for h in range(4)]
        for h in range(4):
            m_prev = m_sc[h]
            m_new = jnp.maximum(m_prev, jnp.max(ss[h], axis=1, keepdims=True))
            p = jnp.exp2(ss[h] - jnp.concatenate([m_new] * (tk // LANES), axis=1)).astype(BF16)
            acc_sc[h] = jnp.exp2(m_prev - m_new) * acc_sc[h] + _dot(p, v1)
            m_sc[h] = m_new

    @pl.when(kt == 0)
    def _():
        m_sc[...] = jnp.full(m_sc.shape, -jnp.inf, F32)
        acc_sc[...] = jnp.zeros(acc_sc.shape, F32)
        update(kc_ref[0], vc_ref[0])

    if has_latent:
        update(k_ref[0], v_ref[0])

    @pl.when(kt == n_kt - 1)
    def _():
        for pr in range(2):
            a0 = acc_sc[2 * pr]
            a1 = acc_sc[2 * pr + 1]
            o0 = a0 / pltpu.roll(a0, HEAD_DIM, 1)
            o1 = pltpu.roll(a1, HEAD_DIM, 1) / a1
            o_ref[0, :, pr * LANES:(pr + 1) * LANES] = jnp.where(low, o0, o1).astype(BF16)


def _attn_b(q, kv_c, kv=None, tk=1024):
    b, n, _ = q.shape
    nc = kv_c.shape[1]
    tq = ROW_TILE
    has_latent = kv is not None
    n_kt = kv.shape[1] // tk if has_latent else 1
    in_specs = [pl.BlockSpec((1, tq, 2 * LANES), lambda bi, g, i, j: (bi, i, 2 + g)),
                pl.BlockSpec((1, nc, LANES), lambda bi, g, i, j: (bi, 0, 4 + g)),
                pl.BlockSpec((1, nc, LANES), lambda bi, g, i, j: (bi, 0, 6 + g))]
    args = [q, kv_c, kv_c]
    if has_latent:
        in_specs += [pl.BlockSpec((1, tk, LANES), lambda bi, g, i, j: (bi, j, 4 + g)),
                     pl.BlockSpec((1, tk, LANES), lambda bi, g, i, j: (bi, j, 6 + g))]
        args += [kv, kv]
    return pl.pallas_call(
        functools.partial(_attn_b_kernel, n_kt=n_kt, has_latent=has_latent),
        grid=(b, 2, n // tq, n_kt),
        in_specs=in_specs,
        out_specs=pl.BlockSpec((1, tq, 2 * LANES), lambda bi, g, i, j: (bi, i, g)),
        out_shape=jax.ShapeDtypeStruct((b, n, 512), BF16),
        scratch_shapes=[pltpu.VMEM((4, tq, LANES), F32), pltpu.VMEM((4, tq, LANES), F32)],
        compiler_params=_cparams("parallel", "parallel", "parallel", "arbitrary"),
        name="attn_b",
    )(*args)


def _attn_a_kernel(sink_ref, q_ref, kc_ref, vc_ref, *rest, has_local):
    if has_local:
        kp_ref, kcur_ref, kn_ref, vp_ref, vcur_ref, vn_ref, o_ref, kbuf, vbuf = rest
    else:
        o_ref, kbuf, vbuf = rest
    g = pl.program_id(1)
    i = pl.program_id(2)
    n_i = pl.num_programs(2)
    tq = q_ref.shape[1]
    nc = kc_ref.shape[1]
    kbuf[0:nc, :] = kc_ref[0]
    vbuf[0:nc, :] = vc_ref[0]
    nk = nc
    if has_local:
        kbuf[nc:nc + WINDOW, :] = kp_ref[0]
        kbuf[nc + WINDOW:nc + WINDOW + tq, :] = kcur_ref[0]
        kbuf[nc + WINDOW + tq:nc + 2 * WINDOW + tq, :] = kn_ref[0]
        vbuf[nc:nc + WINDOW, :] = vp_ref[0]
        vbuf[nc + WINDOW:nc + WINDOW + tq, :] = vcur_ref[0]
        vbuf[nc + WINDOW + tq:nc + 2 * WINDOW + tq, :] = vn_ref[0]
        nk = nc + 2 * WINDOW + tq
        row = lax.broadcasted_iota(jnp.int32, (tq, nk), 0)
        col = lax.broadcasted_iota(jnp.int32, (tq, nk), 1) - nc
        lo_c = jnp.where(i == 0, WINDOW, 0)
        hi_c = jnp.where(i == n_i - 1, WINDOW + tq, 2 * WINDOW + tq)
        valid = (col < 0) | ((col >= row) & (col <= row + 2 * WINDOW) & (col >= lo_c) & (col < hi_c))
    k = kbuf[...]
    v = vbuf[...]
    v1 = jnp.where(lax.broadcasted_iota(jnp.int32, (nk, LANES), 1) < HEAD_DIM, v, jnp.ones_like(v))
    low = lax.broadcasted_iota(jnp.int32, (tq, LANES), 1) < HEAD_DIM
    ss = [lax.dot_general(_masked_q(q_ref, h, low), k, (((1,), (1,)), ((), ())),
                          preferred_element_type=F32) for h in range(4)]
    accs = []
    for h in range(4):
        s = jnp.where(valid, ss[h], -jnp.inf) if has_local else ss[h]
        sink = sink_ref[g * 4 + h] * LOG2_E
        m = jnp.maximum(jnp.max(s, axis=1, keepdims=True), sink)
        p = jnp.exp2(s - m)
        sink_p = jnp.where(low, 0.0, jnp.exp2(sink - m))
        accs.append(_dot(p.astype(BF16), v1) + sink_p)
    for pr in range(2):
        a0, a1 = accs[2 * pr], accs[2 * pr + 1]
        o0 = a0 / pltpu.roll(a0, HEAD_DIM, 1)
        o1 = pltpu.roll(a1, HEAD_DIM, 1) / a1
        o_ref[0, :, pr * LANES:(pr + 1) * LANES] = jnp.where(low, o0, o1).astype(BF16)


def _attn_a(q, kv_c, sink, kv=None):
    b, n, _ = q.shape
    nc = kv_c.shape[1]
    tq = ROW_TILE
    has_local = kv is not None
    in_specs = [pl.BlockSpec(memory_space=pltpu.SMEM),
                pl.BlockSpec((1, tq, 2 * LANES), lambda bi, g, i: (bi, i, g)),
                pl.BlockSpec((1, nc, LANES), lambda bi, g, i: (bi, 0, g)),
                pl.BlockSpec((1, nc, LANES), lambda bi, g, i: (bi, 0, 2 + g))]
    args = [sink, q, kv_c, kv_c]
    nk = nc
    if has_local:
        per = tq // WINDOW
        last = n // WINDOW - 1
        prev_i = lambda i: jnp.maximum(i * per - 1, 0)
        next_i = lambda i: jnp.minimum((i + 1) * per, last)
        for off in (0, 2):
            in_specs += [pl.BlockSpec((1, WINDOW, LANES), lambda bi, g, i, off=off: (bi, prev_i(i), off + g)),
                         pl.BlockSpec((1, tq, LANES), lambda bi, g, i, off=off: (bi, i, off + g)),
                         pl.BlockSpec((1, WINDOW, LANES), lambda bi, g, i, off=off: (bi, next_i(i), off + g))]
            args += [kv, kv, kv]
        nk = nc + 2 * WINDOW + tq
    return pl.pallas_call(
        functools.partial(_attn_a_kernel, has_local=has_local),
        grid=(b, 2, n // tq),
        in_specs=in_specs,
        out_specs=pl.BlockSpec((1, tq, 2 * LANES), lambda bi, g, i: (bi, i, g)),
        out_shape=jax.ShapeDtypeStruct((b, n, 512), BF16),
        scratch_shapes=[pltpu.VMEM((nk, LANES), BF16), pltpu.VMEM((nk, LANES), BF16)],
        compiler_params=_cparams("parallel", "parallel", "parallel"),
        name="attn_a",
    )(*args)


def _out0_kernel(ya_ref, yb_ref, x_ref, mod_ref, w_ref, o_ref):
    half = ya_ref.shape[2]
    y = _dot(ya_ref[0], w_ref[0:half, :]) + _dot(yb_ref[0], w_ref[half:2 * half, :])
    o_ref[0] = x_ref[0] + mod_ref[0, 2:3, :] * y


def _out0(ya, yb, x, mod, w):
    b, n, d = x.shape
    tm = ROW_TILE
    row = lambda bi, i: (bi, i, 0)
    return pl.pallas_call(
        _out0_kernel,
        grid=(b, n // tm),
        in_specs=[pl.BlockSpec((1, tm, ya.shape[2]), row),
                  pl.BlockSpec((1, tm, yb.shape[2]), row),
                  pl.BlockSpec((1, tm, d), row),
                  pl.BlockSpec((1, 6, d), lambda bi, i: (bi, 0, 0)),
                  pl.BlockSpec(w.shape, lambda bi, i: (0, 0))],
        out_specs=pl.BlockSpec((1, tm, d), row),
        out_shape=jax.ShapeDtypeStruct((b, n, d), F32),
        compiler_params=_cparams("parallel", "parallel"),
        name="out0",
    )(ya, yb, x, mod, w)


def _ffn_kernel(x_ref, mod_ref, gain_ref, w1_ref, w2_ref, fg_ref, o_ref, *, final):
    x = x_ref[0]
    h = _norm_mod(x, gain_ref[...], mod_ref[0, 3:4, :], mod_ref[0, 4:5, :]).astype(BF16)
    hid = w2_ref.shape[0]
    ch = hid // 2
    acc = None
    for j in range(2):
        gate = _dot(h, w1_ref[:, j * ch:(j + 1) * ch])
        up = _dot(h, w1_ref[:, hid + j * ch:hid + (j + 1) * ch])
        act = (gate * _sigmoid(gate) * up).astype(BF16)
        part = _dot(act, w2_ref[j * ch:(j + 1) * ch, :])
        acc = part if acc is None else acc + part
    y = x + mod_ref[0, 5:6, :] * acc
    if final:
        ms = jnp.mean(y * y, axis=-1, keepdims=True)
        y = y * lax.rsqrt(ms + NORM_EPS) * fg_ref[...]
    o_ref[0] = y


def _ffn(x, mod, gain, w1, w2, fgain, final):
    b, n, d = x.shape
    tm = ROW_TILE
    row = lambda bi, i: (bi, i, 0)
    const = lambda bi, i: (0, 0)
    return pl.pallas_call(
        functools.partial(_ffn_kernel, final=final),
        grid=(b, n // tm),
        in_specs=[pl.BlockSpec((1, tm, d), row),
                  pl.BlockSpec((1, 6, d), lambda bi, i: (bi, 0, 0)),
                  pl.BlockSpec((1, d), const),
                  pl.BlockSpec(w1.shape, const, pipeline_mode=pl.Buffered(1)),
                  pl.BlockSpec(w2.shape, const, pipeline_mode=pl.Buffered(1)),
                  pl.BlockSpec((1, d), const)],
        out_specs=pl.BlockSpec((1, tm, d), row),
        out_shape=jax.ShapeDtypeStruct((b, n, d), F32),
        compiler_params=_cparams("parallel", "parallel"),
        name="ffn",
    )(x, mod, gain, w1, w2, fgain)


def _proj1_kernel(x_ref, mod_ref, gain_ref, w_ref, o_ref):
    h = _norm_mod(x_ref[0], gain_ref[...], mod_ref[0, 0:1, :], mod_ref[0, 1:2, :])
    o_ref[0] = _dot(h.astype(BF16), w_ref[...])


def _proj1(x, mod, gain, w):
    b, n, d = x.shape
    tm = ROW_TILE
    nw = w.shape[1]
    return pl.pallas_call(
        _proj1_kernel,
        grid=(b, n // tm),
        in_specs=[pl.BlockSpec((1, tm, d), lambda bi, i: (bi, i, 0)),
                  pl.BlockSpec((1, 6, d), lambda bi, i: (bi, 0, 0)),
                  pl.BlockSpec((1, d), lambda bi, i: (0, 0)),
                  pl.BlockSpec((d, nw), lambda bi, i: (0, 0))],
        out_specs=pl.BlockSpec((1, tm, nw), lambda bi, i: (bi, i, 0)),
        out_shape=jax.ShapeDtypeStruct((b, n, nw), F32),
        compiler_params=_cparams("parallel", "parallel"),
        name="proj1",
    )(x, mod, gain, w)


def _prep_kernel(pc_ref, prev_ref, next_ref, mu_ref, w0_ref, w2_ref, a0_ref, a2_ref, g2_ref,
                 kk_ref, ka_ref, rk_ref, seg_ref, tri_ref, *rest, latent):
    if latent:
        pw_ref, ps_ref, v_o, at_o, rt_o, bt_o, kt_o, wl_o, g_o, bonus_o, d_o, ext, tmp = rest
    else:
        v_o, at_o, rt_o, bt_o, kt_o, wl_o, ext, tmp = rest
    i = pl.program_id(1)
    nt = pl.num_programs(1)
    tm = pc_ref.shape[1]
    cw = C_WIDTH
    ext[HALO:HALO + tm, :] = pc_ref[0]
    ext[0:HALO, :] = jnp.where(i > 0, prev_ref[0], 0.0)
    ext[HALO + tm:2 * HALO + tm, :] = jnp.where(i < nt - 1, next_ref[0], 0.0)

    def mixed(lo, hi):
        nb = ext[HALO - 1:HALO - 1 + tm, lo:hi] + ext[HALO + 1:HALO + 1 + tm, lo:hi]
        return ext[HALO:HALO + tm, lo:hi] * mu_ref[0:1, lo:hi] + nb * mu_ref[1:2, lo:hi]

    lora = mixed(3 * cw, C_IN)
    tw = jnp.tanh(lora[:, 0:LANES])
    xa = lora[:, LANES:2 * LANES]
    if latent:
        g_o[0] = _dot(_sigmoid(lora[:, 2 * LANES:3 * LANES]).astype(BF16), g2_ref[...])
    tw_hi, tw_lo = _split(tw, 2)
    xa_b = xa.astype(BF16)
    for d in range(2):
        z = w0_ref[d] + _dot(tw_hi, w2_ref[0, d]) + _dot(tw_lo, w2_ref[0, d]) + _dot(tw_hi, w2_ref[1, d])
        w_log = -(jnp.maximum(-z, 0.0) + jnp.log(1.0 + jnp.exp(-jnp.abs(z)))) - 0.5
        tmp[d] = -jnp.exp(w_log)
        tmp[2 + d] = _sigmoid(a0_ref[d] + _dot(xa_b, a2_ref[d]))
    seg = seg_ref[...]
    n_chunks = tm // CHUNK
    for pb in range(cw // LANES):
        sl = slice(pb * LANES, (pb + 1) * LANES)
        r = mixed(pb * LANES, (pb + 1) * LANES)
        k = mixed(cw + pb * LANES, cw + (pb + 1) * LANES)
        v = mixed(2 * cw + pb * LANES, 2 * cw + (pb + 1) * LANES)
        kk = k * kk_ref[:, sl]
        kk = kk / jnp.maximum(jnp.sqrt(_seg_sum(kk * kk, seg)), 1e-12)
        ksum = jnp.zeros_like(k)
        for d in range(2):
            lw = tmp[d, :, sl]
            a = tmp[2 + d, :, sl]
            k_d = k * (1.0 + (a - 1.0) * ka_ref[:, sl])
            bb = kk * a
            ksum = ksum + k_d
            cs2 = _dot(tri_ref[d], jnp.concatenate(_split(lw, 2), axis=1))
            cs = cs2[:, 0:LANES] + cs2[:, LANES:2 * LANES]
            for j in range(n_chunks):
                last = j * CHUNK if d == 1 else (j + 1) * CHUNK - 1
                wl_o[d, 0, j, :, sl] = jnp.exp(cs[last:last + 1, :])
            e_up = jnp.exp(-cs)
            at_o[d, 0, :, sl] = (-kk * jnp.exp(cs - lw)).astype(BF16)
            rt_o[d, 0, :, sl] = (r * jnp.exp(cs)).astype(BF16)
            bt_o[d, 0, :, sl] = (bb * e_up).astype(BF16)
            kt_o[d, 0, :, sl] = (k_d * e_up).astype(BF16)
        v_o[0, :, sl] = v.astype(BF16)
        if latent:
            bonus_o[0, :, sl] = _seg_sum(r * rk_ref[:, sl] * ksum, seg) * v
    if latent:
        n_tok = nt * tm
        pos = i * tm + lax.broadcasted_iota(jnp.int32, (tm, LANES), 0)
        group1 = lax.broadcasted_iota(jnp.int32, (tm, LANES), 1) >= D_WIDTH // 4
        pooled = []
        for half in range(2):
            lo, hi = C_IN + half * LANES, C_IN + (half + 1) * LANES
            w_small, w_big = POOL_WINDOWS[2 * half], POOL_WINDOWS[2 * half + 1]
            s_small = None
            s_big = None
            for off in range(-(w_big // 2), w_big - w_big // 2):
                piece = ext[HALO + off:HALO + off + tm, lo:hi]
                s_big = piece if s_big is None else s_big + piece
                if -(w_small // 2) <= off < w_small - w_small // 2:
                    s_small = piece if s_small is None else s_small + piece

            def count(w):
                lo_p = jnp.clip(pos - w // 2, 0, n_tok)
                hi_p = jnp.clip(pos + (w - w // 2), 0, n_tok)
                return (hi_p - lo_p).astype(F32)

            mean = jnp.where(group1, s_big / count(w_big), s_small / count(w_small))
            pooled.append((mean - ext[HALO:HALO + tm, lo:hi]).astype(BF16))
        pm = jnp.concatenate(pooled, axis=1)
        d_o[0] = _dot(pm, pw_ref[...]) * ps_ref[...]


def _chunk_cumsum_matrices(tm):
    t = np.arange(tm)[:, None]
    u = np.arange(tm)[None, :]
    same = (t // CHUNK) == (u // CHUNK)
    return np.stack([same & (u <= t), same & (u >= t)]).astype(np.float32)


def _prep(pc, wts, latent):
    b, n, cin = pc.shape
    tm = ROW_TILE
    cw = C_WIDTH
    per = tm // HALO
    last = n // HALO - 1
    n_chunks = tm // CHUNK
    row = lambda bi, i: (bi, i, 0)
    drow = lambda bi, i: (0, bi, i, 0)
    wts = list(wts)
    wts = wts[:10] + [jnp.asarray(_chunk_cumsum_matrices(tm)).astype(BF16)] + (wts[10:] if latent else [])

    def full(a):
        return pl.BlockSpec(a.shape, lambda bi, i, nd=a.ndim: (0,) * nd)

    in_specs = [pl.BlockSpec((1, tm, cin), row),
                pl.BlockSpec((1, HALO, cin), lambda bi, i: (bi, jnp.maximum(i * per - 1, 0), 0)),
                pl.BlockSpec((1, HALO, cin), lambda bi, i: (bi, jnp.minimum((i + 1) * per, last), 0))]
    in_specs += [full(a) for a in wts]
    out_specs = [pl.BlockSpec((1, tm, cw), row)] + [pl.BlockSpec((2, 1, tm, cw), drow)] * 4
    out_shape = [jax.ShapeDtypeStruct((b, n, cw), BF16)] + [jax.ShapeDtypeStruct((2, b, n, cw), BF16)] * 4
    out_specs.append(pl.BlockSpec((2, 1, n_chunks, 1, cw), lambda bi, i: (0, bi, i, 0, 0)))
    out_shape.append(jax.ShapeDtypeStruct((2, b, n // CHUNK, 1, cw), F32))
    if latent:
        out_specs += [pl.BlockSpec((1, tm, cw), row)] * 2 + [pl.BlockSpec((1, tm, D_WIDTH), row)]
        out_shape += [jax.ShapeDtypeStruct((b, n, cw), F32)] * 2 + [jax.ShapeDtypeStruct((b, n, D_WIDTH), F32)]
    return pl.pallas_call(
        functools.partial(_prep_kernel, latent=latent),
        grid=(b, n // tm),
        in_specs=in_specs,
        out_specs=out_specs,
        out_shape=out_shape,
        scratch_shapes=[pltpu.VMEM((tm + 2 * HALO, cin), F32), pltpu.VMEM((4, tm, cw), F32)],
        compiler_params=_cparams("parallel", "parallel"),
        name="prep",
    )(pc, pc, pc, *wts)


def _scan_masks(rev):
    n = 2 * CHUNK
    t = np.arange(n)[:, None]
    u = np.arange(n)[None, :]
    same = (t // CHUNK) == (u // CHUNK)
    before = (u > t) if rev else (u < t)
    masks = [same & before, same & (before | (u == t))]
    s = 1
    while s < CHUNK:
        blk = (t // (2 * s)) == (u // (2 * s))
        t_late = (t % (2 * s) < s) if rev else (t % (2 * s) >= s)
        u_early = (u % (2 * s) >= s) if rev else (u % (2 * s) < s)
        masks.append(blk & t_late & u_early)
        s *= 2
    return np.stack(masks).astype(np.float32)


def _scan_kernel(v_ref, at_ref, rt_ref, bt_ref, kt_ref, wl_ref, h0_ref, msk_ref, *rest,
                 rev, with_y):
    if with_y:
        y_ref, hT_ref, g_sc = rest
    else:
        hT_ref, g_sc = rest
    tb = pl.program_id(2)
    n_tb = pl.num_programs(2)
    n_chunks = v_ref.shape[1] // CHUNK
    n2 = 2 * CHUNK

    @pl.when(tb == 0)
    def _():
        g_sc[...] = h0_ref[0]

    low = lax.broadcasted_iota(jnp.int32, (CHUNK, LANES), 1) < HEAD_DIM
    rid = lax.broadcasted_iota(jnp.int32, (n2, n2), 0)
    cid = lax.broadcasted_iota(jnp.int32, (n2, n2), 1)
    eye = (rid == cid).astype(F32)
    n_levels = msk_ref.shape[0] - 2
    strict = msk_ref[0] > 0
    incl = msk_ref[1] > 0
    nt_dims = (((1,), (1,)), ((), ()))
    tn_dims = (((0,), (0,)), ((), ()))

    def stack(z):
        zero = jnp.zeros_like(z)
        return jnp.concatenate([jnp.where(low, z, zero), jnp.where(low, zero, z)], axis=0)

    order = list(range(n_chunks - 1, -1, -1) if rev else range(n_chunks))
    n_pairs = v_ref.shape[2] // LANES
    chains = [(pp, c) for c in order for pp in range(n_pairs)]
    ch = {}
    for key in chains:
        pp, c = key
        rows = slice(c * CHUNK, (c + 1) * CHUNK)
        lanes = slice(pp * LANES, (pp + 1) * LANES)
        q = ch[key] = {"rows": rows, "lanes": lanes}
        q["v"] = stack(v_ref[0, rows, lanes])
        q["a"] = stack(at_ref[0, 0, rows, lanes])
        q["b"] = stack(bt_ref[0, 0, rows, lanes])
        q["k"] = stack(kt_ref[0, 0, rows, lanes])
        rhs = jnp.concatenate([q["b"], q["k"]], axis=0)
        if with_y:
            q["r"] = stack(rt_ref[0, 0, rows, lanes])
            lhs = jnp.concatenate([q["a"], q["r"]], axis=0)
        else:
            lhs = q["a"]
        gram = lax.dot_general(lhs, rhs, nt_dims, preferred_element_type=F32)
        q["ab"] = jnp.where(strict, gram[0:n2, 0:n2], 0.0)
        q["akv"] = _dot(jnp.where(strict, gram[0:n2, n2:2 * n2], 0.0).astype(BF16), q["v"])
        if with_y:
            q["rb"] = jnp.where(incl, gram[n2:2 * n2, 0:n2], 0.0).astype(BF16)
            q["rkv"] = _dot(jnp.where(incl, gram[n2:2 * n2, n2:2 * n2], 0.0).astype(BF16), q["v"])
        q["t"] = eye + q["ab"] * msk_ref[2]
    for lvl in range(1, n_levels):
        for key in chains:
            q = ch[key]
            q["x"] = _dot((q["ab"] * msk_ref[2 + lvl]).astype(BF16), q["t"].astype(BF16))
        for key in chains:
            q = ch[key]
            q["t"] = q["t"] + _dot(q["t"].astype(BF16), q["x"].astype(BF16))
    for key in chains:
        q = ch[key]
        rows, lanes = q["rows"], q["lanes"]
        au = _dot(q["t"].astype(BF16), jnp.concatenate([q["a"], q["akv"].astype(BF16)], axis=1))
        q["au"] = au.astype(BF16)
        wl = wl_ref[0, 0, key[1], :, lanes]
        mc = lax.dot_general(q["au"], q["b"], tn_dims, preferred_element_type=F32)
        vk = lax.dot_general(q["v"], q["k"], tn_dims, preferred_element_type=F32)
        q["mt"] = (mc[0:LANES] * wl).astype(BF16)
        q["ct"] = (mc[LANES:2 * LANES] + vk) * wl
        if with_y:
            rbau = _dot(q["rb"], q["au"])
            q["rt"] = (q["r"].astype(F32) + rbau[:, 0:LANES]).astype(BF16)
            q["y0"] = rbau[:, LANES:2 * LANES] + q["rkv"]
    g = [g_sc[pp] for pp in range(n_pairs)]
    for key in chains:
        pp, c = key
        q = ch[key]
        g_b = g[pp].astype(BF16)
        if with_y:
            y_st = lax.dot_general(q["rt"], g_b, nt_dims, preferred_element_type=F32) + q["y0"]
            y_ref[0, q["rows"], q["lanes"]] = y_st[0:CHUNK] + y_st[CHUNK:n2]
        g[pp] = g[pp] * wl_ref[0, 0, c, :, q["lanes"]] + _dot(g_b, q["mt"]) + q["ct"]
    for pp in range(n_pairs):
        g_sc[pp] = g[pp]

    @pl.when(tb == n_tb - 1)
    def _():
        hT_ref[0] = g_sc[...]


def _scan(prep, d, h0, rev, with_y):
    v, at, rt, bt, kt, wl = prep
    b, n, cw = v.shape
    tb = min(SCAN_BLOCK, n)
    n_tb = n // tb
    n_pairs = cw // LANES
    sp = SCAN_PAIRS
    wide = sp * LANES
    msk = _scan_masks(rev)
    tmap = (lambda t: n_tb - 1 - t) if rev else (lambda t: t)
    shared = pl.BlockSpec((1, tb, wide), lambda bi, p, t: (bi, tmap(t), p))
    perdir = pl.BlockSpec((1, 1, tb, wide), lambda bi, p, t: (d, bi, tmap(t), p))
    decay = pl.BlockSpec((1, 1, tb // CHUNK, 1, wide), lambda bi, p, t: (d, bi, tmap(t), 0, p))
    state = pl.BlockSpec((1, sp, LANES, LANES), lambda bi, p, t: (bi, p, 0, 0))
    out_specs = [state]
    out_shape = [jax.ShapeDtypeStruct((b, n_pairs, LANES, LANES), F32)]
    if with_y:
        out_specs = [shared] + out_specs
        out_shape = [jax.ShapeDtypeStruct((b, n, cw), F32)] + out_shape
    res = pl.pallas_call(
        functools.partial(_scan_kernel, rev=rev, with_y=with_y),
        grid=(b, n_pairs // sp, n_tb),
        in_specs=[shared, perdir, perdir, perdir, perdir, decay, state,
                  pl.BlockSpec(msk.shape, lambda bi, p, t: (0, 0, 0))],
        out_specs=out_specs,
        out_shape=out_shape,
        scratch_shapes=[pltpu.VMEM((sp, LANES, LANES), F32)],
        compiler_params=_cparams("parallel", "parallel", "arbitrary"),
        name="scan_rev" if rev else "scan_fwd",
    )(v, at, rt, bt, kt, wl, h0, jnp.asarray(msk))
    return (res[0], res[1]) if with_y else (None, res[0])


def _out1_kernel(yf_ref, yb_ref, bonus_ref, g_ref, dp_ref, x_ref, mod_ref, lw_ref, lb_ref, seg_ref,
                 w_ref, o_ref):
    seg = seg_ref[...]
    cw = yf_ref.shape[2]
    acc = _dot(dp_ref[0].astype(BF16), w_ref[cw:cw + D_WIDTH, :])
    for pb in range(cw // LANES):
        sl = slice(pb * LANES, (pb + 1) * LANES)
        y = yf_ref[0, :, sl] + yb_ref[0, :, sl]
        mean = _seg_sum(y, seg) * (1.0 / HEAD_DIM)
        dev = y - mean
        var = _seg_sum(dev * dev, seg) * (1.0 / HEAD_DIM)
        yn = dev * lax.rsqrt(var + LNX_EPS) * lw_ref[:, sl] + lb_ref[:, sl]
        z = (yn + bonus_ref[0, :, sl]) * g_ref[0, :, sl]
        acc = acc + _dot(z.astype(BF16), w_ref[sl, :])
    o_ref[0] = x_ref[0] + mod_ref[0, 2:3, :] * acc


def _out1(yf, yb, bonus, g, dp, x, mod, lnx_w, lnx_b, seg, w):
    b, n, d = x.shape
    tm = ROW_TILE
    cw = yf.shape[2]
    row = lambda bi, i: (bi, i, 0)
    const = lambda bi, i: (0, 0)
    return pl.pallas_call(
        _out1_kernel,
        grid=(b, n // tm),
        in_specs=[pl.BlockSpec((1, tm, cw), row)] * 4
        + [pl.BlockSpec((1, tm, D_WIDTH), row),
           pl.BlockSpec((1, tm, d), row),
           pl.BlockSpec((1, 6, d), lambda bi, i: (bi, 0, 0)),
           pl.BlockSpec((1, cw), const), pl.BlockSpec((1, cw), const),
           pl.BlockSpec((LANES, LANES), const),
           pl.BlockSpec(w.shape, const)],
        out_specs=pl.BlockSpec((1, tm, d), row),
        out_shape=jax.ShapeDtypeStruct((b, n, d), F32),
        compiler_params=_cparams("parallel", "parallel"),
        name="out1",
    )(yf, yb, bonus, g, dp, x, mod, lnx_w, lnx_b, seg, w)


def _rope_tables(n):
    rows = n // GRID_W
    row = jnp.repeat(jnp.arange(rows, dtype=F32), GRID_W)
    col = jnp.tile(jnp.arange(GRID_W, dtype=F32), rows)
    n_freq = HEAD_DIM // 4
    inv = ROPE_THETA ** (-jnp.arange(n_freq, dtype=F32) / n_freq)
    ang = jnp.concatenate([row[:, None] * inv[None, :], col[:, None] * inv[None, :]], axis=-1)
    cos, sin = jnp.cos(ang), jnp.sin(ang)
    cos_t = jnp.tile(cos, (1, LANES // cos.shape[1]))
    sin_t = jnp.tile(jnp.concatenate([-sin, sin], axis=-1), (1, LANES // HEAD_DIM))
    return cos_t, sin_t


def _kv_dup_columns():
    cols = []
    for section in range(4):
        for head in range(2):
            base = AB_Q_COLS + section * 2 * HEAD_DIM + head * HEAD_DIM
            cols += list(range(base, base + HEAD_DIM)) * 2
    return np.concatenate([np.arange(AB_Q_COLS), np.asarray(cols)])


def kernel(x, c, ctx, c_ctx, norm_gain, ada_w, ada_b, ffn_w_in, ffn_w_out, final_gain, ab_w_in, ab_q_gain, ab_k_gain, ab_sink, ab_w_out, cd_w_in, cd_mu, cd_w0, cd_w2, cd_a0, cd_a2, cd_g2, cd_k_k, cd_k_a, cd_r_k, cd_lnx_w, cd_lnx_b, cd_pool_w, cd_pool_scale, cd_w_out):
    b, n, d = x.shape
    nc = ctx.shape[1]
    pad = (-(b + 1)) % 8
    cs = jnp.concatenate([c, c_ctx[None, :], jnp.zeros((pad, d), F32)], axis=0)
    mods = _mods(cs, ada_w, ada_b)
    seg = jnp.asarray(np.kron(np.eye(2), np.ones((HEAD_DIM, HEAD_DIM))).astype(np.float32)).astype(BF16)
    fgain = final_gain.reshape(1, d)

    def layer_mods(i):
        ml = mods[i, :b].reshape(b, 6, d)
        mc = jnp.broadcast_to(mods[i, b].reshape(1, 6, d), (b, 6, d))
        return ml, mc

    ml, mc = layer_mods(0)
    w0 = ab_w_in[0][:, _kv_dup_columns()].astype(BF16)
    gain = norm_gain[0, 0].reshape(1, d)
    qg = jnp.tile(ab_q_gain[0], 2).reshape(1, LANES)
    kg = jnp.tile(ab_k_gain[0], 2).reshape(1, LANES)
    cos_l, sin_l = _rope_tables(n)
    cos_c, sin_c = jnp.ones((nc, LANES), F32), jnp.zeros((nc, LANES), F32)
    q_l, kv_l = _proj0(x, ml, gain, w0, cos_l, sin_l, qg, kg, seg)
    q_c, kv_c = _proj0(ctx, mc, gain, w0, cos_c, sin_c, qg, kg, seg)
    sink = ab_sink[0]
    w_out0 = ab_w_out[0].astype(BF16)
    gain2 = norm_gain[0, 1].reshape(1, d)
    w1 = ffn_w_in[0].astype(BF16)
    w2 = ffn_w_out[0].astype(BF16)
    xl = _out0(_attn_a(q_l, kv_c, sink, kv_l), _attn_b(q_l, kv_c, kv_l), x, ml, w_out0)
    xc = _out0(_attn_a(q_c, kv_c, sink), _attn_b(q_c, kv_c), ctx, mc, w_out0)
    xl = _ffn(xl, ml, gain2, w1, w2, fgain, False)
    xc = _ffn(xc, mc, gain2, w1, w2, fgain, False)

    ml, mc = layer_mods(1)
    gain = norm_gain[1, 0].reshape(1, d)
    w_in1 = cd_w_in[0].astype(BF16)
    pc_l = _proj1(xl, ml, gain, w_in1)
    pc_c = _proj1(xc, mc, gain, w_in1[:, :C_IN])
    zeros = jnp.zeros((DECAY_LORA_PAD, C_WIDTH), F32)
    w2x = jnp.stack([jnp.concatenate([cd_w2[0, 0], zeros]), jnp.concatenate([zeros, cd_w2[0, 1]])])
    a2x = jnp.stack([jnp.concatenate([cd_a2[0, 0], zeros]), jnp.concatenate([zeros, cd_a2[0, 1]])])
    w2x_hi = w2x.astype(BF16)
    w2x_hl = jnp.stack([w2x_hi, (w2x - w2x_hi.astype(F32)).astype(BF16)])
    wts = [jnp.stack([1.0 - cd_mu[0], 0.5 * cd_mu[0]]), cd_w0[0].reshape(2, 1, C_WIDTH), w2x_hl,
           cd_a0[0].reshape(2, 1, C_WIDTH), a2x.astype(BF16), cd_g2[0].astype(BF16),
           cd_k_k[0].reshape(1, C_WIDTH), cd_k_a[0].reshape(1, C_WIDTH),
           cd_r_k[0].reshape(1, C_WIDTH), seg]
    pool_w = jax.scipy.linalg.block_diag(*[cd_pool_w[0, g] for g in range(4)]).astype(BF16)
    pool_wts = [pool_w, cd_pool_scale[0].reshape(1, D_WIDTH)]
    prep_c = _prep(pc_c, wts, False)
    *prep_l, g_l, bonus_l, dp_l = _prep(pc_l, wts + pool_wts, True)
    h_zero = jnp.zeros((b, C_WIDTH // LANES, LANES, LANES), F32)
    _, h_f = _scan(prep_c, 0, h_zero, False, False)
    _, h_b = _scan(prep_c, 1, h_zero, True, False)
    y_f, _ = _scan(prep_l, 0, h_f, False, True)
    y_b, _ = _scan(prep_l, 1, h_b, True, True)
    xl = _out1(y_f, y_b, bonus_l, g_l, dp_l, xl, ml, cd_lnx_w[0].reshape(1, C_WIDTH),
               cd_lnx_b[0].reshape(1, C_WIDTH), seg, cd_w_out[0].astype(BF16))
    return _ffn(xl, ml, norm_gain[1, 1].reshape(1, d), ffn_w_in[1].astype(BF16),
                ffn_w_out[1].astype(BF16), fgain, True)
```

```python
import functools

import numpy as np
import jax
import jax.numpy as jnp
from jax import lax
from jax.experimental import pallas as pl
from jax.experimental.pallas import tpu as pltpu

F32 = jnp.float32
BF16 = jnp.bfloat16
HIGHEST = lax.Precision.HIGHEST
LOG2_E = 1.4426950408889634
SCORE_BOUND_SAFE = 60.0

D_MODEL = 1024
GRID_W = 64
HEAD_DIM = 64
ROPE_THETA = 10000.0
NORM_EPS = 1e-6
WINDOW = 128
AB_Q_COLS = 1024
C_WIDTH = 768
C_IN = 2688
D_WIDTH = 256
CD_IN = C_IN + D_WIDTH
LNX_EPS = 64e-5
FFN_HIDDEN = 2816
POOL_WINDOWS = (2, 4, 8, 16)
DECAY_LORA_PAD = 64

LANES = 128
ROW_TILE = 256
HALO = 8
CHUNK = 64
SCAN_BLOCK = 512
SCAN_PAIRS = 3
VMEM_LIMIT = 56 * 1024 * 1024


def _cparams(*sem):
    return pltpu.CompilerParams(dimension_semantics=sem, vmem_limit_bytes=VMEM_LIMIT)


def _dot(a, b):
    return jnp.dot(a, b, preferred_element_type=F32)


def _dot32(a, b):
    return jnp.dot(a, b, preferred_element_type=F32, precision=HIGHEST)


def _split(a, terms):
    pieces = []
    for _ in range(terms - 1):
        hi = a.astype(BF16)
        pieces.append(hi)
        a = a - hi.astype(F32)
    pieces.append(a.astype(BF16))
    return pieces


def _seg_sum(z, seg):
    hi, lo = _split(z, 2)
    return _dot(hi, seg) + _dot(lo, seg)


def _sigmoid(x):
    return 1.0 / (1.0 + jnp.exp(-x))


def _norm_mod(x, gain, shift, scale):
    ms = jnp.mean(x * x, axis=-1, keepdims=True)
    return (x * lax.rsqrt(ms + NORM_EPS) * gain) * (1.0 + scale) + shift


def _mods_kernel(c_ref, w_ref, b_ref, o_ref):
    c = c_ref[...]
    o_ref[0] = _dot32(c * _sigmoid(c), w_ref[0]) + b_ref[0]


def _mods(cs, ada_w, ada_b):
    depth, d, n6 = ada_w.shape
    tn = 768
    rows = cs.shape[0]
    return pl.pallas_call(
        _mods_kernel,
        grid=(depth, n6 // tn),
        in_specs=[pl.BlockSpec((rows, d), lambda l, j: (0, 0)),
                  pl.BlockSpec((1, d, tn), lambda l, j: (l, 0, j)),
                  pl.BlockSpec((1, 1, tn), lambda l, j: (l, 0, j))],
        out_specs=pl.BlockSpec((1, rows, tn), lambda l, j: (l, 0, j)),
        out_shape=jax.ShapeDtypeStruct((depth, rows, n6), F32),
        compiler_params=_cparams("arbitrary", "arbitrary"),
        name="mods",
    )(cs, ada_w, ada_b.reshape(depth, 1, n6))


def _proj0_kernel(x_ref, mod_ref, gain_ref, w_ref, cos_ref, sin_ref, qg_ref, kg_ref, seg_ref,
                  q_ref, kv_ref):
    h = _norm_mod(x_ref[0], gain_ref[...], mod_ref[0, 0:1, :], mod_ref[0, 1:2, :])
    p = _dot(h.astype(BF16), w_ref[...])
    tm = p.shape[0]
    cos = cos_ref[...]
    sin = sin_ref[...]
    lane = lax.broadcasted_iota(jnp.int32, (tm, LANES), 1)
    first_half = (lane & (HEAD_DIM - 1)) < HEAD_DIM // 2
    seg = seg_ref[...]

    def rope(z):
        partner = jnp.where(first_half, pltpu.roll(z, LANES - HEAD_DIM // 2, 1),
                            pltpu.roll(z, HEAD_DIM // 2, 1))
        return z * cos + partner * sin

    def head_norm(z, g):
        ms = _seg_sum(z * z, seg) * (1.0 / HEAD_DIM)
        return z * lax.rsqrt(ms + NORM_EPS) * g

    scale = HEAD_DIM ** -0.5 * LOG2_E
    for blk in range(8):
        z = p[:, blk * LANES:(blk + 1) * LANES]
        if blk >= 4:
            z = head_norm(z, qg_ref[...])
        q_ref[0, :, blk * LANES:(blk + 1) * LANES] = (rope(z) * scale).astype(BF16)
    for blk in range(8):
        z = p[:, AB_Q_COLS + blk * LANES:AB_Q_COLS + (blk + 1) * LANES]
        if blk in (4, 5):
            z = head_norm(z, kg_ref[...])
        if blk in (0, 1, 4, 5):
            z = rope(z)
        kv_ref[0, :, blk * LANES:(blk + 1) * LANES] = z.astype(BF16)


def _proj0(x, mod, gain, w, cos, sin, qg, kg, seg):
    b, n, d = x.shape
    tm = ROW_TILE
    nw = w.shape[1]
    const = lambda bi, i: (0, 0)
    return pl.pallas_call(
        _proj0_kernel,
        grid=(b, n // tm),
        in_specs=[pl.BlockSpec((1, tm, d), lambda bi, i: (bi, i, 0)),
                  pl.BlockSpec((1, 6, d), lambda bi, i: (bi, 0, 0)),
                  pl.BlockSpec((1, d), const),
                  pl.BlockSpec((d, nw), const),
                  pl.BlockSpec((tm, LANES), lambda bi, i: (i, 0)),
                  pl.BlockSpec((tm, LANES), lambda bi, i: (i, 0)),
                  pl.BlockSpec((1, LANES), const),
                  pl.BlockSpec((1, LANES), const),
                  pl.BlockSpec((LANES, LANES), const)],
        out_specs=[pl.BlockSpec((1, tm, 1024), lambda bi, i: (bi, i, 0)),
                   pl.BlockSpec((1, tm, 1024), lambda bi, i: (bi, i, 0))],
        out_shape=[jax.ShapeDtypeStruct((b, n, 1024), BF16),
                   jax.ShapeDtypeStruct((b, n, 1024), BF16)],
        compiler_params=_cparams("parallel", "parallel"),
        name="proj0",
    )(x, mod, gain, w, cos, sin, qg, kg, seg)


def _masked_q(q_ref, h, low):
    q2 = q_ref[0, :, (h // 2) * LANES:(h // 2 + 1) * LANES]
    keep = low if h % 2 == 0 else jnp.logical_not(low)
    return jnp.where(keep, q2, jnp.zeros_like(q2))


def _attn_b_kernel(q_ref, kc_ref, vc_ref, *rest, n_kt, has_latent, bounded):
    rest = list(rest)
    k_ref, v_ref = (rest.pop(0), rest.pop(0)) if has_latent else (None, None)
    o_ref = rest.pop(0)
    m_sc = None if bounded else rest.pop(0)
    acc_sc = rest.pop(0)
    kt = pl.program_id(3)
    tq = q_ref.shape[1]
    low = lax.broadcasted_iota(jnp.int32, (tq, LANES), 1) < HEAD_DIM

    def update(k, v):
        tk = k.shape[0]
        v1 = jnp.where(lax.broadcasted_iota(jnp.int32, (tk, LANES), 1) < HEAD_DIM, v, jnp.ones_like(v))
        ss = [lax.dot_general(_masked_q(q_ref, h, low), k, (((1,), (1,)), ((), ())),
                              preferred_element_type=F32) for h in range(4)]
        for h in range(4):
            if bounded:
                acc_sc[h] = acc_sc[h] + _dot(jnp.exp2(ss[h]).astype(BF16), v1)
                continue
            m_prev = m_sc[h]
            m_new = jnp.maximum(m_prev, jnp.max(ss[h], axis=1, keepdims=True))
            p = jnp.exp2(ss[h] - jnp.concatenate([m_new] * (tk // LANES), axis=1)).astype(BF16)
            acc_sc[h] = jnp.exp2(m_prev - m_new) * acc_sc[h] + _dot(p, v1)
            m_sc[h] = m_new

    @pl.when(kt == 0)
    def _():
        if not bounded:
            m_sc[...] = jnp.full(m_sc.shape, -jnp.inf, F32)
        acc_sc[...] = jnp.zeros(acc_sc.shape, F32)
        update(kc_ref[0], vc_ref[0])

    if has_latent:
        update(k_ref[0], v_ref[0])

    @pl.when(kt == n_kt - 1)
    def _():
        for pr in range(2):
            a0 = acc_sc[2 * pr]
            a1 = acc_sc[2 * pr + 1]
            o0 = a0 / pltpu.roll(a0, HEAD_DIM, 1)
            o1 = pltpu.roll(a1, HEAD_DIM, 1) / a1
            o_ref[0, :, pr * LANES:(pr + 1) * LANES] = jnp.where(low, o0, o1).astype(BF16)


def _attn_b(q, kv_c, kv=None, tk=1024, bounded=False):
    b, n, _ = q.shape
    nc = kv_c.shape[1]
    tq = ROW_TILE
    has_latent = kv is not None
    n_kt = kv.shape[1] // tk if has_latent else 1
    in_specs = [pl.BlockSpec((1, tq, 2 * LANES), lambda bi, g, i, j: (bi, i, 2 + g)),
                pl.BlockSpec((1, nc, LANES), lambda bi, g, i, j: (bi, 0, 4 + g)),
                pl.BlockSpec((1, nc, LANES), lambda bi, g, i, j: (bi, 0, 6 + g))]
    args = [q, kv_c, kv_c]
    if has_latent:
        in_specs += [pl.BlockSpec((1, tk, LANES), lambda bi, g, i, j: (bi, j, 4 + g)),
                     pl.BlockSpec((1, tk, LANES), lambda bi, g, i, j: (bi, j, 6 + g))]
        args += [kv, kv]
    return pl.pallas_call(
        functools.partial(_attn_b_kernel, n_kt=n_kt, has_latent=has_latent, bounded=bounded),
        grid=(b, 2, n // tq, n_kt),
        in_specs=in_specs,
        out_specs=pl.BlockSpec((1, tq, 2 * LANES), lambda bi, g, i, j: (bi, i, g)),
        out_shape=jax.ShapeDtypeStruct((b, n, 512), BF16),
        scratch_shapes=[pltpu.VMEM((4, tq, LANES), F32)] * (1 if bounded else 2),
        compiler_params=_cparams("parallel", "parallel", "parallel", "arbitrary"),
        name="attn_b_bounded" if bounded else "attn_b",
    )(*args)


def _attn_b_auto(q, kv_c, kv, q_gain, k_gain):
    bound = (1.02 * HEAD_DIM * HEAD_DIM ** -0.5 * LOG2_E) * jnp.max(jnp.abs(q_gain)) * jnp.max(jnp.abs(k_gain))
    return lax.cond(bound <= SCORE_BOUND_SAFE,
                    lambda: _attn_b(q, kv_c, kv, bounded=True),
                    lambda: _attn_b(q, kv_c, kv, bounded=False))


def _attn_a_kernel(sink_ref, q_ref, kc_ref, vc_ref, *rest, has_local):
    if has_local:
        kp_ref, kcur_ref, kn_ref, vp_ref, vcur_ref, vn_ref, o_ref, kbuf, vbuf = rest
    else:
        o_ref, kbuf, vbuf = rest
    g = pl.program_id(1)
    i = pl.program_id(2)
    n_i = pl.num_programs(2)
    tq = q_ref.shape[1]
    nc = kc_ref.shape[1]
    kbuf[0:nc, :] = kc_ref[0]
    vbuf[0:nc, :] = vc_ref[0]
    nk = nc
    if has_local:
        kbuf[nc:nc + WINDOW, :] = kp_ref[0]
        kbuf[nc + WINDOW:nc + WINDOW + tq, :] = kcur_ref[0]
        kbuf[nc + WINDOW + tq:nc + 2 * WINDOW + tq, :] = kn_ref[0]
        vbuf[nc:nc + WINDOW, :] = vp_ref[0]
        vbuf[nc + WINDOW:nc + WINDOW + tq, :] = vcur_ref[0]
        vbuf[nc + WINDOW + tq:nc + 2 * WINDOW + tq, :] = vn_ref[0]
        nk = nc + 2 * WINDOW + tq
        row = lax.broadcasted_iota(jnp.int32, (tq, nk), 0)
        col = lax.broadcasted_iota(jnp.int32, (tq, nk), 1) - nc
        lo_c = jnp.where(i == 0, WINDOW, 0)
        hi_c = jnp.where(i == n_i - 1, WINDOW + tq, 2 * WINDOW + tq)
        valid = (col < 0) | ((col >= row) & (col <= row + 2 * WINDOW) & (col >= lo_c) & (col < hi_c))
    k = kbuf[...]
    v = vbuf[...]
    v1 = jnp.where(lax.broadcasted_iota(jnp.int32, (nk, LANES), 1) < HEAD_DIM, v, jnp.ones_like(v))
    low = lax.broadcasted_iota(jnp.int32, (tq, LANES), 1) < HEAD_DIM
    ss = [lax.dot_general(_masked_q(q_ref, h, low), k, (((1,), (1,)), ((), ())),
                          preferred_element_type=F32) for h in range(4)]
    accs = []
    for h in range(4):
        s = jnp.where(valid, ss[h], -jnp.inf) if has_local else ss[h]
        sink = sink_ref[g * 4 + h] * LOG2_E
        m = jnp.maximum(jnp.max(s, axis=1, keepdims=True), sink)
        p = jnp.exp2(s - m)
        sink_p = jnp.where(low, 0.0, jnp.exp2(sink - m))
        accs.append(_dot(p.astype(BF16), v1) + sink_p)
    for pr in range(2):
        a0, a1 = accs[2 * pr], accs[2 * pr + 1]
        o0 = a0 / pltpu.roll(a0, HEAD_DIM, 1)
        o1 = pltpu.roll(a1, HEAD_DIM, 1) / a1
        o_ref[0, :, pr * LANES:(pr + 1) * LANES] = jnp.where(low, o0, o1).astype(BF16)


def _attn_a(q, kv_c, sink, kv=None):
    b, n, _ = q.shape
    nc = kv_c.shape[1]
    tq = ROW_TILE
    has_local = kv is not None
    in_specs = [pl.BlockSpec(memory_space=pltpu.SMEM),
                pl.BlockSpec((1, tq, 2 * LANES), lambda bi, g, i: (bi, i, g)),
                pl.BlockSpec((1, nc, LANES), lambda bi, g, i: (bi, 0, g)),
                pl.BlockSpec((1, nc, LANES), lambda bi, g, i: (bi, 0, 2 + g))]
    args = [sink, q, kv_c, kv_c]
    nk = nc
    if has_local:
        per = tq // WINDOW
        last = n // WINDOW - 1
        prev_i = lambda i: jnp.maximum(i * per - 1, 0)
        next_i = lambda i: jnp.minimum((i + 1) * per, last)
        for off in (0, 2):
            in_specs += [pl.BlockSpec((1, WINDOW, LANES), lambda bi, g, i, off=off: (bi, prev_i(i), off + g)),
                         pl.BlockSpec((1, tq, LANES), lambda bi, g, i, off=off: (bi, i, off + g)),
                         pl.BlockSpec((1, WINDOW, LANES), lambda bi, g, i, off=off: (bi, next_i(i), off + g))]
            args += [kv, kv, kv]
        nk = nc + 2 * WINDOW + tq
    return pl.pallas_call(
        functools.partial(_attn_a_kernel, has_local=has_local),
        grid=(b, 2, n // tq),
        in_specs=in_specs,
        out_specs=pl.BlockSpec((1, tq, 2 * LANES), lambda bi, g, i: (bi, i, g)),
        out_shape=jax.ShapeDtypeStruct((b, n, 512), BF16),
        scratch_shapes=[pltpu.VMEM((nk, LANES), BF16), pltpu.VMEM((nk, LANES), BF16)],
        compiler_params=_cparams("parallel", "parallel", "parallel"),
        name="attn_a",
    )(*args)


def _out0_kernel(ya_ref, yb_ref, x_ref, mod_ref, w_ref, o_ref):
    half = ya_ref.shape[2]
    y = _dot(ya_ref[0], w_ref[0:half, :]) + _dot(yb_ref[0], w_ref[half:2 * half, :])
    o_ref[0] = x_ref[0] + mod_ref[0, 2:3, :] * y


def _out0(ya, yb, x, mod, w):
    b, n, d = x.shape
    tm = ROW_TILE
    row = lambda bi, i: (bi, i, 0)
    return pl.pallas_call(
        _out0_kernel,
        grid=(b, n // tm),
        in_specs=[pl.BlockSpec((1, tm, ya.shape[2]), row),
                  pl.BlockSpec((1, tm, yb.shape[2]), row),
                  pl.BlockSpec((1, tm, d), row),
                  pl.BlockSpec((1, 6, d), lambda bi, i: (bi, 0, 0)),
                  pl.BlockSpec(w.shape, lambda bi, i: (0, 0))],
        out_specs=pl.BlockSpec((1, tm, d), row),
        out_shape=jax.ShapeDtypeStruct((b, n, d), F32),
        compiler_params=_cparams("parallel", "parallel"),
        name="out0",
    )(ya, yb, x, mod, w)


def _ffn_kernel(x_ref, mod_ref, gain_ref, w1_ref, w2_ref, fg_ref, o_ref, *, final):
    x = x_ref[0]
    h = _norm_mod(x, gain_ref[...], mod_ref[0, 3:4, :], mod_ref[0, 4:5, :]).astype(BF16)
    hid = w2_ref.shape[0]
    ch = hid // 2
    acc = None
    for j in range(2):
        gate = _dot(h, w1_ref[:, j * ch:(j + 1) * ch])
        up = _dot(h, w1_ref[:, hid + j * ch:hid + (j + 1) * ch])
        act = (gate * _sigmoid(gate) * up).astype(BF16)
        part = _dot(act, w2_ref[j * ch:(j + 1) * ch, :])
        acc = part if acc is None else acc + part
    y = x + mod_ref[0, 5:6, :] * acc
    if final:
        ms = jnp.mean(y * y, axis=-1, keepdims=True)
        y = y * lax.rsqrt(ms + NORM_EPS) * fg_ref[...]
    o_ref[0] = y


def _ffn(x, mod, gain, w1, w2, fgain, final):
    b, n, d = x.shape
    tm = ROW_TILE
    row = lambda bi, i: (bi, i, 0)
    const = lambda bi, i: (0, 0)
    return pl.pallas_call(
        functools.partial(_ffn_kernel, final=final),
        grid=(b, n // tm),
        in_specs=[pl.BlockSpec((1, tm, d), row),
                  pl.BlockSpec((1, 6, d), lambda bi, i: (bi, 0, 0)),
                  pl.BlockSpec((1, d), const),
                  pl.BlockSpec(w1.shape, const, pipeline_mode=pl.Buffered(1)),
                  pl.BlockSpec(w2.shape, const, pipeline_mode=pl.Buffered(1)),
                  pl.BlockSpec((1, d), const)],
        out_specs=pl.BlockSpec((1, tm, d), row),
        out_shape=jax.ShapeDtypeStruct((b, n, d), F32),
        compiler_params=_cparams("parallel", "parallel"),
        name="ffn",
    )(x, mod, gain, w1, w2, fgain)


def _proj1_kernel(x_ref, mod_ref, gain_ref, w_ref, o_ref):
    h = _norm_mod(x_ref[0], gain_ref[...], mod_ref[0, 0:1, :], mod_ref[0, 1:2, :])
    o_ref[0] = _dot(h.astype(BF16), w_ref[...])


def _proj1(x, mod, gain, w):
    b, n, d = x.shape
    tm = ROW_TILE
    nw = w.shape[1]
    return pl.pallas_call(
        _proj1_kernel,
        grid=(b, n // tm),
        in_specs=[pl.BlockSpec((1, tm, d), lambda bi, i: (bi, i, 0)),
                  pl.BlockSpec((1, 6, d), lambda bi, i: (bi, 0, 0)),
                  pl.BlockSpec((1, d), lambda bi, i: (0, 0)),
                  pl.BlockSpec((d, nw), lambda bi, i: (0, 0))],
        out_specs=pl.BlockSpec((1, tm, nw), lambda bi, i: (bi, i, 0)),
        out_shape=jax.ShapeDtypeStruct((b, n, nw), F32),
        compiler_params=_cparams("parallel", "parallel"),
        name="proj1",
    )(x, mod, gain, w)


def _prep_kernel(pc_ref, prev_ref, next_ref, mu_ref, w0_ref, w2_ref, a0_ref, a2_ref, g2_ref,
                 kk_ref, ka_ref, rk_ref, seg_ref, tri_ref, *rest, latent):
    if latent:
        pw_ref, ps_ref, v_o, at_o, rt_o, bt_o, kt_o, wl_o, g_o, bonus_o, d_o, ext, tmp = rest
    else:
        v_o, at_o, rt_o, bt_o, kt_o, wl_o, ext, tmp = rest
    i = pl.program_id(1)
    nt = pl.num_programs(1)
    tm = pc_ref.shape[1]
    cw = C_WIDTH
    ext[HALO:HALO + tm, :] = pc_ref[0]
    ext[0:HALO, :] = jnp.where(i > 0, prev_ref[0], 0.0)
    ext[HALO + tm:2 * HALO + tm, :] = jnp.where(i < nt - 1, next_ref[0], 0.0)

    def mixed(lo, hi):
        nb = ext[HALO - 1:HALO - 1 + tm, lo:hi] + ext[HALO + 1:HALO + 1 + tm, lo:hi]
        return ext[HALO:HALO + tm, lo:hi] * mu_ref[0:1, lo:hi] + nb * mu_ref[1:2, lo:hi]

    lora = mixed(3 * cw, C_IN)
    tw = jnp.tanh(lora[:, 0:LANES])
    xa = lora[:, LANES:2 * LANES]
    if latent:
        g_o[0] = _dot(_sigmoid(lora[:, 2 * LANES:3 * LANES]).astype(BF16), g2_ref[...])
    tw_hi, tw_lo = _split(tw, 2)
    xa_b = xa.astype(BF16)
    for d in range(2):
        z = w0_ref[d] + _dot(tw_hi, w2_ref[0, d]) + _dot(tw_lo, w2_ref[0, d]) + _dot(tw_hi, w2_ref[1, d])
        w_log = -(jnp.maximum(-z, 0.0) + jnp.log(1.0 + jnp.exp(-jnp.abs(z)))) - 0.5
        tmp[d] = -jnp.exp(w_log)
        tmp[2 + d] = _sigmoid(a0_ref[d] + _dot(xa_b, a2_ref[d]))
    seg = seg_ref[...]
    n_chunks = tm // CHUNK
    for pb in range(cw // LANES):
        sl = slice(pb * LANES, (pb + 1) * LANES)
        r = mixed(pb * LANES, (pb + 1) * LANES)
        k = mixed(cw + pb * LANES, cw + (pb + 1) * LANES)
        v = mixed(2 * cw + pb * LANES, 2 * cw + (pb + 1) * LANES)
        kk = k * kk_ref[:, sl]
        kk = kk / jnp.maximum(jnp.sqrt(_seg_sum(kk * kk, seg)), 1e-12)
        ksum = jnp.zeros_like(k)
        for d in range(2):
            lw = tmp[d, :, sl]
            a = tmp[2 + d, :, sl]
            k_d = k * (1.0 + (a - 1.0) * ka_ref[:, sl])
            bb = kk * a
            ksum = ksum + k_d
            cs2 = _dot(tri_ref[d], jnp.concatenate(_split(lw, 2), axis=1))
            cs = cs2[:, 0:LANES] + cs2[:, LANES:2 * LANES]
            for j in range(n_chunks):
                last = j * CHUNK if d == 1 else (j + 1) * CHUNK - 1
                wl_o[d, 0, j, :, sl] = jnp.exp(cs[last:last + 1, :])
            e_up = jnp.exp(-cs)
            at_o[d, 0, :, sl] = (-kk * jnp.exp(cs - lw)).astype(BF16)
            rt_o[d, 0, :, sl] = (r * jnp.exp(cs)).astype(BF16)
            bt_o[d, 0, :, sl] = (bb * e_up).astype(BF16)
            kt_o[d, 0, :, sl] = (k_d * e_up).astype(BF16)
        v_o[0, :, sl] = v.astype(BF16)
        if latent:
            bonus_o[0, :, sl] = _seg_sum(r * rk_ref[:, sl] * ksum, seg) * v
    if latent:
        n_tok = nt * tm
        pos = i * tm + lax.broadcasted_iota(jnp.int32, (tm, LANES), 0)
        group1 = lax.broadcasted_iota(jnp.int32, (tm, LANES), 1) >= D_WIDTH // 4
        pooled = []
        for half in range(2):
            lo, hi = C_IN + half * LANES, C_IN + (half + 1) * LANES
            w_small, w_big = POOL_WINDOWS[2 * half], POOL_WINDOWS[2 * half + 1]
            s_small = None
            s_big = None
            for off in range(-(w_big // 2), w_big - w_big // 2):
                piece = ext[HALO + off:HALO + off + tm, lo:hi]
                s_big = piece if s_big is None else s_big + piece
                if -(w_small // 2) <= off < w_small - w_small // 2:
                    s_small = piece if s_small is None else s_small + piece

            def count(w):
                lo_p = jnp.clip(pos - w // 2, 0, n_tok)
                hi_p = jnp.clip(pos + (w - w // 2), 0, n_tok)
                return (hi_p - lo_p).astype(F32)

            mean = jnp.where(group1, s_big / count(w_big), s_small / count(w_small))
            pooled.append((mean - ext[HALO:HALO + tm, lo:hi]).astype(BF16))
        pm = jnp.concatenate(pooled, axis=1)
        d_o[0] = _dot(pm, pw_ref[...]) * ps_ref[...]


def _chunk_cumsum_matrices(tm):
    t = np.arange(tm)[:, None]
    u = np.arange(tm)[None, :]
    same = (t // CHUNK) == (u // CHUNK)
    return np.stack([same & (u <= t), same & (u >= t)]).astype(np.float32)


def _prep(pc, wts, latent):
    b, n, cin = pc.shape
    tm = ROW_TILE
    cw = C_WIDTH
    per = tm // HALO
    last = n // HALO - 1
    n_chunks = tm // CHUNK
    row = lambda bi, i: (bi, i, 0)
    drow = lambda bi, i: (0, bi, i, 0)
    wts = list(wts)
    wts = wts[:10] + [jnp.asarray(_chunk_cumsum_matrices(tm)).astype(BF16)] + (wts[10:] if latent else [])

    def full(a):
        return pl.BlockSpec(a.shape, lambda bi, i, nd=a.ndim: (0,) * nd)

    in_specs = [pl.BlockSpec((1, tm, cin), row),
                pl.BlockSpec((1, HALO, cin), lambda bi, i: (bi, jnp.maximum(i * per - 1, 0), 0)),
                pl.BlockSpec((1, HALO, cin), lambda bi, i: (bi, jnp.minimum((i + 1) * per, last), 0))]
    in_specs += [full(a) for a in wts]
    out_specs = [pl.BlockSpec((1, tm, cw), row)] + [pl.BlockSpec((2, 1, tm, cw), drow)] * 4
    out_shape = [jax.ShapeDtypeStruct((b, n, cw), BF16)] + [jax.ShapeDtypeStruct((2, b, n, cw), BF16)] * 4
    out_specs.append(pl.BlockSpec((2, 1, n_chunks, 1, cw), lambda bi, i: (0, bi, i, 0, 0)))
    out_shape.append(jax.ShapeDtypeStruct((2, b, n // CHUNK, 1, cw), F32))
    if latent:
        out_specs += [pl.BlockSpec((1, tm, cw), row)] * 2 + [pl.BlockSpec((1, tm, D_WIDTH), row)]
        out_shape += [jax.ShapeDtypeStruct((b, n, cw), F32)] * 2 + [jax.ShapeDtypeStruct((b, n, D_WIDTH), F32)]
    return pl.pallas_call(
        functools.partial(_prep_kernel, latent=latent),
        grid=(b, n // tm),
        in_specs=in_specs,
        out_specs=out_specs,
        out_shape=out_shape,
        scratch_shapes=[pltpu.VMEM((tm + 2 * HALO, cin), F32), pltpu.VMEM((4, tm, cw), F32)],
        compiler_params=_cparams("parallel", "parallel"),
        name="prep",
    )(pc, pc, pc, *wts)


def _scan_masks(rev):
    n = 2 * CHUNK
    t = np.arange(n)[:, None]
    u = np.arange(n)[None, :]
    same = (t // CHUNK) == (u // CHUNK)
    before = (u > t) if rev else (u < t)
    masks = [same & before, same & (before | (u == t))]
    s = 1
    while s < CHUNK:
        blk = (t // (2 * s)) == (u // (2 * s))
        t_late = (t % (2 * s) < s) if rev else (t % (2 * s) >= s)
        u_early = (u % (2 * s) >= s) if rev else (u % (2 * s) < s)
        masks.append(blk & t_late & u_early)
        s *= 2
    return np.stack(masks).astype(np.float32)


def _scan_kernel(v_ref, at_ref, rt_ref, bt_ref, kt_ref, wl_ref, h0_ref, msk_ref, *rest,
                 rev, with_y):
    if with_y:
        y_ref, hT_ref, g_sc = rest
    else:
        hT_ref, g_sc = rest
    tb = pl.program_id(2)
    n_tb = pl.num_programs(2)
    n_chunks = v_ref.shape[1] // CHUNK
    n2 = 2 * CHUNK

    @pl.when(tb == 0)
    def _():
        g_sc[...] = h0_ref[0]

    low = lax.broadcasted_iota(jnp.int32, (CHUNK, LANES), 1) < HEAD_DIM
    rid = lax.broadcasted_iota(jnp.int32, (n2, n2), 0)
    cid = lax.broadcasted_iota(jnp.int32, (n2, n2), 1)
    eye = (rid == cid).astype(F32)
    n_levels = msk_ref.shape[0] - 2
    strict = msk_ref[0] > 0
    incl = msk_ref[1] > 0
    nt_dims = (((1,), (1,)), ((), ()))
    tn_dims = (((0,), (0,)), ((), ()))

    def stack(z):
        zero = jnp.zeros_like(z)
        return jnp.concatenate([jnp.where(low, z, zero), jnp.where(low, zero, z)], axis=0)

    order = list(range(n_chunks - 1, -1, -1) if rev else range(n_chunks))
    n_pairs = v_ref.shape[2] // LANES
    chains = [(pp, c) for c in order for pp in range(n_pairs)]
    ch = {}
    for key in chains:
        pp, c = key
        rows = slice(c * CHUNK, (c + 1) * CHUNK)
        lanes = slice(pp * LANES, (pp + 1) * LANES)
        q = ch[key] = {"rows": rows, "lanes": lanes}
        q["v"] = stack(v_ref[0, rows, lanes])
        q["a"] = stack(at_ref[0, 0, rows, lanes])
        q["b"] = stack(bt_ref[0, 0, rows, lanes])
        q["k"] = stack(kt_ref[0, 0, rows, lanes])
        rhs = jnp.concatenate([q["b"], q["k"]], axis=0)
        if with_y:
            q["r"] = stack(rt_ref[0, 0, rows, lanes])
            lhs = jnp.concatenate([q["a"], q["r"]], axis=0)
        else:
            lhs = q["a"]
        gram = lax.dot_general(lhs, rhs, nt_dims, preferred_element_type=F32)
        q["ab"] = jnp.where(strict, gram[0:n2, 0:n2], 0.0)
        q["akv"] = _dot(jnp.where(strict, gram[0:n2, n2:2 * n2], 0.0).astype(BF16), q["v"])
        if with_y:
            q["rb"] = jnp.where(incl, gram[n2:2 * n2, 0:n2], 0.0).astype(BF16)
            q["rkv"] = _dot(jnp.where(incl, gram[n2:2 * n2, n2:2 * n2], 0.0).astype(BF16), q["v"])
        q["t"] = eye + q["ab"] * msk_ref[2]
    for lvl in range(1, n_levels):
        for key in chains:
            q = ch[key]
            q["x"] = _dot((q["ab"] * msk_ref[2 + lvl]).astype(BF16), q["t"].astype(BF16))
        for key in chains:
            q = ch[key]
            q["t"] = q["t"] + _dot(q["t"].astype(BF16), q["x"].astype(BF16))
    for key in chains:
        q = ch[key]
        rows, lanes = q["rows"], q["lanes"]
        au = _dot(q["t"].astype(BF16), jnp.concatenate([q["a"], q["akv"].astype(BF16)], axis=1))
        q["au"] = au.astype(BF16)
        wl = wl_ref[0, 0, key[1], :, lanes]
        mc = lax.dot_general(q["au"], q["b"], tn_dims, preferred_element_type=F32)
        vk = lax.dot_general(q["v"], q["k"], tn_dims, preferred_element_type=F32)
        q["mt"] = (mc[0:LANES] * wl).astype(BF16)
        q["ct"] = (mc[LANES:2 * LANES] + vk) * wl
        if with_y:
            rbau = _dot(q["rb"], q["au"])
            q["rt"] = (q["r"].astype(F32) + rbau[:, 0:LANES]).astype(BF16)
            q["y0"] = rbau[:, LANES:2 * LANES] + q["rkv"]
    g = [g_sc[pp] for pp in range(n_pairs)]
    for key in chains:
        pp, c = key
        q = ch[key]
        g_b = g[pp].astype(BF16)
        if with_y:
            y_st = lax.dot_general(q["rt"], g_b, nt_dims, preferred_element_type=F32) + q["y0"]
            y_ref[0, q["rows"], q["lanes"]] = y_st[0:CHUNK] + y_st[CHUNK:n2]
        g[pp] = g[pp] * wl_ref[0, 0, c, :, q["lanes"]] + _dot(g_b, q["mt"]) + q["ct"]
    for pp in range(n_pairs):
        g_sc[pp] = g[pp]

    @pl.when(tb == n_tb - 1)
    def _():
        hT_ref[0] = g_sc[...]


def _scan(prep, d, h0, rev, with_y):
    v, at, rt, bt, kt, wl = prep
    b, n, cw = v.shape
    tb = min(SCAN_BLOCK, n)
    n_tb = n // tb
    n_pairs = cw // LANES
    sp = SCAN_PAIRS
    wide = sp * LANES
    msk = _scan_masks(rev)
    tmap = (lambda t: n_tb - 1 - t) if rev else (lambda t: t)
    shared = pl.BlockSpec((1, tb, wide), lambda bi, p, t: (bi, tmap(t), p))
    perdir = pl.BlockSpec((1, 1, tb, wide), lambda bi, p, t: (d, bi, tmap(t), p))
    decay = pl.BlockSpec((1, 1, tb // CHUNK, 1, wide), lambda bi, p, t: (d, bi, tmap(t), 0, p))
    state = pl.BlockSpec((1, sp, LANES, LANES), lambda bi, p, t: (bi, p, 0, 0))
    out_specs = [state]
    out_shape = [jax.ShapeDtypeStruct((b, n_pairs, LANES, LANES), F32)]
    if with_y:
        out_specs = [shared] + out_specs
        out_shape = [jax.ShapeDtypeStruct((b, n, cw), F32)] + out_shape
    res = pl.pallas_call(
        functools.partial(_scan_kernel, rev=rev, with_y=with_y),
        grid=(b, n_pairs // sp, n_tb),
        in_specs=[shared, perdir, perdir, perdir, perdir, decay, state,
                  pl.BlockSpec(msk.shape, lambda bi, p, t: (0, 0, 0))],
        out_specs=out_specs,
        out_shape=out_shape,
        scratch_shapes=[pltpu.VMEM((sp, LANES, LANES), F32)],
        compiler_params=_cparams("parallel", "parallel", "arbitrary"),
        name="scan_rev" if rev else "scan_fwd",
    )(v, at, rt, bt, kt, wl, h0, jnp.asarray(msk))
    return (res[0], res[1]) if with_y else (None, res[0])


def _out1_kernel(yf_ref, yb_ref, bonus_ref, g_ref, dp_ref, x_ref, mod_ref, lw_ref, lb_ref, seg_ref,
                 w_ref, o_ref):
    seg = seg_ref[...]
    cw = yf_ref.shape[2]
    acc = _dot(dp_ref[0].astype(BF16), w_ref[cw:cw + D_WIDTH, :])
    for pb in range(cw // LANES):
        sl = slice(pb * LANES, (pb + 1) * LANES)
        y = yf_ref[0, :, sl] + yb_ref[0, :, sl]
        mean = _seg_sum(y, seg) * (1.0 / HEAD_DIM)
        dev = y - mean
        var = _seg_sum(dev * dev, seg) * (1.0 / HEAD_DIM)
        yn = dev * lax.rsqrt(var + LNX_EPS) * lw_ref[:, sl] + lb_ref[:, sl]
        z = (yn + bonus_ref[0, :, sl]) * g_ref[0, :, sl]
        acc = acc + _dot(z.astype(BF16), w_ref[sl, :])
    o_ref[0] = x_ref[0] + mod_ref[0, 2:3, :] * acc


def _out1(yf, yb, bonus, g, dp, x, mod, lnx_w, lnx_b, seg, w):
    b, n, d = x.shape
    tm = ROW_TILE
    cw = yf.shape[2]
    row = lambda bi, i: (bi, i, 0)
    const = lambda bi, i: (0, 0)
    return pl.pallas_call(
        _out1_kernel,
        grid=(b, n // tm),
        in_specs=[pl.BlockSpec((1, tm, cw), row)] * 4
        + [pl.BlockSpec((1, tm, D_WIDTH), row),
           pl.BlockSpec((1, tm, d), row),
           pl.BlockSpec((1, 6, d), lambda bi, i: (bi, 0, 0)),
           pl.BlockSpec((1, cw), const), pl.BlockSpec((1, cw), const),
           pl.BlockSpec((LANES, LANES), const),
           pl.BlockSpec(w.shape, const)],
        out_specs=pl.BlockSpec((1, tm, d), row),
        out_shape=jax.ShapeDtypeStruct((b, n, d), F32),
        compiler_params=_cparams("parallel", "parallel"),
        name="out1",
    )(yf, yb, bonus, g, dp, x, mod, lnx_w, lnx_b, seg, w)


def _rope_tables(n):
    rows = n // GRID_W
    row = jnp.repeat(jnp.arange(rows, dtype=F32), GRID_W)
    col = jnp.tile(jnp.arange(GRID_W, dtype=F32), rows)
    n_freq = HEAD_DIM // 4
    inv = ROPE_THETA ** (-jnp.arange(n_freq, dtype=F32) / n_freq)
    ang = jnp.concatenate([row[:, None] * inv[None, :], col[:, None] * inv[None, :]], axis=-1)
    cos, sin = jnp.cos(ang), jnp.sin(ang)
    cos_t = jnp.tile(cos, (1, LANES // cos.shape[1]))
    sin_t = jnp.tile(jnp.concatenate([-sin, sin], axis=-1), (1, LANES // HEAD_DIM))
    return cos_t, sin_t


def _kv_dup_columns():
    cols = []
    for section in range(4):
        for head in range(2):
            base = AB_Q_COLS + section * 2 * HEAD_DIM + head * HEAD_DIM
            cols += list(range(base, base + HEAD_DIM)) * 2
    return np.concatenate([np.arange(AB_Q_COLS), np.asarray(cols)])


def kernel(x, c, ctx, c_ctx, norm_gain, ada_w, ada_b, ffn_w_in, ffn_w_out, final_gain, ab_w_in, ab_q_gain, ab_k_gain, ab_sink, ab_w_out, cd_w_in, cd_mu, cd_w0, cd_w2, cd_a0, cd_a2, cd_g2, cd_k_k, cd_k_a, cd_r_k, cd_lnx_w, cd_lnx_b, cd_pool_w, cd_pool_scale, cd_w_out):
    b, n, d = x.shape
    nc = ctx.shape[1]
    pad = (-(b + 1)) % 8
    cs = jnp.concatenate([c, c_ctx[None, :], jnp.zeros((pad, d), F32)], axis=0)
    mods = _mods(cs, ada_w, ada_b)
    seg = jnp.asarray(np.kron(np.eye(2), np.ones((HEAD_DIM, HEAD_DIM))).astype(np.float32)).astype(BF16)
    fgain = final_gain.reshape(1, d)

    def layer_mods(i):
        ml = mods[i, :b].reshape(b, 6, d)
        mc = jnp.broadcast_to(mods[i, b].reshape(1, 6, d), (b, 6, d))
        return ml, mc

    ml, mc = layer_mods(0)
    w0 = ab_w_in[0][:, _kv_dup_columns()].astype(BF16)
    gain = norm_gain[0, 0].reshape(1, d)
    qg = jnp.tile(ab_q_gain[0], 2).reshape(1, LANES)
    kg = jnp.tile(ab_k_gain[0], 2).reshape(1, LANES)
    cos_l, sin_l = _rope_tables(n)
    cos_c, sin_c = jnp.ones((nc, LANES), F32), jnp.zeros((nc, LANES), F32)
    q_l, kv_l = _proj0(x, ml, gain, w0, cos_l, sin_l, qg, kg, seg)
    q_c, kv_c = _proj0(ctx, mc, gain, w0, cos_c, sin_c, qg, kg, seg)
    sink = ab_sink[0]
    w_out0 = ab_w_out[0].astype(BF16)
    gain2 = norm_gain[0, 1].reshape(1, d)
    w1 = ffn_w_in[0].astype(BF16)
    w2 = ffn_w_out[0].astype(BF16)
    yb_l = _attn_b_auto(q_l, kv_c, kv_l, ab_q_gain[0], ab_k_gain[0])
    xl = _out0(_attn_a(q_l, kv_c, sink, kv_l), yb_l, x, ml, w_out0)
    xc = _out0(_attn_a(q_c, kv_c, sink), _attn_b(q_c, kv_c), ctx, mc, w_out0)
    xl = _ffn(xl, ml, gain2, w1, w2, fgain, False)
    xc = _ffn(xc, mc, gain2, w1, w2, fgain, False)

    ml, mc = layer_mods(1)
    gain = norm_gain[1, 0].reshape(1, d)
    w_in1 = cd_w_in[0].astype(BF16)
    pc_l = _proj1(xl, ml, gain, w_in1)
    pc_c = _proj1(xc, mc, gain, w_in1[:, :C_IN])
    zeros = jnp.zeros((DECAY_LORA_PAD, C_WIDTH), F32)
    w2x = jnp.stack([jnp.concatenate([cd_w2[0, 0], zeros]), jnp.concatenate([zeros, cd_w2[0, 1]])])
    a2x = jnp.stack([jnp.concatenate([cd_a2[0, 0], zeros]), jnp.concatenate([zeros, cd_a2[0, 1]])])
    w2x_hi = w2x.astype(BF16)
    w2x_hl = jnp.stack([w2x_hi, (w2x - w2x_hi.astype(F32)).astype(BF16)])
    wts = [jnp.stack([1.0 - cd_mu[0], 0.5 * cd_mu[0]]), cd_w0[0].reshape(2, 1, C_WIDTH), w2x_hl,
           cd_a0[0].reshape(2, 1, C_WIDTH), a2x.astype(BF16), cd_g2[0].astype(BF16),
           cd_k_k[0].reshape(1, C_WIDTH), cd_k_a[0].reshape(1, C_WIDTH),
           cd_r_k[0].reshape(1, C_WIDTH), seg]
    pool_w = jax.scipy.linalg.block_diag(*[cd_pool_w[0, g] for g in range(4)]).astype(BF16)
    pool_wts = [pool_w, cd_pool_scale[0].reshape(1, D_WIDTH)]
    prep_c = _prep(pc_c, wts, False)
    *prep_l, g_l, bonus_l, dp_l = _prep(pc_l, wts + pool_wts, True)
    h_zero = jnp.zeros((b, C_WIDTH // LANES, LANES, LANES), F32)
    _, h_f = _scan(prep_c, 0, h_zero, False, False)
    _, h_b = _scan(prep_c, 1, h_zero, True, False)
    y_f, _ = _scan(prep_l, 0, h_f, False, True)
    y_b, _ = _scan(prep_l, 1, h_b, True, True)
    xl = _out1(y_f, y_b, bonus_l, g_l, dp_l, xl, ml, cd_lnx_w[0].reshape(1, C_WIDTH),
               cd_lnx_b[0].reshape(1, C_WIDTH), seg, cd_w_out[0].astype(BF16))
    return _ffn(xl, ml, norm_gain[1, 1].reshape(1, d), ffn_w_in[1].astype(BF16),
                ffn_w_out[1].astype(BF16), fgain, True)
```

```python
import functools

import numpy as np
import jax
import jax.numpy as jnp
from jax import lax
from jax.experimental import pallas as pl
from jax.experimental.pallas import tpu as pltpu

F32 = jnp.float32
BF16 = jnp.bfloat16
HIGHEST = lax.Precision.HIGHEST
LOG2_E = 1.4426950408889634
SCORE_BOUND_SAFE = 60.0

D_MODEL = 1024
GRID_W = 64
HEAD_DIM = 64
ROPE_THETA = 10000.0
NORM_EPS = 1e-6
WINDOW = 128
AB_Q_COLS = 1024
C_WIDTH = 768
C_IN = 2688
D_WIDTH = 256
CD_IN = C_IN + D_WIDTH
LNX_EPS = 64e-5
FFN_HIDDEN = 2816
POOL_WINDOWS = (2, 4, 8, 16)
DECAY_LORA_PAD = 64

LANES = 128
ROW_TILE = 256
HALO = 8
CHUNK = 64
SCAN_BLOCK = 512
SCAN_PAIRS = 3
VMEM_LIMIT = 56 * 1024 * 1024


def _cparams(*sem):
    return pltpu.CompilerParams(dimension_semantics=sem, vmem_limit_bytes=VMEM_LIMIT)


def _dot(a, b):
    return jnp.dot(a, b, preferred_element_type=F32)


def _dot32(a, b):
    return jnp.dot(a, b, preferred_element_type=F32, precision=HIGHEST)


def _split(a, terms):
    pieces = []
    for _ in range(terms - 1):
        hi = a.astype(BF16)
        pieces.append(hi)
        a = a - hi.astype(F32)
    pieces.append(a.astype(BF16))
    return pieces


def _seg_sum(z, seg):
    hi, lo = _split(z, 2)
    return _dot(hi, seg) + _dot(lo, seg)


def _sigmoid(x):
    return 1.0 / (1.0 + jnp.exp(-x))


def _norm_mod(x, gain, shift, scale):
    ms = jnp.mean(x * x, axis=-1, keepdims=True)
    return (x * lax.rsqrt(ms + NORM_EPS) * gain) * (1.0 + scale) + shift


def _mods_kernel(c_ref, w_ref, b_ref, o_ref):
    c = c_ref[...]
    o_ref[0] = _dot32(c * _sigmoid(c), w_ref[0]) + b_ref[0]


def _mods(cs, ada_w, ada_b):
    depth, d, n6 = ada_w.shape
    tn = 768
    rows = cs.shape[0]
    return pl.pallas_call(
        _mods_kernel,
        grid=(depth, n6 // tn),
        in_specs=[pl.BlockSpec((rows, d), lambda l, j: (0, 0)),
                  pl.BlockSpec((1, d, tn), lambda l, j: (l, 0, j)),
                  pl.BlockSpec((1, 1, tn), lambda l, j: (l, 0, j))],
        out_specs=pl.BlockSpec((1, rows, tn), lambda l, j: (l, 0, j)),
        out_shape=jax.ShapeDtypeStruct((depth, rows, n6), F32),
        compiler_params=_cparams("arbitrary", "arbitrary"),
        name="mods",
    )(cs, ada_w, ada_b.reshape(depth, 1, n6))


def _proj0_kernel(x_ref, mod_ref, gain_ref, w_ref, cos_ref, sin_ref, qg_ref, kg_ref, seg_ref,
                  q_ref, kv_ref, vt_ref):
    h = _norm_mod(x_ref[0], gain_ref[...], mod_ref[0, 0:1, :], mod_ref[0, 1:2, :])
    p = _dot(h.astype(BF16), w_ref[...])
    tm = p.shape[0]
    cos = cos_ref[...]
    sin = sin_ref[...]
    lane = lax.broadcasted_iota(jnp.int32, (tm, LANES), 1)
    first_half = (lane & (HEAD_DIM - 1)) < HEAD_DIM // 2
    seg = seg_ref[...]

    def rope(z):
        partner = jnp.where(first_half, pltpu.roll(z, LANES - HEAD_DIM // 2, 1),
                            pltpu.roll(z, HEAD_DIM // 2, 1))
        return z * cos + partner * sin

    def head_norm(z, g):
        ms = _seg_sum(z * z, seg) * (1.0 / HEAD_DIM)
        return z * lax.rsqrt(ms + NORM_EPS) * g

    scale = HEAD_DIM ** -0.5 * LOG2_E
    for blk in range(8):
        z = p[:, blk * LANES:(blk + 1) * LANES]
        if blk >= 4:
            z = head_norm(z, qg_ref[...])
        q_ref[0, :, blk * LANES:(blk + 1) * LANES] = (rope(z) * scale).astype(BF16)
    for blk in range(8):
        z = p[:, AB_Q_COLS + blk * LANES:AB_Q_COLS + (blk + 1) * LANES]
        if blk in (4, 5):
            z = head_norm(z, kg_ref[...])
        if blk in (0, 1, 4, 5):
            z = rope(z)
        kv_ref[0, :, blk * LANES:(blk + 1) * LANES] = z.astype(BF16)
        if blk in (6, 7):
            zt = z.T
            top = lax.broadcasted_iota(jnp.int32, zt.shape, 0) < HEAD_DIM
            vt_ref[0, blk - 6] = jnp.where(top, zt, 1.0).astype(BF16)


def _proj0(x, mod, gain, w, cos, sin, qg, kg, seg):
    b, n, d = x.shape
    tm = ROW_TILE
    nw = w.shape[1]
    const = lambda bi, i: (0, 0)
    return pl.pallas_call(
        _proj0_kernel,
        grid=(b, n // tm),
        in_specs=[pl.BlockSpec((1, tm, d), lambda bi, i: (bi, i, 0)),
                  pl.BlockSpec((1, 6, d), lambda bi, i: (bi, 0, 0)),
                  pl.BlockSpec((1, d), const),
                  pl.BlockSpec((d, nw), const),
                  pl.BlockSpec((tm, LANES), lambda bi, i: (i, 0)),
                  pl.BlockSpec((tm, LANES), lambda bi, i: (i, 0)),
                  pl.BlockSpec((1, LANES), const),
                  pl.BlockSpec((1, LANES), const),
                  pl.BlockSpec((LANES, LANES), const)],
        out_specs=[pl.BlockSpec((1, tm, 1024), lambda bi, i: (bi, i, 0)),
                   pl.BlockSpec((1, tm, 1024), lambda bi, i: (bi, i, 0)),
                   pl.BlockSpec((1, 2, LANES, tm), lambda bi, i: (bi, 0, 0, i))],
        out_shape=[jax.ShapeDtypeStruct((b, n, 1024), BF16),
                   jax.ShapeDtypeStruct((b, n, 1024), BF16),
                   jax.ShapeDtypeStruct((b, 2, LANES, n), BF16)],
        compiler_params=_cparams("parallel", "parallel"),
        name="proj0",
    )(x, mod, gain, w, cos, sin, qg, kg, seg)


def _masked_q(q_ref, h, low):
    q2 = q_ref[0, :, (h // 2) * LANES:(h // 2 + 1) * LANES]
    keep = low if h % 2 == 0 else jnp.logical_not(low)
    return jnp.where(keep, q2, jnp.zeros_like(q2))


def _attn_b_kernel(q_ref, kc_ref, vc_ref, *rest, n_kt, has_latent):
    if has_latent:
        k_ref, v_ref, o_ref, m_sc, acc_sc = rest
    else:
        o_ref, m_sc, acc_sc = rest
    kt = pl.program_id(3)
    tq = q_ref.shape[1]
    low = lax.broadcasted_iota(jnp.int32, (tq, LANES), 1) < HEAD_DIM

    def update(k, v):
        tk = k.shape[0]
        v1 = jnp.where(lax.broadcasted_iota(jnp.int32, (tk, LANES), 1) < HEAD_DIM, v, jnp.ones_like(v))
        ss = [lax.dot_general(_masked_q(q_ref, h, low), k, (((1,), (1,)), ((), ())),
                              preferred_element_type=F32) for h in range(4)]
        for h in range(4):
            m_prev = m_sc[h]
            m_new = jnp.maximum(m_prev, jnp.max(ss[h], axis=1, keepdims=True))
            p = jnp.exp2(ss[h] - jnp.concatenate([m_new] * (tk // LANES), axis=1)).astype(BF16)
            acc_sc[h] = jnp.exp2(m_prev - m_new) * acc_sc[h] + _dot(p, v1)
            m_sc[h] = m_new

    @pl.when(kt == 0)
    def _():
        m_sc[...] = jnp.full(m_sc.shape, -jnp.inf, F32)
        acc_sc[...] = jnp.zeros(acc_sc.shape, F32)
        update(kc_ref[0], vc_ref[0])

    if has_latent:
        update(k_ref[0], v_ref[0])

    @pl.when(kt == n_kt - 1)
    def _():
        for pr in range(2):
            a0 = acc_sc[2 * pr]
            a1 = acc_sc[2 * pr + 1]
            o0 = a0 / pltpu.roll(a0, HEAD_DIM, 1)
            o1 = pltpu.roll(a1, HEAD_DIM, 1) / a1
            o_ref[0, :, pr * LANES:(pr + 1) * LANES] = jnp.where(low, o0, o1).astype(BF16)


def _attn_b(q, kv_c, kv=None, tk=1024):
    b, n, _ = q.shape
    nc = kv_c.shape[1]
    tq = ROW_TILE
    has_latent = kv is not None
    n_kt = kv.shape[1] // tk if has_latent else 1
    in_specs = [pl.BlockSpec((1, tq, 2 * LANES), lambda bi, g, i, j: (bi, i, 2 + g)),
                pl.BlockSpec((1, nc, LANES), lambda bi, g, i, j: (bi, 0, 4 + g)),
                pl.BlockSpec((1, nc, LANES), lambda bi, g, i, j: (bi, 0, 6 + g))]
    args = [q, kv_c, kv_c]
    if has_latent:
        in_specs += [pl.BlockSpec((1, tk, LANES), lambda bi, g, i, j: (bi, j, 4 + g)),
                     pl.BlockSpec((1, tk, LANES), lambda bi, g, i, j: (bi, j, 6 + g))]
        args += [kv, kv]
    return pl.pallas_call(
        functools.partial(_attn_b_kernel, n_kt=n_kt, has_latent=has_latent),
        grid=(b, 2, n // tq, n_kt),
        in_specs=in_specs,
        out_specs=pl.BlockSpec((1, tq, 2 * LANES), lambda bi, g, i, j: (bi, i, g)),
        out_shape=jax.ShapeDtypeStruct((b, n, 512), BF16),
        scratch_shapes=[pltpu.VMEM((4, tq, LANES), F32), pltpu.VMEM((4, tq, LANES), F32)],
        compiler_params=_cparams("parallel", "parallel", "parallel", "arbitrary"),
        name="attn_b",
    )(*args)


def _attn_bt_kernel(q_ref, kc_ref, vtc_ref, k_ref, vt_ref, o_ref, acc_sc, *, n_kt):
    kt = pl.program_id(3)
    tq = q_ref.shape[1]
    low = lax.broadcasted_iota(jnp.int32, (tq, LANES), 1) < HEAD_DIM

    def update(k, vt):
        sts = [lax.dot_general(k, _masked_q(q_ref, h, low), (((1,), (1,)), ((), ())),
                               preferred_element_type=F32) for h in range(4)]
        for h in range(4):
            acc_sc[h] = acc_sc[h] + _dot(vt, jnp.exp2(sts[h]).astype(BF16))

    @pl.when(kt == 0)
    def _():
        acc_sc[...] = jnp.zeros(acc_sc.shape, F32)
        update(kc_ref[0], vtc_ref[0, 0])

    update(k_ref[0], vt_ref[0, 0])

    @pl.when(kt == n_kt - 1)
    def _():
        ot = jnp.concatenate([acc_sc[h, 0:HEAD_DIM, :] / acc_sc[h, HEAD_DIM:2 * HEAD_DIM, :]
                              for h in range(4)], axis=0)
        o_ref[0] = ot.T.astype(BF16)


def _attn_bt(q, kv_c, vt_c, kv, vt, tk=4096):
    b, n, _ = q.shape
    nc = kv_c.shape[1]
    tq = ROW_TILE
    n_kt = kv.shape[1] // tk
    return pl.pallas_call(
        functools.partial(_attn_bt_kernel, n_kt=n_kt),
        grid=(b, 2, n // tq, n_kt),
        in_specs=[pl.BlockSpec((1, tq, 2 * LANES), lambda bi, g, i, j: (bi, i, 2 + g)),
                  pl.BlockSpec((1, nc, LANES), lambda bi, g, i, j: (bi, 0, 4 + g)),
                  pl.BlockSpec((1, 1, LANES, nc), lambda bi, g, i, j: (bi, g, 0, 0)),
                  pl.BlockSpec((1, tk, LANES), lambda bi, g, i, j: (bi, j, 4 + g)),
                  pl.BlockSpec((1, 1, LANES, tk), lambda bi, g, i, j: (bi, g, 0, j))],
        out_specs=pl.BlockSpec((1, tq, 2 * LANES), lambda bi, g, i, j: (bi, i, g)),
        out_shape=jax.ShapeDtypeStruct((b, n, 512), BF16),
        scratch_shapes=[pltpu.VMEM((4, LANES, tq), F32)],
        compiler_params=_cparams("parallel", "parallel", "parallel", "arbitrary"),
        name="attn_b_bounded",
    )(q, kv_c, vt_c, kv, vt)


def _attn_b_auto(q, kv_c, vt_c, kv, vt, q_gain, k_gain):
    bound = (1.02 * HEAD_DIM * HEAD_DIM ** -0.5 * LOG2_E) * jnp.max(jnp.abs(q_gain)) * jnp.max(jnp.abs(k_gain))
    return lax.cond(bound <= SCORE_BOUND_SAFE,
                    lambda: _attn_bt(q, kv_c, vt_c, kv, vt),
                    lambda: _attn_b(q, kv_c, kv))


def _attn_a_kernel(sink_ref, q_ref, kc_ref, vc_ref, *rest, has_local):
    if has_local:
        kp_ref, kcur_ref, kn_ref, vp_ref, vcur_ref, vn_ref, o_ref, kbuf, vbuf = rest
    else:
        o_ref, kbuf, vbuf = rest
    g = pl.program_id(1)
    i = pl.program_id(2)
    n_i = pl.num_programs(2)
    tq = q_ref.shape[1]
    nc = kc_ref.shape[1]
    kbuf[0:nc, :] = kc_ref[0]
    vbuf[0:nc, :] = vc_ref[0]
    nk = nc
    if has_local:
        kbuf[nc:nc + WINDOW, :] = kp_ref[0]
        kbuf[nc + WINDOW:nc + WINDOW + tq, :] = kcur_ref[0]
        kbuf[nc + WINDOW + tq:nc + 2 * WINDOW + tq, :] = kn_ref[0]
        vbuf[nc:nc + WINDOW, :] = vp_ref[0]
        vbuf[nc + WINDOW:nc + WINDOW + tq, :] = vcur_ref[0]
        vbuf[nc + WINDOW + tq:nc + 2 * WINDOW + tq, :] = vn_ref[0]
        nk = nc + 2 * WINDOW + tq
        row = lax.broadcasted_iota(jnp.int32, (tq, nk), 0)
        col = lax.broadcasted_iota(jnp.int32, (tq, nk), 1) - nc
        lo_c = jnp.where(i == 0, WINDOW, 0)
        hi_c = jnp.where(i == n_i - 1, WINDOW + tq, 2 * WINDOW + tq)
        valid = (col < 0) | ((col >= row) & (col <= row + 2 * WINDOW) & (col >= lo_c) & (col < hi_c))
    k = kbuf[...]
    v = vbuf[...]
    v1 = jnp.where(lax.broadcasted_iota(jnp.int32, (nk, LANES), 1) < HEAD_DIM, v, jnp.ones_like(v))
    low = lax.broadcasted_iota(jnp.int32, (tq, LANES), 1) < HEAD_DIM
    ss = [lax.dot_general(_masked_q(q_ref, h, low), k, (((1,), (1,)), ((), ())),
                          preferred_element_type=F32) for h in range(4)]
    accs = []
    for h in range(4):
        s = jnp.where(valid, ss[h], -jnp.inf) if has_local else ss[h]
        sink = sink_ref[g * 4 + h] * LOG2_E
        m = jnp.maximum(jnp.max(s, axis=1, keepdims=True), sink)
        p = jnp.exp2(s - m)
        sink_p = jnp.where(low, 0.0, jnp.exp2(sink - m))
        accs.append(_dot(p.astype(BF16), v1) + sink_p)
    for pr in range(2):
        a0, a1 = accs[2 * pr], accs[2 * pr + 1]
        o0 = a0 / pltpu.roll(a0, HEAD_DIM, 1)
        o1 = pltpu.roll(a1, HEAD_DIM, 1) / a1
        o_ref[0, :, pr * LANES:(pr + 1) * LANES] = jnp.where(low, o0, o1).astype(BF16)


def _attn_a(q, kv_c, sink, kv=None):
    b, n, _ = q.shape
    nc = kv_c.shape[1]
    tq = ROW_TILE
    has_local = kv is not None
    in_specs = [pl.BlockSpec(memory_space=pltpu.SMEM),
                pl.BlockSpec((1, tq, 2 * LANES), lambda bi, g, i: (bi, i, g)),
                pl.BlockSpec((1, nc, LANES), lambda bi, g, i: (bi, 0, g)),
                pl.BlockSpec((1, nc, LANES), lambda bi, g, i: (bi, 0, 2 + g))]
    args = [sink, q, kv_c, kv_c]
    nk = nc
    if has_local:
        per = tq // WINDOW
        last = n // WINDOW - 1
        prev_i = lambda i: jnp.maximum(i * per - 1, 0)
        next_i = lambda i: jnp.minimum((i + 1) * per, last)
        for off in (0, 2):
            in_specs += [pl.BlockSpec((1, WINDOW, LANES), lambda bi, g, i, off=off: (bi, prev_i(i), off + g)),
                         pl.BlockSpec((1, tq, LANES), lambda bi, g, i, off=off: (bi, i, off + g)),
                         pl.BlockSpec((1, WINDOW, LANES), lambda bi, g, i, off=off: (bi, next_i(i), off + g))]
            args += [kv, kv, kv]
        nk = nc + 2 * WINDOW + tq
    return pl.pallas_call(
        functools.partial(_attn_a_kernel, has_local=has_local),
        grid=(b, 2, n // tq),
        in_specs=in_specs,
        out_specs=pl.BlockSpec((1, tq, 2 * LANES), lambda bi, g, i: (bi, i, g)),
        out_shape=jax.ShapeDtypeStruct((b, n, 512), BF16),
        scratch_shapes=[pltpu.VMEM((nk, LANES), BF16), pltpu.VMEM((nk, LANES), BF16)],
        compiler_params=_cparams("parallel", "parallel", "parallel"),
        name="attn_a",
    )(*args)


def _out0_kernel(ya_ref, yb_ref, x_ref, mod_ref, w_ref, o_ref):
    half = ya_ref.shape[2]
    y = _dot(ya_ref[0], w_ref[0:half, :]) + _dot(yb_ref[0], w_ref[half:2 * half, :])
    o_ref[0] = x_ref[0] + mod_ref[0, 2:3, :] * y


def _out0(ya, yb, x, mod, w):
    b, n, d = x.shape
    tm = ROW_TILE
    row = lambda bi, i: (bi, i, 0)
    return pl.pallas_call(
        _out0_kernel,
        grid=(b, n // tm),
        in_specs=[pl.BlockSpec((1, tm, ya.shape[2]), row),
                  pl.BlockSpec((1, tm, yb.shape[2]), row),
                  pl.BlockSpec((1, tm, d), row),
                  pl.BlockSpec((1, 6, d), lambda bi, i: (bi, 0, 0)),
                  pl.BlockSpec(w.shape, lambda bi, i: (0, 0))],
        out_specs=pl.BlockSpec((1, tm, d), row),
        out_shape=jax.ShapeDtypeStruct((b, n, d), F32),
        compiler_params=_cparams("parallel", "parallel"),
        name="out0",
    )(ya, yb, x, mod, w)


def _ffn_kernel(x_ref, mod_ref, gain_ref, w1_ref, w2_ref, fg_ref, o_ref, *, final):
    x = x_ref[0]
    h = _norm_mod(x, gain_ref[...], mod_ref[0, 3:4, :], mod_ref[0, 4:5, :]).astype(BF16)
    hid = w2_ref.shape[0]
    ch = hid // 2
    acc = None
    for j in range(2):
        gate = _dot(h, w1_ref[:, j * ch:(j + 1) * ch])
        up = _dot(h, w1_ref[:, hid + j * ch:hid + (j + 1) * ch])
        act = (gate * _sigmoid(gate) * up).astype(BF16)
        part = _dot(act, w2_ref[j * ch:(j + 1) * ch, :])
        acc = part if acc is None else acc + part
    y = x + mod_ref[0, 5:6, :] * acc
    if final:
        ms = jnp.mean(y * y, axis=-1, keepdims=True)
        y = y * lax.rsqrt(ms + NORM_EPS) * fg_ref[...]
    o_ref[0] = y


def _ffn(x, mod, gain, w1, w2, fgain, final):
    b, n, d = x.shape
    tm = ROW_TILE
    row = lambda bi, i: (bi, i, 0)
    const = lambda bi, i: (0, 0)
    return pl.pallas_call(
        functools.partial(_ffn_kernel, final=final),
        grid=(b, n // tm),
        in_specs=[pl.BlockSpec((1, tm, d), row),
                  pl.BlockSpec((1, 6, d), lambda bi, i: (bi, 0, 0)),
                  pl.BlockSpec((1, d), const),
                  pl.BlockSpec(w1.shape, const, pipeline_mode=pl.Buffered(1)),
                  pl.BlockSpec(w2.shape, const, pipeline_mode=pl.Buffered(1)),
                  pl.BlockSpec((1, d), const)],
        out_specs=pl.BlockSpec((1, tm, d), row),
        out_shape=jax.ShapeDtypeStruct((b, n, d), F32),
        compiler_params=_cparams("parallel", "parallel"),
        name="ffn",
    )(x, mod, gain, w1, w2, fgain)


def _proj1_kernel(x_ref, mod_ref, gain_ref, w_ref, o_ref):
    h = _norm_mod(x_ref[0], gain_ref[...], mod_ref[0, 0:1, :], mod_ref[0, 1:2, :])
    o_ref[0] = _dot(h.astype(BF16), w_ref[...])


def _proj1(x, mod, gain, w):
    b, n, d = x.shape
    tm = ROW_TILE
    nw = w.shape[1]
    return pl.pallas_call(
        _proj1_kernel,
        grid=(b, n // tm),
        in_specs=[pl.BlockSpec((1, tm, d), lambda bi, i: (bi, i, 0)),
                  pl.BlockSpec((1, 6, d), lambda bi, i: (bi, 0, 0)),
                  pl.BlockSpec((1, d), lambda bi, i: (0, 0)),
                  pl.BlockSpec((d, nw), lambda bi, i: (0, 0))],
        out_specs=pl.BlockSpec((1, tm, nw), lambda bi, i: (bi, i, 0)),
        out_shape=jax.ShapeDtypeStruct((b, n, nw), F32),
        compiler_params=_cparams("parallel", "parallel"),
        name="proj1",
    )(x, mod, gain, w)


def _prep_kernel(pc_ref, prev_ref, next_ref, mu_ref, w0_ref, w2_ref, a0_ref, a2_ref, g2_ref,
                 kk_ref, ka_ref, rk_ref, seg_ref, tri_ref, *rest, latent):
    if latent:
        pw_ref, ps_ref, v_o, at_o, rt_o, bt_o, kt_o, wl_o, g_o, bonus_o, d_o, ext, tmp = rest
    else:
        v_o, at_o, rt_o, bt_o, kt_o, wl_o, ext, tmp = rest
    i = pl.program_id(1)
    nt = pl.num_programs(1)
    tm = pc_ref.shape[1]
    cw = C_WIDTH
    ext[HALO:HALO + tm, :] = pc_ref[0]
    ext[0:HALO, :] = jnp.where(i > 0, prev_ref[0], 0.0)
    ext[HALO + tm:2 * HALO + tm, :] = jnp.where(i < nt - 1, next_ref[0], 0.0)

    def mixed(lo, hi):
        nb = ext[HALO - 1:HALO - 1 + tm, lo:hi] + ext[HALO + 1:HALO + 1 + tm, lo:hi]
        return ext[HALO:HALO + tm, lo:hi] * mu_ref[0:1, lo:hi] + nb * mu_ref[1:2, lo:hi]

    lora = mixed(3 * cw, C_IN)
    tw = jnp.tanh(lora[:, 0:LANES])
    xa = lora[:, LANES:2 * LANES]
    if latent:
        g_o[0] = _dot(_sigmoid(lora[:, 2 * LANES:3 * LANES]).astype(BF16), g2_ref[...])
    tw_hi, tw_lo = _split(tw, 2)
    xa_b = xa.astype(BF16)
    for d in range(2):
        z = w0_ref[d] + _dot(tw_hi, w2_ref[0, d]) + _dot(tw_lo, w2_ref[0, d]) + _dot(tw_hi, w2_ref[1, d])
        w_log = -(jnp.maximum(-z, 0.0) + jnp.log(1.0 + jnp.exp(-jnp.abs(z)))) - 0.5
        tmp[d] = -jnp.exp(w_log)
        tmp[2 + d] = _sigmoid(a0_ref[d] + _dot(xa_b, a2_ref[d]))
    seg = seg_ref[...]
    n_chunks = tm // CHUNK
    for pb in range(cw // LANES):
        sl = slice(pb * LANES, (pb + 1) * LANES)
        r = mixed(pb * LANES, (pb + 1) * LANES)
        k = mixed(cw + pb * LANES, cw + (pb + 1) * LANES)
        v = mixed(2 * cw + pb * LANES, 2 * cw + (pb + 1) * LANES)
        kk = k * kk_ref[:, sl]
        kk = kk / jnp.maximum(jnp.sqrt(_seg_sum(kk * kk, seg)), 1e-12)
        ksum = jnp.zeros_like(k)
        for d in range(2):
            lw = tmp[d, :, sl]
            a = tmp[2 + d, :, sl]
            k_d = k * (1.0 + (a - 1.0) * ka_ref[:, sl])
            bb = kk * a
            ksum = ksum + k_d
            cs2 = _dot(tri_ref[d], jnp.concatenate(_split(lw, 2), axis=1))
            cs = cs2[:, 0:LANES] + cs2[:, LANES:2 * LANES]
            for j in range(n_chunks):
                last = j * CHUNK if d == 1 else (j + 1) * CHUNK - 1
                wl_o[d, 0, j, :, sl] = jnp.exp(cs[last:last + 1, :])
            e_up = jnp.exp(-cs)
            at_o[d, 0, :, sl] = (-kk * jnp.exp(cs - lw)).astype(BF16)
            rt_o[d, 0, :, sl] = (r * jnp.exp(cs)).astype(BF16)
            bt_o[d, 0, :, sl] = (bb * e_up).astype(BF16)
            kt_o[d, 0, :, sl] = (k_d * e_up).astype(BF16)
        v_o[0, :, sl] = v.astype(BF16)
        if latent:
            bonus_o[0, :, sl] = _seg_sum(r * rk_ref[:, sl] * ksum, seg) * v
    if latent:
        n_tok = nt * tm
        pos = i * tm + lax.broadcasted_iota(jnp.int32, (tm, LANES), 0)
        group1 = lax.broadcasted_iota(jnp.int32, (tm, LANES), 1) >= D_WIDTH // 4
        pooled = []
        for half in range(2):
            lo, hi = C_IN + half * LANES, C_IN + (half + 1) * LANES
            w_small, w_big = POOL_WINDOWS[2 * half], POOL_WINDOWS[2 * half + 1]
            s_small = None
            s_big = None
            for off in range(-(w_big // 2), w_big - w_big // 2):
                piece = ext[HALO + off:HALO + off + tm, lo:hi]
                s_big = piece if s_big is None else s_big + piece
                if -(w_small // 2) <= off < w_small - w_small // 2:
                    s_small = piece if s_small is None else s_small + piece

            def count(w):
                lo_p = jnp.clip(pos - w // 2, 0, n_tok)
                hi_p = jnp.clip(pos + (w - w // 2), 0, n_tok)
                return (hi_p - lo_p).astype(F32)

            mean = jnp.where(group1, s_big / count(w_big), s_small / count(w_small))
            pooled.append((mean - ext[HALO:HALO + tm, lo:hi]).astype(BF16))
        pm = jnp.concatenate(pooled, axis=1)
        d_o[0] = _dot(pm, pw_ref[...]) * ps_ref[...]


def _chunk_cumsum_matrices(tm):
    t = np.arange(tm)[:, None]
    u = np.arange(tm)[None, :]
    same = (t // CHUNK) == (u // CHUNK)
    return np.stack([same & (u <= t), same & (u >= t)]).astype(np.float32)


def _prep(pc, wts, latent):
    b, n, cin = pc.shape
    tm = ROW_TILE
    cw = C_WIDTH
    per = tm // HALO
    last = n // HALO - 1
    n_chunks = tm // CHUNK
    row = lambda bi, i: (bi, i, 0)
    drow = lambda bi, i: (0, bi, i, 0)
    wts = list(wts)
    wts = wts[:10] + [jnp.asarray(_chunk_cumsum_matrices(tm)).astype(BF16)] + (wts[10:] if latent else [])

    def full(a):
        return pl.BlockSpec(a.shape, lambda bi, i, nd=a.ndim: (0,) * nd)

    in_specs = [pl.BlockSpec((1, tm, cin), row),
                pl.BlockSpec((1, HALO, cin), lambda bi, i: (bi, jnp.maximum(i * per - 1, 0), 0)),
                pl.BlockSpec((1, HALO, cin), lambda bi, i: (bi, jnp.minimum((i + 1) * per, last), 0))]
    in_specs += [full(a) for a in wts]
    out_specs = [pl.BlockSpec((1, tm, cw), row)] + [pl.BlockSpec((2, 1, tm, cw), drow)] * 4
    out_shape = [jax.ShapeDtypeStruct((b, n, cw), BF16)] + [jax.ShapeDtypeStruct((2, b, n, cw), BF16)] * 4
    out_specs.append(pl.BlockSpec((2, 1, n_chunks, 1, cw), lambda bi, i: (0, bi, i, 0, 0)))
    out_shape.append(jax.ShapeDtypeStruct((2, b, n // CHUNK, 1, cw), F32))
    if latent:
        out_specs += [pl.BlockSpec((1, tm, cw), row)] * 2 + [pl.BlockSpec((1, tm, D_WIDTH), row)]
        out_shape += [jax.ShapeDtypeStruct((b, n, cw), F32)] * 2 + [jax.ShapeDtypeStruct((b, n, D_WIDTH), F32)]
    return pl.pallas_call(
        functools.partial(_prep_kernel, latent=latent),
        grid=(b, n // tm),
        in_specs=in_specs,
        out_specs=out_specs,
        out_shape=out_shape,
        scratch_shapes=[pltpu.VMEM((tm + 2 * HALO, cin), F32), pltpu.VMEM((4, tm, cw), F32)],
        compiler_params=_cparams("parallel", "parallel"),
        name="prep",
    )(pc, pc, pc, *wts)


def _scan_masks(rev):
    n = 2 * CHUNK
    t = np.arange(n)[:, None]
    u = np.arange(n)[None, :]
    same = (t // CHUNK) == (u // CHUNK)
    before = (u > t) if rev else (u < t)
    masks = [same & before, same & (before | (u == t))]
    s = 1
    while s < CHUNK:
        blk = (t // (2 * s)) == (u // (2 * s))
        t_late = (t % (2 * s) < s) if rev else (t % (2 * s) >= s)
        u_early = (u % (2 * s) >= s) if rev else (u % (2 * s) < s)
        masks.append(blk & t_late & u_early)
        s *= 2
    return np.stack(masks).astype(np.float32)


def _scan_kernel(v_ref, at_ref, rt_ref, bt_ref, kt_ref, wl_ref, h0_ref, msk_ref, *rest,
                 rev, with_y):
    if with_y:
        y_ref, hT_ref, g_sc = rest
    else:
        hT_ref, g_sc = rest
    tb = pl.program_id(2)
    n_tb = pl.num_programs(2)
    n_chunks = v_ref.shape[1] // CHUNK
    n2 = 2 * CHUNK

    @pl.when(tb == 0)
    def _():
        g_sc[...] = h0_ref[0]

    low = lax.broadcasted_iota(jnp.int32, (CHUNK, LANES), 1) < HEAD_DIM
    rid = lax.broadcasted_iota(jnp.int32, (n2, n2), 0)
    cid = lax.broadcasted_iota(jnp.int32, (n2, n2), 1)
    eye = (rid == cid).astype(F32)
    n_levels = msk_ref.shape[0] - 2
    strict = msk_ref[0] > 0
    incl = msk_ref[1] > 0
    nt_dims = (((1,), (1,)), ((), ()))
    tn_dims = (((0,), (0,)), ((), ()))

    def stack(z):
        zero = jnp.zeros_like(z)
        return jnp.concatenate([jnp.where(low, z, zero), jnp.where(low, zero, z)], axis=0)

    order = list(range(n_chunks - 1, -1, -1) if rev else range(n_chunks))
    n_pairs = v_ref.shape[2] // LANES
    chains = [(pp, c) for c in order for pp in range(n_pairs)]
    ch = {}
    for key in chains:
        pp, c = key
        rows = slice(c * CHUNK, (c + 1) * CHUNK)
        lanes = slice(pp * LANES, (pp + 1) * LANES)
        q = ch[key] = {"rows": rows, "lanes": lanes}
        q["v"] = stack(v_ref[0, rows, lanes])
        q["a"] = stack(at_ref[0, 0, rows, lanes])
        q["b"] = stack(bt_ref[0, 0, rows, lanes])
        q["k"] = stack(kt_ref[0, 0, rows, lanes])
        rhs = jnp.concatenate([q["b"], q["k"]], axis=0)
        if with_y:
            q["r"] = stack(rt_ref[0, 0, rows, lanes])
            lhs = jnp.concatenate([q["a"], q["r"]], axis=0)
        else:
            lhs = q["a"]
        gram = lax.dot_general(lhs, rhs, nt_dims, preferred_element_type=F32)
        q["ab"] = jnp.where(strict, gram[0:n2, 0:n2], 0.0)
        q["akv"] = _dot(jnp.where(strict, gram[0:n2, n2:2 * n2], 0.0).astype(BF16), q["v"])
        if with_y:
            q["rb"] = jnp.where(incl, gram[n2:2 * n2, 0:n2], 0.0).astype(BF16)
            q["rkv"] = _dot(jnp.where(incl, gram[n2:2 * n2, n2:2 * n2], 0.0).astype(BF16), q["v"])
        q["t"] = eye + q["ab"] * msk_ref[2]
    for lvl in range(1, n_levels):
        for key in chains:
            q = ch[key]
            q["x"] = _dot((q["ab"] * msk_ref[2 + lvl]).astype(BF16), q["t"].astype(BF16))
        for key in chains:
            q = ch[key]
            q["t"] = q["t"] + _dot(q["t"].astype(BF16), q["x"].astype(BF16))
    for key in chains:
        q = ch[key]
        rows, lanes = q["rows"], q["lanes"]
        au = _dot(q["t"].astype(BF16), jnp.concatenate([q["a"], q["akv"].astype(BF16)], axis=1))
        q["au"] = au.astype(BF16)
        wl = wl_ref[0, 0, key[1], :, lanes]
        mc = lax.dot_general(q["au"], q["b"], tn_dims, preferred_element_type=F32)
        vk = lax.dot_general(q["v"], q["k"], tn_dims, preferred_element_type=F32)
        q["mt"] = (mc[0:LANES] * wl).astype(BF16)
        q["ct"] = (mc[LANES:2 * LANES] + vk) * wl
        if with_y:
            rbau = _dot(q["rb"], q["au"])
            q["rt"] = (q["r"].astype(F32) + rbau[:, 0:LANES]).astype(BF16)
            q["y0"] = rbau[:, LANES:2 * LANES] + q["rkv"]
    g = [g_sc[pp] for pp in range(n_pairs)]
    for key in chains:
        pp, c = key
        q = ch[key]
        g_b = g[pp].astype(BF16)
        if with_y:
            y_st = lax.dot_general(q["rt"], g_b, nt_dims, preferred_element_type=F32) + q["y0"]
            y_ref[0, q["rows"], q["lanes"]] = y_st[0:CHUNK] + y_st[CHUNK:n2]
        g[pp] = g[pp] * wl_ref[0, 0, c, :, q["lanes"]] + _dot(g_b, q["mt"]) + q["ct"]
    for pp in range(n_pairs):
        g_sc[pp] = g[pp]

    @pl.when(tb == n_tb - 1)
    def _():
        hT_ref[0] = g_sc[...]


def _scan(prep, d, h0, rev, with_y):
    v, at, rt, bt, kt, wl = prep
    b, n, cw = v.shape
    tb = min(SCAN_BLOCK, n)
    n_tb = n // tb
    n_pairs = cw // LANES
    sp = SCAN_PAIRS
    wide = sp * LANES
    msk = _scan_masks(rev)
    tmap = (lambda t: n_tb - 1 - t) if rev else (lambda t: t)
    shared = pl.BlockSpec((1, tb, wide), lambda bi, p, t: (bi, tmap(t), p))
    perdir = pl.BlockSpec((1, 1, tb, wide), lambda bi, p, t: (d, bi, tmap(t), p))
    decay = pl.BlockSpec((1, 1, tb // CHUNK, 1, wide), lambda bi, p, t: (d, bi, tmap(t), 0, p))
    state = pl.BlockSpec((1, sp, LANES, LANES), lambda bi, p, t: (bi, p, 0, 0))
    out_specs = [state]
    out_shape = [jax.ShapeDtypeStruct((b, n_pairs, LANES, LANES), F32)]
    if with_y:
        out_specs = [shared] + out_specs
        out_shape = [jax.ShapeDtypeStruct((b, n, cw), F32)] + out_shape
    res = pl.pallas_call(
        functools.partial(_scan_kernel, rev=rev, with_y=with_y),
        grid=(b, n_pairs // sp, n_tb),
        in_specs=[shared, perdir, perdir, perdir, perdir, decay, state,
                  pl.BlockSpec(msk.shape, lambda bi, p, t: (0, 0, 0))],
        out_specs=out_specs,
        out_shape=out_shape,
        scratch_shapes=[pltpu.VMEM((sp, LANES, LANES), F32)],
        compiler_params=_cparams("parallel", "parallel", "arbitrary"),
        name="scan_rev" if rev else "scan_fwd",
    )(v, at, rt, bt, kt, wl, h0, jnp.asarray(msk))
    return (res[0], res[1]) if with_y else (None, res[0])


def _out1_kernel(yf_ref, yb_ref, bonus_ref, g_ref, dp_ref, x_ref, mod_ref, lw_ref, lb_ref, seg_ref,
                 w_ref, o_ref):
    seg = seg_ref[...]
    cw = yf_ref.shape[2]
    acc = _dot(dp_ref[0].astype(BF16), w_ref[cw:cw + D_WIDTH, :])
    for pb in range(cw // LANES):
        sl = slice(pb * LANES, (pb + 1) * LANES)
        y = yf_ref[0, :, sl] + yb_ref[0, :, sl]
        mean = _seg_sum(y, seg) * (1.0 / HEAD_DIM)
        dev = y - mean
        var = _seg_sum(dev * dev, seg) * (1.0 / HEAD_DIM)
        yn = dev * lax.rsqrt(var + LNX_EPS) * lw_ref[:, sl] + lb_ref[:, sl]
        z = (yn + bonus_ref[0, :, sl]) * g_ref[0, :, sl]
        acc = acc + _dot(z.astype(BF16), w_ref[sl, :])
    o_ref[0] = x_ref[0] + mod_ref[0, 2:3, :] * acc


def _out1(yf, yb, bonus, g, dp, x, mod, lnx_w, lnx_b, seg, w):
    b, n, d = x.shape
    tm = ROW_TILE
    cw = yf.shape[2]
    row = lambda bi, i: (bi, i, 0)
    const = lambda bi, i: (0, 0)
    return pl.pallas_call(
        _out1_kernel,
        grid=(b, n // tm),
        in_specs=[pl.BlockSpec((1, tm, cw), row)] * 4
        + [pl.BlockSpec((1, tm, D_WIDTH), row),
           pl.BlockSpec((1, tm, d), row),
           pl.BlockSpec((1, 6, d), lambda bi, i: (bi, 0, 0)),
           pl.BlockSpec((1, cw), const), pl.BlockSpec((1, cw), const),
           pl.BlockSpec((LANES, LANES), const),
           pl.BlockSpec(w.shape, const)],
        out_specs=pl.BlockSpec((1, tm, d), row),
        out_shape=jax.ShapeDtypeStruct((b, n, d), F32),
        compiler_params=_cparams("parallel", "parallel"),
        name="out1",
    )(yf, yb, bonus, g, dp, x, mod, lnx_w, lnx_b, seg, w)


def _rope_tables(n):
    rows = n // GRID_W
    row = jnp.repeat(jnp.arange(rows, dtype=F32), GRID_W)
    col = jnp.tile(jnp.arange(GRID_W, dtype=F32), rows)
    n_freq = HEAD_DIM // 4
    inv = ROPE_THETA ** (-jnp.arange(n_freq, dtype=F32) / n_freq)
    ang = jnp.concatenate([row[:, None] * inv[None, :], col[:, None] * inv[None, :]], axis=-1)
    cos, sin = jnp.cos(ang), jnp.sin(ang)
    cos_t = jnp.tile(cos, (1, LANES // cos.shape[1]))
    sin_t = jnp.tile(jnp.concatenate([-sin, sin], axis=-1), (1, LANES // HEAD_DIM))
    return cos_t, sin_t


def _kv_dup_columns():
    cols = []
    for section in range(4):
        for head in range(2):
            base = AB_Q_COLS + section * 2 * HEAD_DIM + head * HEAD_DIM
            cols += list(range(base, base + HEAD_DIM)) * 2
    return np.concatenate([np.arange(AB_Q_COLS), np.asarray(cols)])


def kernel(x, c, ctx, c_ctx, norm_gain, ada_w, ada_b, ffn_w_in, ffn_w_out, final_gain, ab_w_in, ab_q_gain, ab_k_gain, ab_sink, ab_w_out, cd_w_in, cd_mu, cd_w0, cd_w2, cd_a0, cd_a2, cd_g2, cd_k_k, cd_k_a, cd_r_k, cd_lnx_w, cd_lnx_b, cd_pool_w, cd_pool_scale, cd_w_out):
    b, n, d = x.shape
    nc = ctx.shape[1]
    pad = (-(b + 1)) % 8
    cs = jnp.concatenate([c, c_ctx[None, :], jnp.zeros((pad, d), F32)], axis=0)
    mods = _mods(cs, ada_w, ada_b)
    seg = jnp.asarray(np.kron(np.eye(2), np.ones((HEAD_DIM, HEAD_DIM))).astype(np.float32)).astype(BF16)
    fgain = final_gain.reshape(1, d)

    def layer_mods(i):
        ml = mods[i, :b].reshape(b, 6, d)
        mc = jnp.broadcast_to(mods[i, b].reshape(1, 6, d), (b, 6, d))
        return ml, mc

    ml, mc = layer_mods(0)
    w0 = ab_w_in[0][:, _kv_dup_columns()].astype(BF16)
    gain = norm_gain[0, 0].reshape(1, d)
    qg = jnp.tile(ab_q_gain[0], 2).reshape(1, LANES)
    kg = jnp.tile(ab_k_gain[0], 2).reshape(1, LANES)
    cos_l, sin_l = _rope_tables(n)
    cos_c, sin_c = jnp.ones((nc, LANES), F32), jnp.zeros((nc, LANES), F32)
    q_l, kv_l, vt_l = _proj0(x, ml, gain, w0, cos_l, sin_l, qg, kg, seg)
    q_c, kv_c, vt_c = _proj0(ctx, mc, gain, w0, cos_c, sin_c, qg, kg, seg)
    sink = ab_sink[0]
    w_out0 = ab_w_out[0].astype(BF16)
    gain2 = norm_gain[0, 1].reshape(1, d)
    w1 = ffn_w_in[0].astype(BF16)
    w2 = ffn_w_out[0].astype(BF16)
    yb_l = _attn_b_auto(q_l, kv_c, vt_c, kv_l, vt_l, ab_q_gain[0], ab_k_gain[0])
    xl = _out0(_attn_a(q_l, kv_c, sink, kv_l), yb_l, x, ml, w_out0)
    xc = _out0(_attn_a(q_c, kv_c, sink), _attn_b(q_c, kv_c), ctx, mc, w_out0)
    xl = _ffn(xl, ml, gain2, w1, w2, fgain, False)
    xc = _ffn(xc, mc, gain2, w1, w2, fgain, False)

    ml, mc = layer_mods(1)
    gain = norm_gain[1, 0].reshape(1, d)
    w_in1 = cd_w_in[0].astype(BF16)
    pc_l = _proj1(xl, ml, gain, w_in1)
    pc_c = _proj1(xc, mc, gain, w_in1[:, :C_IN])
    zeros = jnp.zeros((DECAY_LORA_PAD, C_WIDTH), F32)
    w2x = jnp.stack([jnp.concatenate([cd_w2[0, 0], zeros]), jnp.concatenate([zeros, cd_w2[0, 1]])])
    a2x = jnp.stack([jnp.concatenate([cd_a2[0, 0], zeros]), jnp.concatenate([zeros, cd_a2[0, 1]])])
    w2x_hi = w2x.astype(BF16)
    w2x_hl = jnp.stack([w2x_hi, (w2x - w2x_hi.astype(F32)).astype(BF16)])
    wts = [jnp.stack([1.0 - cd_mu[0], 0.5 * cd_mu[0]]), cd_w0[0].reshape(2, 1, C_WIDTH), w2x_hl,
           cd_a0[0].reshape(2, 1, C_WIDTH), a2x.astype(BF16), cd_g2[0].astype(BF16),
           cd_k_k[0].reshape(1, C_WIDTH), cd_k_a[0].reshape(1, C_WIDTH),
           cd_r_k[0].reshape(1, C_WIDTH), seg]
    pool_w = jax.scipy.linalg.block_diag(*[cd_pool_w[0, g] for g in range(4)]).astype(BF16)
    pool_wts = [pool_w, cd_pool_scale[0].reshape(1, D_WIDTH)]
    prep_c = _prep(pc_c, wts, False)
    *prep_l, g_l, bonus_l, dp_l = _prep(pc_l, wts + pool_wts, True)
    h_zero = jnp.zeros((b, C_WIDTH // LANES, LANES, LANES), F32)
    _, h_f = _scan(prep_c, 0, h_zero, False, False)
    _, h_b = _scan(prep_c, 1, h_zero, True, False)
    y_f, _ = _scan(prep_l, 0, h_f, False, True)
    y_b, _ = _scan(prep_l, 1, h_b, True, True)
    xl = _out1(y_f, y_b, bonus_l, g_l, dp_l, xl, ml, cd_lnx_w[0].reshape(1, C_WIDTH),
               cd_lnx_b[0].reshape(1, C_WIDTH), seg, cd_w_out[0].astype(BF16))
    return _ffn(xl, ml, norm_gain[1, 1].reshape(1, d), ffn_w_in[1].astype(BF16),
                ffn_w_out[1].astype(BF16), fgain, True)
```

```python
import functools

import numpy as np
import jax
import jax.numpy as jnp
from jax import lax
from jax.experimental import pallas as pl
from jax.experimental.pallas import tpu as pltpu

F32 = jnp.float32
BF16 = jnp.bfloat16
HIGHEST = lax.Precision.HIGHEST
LOG2_E = 1.4426950408889634
SCORE_BOUND_SAFE = 60.0

D_MODEL = 1024
GRID_W = 64
HEAD_DIM = 64
ROPE_THETA = 10000.0
NORM_EPS = 1e-6
WINDOW = 128
AB_Q_COLS = 1024
C_WIDTH = 768
C_IN = 2688
D_WIDTH = 256
CD_IN = C_IN + D_WIDTH
LNX_EPS = 64e-5
FFN_HIDDEN = 2816
POOL_WINDOWS = (2, 4, 8, 16)
DECAY_LORA_PAD = 64

LANES = 128
ROW_TILE = 256
HALO = 8
CHUNK = 64
SCAN_BLOCK = 512
SCAN_QUADS = 3
VMEM_LIMIT = 56 * 1024 * 1024


def _cparams(*sem):
    return pltpu.CompilerParams(dimension_semantics=sem, vmem_limit_bytes=VMEM_LIMIT)


def _dot(a, b):
    return jnp.dot(a, b, preferred_element_type=F32)


def _dot32(a, b):
    return jnp.dot(a, b, preferred_element_type=F32, precision=HIGHEST)


def _split(a, terms):
    pieces = []
    for _ in range(terms - 1):
        hi = a.astype(BF16)
        pieces.append(hi)
        a = a - hi.astype(F32)
    pieces.append(a.astype(BF16))
    return pieces


def _seg_sum(z, seg):
    hi, lo = _split(z, 2)
    return _dot(hi, seg) + _dot(lo, seg)


def _sigmoid(x):
    return 1.0 / (1.0 + jnp.exp(-x))


def _norm_mod(x, gain, shift, scale):
    ms = jnp.mean(x * x, axis=-1, keepdims=True)
    return (x * lax.rsqrt(ms + NORM_EPS) * gain) * (1.0 + scale) + shift


def _mods_kernel(c_ref, w_ref, b_ref, o_ref):
    c = c_ref[...]
    o_ref[0] = _dot32(c * _sigmoid(c), w_ref[0]) + b_ref[0]


def _mods(cs, ada_w, ada_b):
    depth, d, n6 = ada_w.shape
    tn = 768
    rows = cs.shape[0]
    return pl.pallas_call(
        _mods_kernel,
        grid=(depth, n6 // tn),
        in_specs=[pl.BlockSpec((rows, d), lambda l, j: (0, 0)),
                  pl.BlockSpec((1, d, tn), lambda l, j: (l, 0, j)),
                  pl.BlockSpec((1, 1, tn), lambda l, j: (l, 0, j))],
        out_specs=pl.BlockSpec((1, rows, tn), lambda l, j: (l, 0, j)),
        out_shape=jax.ShapeDtypeStruct((depth, rows, n6), F32),
        compiler_params=_cparams("arbitrary", "arbitrary"),
        name="mods",
    )(cs, ada_w, ada_b.reshape(depth, 1, n6))


def _proj0_kernel(x_ref, mod_ref, gain_ref, w_ref, cos_ref, sin_ref, qg_ref, kg_ref, seg_ref,
                  q_ref, kv_ref, vt_ref):
    h = _norm_mod(x_ref[0], gain_ref[...], mod_ref[0, 0:1, :], mod_ref[0, 1:2, :])
    p = _dot(h.astype(BF16), w_ref[...])
    tm = p.shape[0]
    cos = cos_ref[...]
    sin = sin_ref[...]
    lane = lax.broadcasted_iota(jnp.int32, (tm, LANES), 1)
    first_half = (lane & (HEAD_DIM - 1)) < HEAD_DIM // 2
    seg = seg_ref[...]

    def rope(z):
        partner = jnp.where(first_half, pltpu.roll(z, LANES - HEAD_DIM // 2, 1),
                            pltpu.roll(z, HEAD_DIM // 2, 1))
        return z * cos + partner * sin

    def head_norm(z, g):
        ms = _seg_sum(z * z, seg) * (1.0 / HEAD_DIM)
        return z * lax.rsqrt(ms + NORM_EPS) * g

    scale = HEAD_DIM ** -0.5 * LOG2_E
    for blk in range(8):
        z = p[:, blk * LANES:(blk + 1) * LANES]
        if blk >= 4:
            z = head_norm(z, qg_ref[...])
        q_ref[0, :, blk * LANES:(blk + 1) * LANES] = (rope(z) * scale).astype(BF16)
    for blk in range(8):
        z = p[:, AB_Q_COLS + blk * LANES:AB_Q_COLS + (blk + 1) * LANES]
        if blk in (4, 5):
            z = head_norm(z, kg_ref[...])
        if blk in (0, 1, 4, 5):
            z = rope(z)
        kv_ref[0, :, blk * LANES:(blk + 1) * LANES] = z.astype(BF16)
        if blk in (6, 7):
            zt = z.T
            top = lax.broadcasted_iota(jnp.int32, zt.shape, 0) < HEAD_DIM
            vt_ref[0, blk - 6] = jnp.where(top, zt, 1.0).astype(BF16)


def _proj0(x, mod, gain, w, cos, sin, qg, kg, seg):
    b, n, d = x.shape
    tm = ROW_TILE
    nw = w.shape[1]
    const = lambda bi, i: (0, 0)
    return pl.pallas_call(
        _proj0_kernel,
        grid=(b, n // tm),
        in_specs=[pl.BlockSpec((1, tm, d), lambda bi, i: (bi, i, 0)),
                  pl.BlockSpec((1, 6, d), lambda bi, i: (bi, 0, 0)),
                  pl.BlockSpec((1, d), const),
                  pl.BlockSpec((d, nw), const),
                  pl.BlockSpec((tm, LANES), lambda bi, i: (i, 0)),
                  pl.BlockSpec((tm, LANES), lambda bi, i: (i, 0)),
                  pl.BlockSpec((1, LANES), const),
                  pl.BlockSpec((1, LANES), const),
                  pl.BlockSpec((LANES, LANES), const)],
        out_specs=[pl.BlockSpec((1, tm, 1024), lambda bi, i: (bi, i, 0)),
                   pl.BlockSpec((1, tm, 1024), lambda bi, i: (bi, i, 0)),
                   pl.BlockSpec((1, 2, LANES, tm), lambda bi, i: (bi, 0, 0, i))],
        out_shape=[jax.ShapeDtypeStruct((b, n, 1024), BF16),
                   jax.ShapeDtypeStruct((b, n, 1024), BF16),
                   jax.ShapeDtypeStruct((b, 2, LANES, n), BF16)],
        compiler_params=_cparams("parallel", "parallel"),
        name="proj0",
    )(x, mod, gain, w, cos, sin, qg, kg, seg)


def _masked_q(q_ref, h, low):
    q2 = q_ref[0, :, (h // 2) * LANES:(h // 2 + 1) * LANES]
    keep = low if h % 2 == 0 else jnp.logical_not(low)
    return jnp.where(keep, q2, jnp.zeros_like(q2))


def _attn_b_kernel(q_ref, kc_ref, vc_ref, *rest, n_kt, has_latent):
    if has_latent:
        k_ref, v_ref, o_ref, m_sc, acc_sc = rest
    else:
        o_ref, m_sc, acc_sc = rest
    kt = pl.program_id(3)
    tq = q_ref.shape[1]
    low = lax.broadcasted_iota(jnp.int32, (tq, LANES), 1) < HEAD_DIM

    def update(k, v):
        tk = k.shape[0]
        v1 = jnp.where(lax.broadcasted_iota(jnp.int32, (tk, LANES), 1) < HEAD_DIM, v, jnp.ones_like(v))
        ss = [lax.dot_general(_masked_q(q_ref, h, low), k, (((1,), (1,)), ((), ())),
                              preferred_element_type=F32) for h in range(4)]
        for h in range(4):
            m_prev = m_sc[h]
            m_new = jnp.maximum(m_prev, jnp.max(ss[h], axis=1, keepdims=True))
            p = jnp.exp2(ss[h] - jnp.concatenate([m_new] * (tk // LANES), axis=1)).astype(BF16)
            acc_sc[h] = jnp.exp2(m_prev - m_new) * acc_sc[h] + _dot(p, v1)
            m_sc[h] = m_new

    @pl.when(kt == 0)
    def _():
        m_sc[...] = jnp.full(m_sc.shape, -jnp.inf, F32)
        acc_sc[...] = jnp.zeros(acc_sc.shape, F32)
        update(kc_ref[0], vc_ref[0])

    if has_latent:
        update(k_ref[0], v_ref[0])

    @pl.when(kt == n_kt - 1)
    def _():
        for pr in range(2):
            a0 = acc_sc[2 * pr]
            a1 = acc_sc[2 * pr + 1]
            o0 = a0 / pltpu.roll(a0, HEAD_DIM, 1)
            o1 = pltpu.roll(a1, HEAD_DIM, 1) / a1
            o_ref[0, :, pr * LANES:(pr + 1) * LANES] = jnp.where(low, o0, o1).astype(BF16)


def _attn_b(q, kv_c, kv=None, tk=1024):
    b, n, _ = q.shape
    nc = kv_c.shape[1]
    tq = ROW_TILE
    has_latent = kv is not None
    n_kt = kv.shape[1] // tk if has_latent else 1
    in_specs = [pl.BlockSpec((1, tq, 2 * LANES), lambda bi, g, i, j: (bi, i, 2 + g)),
                pl.BlockSpec((1, nc, LANES), lambda bi, g, i, j: (bi, 0, 4 + g)),
                pl.BlockSpec((1, nc, LANES), lambda bi, g, i, j: (bi, 0, 6 + g))]
    args = [q, kv_c, kv_c]
    if has_latent:
        in_specs += [pl.BlockSpec((1, tk, LANES), lambda bi, g, i, j: (bi, j, 4 + g)),
                     pl.BlockSpec((1, tk, LANES), lambda bi, g, i, j: (bi, j, 6 + g))]
        args += [kv, kv]
    return pl.pallas_call(
        functools.partial(_attn_b_kernel, n_kt=n_kt, has_latent=has_latent),
        grid=(b, 2, n // tq, n_kt),
        in_specs=in_specs,
        out_specs=pl.BlockSpec((1, tq, 2 * LANES), lambda bi, g, i, j: (bi, i, g)),
        out_shape=jax.ShapeDtypeStruct((b, n, 512), BF16),
        scratch_shapes=[pltpu.VMEM((4, tq, LANES), F32), pltpu.VMEM((4, tq, LANES), F32)],
        compiler_params=_cparams("parallel", "parallel", "parallel", "arbitrary"),
        name="attn_b",
    )(*args)


def _attn_bt_kernel(q_ref, kc_ref, vtc_ref, k_ref, vt_ref, o_ref, acc_sc, *, n_kt):
    kt = pl.program_id(3)
    tq = q_ref.shape[1]
    low = lax.broadcasted_iota(jnp.int32, (tq, LANES), 1) < HEAD_DIM

    def update(k, vt):
        sts = [lax.dot_general(k, _masked_q(q_ref, h, low), (((1,), (1,)), ((), ())),
                               preferred_element_type=F32) for h in range(4)]
        for h in range(4):
            acc_sc[h] = acc_sc[h] + _dot(vt, jnp.exp2(sts[h]).astype(BF16))

    @pl.when(kt == 0)
    def _():
        acc_sc[...] = jnp.zeros(acc_sc.shape, F32)
        update(kc_ref[0], vtc_ref[0, 0])

    update(k_ref[0], vt_ref[0, 0])

    @pl.when(kt == n_kt - 1)
    def _():
        ot = jnp.concatenate([acc_sc[h, 0:HEAD_DIM, :] / acc_sc[h, HEAD_DIM:2 * HEAD_DIM, :]
                              for h in range(4)], axis=0)
        o_ref[0] = ot.T.astype(BF16)


def _attn_bt(q, kv_c, vt_c, kv, vt, tk=4096):
    b, n, _ = q.shape
    nc = kv_c.shape[1]
    tq = ROW_TILE
    tk = min(tk, kv.shape[1])
    n_kt = kv.shape[1] // tk
    return pl.pallas_call(
        functools.partial(_attn_bt_kernel, n_kt=n_kt),
        grid=(b, 2, n // tq, n_kt),
        in_specs=[pl.BlockSpec((1, tq, 2 * LANES), lambda bi, g, i, j: (bi, i, 2 + g)),
                  pl.BlockSpec((1, nc, LANES), lambda bi, g, i, j: (bi, 0, 4 + g)),
                  pl.BlockSpec((1, 1, LANES, nc), lambda bi, g, i, j: (bi, g, 0, 0)),
                  pl.BlockSpec((1, tk, LANES), lambda bi, g, i, j: (bi, j, 4 + g)),
                  pl.BlockSpec((1, 1, LANES, tk), lambda bi, g, i, j: (bi, g, 0, j))],
        out_specs=pl.BlockSpec((1, tq, 2 * LANES), lambda bi, g, i, j: (bi, i, g)),
        out_shape=jax.ShapeDtypeStruct((b, n, 512), BF16),
        scratch_shapes=[pltpu.VMEM((4, LANES, tq), F32)],
        compiler_params=_cparams("parallel", "parallel", "parallel", "arbitrary"),
        name="attn_b_bounded",
    )(q, kv_c, vt_c, kv, vt)


def _attn_b_auto(q, kv_c, vt_c, kv, vt, q_gain, k_gain):
    bound = (1.02 * HEAD_DIM * HEAD_DIM ** -0.5 * LOG2_E) * jnp.max(jnp.abs(q_gain)) * jnp.max(jnp.abs(k_gain))
    return lax.cond(bound <= SCORE_BOUND_SAFE,
                    lambda: _attn_bt(q, kv_c, vt_c, kv, vt),
                    lambda: _attn_b(q, kv_c, kv))


def _attn_a_kernel(sink_ref, q_ref, kc_ref, vc_ref, *rest, has_local):
    if has_local:
        kp_ref, kcur_ref, kn_ref, vp_ref, vcur_ref, vn_ref, o_ref, kbuf, vbuf = rest
    else:
        o_ref, kbuf, vbuf = rest
    g = pl.program_id(1)
    i = pl.program_id(2)
    n_i = pl.num_programs(2)
    tq = q_ref.shape[1]
    nc = kc_ref.shape[1]
    kbuf[0:nc, :] = kc_ref[0]
    vbuf[0:nc, :] = vc_ref[0]
    nk = nc
    if has_local:
        kbuf[nc:nc + WINDOW, :] = kp_ref[0]
        kbuf[nc + WINDOW:nc + WINDOW + tq, :] = kcur_ref[0]
        kbuf[nc + WINDOW + tq:nc + 2 * WINDOW + tq, :] = kn_ref[0]
        vbuf[nc:nc + WINDOW, :] = vp_ref[0]
        vbuf[nc + WINDOW:nc + WINDOW + tq, :] = vcur_ref[0]
        vbuf[nc + WINDOW + tq:nc + 2 * WINDOW + tq, :] = vn_ref[0]
        nk = nc + 2 * WINDOW + tq
        row = lax.broadcasted_iota(jnp.int32, (tq, nk), 0)
        col = lax.broadcasted_iota(jnp.int32, (tq, nk), 1) - nc
        lo_c = jnp.where(i == 0, WINDOW, 0)
        hi_c = jnp.where(i == n_i - 1, WINDOW + tq, 2 * WINDOW + tq)
        valid = (col < 0) | ((col >= row) & (col <= row + 2 * WINDOW) & (col >= lo_c) & (col < hi_c))
    k = kbuf[...]
    v = vbuf[...]
    v1 = jnp.where(lax.broadcasted_iota(jnp.int32, (nk, LANES), 1) < HEAD_DIM, v, jnp.ones_like(v))
    low = lax.broadcasted_iota(jnp.int32, (tq, LANES), 1) < HEAD_DIM
    ss = [lax.dot_general(_masked_q(q_ref, h, low), k, (((1,), (1,)), ((), ())),
                          preferred_element_type=F32) for h in range(4)]
    accs = []
    for h in range(4):
        s = jnp.where(valid, ss[h], -jnp.inf) if has_local else ss[h]
        sink = sink_ref[g * 4 + h] * LOG2_E
        m = jnp.maximum(jnp.max(s, axis=1, keepdims=True), sink)
        p = jnp.exp2(s - m)
        sink_p = jnp.where(low, 0.0, jnp.exp2(sink - m))
        accs.append(_dot(p.astype(BF16), v1) + sink_p)
    for pr in range(2):
        a0, a1 = accs[2 * pr], accs[2 * pr + 1]
        o0 = a0 / pltpu.roll(a0, HEAD_DIM, 1)
        o1 = pltpu.roll(a1, HEAD_DIM, 1) / a1
        o_ref[0, :, pr * LANES:(pr + 1) * LANES] = jnp.where(low, o0, o1).astype(BF16)


def _attn_a(q, kv_c, sink, kv=None):
    b, n, _ = q.shape
    nc = kv_c.shape[1]
    tq = ROW_TILE
    has_local = kv is not None
    in_specs = [pl.BlockSpec(memory_space=pltpu.SMEM),
                pl.BlockSpec((1, tq, 2 * LANES), lambda bi, g, i: (bi, i, g)),
                pl.BlockSpec((1, nc, LANES), lambda bi, g, i: (bi, 0, g)),
                pl.BlockSpec((1, nc, LANES), lambda bi, g, i: (bi, 0, 2 + g))]
    args = [sink, q, kv_c, kv_c]
    nk = nc
    if has_local:
        per = tq // WINDOW
        last = n // WINDOW - 1
        prev_i = lambda i: jnp.maximum(i * per - 1, 0)
        next_i = lambda i: jnp.minimum((i + 1) * per, last)
        for off in (0, 2):
            in_specs += [pl.BlockSpec((1, WINDOW, LANES), lambda bi, g, i, off=off: (bi, prev_i(i), off + g)),
                         pl.BlockSpec((1, tq, LANES), lambda bi, g, i, off=off: (bi, i, off + g)),
                         pl.BlockSpec((1, WINDOW, LANES), lambda bi, g, i, off=off: (bi, next_i(i), off + g))]
            args += [kv, kv, kv]
        nk = nc + 2 * WINDOW + tq
    return pl.pallas_call(
        functools.partial(_attn_a_kernel, has_local=has_local),
        grid=(b, 2, n // tq),
        in_specs=in_specs,
        out_specs=pl.BlockSpec((1, tq, 2 * LANES), lambda bi, g, i: (bi, i, g)),
        out_shape=jax.ShapeDtypeStruct((b, n, 512), BF16),
        scratch_shapes=[pltpu.VMEM((nk, LANES), BF16), pltpu.VMEM((nk, LANES), BF16)],
        compiler_params=_cparams("parallel", "parallel", "parallel"),
        name="attn_a",
    )(*args)


def _out0_kernel(ya_ref, yb_ref, x_ref, mod_ref, w_ref, o_ref):
    half = ya_ref.shape[2]
    y = _dot(ya_ref[0], w_ref[0:half, :]) + _dot(yb_ref[0], w_ref[half:2 * half, :])
    o_ref[0] = x_ref[0] + mod_ref[0, 2:3, :] * y


def _out0(ya, yb, x, mod, w):
    b, n, d = x.shape
    tm = ROW_TILE
    row = lambda bi, i: (bi, i, 0)
    return pl.pallas_call(
        _out0_kernel,
        grid=(b, n // tm),
        in_specs=[pl.BlockSpec((1, tm, ya.shape[2]), row),
                  pl.BlockSpec((1, tm, yb.shape[2]), row),
                  pl.BlockSpec((1, tm, d), row),
                  pl.BlockSpec((1, 6, d), lambda bi, i: (bi, 0, 0)),
                  pl.BlockSpec(w.shape, lambda bi, i: (0, 0))],
        out_specs=pl.BlockSpec((1, tm, d), row),
        out_shape=jax.ShapeDtypeStruct((b, n, d), F32),
        compiler_params=_cparams("parallel", "parallel"),
        name="out0",
    )(ya, yb, x, mod, w)


def _ffn_kernel(x_ref, mod_ref, gain_ref, w1_ref, w2_ref, fg_ref, o_ref, *, final):
    x = x_ref[0]
    h = _norm_mod(x, gain_ref[...], mod_ref[0, 3:4, :], mod_ref[0, 4:5, :]).astype(BF16)
    hid = w2_ref.shape[0]
    ch = hid // 2
    acc = None
    for j in range(2):
        gate = _dot(h, w1_ref[:, j * ch:(j + 1) * ch])
        up = _dot(h, w1_ref[:, hid + j * ch:hid + (j + 1) * ch])
        act = (gate * _sigmoid(gate) * up).astype(BF16)
        part = _dot(act, w2_ref[j * ch:(j + 1) * ch, :])
        acc = part if acc is None else acc + part
    y = x + mod_ref[0, 5:6, :] * acc
    if final:
        ms = jnp.mean(y * y, axis=-1, keepdims=True)
        y = y * lax.rsqrt(ms + NORM_EPS) * fg_ref[...]
    o_ref[0] = y


def _ffn(x, mod, gain, w1, w2, fgain, final):
    b, n, d = x.shape
    tm = ROW_TILE
    row = lambda bi, i: (bi, i, 0)
    const = lambda bi, i: (0, 0)
    return pl.pallas_call(
        functools.partial(_ffn_kernel, final=final),
        grid=(b, n // tm),
        in_specs=[pl.BlockSpec((1, tm, d), row),
                  pl.BlockSpec((1, 6, d), lambda bi, i: (bi, 0, 0)),
                  pl.BlockSpec((1, d), const),
                  pl.BlockSpec(w1.shape, const, pipeline_mode=pl.Buffered(1)),
                  pl.BlockSpec(w2.shape, const, pipeline_mode=pl.Buffered(1)),
                  pl.BlockSpec((1, d), const)],
        out_specs=pl.BlockSpec((1, tm, d), row),
        out_shape=jax.ShapeDtypeStruct((b, n, d), F32),
        compiler_params=_cparams("parallel", "parallel"),
        name="ffn",
    )(x, mod, gain, w1, w2, fgain)


def _proj1_kernel(x_ref, mod_ref, gain_ref, w_ref, o_ref):
    h = _norm_mod(x_ref[0], gain_ref[...], mod_ref[0, 0:1, :], mod_ref[0, 1:2, :])
    o_ref[0] = _dot(h.astype(BF16), w_ref[...])


def _proj1(x, mod, gain, w):
    b, n, d = x.shape
    tm = ROW_TILE
    nw = w.shape[1]
    return pl.pallas_call(
        _proj1_kernel,
        grid=(b, n // tm),
        in_specs=[pl.BlockSpec((1, tm, d), lambda bi, i: (bi, i, 0)),
                  pl.BlockSpec((1, 6, d), lambda bi, i: (bi, 0, 0)),
                  pl.BlockSpec((1, d), lambda bi, i: (0, 0)),
                  pl.BlockSpec((d, nw), lambda bi, i: (0, 0))],
        out_specs=pl.BlockSpec((1, tm, nw), lambda bi, i: (bi, i, 0)),
        out_shape=jax.ShapeDtypeStruct((b, n, nw), F32),
        compiler_params=_cparams("parallel", "parallel"),
        name="proj1",
    )(x, mod, gain, w)


def _prep_kernel(pc_ref, prev_ref, next_ref, mu_ref, w0_ref, w2_ref, a0_ref, a2_ref, g2_ref,
                 kk_ref, ka_ref, rk_ref, seg_ref, tri_ref, *rest, latent):
    if latent:
        pw_ref, ps_ref, v_o, at_o, rt_o, bt_o, kt_o, wl_o, g_o, bonus_o, d_o, ext, tmp = rest
    else:
        v_o, at_o, rt_o, bt_o, kt_o, wl_o, ext, tmp = rest
    i = pl.program_id(1)
    nt = pl.num_programs(1)
    tm = pc_ref.shape[1]
    cw = C_WIDTH
    ext[HALO:HALO + tm, :] = pc_ref[0]
    ext[0:HALO, :] = jnp.where(i > 0, prev_ref[0], 0.0)
    ext[HALO + tm:2 * HALO + tm, :] = jnp.where(i < nt - 1, next_ref[0], 0.0)

    def mixed(lo, hi):
        nb = ext[HALO - 1:HALO - 1 + tm, lo:hi] + ext[HALO + 1:HALO + 1 + tm, lo:hi]
        return ext[HALO:HALO + tm, lo:hi] * mu_ref[0:1, lo:hi] + nb * mu_ref[1:2, lo:hi]

    lora = mixed(3 * cw, C_IN)
    tw = jnp.tanh(lora[:, 0:LANES])
    xa = lora[:, LANES:2 * LANES]
    if latent:
        g_o[0] = _dot(_sigmoid(lora[:, 2 * LANES:3 * LANES]).astype(BF16), g2_ref[...])
    tw_hi, tw_lo = _split(tw, 2)
    xa_b = xa.astype(BF16)
    for d in range(2):
        z = w0_ref[d] + _dot(tw_hi, w2_ref[0, d]) + _dot(tw_lo, w2_ref[0, d]) + _dot(tw_hi, w2_ref[1, d])
        w_log = -(jnp.maximum(-z, 0.0) + jnp.log(1.0 + jnp.exp(-jnp.abs(z)))) - 0.5
        tmp[d] = -jnp.exp(w_log)
        tmp[2 + d] = _sigmoid(a0_ref[d] + _dot(xa_b, a2_ref[d]))
    seg = seg_ref[...]
    n_chunks = tm // CHUNK
    for pb in range(cw // LANES):
        sl = slice(pb * LANES, (pb + 1) * LANES)
        r = mixed(pb * LANES, (pb + 1) * LANES)
        k = mixed(cw + pb * LANES, cw + (pb + 1) * LANES)
        v = mixed(2 * cw + pb * LANES, 2 * cw + (pb + 1) * LANES)
        kk = k * kk_ref[:, sl]
        kk = kk / jnp.maximum(jnp.sqrt(_seg_sum(kk * kk, seg)), 1e-12)
        ksum = jnp.zeros_like(k)
        for d in range(2):
            lw = tmp[d, :, sl]
            a = tmp[2 + d, :, sl]
            k_d = k * (1.0 + (a - 1.0) * ka_ref[:, sl])
            bb = kk * a
            ksum = ksum + k_d
            cs2 = _dot(tri_ref[d], jnp.concatenate(_split(lw, 2), axis=1))
            cs = cs2[:, 0:LANES] + cs2[:, LANES:2 * LANES]
            for j in range(n_chunks):
                last = j * CHUNK if d == 1 else (j + 1) * CHUNK - 1
                wl_o[d, 0, j, :, sl] = jnp.exp(cs[last:last + 1, :])
            e_up = jnp.exp(-cs)
            at_o[d, 0, :, sl] = (-kk * jnp.exp(cs - lw)).astype(BF16)
            rt_o[d, 0, :, sl] = (r * jnp.exp(cs)).astype(BF16)
            bt_o[d, 0, :, sl] = (bb * e_up).astype(BF16)
            kt_o[d, 0, :, sl] = (k_d * e_up).astype(BF16)
        v_o[0, :, sl] = v.astype(BF16)
        if latent:
            bonus_o[0, :, sl] = _seg_sum(r * rk_ref[:, sl] * ksum, seg) * v
    if latent:
        n_tok = nt * tm
        pos = i * tm + lax.broadcasted_iota(jnp.int32, (tm, LANES), 0)
        group1 = lax.broadcasted_iota(jnp.int32, (tm, LANES), 1) >= D_WIDTH // 4
        pooled = []
        for half in range(2):
            lo, hi = C_IN + half * LANES, C_IN + (half + 1) * LANES
            w_small, w_big = POOL_WINDOWS[2 * half], POOL_WINDOWS[2 * half + 1]
            s_small = None
            s_big = None
            for off in range(-(w_big // 2), w_big - w_big // 2):
                piece = ext[HALO + off:HALO + off + tm, lo:hi]
                s_big = piece if s_big is None else s_big + piece
                if -(w_small // 2) <= off < w_small - w_small // 2:
                    s_small = piece if s_small is None else s_small + piece

            def count(w):
                lo_p = jnp.clip(pos - w // 2, 0, n_tok)
                hi_p = jnp.clip(pos + (w - w // 2), 0, n_tok)
                return (hi_p - lo_p).astype(F32)

            mean = jnp.where(group1, s_big / count(w_big), s_small / count(w_small))
            pooled.append((mean - ext[HALO:HALO + tm, lo:hi]).astype(BF16))
        pm = jnp.concatenate(pooled, axis=1)
        d_o[0] = _dot(pm, pw_ref[...]) * ps_ref[...]


def _chunk_cumsum_matrices(tm):
    t = np.arange(tm)[:, None]
    u = np.arange(tm)[None, :]
    same = (t // CHUNK) == (u // CHUNK)
    return np.stack([same & (u <= t), same & (u >= t)]).astype(np.float32)


def _prep(pc, wts, latent):
    b, n, cin = pc.shape
    tm = ROW_TILE
    cw = C_WIDTH
    per = tm // HALO
    last = n // HALO - 1
    n_chunks = tm // CHUNK
    row = lambda bi, i: (bi, i, 0)
    drow = lambda bi, i: (0, bi, i, 0)
    wts = list(wts)
    wts = wts[:10] + [jnp.asarray(_chunk_cumsum_matrices(tm)).astype(BF16)] + (wts[10:] if latent else [])

    def full(a):
        return pl.BlockSpec(a.shape, lambda bi, i, nd=a.ndim: (0,) * nd)

    in_specs = [pl.BlockSpec((1, tm, cin), row),
                pl.BlockSpec((1, HALO, cin), lambda bi, i: (bi, jnp.maximum(i * per - 1, 0), 0)),
                pl.BlockSpec((1, HALO, cin), lambda bi, i: (bi, jnp.minimum((i + 1) * per, last), 0))]
    in_specs += [full(a) for a in wts]
    out_specs = [pl.BlockSpec((1, tm, cw), row)] + [pl.BlockSpec((2, 1, tm, cw), drow)] * 4
    out_shape = [jax.ShapeDtypeStruct((b, n, cw), BF16)] + [jax.ShapeDtypeStruct((2, b, n, cw), BF16)] * 4
    out_specs.append(pl.BlockSpec((2, 1, n_chunks, 1, cw), lambda bi, i: (0, bi, i, 0, 0)))
    out_shape.append(jax.ShapeDtypeStruct((2, b, n // CHUNK, 1, cw), F32))
    if latent:
        out_specs += [pl.BlockSpec((1, tm, cw), row)] * 2 + [pl.BlockSpec((1, tm, D_WIDTH), row)]
        out_shape += [jax.ShapeDtypeStruct((b, n, cw), F32)] * 2 + [jax.ShapeDtypeStruct((b, n, D_WIDTH), F32)]
    return pl.pallas_call(
        functools.partial(_prep_kernel, latent=latent),
        grid=(b, n // tm),
        in_specs=in_specs,
        out_specs=out_specs,
        out_shape=out_shape,
        scratch_shapes=[pltpu.VMEM((tm + 2 * HALO, cin), F32), pltpu.VMEM((4, tm, cw), F32)],
        compiler_params=_cparams("parallel", "parallel"),
        name="prep",
    )(pc, pc, pc, *wts)


QUAD = 4 * HEAD_DIM


def _quad_masks(rev):
    t = np.arange(CHUNK)[:, None]
    u = np.arange(CHUNK)[None, :]
    before = (u > t) if rev else (u < t)
    masks = [before, before | (u == t), u == t]
    s = 1
    while s < CHUNK:
        blk = (t // (2 * s)) == (u // (2 * s))
        t_late = (t % (2 * s) < s) if rev else (t % (2 * s) >= s)
        u_early = (u % (2 * s) >= s) if rev else (u % (2 * s) < s)
        masks.append(blk & t_late & u_early)
        s *= 2
    masks = np.stack([np.tile(m, (1, QUAD // CHUNK)) for m in masks]).astype(np.float32)
    hid = np.arange(QUAD) // HEAD_DIM
    return masks, (hid[:, None] == hid[None, :]).astype(np.float32)


def _scanq_kernel(v_ref, at_ref, rt_ref, bt_ref, kt_ref, wl_ref, h0_ref, msk_ref, bd_ref, *rest,
                  rev, with_y):
    if with_y:
        y_ref, hT_ref, g_sc = rest
    else:
        hT_ref, g_sc = rest
    tb = pl.program_id(2)
    n_tb = pl.num_programs(2)
    n_chunks = v_ref.shape[1] // CHUNK
    n_quads = v_ref.shape[2] // QUAD
    reps = QUAD // CHUNK

    @pl.when(tb == 0)
    def _():
        g_sc[...] = h0_ref[0]

    bd_mask = bd_ref[...] > 0
    strict = msk_ref[0] > 0
    incl = msk_ref[1] > 0
    eye4 = msk_ref[2]
    n_levels = msk_ref.shape[0] - 3
    nt_dims = (((1,), (1,)), ((), ()))
    tn_dims = (((0,), (0,)), ((), ()))

    def bd(x4):
        return jnp.where(bd_mask, jnp.concatenate([x4] * reps, axis=0), jnp.zeros((QUAD, QUAD), BF16))

    def fold(m):
        m = jnp.where(bd_mask, m, 0.0)
        out = m[0:CHUNK]
        for r in range(1, reps):
            out = out + m[r * CHUNK:(r + 1) * CHUNK]
        return out

    order = list(range(n_chunks - 1, -1, -1) if rev else range(n_chunks))
    chains = [(qq, c) for c in order for qq in range(n_quads)]
    ch = {}
    for key in chains:
        qq, c = key
        rows = slice(c * CHUNK, (c + 1) * CHUNK)
        lanes = slice(qq * QUAD, (qq + 1) * QUAD)
        q = ch[key] = {"rows": rows, "lanes": lanes}
        q["v"] = v_ref[0, rows, lanes]
        q["a"] = at_ref[0, 0, rows, lanes]
        q["b"] = bt_ref[0, 0, rows, lanes]
        q["k"] = kt_ref[0, 0, rows, lanes]
        q["bdv"] = bd(q["v"])
        rhs = jnp.concatenate([bd(q["b"]), bd(q["k"])], axis=0)
        if with_y:
            q["r"] = rt_ref[0, 0, rows, lanes]
            lhs = jnp.concatenate([q["a"], q["r"]], axis=0)
        else:
            lhs = q["a"]
        gram = lax.dot_general(lhs, rhs, nt_dims, preferred_element_type=F32)
        q["ab"] = jnp.where(strict, gram[0:CHUNK, 0:QUAD], 0.0)
        q["akv"] = _dot(jnp.where(strict, gram[0:CHUNK, QUAD:2 * QUAD], 0.0).astype(BF16), q["bdv"])
        if with_y:
            q["rb"] = jnp.where(incl, gram[CHUNK:2 * CHUNK, 0:QUAD], 0.0).astype(BF16)
            q["rkv"] = _dot(jnp.where(incl, gram[CHUNK:2 * CHUNK, QUAD:2 * QUAD], 0.0).astype(BF16), q["bdv"])
        q["t"] = eye4 + q["ab"] * msk_ref[3]
    for lvl in range(1, n_levels):
        for key in chains:
            q = ch[key]
            q["x"] = _dot((q["ab"] * msk_ref[3 + lvl]).astype(BF16), bd(q["t"].astype(BF16)))
        for key in chains:
            q = ch[key]
            q["t"] = q["t"] + _dot(q["t"].astype(BF16), bd(q["x"].astype(BF16)))
    for key in chains:
        q = ch[key]
        au = _dot(q["t"].astype(BF16),
                  jnp.concatenate([bd(q["a"]), bd(q["akv"].astype(BF16))], axis=1))
        au_b = au.astype(BF16)
        wl = wl_ref[0, 0, key[1], :, q["lanes"]]
        mc = lax.dot_general(au_b, q["b"], tn_dims, preferred_element_type=F32)
        vk = lax.dot_general(q["v"], q["k"], tn_dims, preferred_element_type=F32)
        q["mt"] = jnp.where(bd_mask, mc[0:QUAD] * wl, 0.0).astype(BF16)
        q["ct"] = fold(mc[QUAD:2 * QUAD] + vk) * wl
        if with_y:
            rbau = _dot(q["rb"], jnp.concatenate([bd(au_b[:, 0:QUAD]), bd(au_b[:, QUAD:2 * QUAD])], axis=1))
            q["rt"] = (q["r"].astype(F32) + rbau[:, 0:QUAD]).astype(BF16)
            q["y0"] = rbau[:, QUAD:2 * QUAD] + q["rkv"]
    g = [g_sc[qq] for qq in range(n_quads)]
    for key in chains:
        qq, c = key
        q = ch[key]
        g_b = g[qq].astype(BF16)
        if with_y:
            y_ref[0, q["rows"], q["lanes"]] = (
                lax.dot_general(q["rt"], bd(g_b), nt_dims, preferred_element_type=F32) + q["y0"])
        g[qq] = g[qq] * wl_ref[0, 0, c, :, q["lanes"]] + _dot(g_b, q["mt"]) + q["ct"]
    for qq in range(n_quads):
        g_sc[qq] = g[qq]

    @pl.when(tb == n_tb - 1)
    def _():
        hT_ref[0] = g_sc[...]


def _scanq(prep, d, h0, rev, with_y):
    v, at, rt, bt, kt, wl = prep
    b, n, cw = v.shape
    tb = min(SCAN_BLOCK, n)
    n_tb = n // tb
    n_quads = cw // QUAD
    sq = SCAN_QUADS
    wide = sq * QUAD
    msk, bdm = _quad_masks(rev)
    tmap = (lambda t: n_tb - 1 - t) if rev else (lambda t: t)
    shared = pl.BlockSpec((1, tb, wide), lambda bi, p, t: (bi, tmap(t), p))
    perdir = pl.BlockSpec((1, 1, tb, wide), lambda bi, p, t: (d, bi, tmap(t), p))
    decay = pl.BlockSpec((1, 1, tb // CHUNK, 1, wide), lambda bi, p, t: (d, bi, tmap(t), 0, p))
    state = pl.BlockSpec((1, sq, CHUNK, QUAD), lambda bi, p, t: (bi, p, 0, 0))
    out_specs = [state]
    out_shape = [jax.ShapeDtypeStruct((b, n_quads, CHUNK, QUAD), F32)]
    if with_y:
        out_specs = [shared] + out_specs
        out_shape = [jax.ShapeDtypeStruct((b, n, cw), F32)] + out_shape
    res = pl.pallas_call(
        functools.partial(_scanq_kernel, rev=rev, with_y=with_y),
        grid=(b, n_quads // sq, n_tb),
        in_specs=[shared, perdir, perdir, perdir, perdir, decay, state,
                  pl.BlockSpec(msk.shape, lambda bi, p, t: (0, 0, 0)),
                  pl.BlockSpec(bdm.shape, lambda bi, p, t: (0, 0))],
        out_specs=out_specs,
        out_shape=out_shape,
        scratch_shapes=[pltpu.VMEM((sq, CHUNK, QUAD), F32)],
        compiler_params=_cparams("parallel", "parallel", "arbitrary"),
        name="scan_rev" if rev else "scan_fwd",
    )(v, at, rt, bt, kt, wl, h0, jnp.asarray(msk), jnp.asarray(bdm))
    return (res[0], res[1]) if with_y else (None, res[0])


def _out1_kernel(yf_ref, yb_ref, bonus_ref, g_ref, dp_ref, x_ref, mod_ref, lw_ref, lb_ref, seg_ref,
                 w_ref, o_ref):
    seg = seg_ref[...]
    cw = yf_ref.shape[2]
    acc = _dot(dp_ref[0].astype(BF16), w_ref[cw:cw + D_WIDTH, :])
    for pb in range(cw // LANES):
        sl = slice(pb * LANES, (pb + 1) * LANES)
        y = yf_ref[0, :, sl] + yb_ref[0, :, sl]
        mean = _seg_sum(y, seg) * (1.0 / HEAD_DIM)
        dev = y - mean
        var = _seg_sum(dev * dev, seg) * (1.0 / HEAD_DIM)
        yn = dev * lax.rsqrt(var + LNX_EPS) * lw_ref[:, sl] + lb_ref[:, sl]
        z = (yn + bonus_ref[0, :, sl]) * g_ref[0, :, sl]
        acc = acc + _dot(z.astype(BF16), w_ref[sl, :])
    o_ref[0] = x_ref[0] + mod_ref[0, 2:3, :] * acc


def _out1(yf, yb, bonus, g, dp, x, mod, lnx_w, lnx_b, seg, w):
    b, n, d = x.shape
    tm = ROW_TILE
    cw = yf.shape[2]
    row = lambda bi, i: (bi, i, 0)
    const = lambda bi, i: (0, 0)
    return pl.pallas_call(
        _out1_kernel,
        grid=(b, n // tm),
        in_specs=[pl.BlockSpec((1, tm, cw), row)] * 4
        + [pl.BlockSpec((1, tm, D_WIDTH), row),
           pl.BlockSpec((1, tm, d), row),
           pl.BlockSpec((1, 6, d), lambda bi, i: (bi, 0, 0)),
           pl.BlockSpec((1, cw), const), pl.BlockSpec((1, cw), const),
           pl.BlockSpec((LANES, LANES), const),
           pl.BlockSpec(w.shape, const)],
        out_specs=pl.BlockSpec((1, tm, d), row),
        out_shape=jax.ShapeDtypeStruct((b, n, d), F32),
        compiler_params=_cparams("parallel", "parallel"),
        name="out1",
    )(yf, yb, bonus, g, dp, x, mod, lnx_w, lnx_b, seg, w)


def _rope_tables(n):
    rows = n // GRID_W
    row = jnp.repeat(jnp.arange(rows, dtype=F32), GRID_W)
    col = jnp.tile(jnp.arange(GRID_W, dtype=F32), rows)
    n_freq = HEAD_DIM // 4
    inv = ROPE_THETA ** (-jnp.arange(n_freq, dtype=F32) / n_freq)
    ang = jnp.concatenate([row[:, None] * inv[None, :], col[:, None] * inv[None, :]], axis=-1)
    cos, sin = jnp.cos(ang), jnp.sin(ang)
    cos_t = jnp.tile(cos, (1, LANES // cos.shape[1]))
    sin_t = jnp.tile(jnp.concatenate([-sin, sin], axis=-1), (1, LANES // HEAD_DIM))
    return cos_t, sin_t


def _kv_dup_columns():
    cols = []
    for section in range(4):
        for head in range(2):
            base = AB_Q_COLS + section * 2 * HEAD_DIM + head * HEAD_DIM
            cols += list(range(base, base + HEAD_DIM)) * 2
    return np.concatenate([np.arange(AB_Q_COLS), np.asarray(cols)])


def kernel(x, c, ctx, c_ctx, norm_gain, ada_w, ada_b, ffn_w_in, ffn_w_out, final_gain, ab_w_in, ab_q_gain, ab_k_gain, ab_sink, ab_w_out, cd_w_in, cd_mu, cd_w0, cd_w2, cd_a0, cd_a2, cd_g2, cd_k_k, cd_k_a, cd_r_k, cd_lnx_w, cd_lnx_b, cd_pool_w, cd_pool_scale, cd_w_out):
    b, n, d = x.shape
    nc = ctx.shape[1]
    pad = (-(b + 1)) % 8
    cs = jnp.concatenate([c, c_ctx[None, :], jnp.zeros((pad, d), F32)], axis=0)
    mods = _mods(cs, ada_w, ada_b)
    seg = jnp.asarray(np.kron(np.eye(2), np.ones((HEAD_DIM, HEAD_DIM))).astype(np.float32)).astype(BF16)
    fgain = final_gain.reshape(1, d)

    def layer_mods(i):
        ml = mods[i, :b].reshape(b, 6, d)
        mc = jnp.broadcast_to(mods[i, b].reshape(1, 6, d), (b, 6, d))
        return ml, mc

    ml, mc = layer_mods(0)
    w0 = ab_w_in[0][:, _kv_dup_columns()].astype(BF16)
    gain = norm_gain[0, 0].reshape(1, d)
    qg = jnp.tile(ab_q_gain[0], 2).reshape(1, LANES)
    kg = jnp.tile(ab_k_gain[0], 2).reshape(1, LANES)
    cos_l, sin_l = _rope_tables(n)
    cos_c, sin_c = jnp.ones((nc, LANES), F32), jnp.zeros((nc, LANES), F32)
    q_l, kv_l, vt_l = _proj0(x, ml, gain, w0, cos_l, sin_l, qg, kg, seg)
    q_c, kv_c, vt_c = _proj0(ctx, mc, gain, w0, cos_c, sin_c, qg, kg, seg)
    sink = ab_sink[0]
    w_out0 = ab_w_out[0].astype(BF16)
    gain2 = norm_gain[0, 1].reshape(1, d)
    w1 = ffn_w_in[0].astype(BF16)
    w2 = ffn_w_out[0].astype(BF16)
    yb_l = _attn_b_auto(q_l, kv_c, vt_c, kv_l, vt_l, ab_q_gain[0], ab_k_gain[0])
    xl = _out0(_attn_a(q_l, kv_c, sink, kv_l), yb_l, x, ml, w_out0)
    xc = _out0(_attn_a(q_c, kv_c, sink), _attn_b(q_c, kv_c), ctx, mc, w_out0)
    xl = _ffn(xl, ml, gain2, w1, w2, fgain, False)
    xc = _ffn(xc, mc, gain2, w1, w2, fgain, False)

    ml, mc = layer_mods(1)
    gain = norm_gain[1, 0].reshape(1, d)
    w_in1 = cd_w_in[0].astype(BF16)
    pc_l = _proj1(xl, ml, gain, w_in1)
    pc_c = _proj1(xc, mc, gain, w_in1[:, :C_IN])
    zeros = jnp.zeros((DECAY_LORA_PAD, C_WIDTH), F32)
    w2x = jnp.stack([jnp.concatenate([cd_w2[0, 0], zeros]), jnp.concatenate([zeros, cd_w2[0, 1]])])
    a2x = jnp.stack([jnp.concatenate([cd_a2[0, 0], zeros]), jnp.concatenate([zeros, cd_a2[0, 1]])])
    w2x_hi = w2x.astype(BF16)
    w2x_hl = jnp.stack([w2x_hi, (w2x - w2x_hi.astype(F32)).astype(BF16)])
    wts = [jnp.stack([1.0 - cd_mu[0], 0.5 * cd_mu[0]]), cd_w0[0].reshape(2, 1, C_WIDTH), w2x_hl,
           cd_a0[0].reshape(2, 1, C_WIDTH), a2x.astype(BF16), cd_g2[0].astype(BF16),
           cd_k_k[0].reshape(1, C_WIDTH), cd_k_a[0].reshape(1, C_WIDTH),
           cd_r_k[0].reshape(1, C_WIDTH), seg]
    pool_w = jax.scipy.linalg.block_diag(*[cd_pool_w[0, g] for g in range(4)]).astype(BF16)
    pool_wts = [pool_w, cd_pool_scale[0].reshape(1, D_WIDTH)]
    prep_c = _prep(pc_c, wts, False)
    *prep_l, g_l, bonus_l, dp_l = _prep(pc_l, wts + pool_wts, True)
    h_zero = jnp.zeros((b, C_WIDTH // QUAD, CHUNK, QUAD), F32)
    _, h_f = _scanq(prep_c, 0, h_zero, False, False)
    _, h_b = _scanq(prep_c, 1, h_zero, True, False)
    y_f, _ = _scanq(prep_l, 0, h_f, False, True)
    y_b, _ = _scanq(prep_l, 1, h_b, True, True)
    xl = _out1(y_f, y_b, bonus_l, g_l, dp_l, xl, ml, cd_lnx_w[0].reshape(1, C_WIDTH),
               cd_lnx_b[0].reshape(1, C_WIDTH), seg, cd_w_out[0].astype(BF16))
    return _ffn(xl, ml, norm_gain[1, 1].reshape(1, d), ffn_w_in[1].astype(BF16),
                ffn_w_out[1].astype(BF16), fgain, True)
```

```python
import functools

import numpy as np
import jax
import jax.numpy as jnp
from jax import lax
from jax.experimental import pallas as pl
from jax.experimental.pallas import tpu as pltpu

F32 = jnp.float32
BF16 = jnp.bfloat16
HIGHEST = lax.Precision.HIGHEST
LOG2_E = 1.4426950408889634
SCORE_BOUND_SAFE = 60.0

D_MODEL = 1024
GRID_W = 64
HEAD_DIM = 64
ROPE_THETA = 10000.0
NORM_EPS = 1e-6
WINDOW = 128
AB_Q_COLS = 1024
C_WIDTH = 768
C_IN = 2688
D_WIDTH = 256
CD_IN = C_IN + D_WIDTH
LNX_EPS = 64e-5
FFN_HIDDEN = 2816
POOL_WINDOWS = (2, 4, 8, 16)
DECAY_LORA_PAD = 64

LANES = 128
ROW_TILE = 256
HALO = 8
CHUNK = 64
SCAN_BLOCK = 512
SCAN_QUADS = 3
VMEM_LIMIT = 56 * 1024 * 1024


def _cparams(*sem):
    return pltpu.CompilerParams(dimension_semantics=sem, vmem_limit_bytes=VMEM_LIMIT)


def _dot(a, b):
    return jnp.dot(a, b, preferred_element_type=F32)


def _dot32(a, b):
    return jnp.dot(a, b, preferred_element_type=F32, precision=HIGHEST)


def _split(a, terms):
    pieces = []
    for _ in range(terms - 1):
        hi = a.astype(BF16)
        pieces.append(hi)
        a = a - hi.astype(F32)
    pieces.append(a.astype(BF16))
    return pieces


def _seg_sum(z, seg):
    hi, lo = _split(z, 2)
    return _dot(hi, seg) + _dot(lo, seg)


def _sigmoid(x):
    return 1.0 / (1.0 + jnp.exp(-x))


def _norm_mod(x, gain, shift, scale):
    ms = jnp.mean(x * x, axis=-1, keepdims=True)
    return (x * lax.rsqrt(ms + NORM_EPS) * gain) * (1.0 + scale) + shift


def _mods_kernel(c_ref, w_ref, b_ref, o_ref):
    c = c_ref[...]
    o_ref[0] = _dot32(c * _sigmoid(c), w_ref[0]) + b_ref[0]


def _mods(cs, ada_w, ada_b):
    depth, d, n6 = ada_w.shape
    tn = 768
    rows = cs.shape[0]
    return pl.pallas_call(
        _mods_kernel,
        grid=(depth, n6 // tn),
        in_specs=[pl.BlockSpec((rows, d), lambda l, j: (0, 0)),
                  pl.BlockSpec((1, d, tn), lambda l, j: (l, 0, j)),
                  pl.BlockSpec((1, 1, tn), lambda l, j: (l, 0, j))],
        out_specs=pl.BlockSpec((1, rows, tn), lambda l, j: (l, 0, j)),
        out_shape=jax.ShapeDtypeStruct((depth, rows, n6), F32),
        compiler_params=_cparams("arbitrary", "arbitrary"),
        name="mods",
    )(cs, ada_w, ada_b.reshape(depth, 1, n6))


def _proj0_kernel(x_ref, mod_ref, gain_ref, w_ref, cos_ref, sin_ref, qg_ref, kg_ref, seg_ref,
                  q_ref, kv_ref, vt_ref):
    h = _norm_mod(x_ref[0], gain_ref[...], mod_ref[0, 0:1, :], mod_ref[0, 1:2, :])
    p = _dot(h.astype(BF16), w_ref[...])
    tm = p.shape[0]
    cos = cos_ref[...]
    sin = sin_ref[...]
    lane = lax.broadcasted_iota(jnp.int32, (tm, LANES), 1)
    first_half = (lane & (HEAD_DIM - 1)) < HEAD_DIM // 2
    seg = seg_ref[...]

    def rope(z):
        partner = jnp.where(first_half, pltpu.roll(z, LANES - HEAD_DIM // 2, 1),
                            pltpu.roll(z, HEAD_DIM // 2, 1))
        return z * cos + partner * sin

    def head_norm(z, g):
        ms = _seg_sum(z * z, seg) * (1.0 / HEAD_DIM)
        return z * lax.rsqrt(ms + NORM_EPS) * g

    scale = HEAD_DIM ** -0.5 * LOG2_E
    for blk in range(8):
        z = p[:, blk * LANES:(blk + 1) * LANES]
        if blk >= 4:
            z = head_norm(z, qg_ref[...])
        q_ref[0, :, blk * LANES:(blk + 1) * LANES] = (rope(z) * scale).astype(BF16)
    for blk in range(8):
        z = p[:, AB_Q_COLS + blk * LANES:AB_Q_COLS + (blk + 1) * LANES]
        if blk in (4, 5):
            z = head_norm(z, kg_ref[...])
        if blk in (0, 1, 4, 5):
            z = rope(z)
        kv_ref[0, :, blk * LANES:(blk + 1) * LANES] = z.astype(BF16)
        if blk in (6, 7):
            zt = z.T
            top = lax.broadcasted_iota(jnp.int32, zt.shape, 0) < HEAD_DIM
            vt_ref[0, blk - 6] = jnp.where(top, zt, 1.0).astype(BF16)


def _proj0(x, mod, gain, w, cos, sin, qg, kg, seg):
    b, n, d = x.shape
    tm = ROW_TILE
    nw = w.shape[1]
    const = lambda bi, i: (0, 0)
    return pl.pallas_call(
        _proj0_kernel,
        grid=(b, n // tm),
        in_specs=[pl.BlockSpec((1, tm, d), lambda bi, i: (bi, i, 0)),
                  pl.BlockSpec((1, 6, d), lambda bi, i: (bi, 0, 0)),
                  pl.BlockSpec((1, d), const),
                  pl.BlockSpec((d, nw), const),
                  pl.BlockSpec((tm, LANES), lambda bi, i: (i, 0)),
                  pl.BlockSpec((tm, LANES), lambda bi, i: (i, 0)),
                  pl.BlockSpec((1, LANES), const),
                  pl.BlockSpec((1, LANES), const),
                  pl.BlockSpec((LANES, LANES), const)],
        out_specs=[pl.BlockSpec((1, tm, 1024), lambda bi, i: (bi, i, 0)),
                   pl.BlockSpec((1, tm, 1024), lambda bi, i: (bi, i, 0)),
                   pl.BlockSpec((1, 2, LANES, tm), lambda bi, i: (bi, 0, 0, i))],
        out_shape=[jax.ShapeDtypeStruct((b, n, 1024), BF16),
                   jax.ShapeDtypeStruct((b, n, 1024), BF16),
                   jax.ShapeDtypeStruct((b, 2, LANES, n), BF16)],
        compiler_params=_cparams("parallel", "parallel"),
        name="proj0",
    )(x, mod, gain, w, cos, sin, qg, kg, seg)


def _masked_q(q_ref, h, low):
    q2 = q_ref[0, :, (h // 2) * LANES:(h // 2 + 1) * LANES]
    keep = low if h % 2 == 0 else jnp.logical_not(low)
    return jnp.where(keep, q2, jnp.zeros_like(q2))


def _attn_b_kernel(q_ref, kc_ref, vc_ref, *rest, n_kt, has_latent):
    if has_latent:
        k_ref, v_ref, o_ref, m_sc, acc_sc = rest
    else:
        o_ref, m_sc, acc_sc = rest
    kt = pl.program_id(3)
    tq = q_ref.shape[1]
    low = lax.broadcasted_iota(jnp.int32, (tq, LANES), 1) < HEAD_DIM

    def update(k, v):
        tk = k.shape[0]
        v1 = jnp.where(lax.broadcasted_iota(jnp.int32, (tk, LANES), 1) < HEAD_DIM, v, jnp.ones_like(v))
        ss = [lax.dot_general(_masked_q(q_ref, h, low), k, (((1,), (1,)), ((), ())),
                              preferred_element_type=F32) for h in range(4)]
        for h in range(4):
            m_prev = m_sc[h]
            m_new = jnp.maximum(m_prev, jnp.max(ss[h], axis=1, keepdims=True))
            p = jnp.exp2(ss[h] - jnp.concatenate([m_new] * (tk // LANES), axis=1)).astype(BF16)
            acc_sc[h] = jnp.exp2(m_prev - m_new) * acc_sc[h] + _dot(p, v1)
            m_sc[h] = m_new

    @pl.when(kt == 0)
    def _():
        m_sc[...] = jnp.full(m_sc.shape, -jnp.inf, F32)
        acc_sc[...] = jnp.zeros(acc_sc.shape, F32)
        update(kc_ref[0], vc_ref[0])

    if has_latent:
        update(k_ref[0], v_ref[0])

    @pl.when(kt == n_kt - 1)
    def _():
        for pr in range(2):
            a0 = acc_sc[2 * pr]
            a1 = acc_sc[2 * pr + 1]
            o0 = a0 / pltpu.roll(a0, HEAD_DIM, 1)
            o1 = pltpu.roll(a1, HEAD_DIM, 1) / a1
            o_ref[0, :, pr * LANES:(pr + 1) * LANES] = jnp.where(low, o0, o1).astype(BF16)


def _attn_b(q, kv_c, kv=None, tk=1024):
    b, n, _ = q.shape
    nc = kv_c.shape[1]
    tq = ROW_TILE
    has_latent = kv is not None
    n_kt = kv.shape[1] // tk if has_latent else 1
    in_specs = [pl.BlockSpec((1, tq, 2 * LANES), lambda bi, g, i, j: (bi, i, 2 + g)),
                pl.BlockSpec((1, nc, LANES), lambda bi, g, i, j: (bi, 0, 4 + g)),
                pl.BlockSpec((1, nc, LANES), lambda bi, g, i, j: (bi, 0, 6 + g))]
    args = [q, kv_c, kv_c]
    if has_latent:
        in_specs += [pl.BlockSpec((1, tk, LANES), lambda bi, g, i, j: (bi, j, 4 + g)),
                     pl.BlockSpec((1, tk, LANES), lambda bi, g, i, j: (bi, j, 6 + g))]
        args += [kv, kv]
    return pl.pallas_call(
        functools.partial(_attn_b_kernel, n_kt=n_kt, has_latent=has_latent),
        grid=(b, 2, n // tq, n_kt),
        in_specs=in_specs,
        out_specs=pl.BlockSpec((1, tq, 2 * LANES), lambda bi, g, i, j: (bi, i, g)),
        out_shape=jax.ShapeDtypeStruct((b, n, 512), BF16),
        scratch_shapes=[pltpu.VMEM((4, tq, LANES), F32), pltpu.VMEM((4, tq, LANES), F32)],
        compiler_params=_cparams("parallel", "parallel", "parallel", "arbitrary"),
        name="attn_b",
    )(*args)


def _attn_bt_kernel(q_ref, kc_ref, vtc_ref, k_ref, vt_ref, o_ref, acc_sc, *, n_kt):
    kt = pl.program_id(3)
    tq = q_ref.shape[1]
    low = lax.broadcasted_iota(jnp.int32, (tq, LANES), 1) < HEAD_DIM

    def update(k, vt):
        sts = [lax.dot_general(k, _masked_q(q_ref, h, low), (((1,), (1,)), ((), ())),
                               preferred_element_type=F32) for h in range(4)]
        for h in range(4):
            acc_sc[h] = acc_sc[h] + _dot(vt, jnp.exp2(sts[h]).astype(BF16))

    @pl.when(kt == 0)
    def _():
        acc_sc[...] = jnp.zeros(acc_sc.shape, F32)
        update(kc_ref[0], vtc_ref[0, 0])

    update(k_ref[0], vt_ref[0, 0])

    @pl.when(kt == n_kt - 1)
    def _():
        ot = jnp.concatenate([acc_sc[h, 0:HEAD_DIM, :] / acc_sc[h, HEAD_DIM:2 * HEAD_DIM, :]
                              for h in range(4)], axis=0)
        o_ref[0] = ot.T.astype(BF16)


def _attn_bt(q, kv_c, vt_c, kv, vt, tk=4096):
    b, n, _ = q.shape
    nc = kv_c.shape[1]
    tq = ROW_TILE
    tk = min(tk, kv.shape[1])
    n_kt = kv.shape[1] // tk
    return pl.pallas_call(
        functools.partial(_attn_bt_kernel, n_kt=n_kt),
        grid=(b, 2, n // tq, n_kt),
        in_specs=[pl.BlockSpec((1, tq, 2 * LANES), lambda bi, g, i, j: (bi, i, 2 + g)),
                  pl.BlockSpec((1, nc, LANES), lambda bi, g, i, j: (bi, 0, 4 + g)),
                  pl.BlockSpec((1, 1, LANES, nc), lambda bi, g, i, j: (bi, g, 0, 0)),
                  pl.BlockSpec((1, tk, LANES), lambda bi, g, i, j: (bi, j, 4 + g)),
                  pl.BlockSpec((1, 1, LANES, tk), lambda bi, g, i, j: (bi, g, 0, j))],
        out_specs=pl.BlockSpec((1, tq, 2 * LANES), lambda bi, g, i, j: (bi, i, g)),
        out_shape=jax.ShapeDtypeStruct((b, n, 512), BF16),
        scratch_shapes=[pltpu.VMEM((4, LANES, tq), F32)],
        compiler_params=_cparams("parallel", "parallel", "parallel", "arbitrary"),
        name="attn_b_bounded",
    )(q, kv_c, vt_c, kv, vt)


def _attn_b_auto(q, kv_c, vt_c, kv, vt, q_gain, k_gain):
    bound = (1.02 * HEAD_DIM * HEAD_DIM ** -0.5 * LOG2_E) * jnp.max(jnp.abs(q_gain)) * jnp.max(jnp.abs(k_gain))
    return lax.cond(bound <= SCORE_BOUND_SAFE,
                    lambda: _attn_bt(q, kv_c, vt_c, kv, vt),
                    lambda: _attn_b(q, kv_c, kv))


def _attn_a_kernel(sink_ref, q_ref, kc_ref, vc_ref, *rest, has_local):
    if has_local:
        kp_ref, kcur_ref, kn_ref, vp_ref, vcur_ref, vn_ref, o_ref, kbuf, vbuf = rest
    else:
        o_ref, kbuf, vbuf = rest
    g = pl.program_id(1)
    i = pl.program_id(2)
    n_i = pl.num_programs(2)
    tq = q_ref.shape[1]
    nc = kc_ref.shape[1]
    kbuf[0:nc, :] = kc_ref[0]
    vbuf[0:nc, :] = vc_ref[0]
    nk = nc
    if has_local:
        kbuf[nc:nc + WINDOW, :] = kp_ref[0]
        kbuf[nc + WINDOW:nc + WINDOW + tq, :] = kcur_ref[0]
        kbuf[nc + WINDOW + tq:nc + 2 * WINDOW + tq, :] = kn_ref[0]
        vbuf[nc:nc + WINDOW, :] = vp_ref[0]
        vbuf[nc + WINDOW:nc + WINDOW + tq, :] = vcur_ref[0]
        vbuf[nc + WINDOW + tq:nc + 2 * WINDOW + tq, :] = vn_ref[0]
        nk = nc + 2 * WINDOW + tq
        row = lax.broadcasted_iota(jnp.int32, (tq, nk), 0)
        col = lax.broadcasted_iota(jnp.int32, (tq, nk), 1) - nc
        lo_c = jnp.where(i == 0, WINDOW, 0)
        hi_c = jnp.where(i == n_i - 1, WINDOW + tq, 2 * WINDOW + tq)
        valid = (col < 0) | ((col >= row) & (col <= row + 2 * WINDOW) & (col >= lo_c) & (col < hi_c))
    k = kbuf[...]
    v = vbuf[...]
    v1 = jnp.where(lax.broadcasted_iota(jnp.int32, (nk, LANES), 1) < HEAD_DIM, v, jnp.ones_like(v))
    low = lax.broadcasted_iota(jnp.int32, (tq, LANES), 1) < HEAD_DIM
    ss = [lax.dot_general(_masked_q(q_ref, h, low), k, (((1,), (1,)), ((), ())),
                          preferred_element_type=F32) for h in range(4)]
    accs = []
    for h in range(4):
        s = jnp.where(valid, ss[h], -jnp.inf) if has_local else ss[h]
        sink = sink_ref[g * 4 + h] * LOG2_E
        m = jnp.maximum(jnp.max(s, axis=1, keepdims=True), sink)
        p = jnp.exp2(s - m)
        sink_p = jnp.where(low, 0.0, jnp.exp2(sink - m))
        accs.append(_dot(p.astype(BF16), v1) + sink_p)
    for pr in range(2):
        a0, a1 = accs[2 * pr], accs[2 * pr + 1]
        o0 = a0 / pltpu.roll(a0, HEAD_DIM, 1)
        o1 = pltpu.roll(a1, HEAD_DIM, 1) / a1
        o_ref[0, :, pr * LANES:(pr + 1) * LANES] = jnp.where(low, o0, o1).astype(BF16)


def _attn_a(q, kv_c, sink, kv=None):
    b, n, _ = q.shape
    nc = kv_c.shape[1]
    tq = ROW_TILE
    has_local = kv is not None
    in_specs = [pl.BlockSpec(memory_space=pltpu.SMEM),
                pl.BlockSpec((1, tq, 2 * LANES), lambda bi, g, i: (bi, i, g)),
                pl.BlockSpec((1, nc, LANES), lambda bi, g, i: (bi, 0, g)),
                pl.BlockSpec((1, nc, LANES), lambda bi, g, i: (bi, 0, 2 + g))]
    args = [sink, q, kv_c, kv_c]
    nk = nc
    if has_local:
        per = tq // WINDOW
        last = n // WINDOW - 1
        prev_i = lambda i: jnp.maximum(i * per - 1, 0)
        next_i = lambda i: jnp.minimum((i + 1) * per, last)
        for off in (0, 2):
            in_specs += [pl.BlockSpec((1, WINDOW, LANES), lambda bi, g, i, off=off: (bi, prev_i(i), off + g)),
                         pl.BlockSpec((1, tq, LANES), lambda bi, g, i, off=off: (bi, i, off + g)),
                         pl.BlockSpec((1, WINDOW, LANES), lambda bi, g, i, off=off: (bi, next_i(i), off + g))]
            args += [kv, kv, kv]
        nk = nc + 2 * WINDOW + tq
    return pl.pallas_call(
        functools.partial(_attn_a_kernel, has_local=has_local),
        grid=(b, 2, n // tq),
        in_specs=in_specs,
        out_specs=pl.BlockSpec((1, tq, 2 * LANES), lambda bi, g, i: (bi, i, g)),
        out_shape=jax.ShapeDtypeStruct((b, n, 512), BF16),
        scratch_shapes=[pltpu.VMEM((nk, LANES), BF16), pltpu.VMEM((nk, LANES), BF16)],
        compiler_params=_cparams("parallel", "parallel", "parallel"),
        name="attn_a",
    )(*args)


def _ffn_tile(x, mod_ref, gain_ref, w1_ref, w2_ref, fg_ref, final):
    h = _norm_mod(x, gain_ref[...], mod_ref[0, 3:4, :], mod_ref[0, 4:5, :]).astype(BF16)
    hid = w2_ref.shape[0]
    ch = hid // 2
    acc = None
    for j in range(2):
        gate = _dot(h, w1_ref[:, j * ch:(j + 1) * ch])
        up = _dot(h, w1_ref[:, hid + j * ch:hid + (j + 1) * ch])
        act = (gate * _sigmoid(gate) * up).astype(BF16)
        part = _dot(act, w2_ref[j * ch:(j + 1) * ch, :])
        acc = part if acc is None else acc + part
    y = x + mod_ref[0, 5:6, :] * acc
    if final:
        ms = jnp.mean(y * y, axis=-1, keepdims=True)
        y = y * lax.rsqrt(ms + NORM_EPS) * fg_ref[...]
    return y


def _out0_ffn_kernel(ya_ref, yb_ref, x_ref, mod_ref, wo_ref, gain_ref, w1_ref, w2_ref, fg_ref, o_ref):
    half = ya_ref.shape[2]
    y = _dot(ya_ref[0], wo_ref[0:half, :]) + _dot(yb_ref[0], wo_ref[half:2 * half, :])
    x1 = x_ref[0] + mod_ref[0, 2:3, :] * y
    o_ref[0] = _ffn_tile(x1, mod_ref, gain_ref, w1_ref, w2_ref, fg_ref, False)


def _ffn_specs(d, w1, w2):
    const = lambda bi, i: (0, 0)
    return [pl.BlockSpec((1, d), const),
            pl.BlockSpec(w1.shape, const, pipeline_mode=pl.Buffered(1)),
            pl.BlockSpec(w2.shape, const, pipeline_mode=pl.Buffered(1)),
            pl.BlockSpec((1, d), const)]


def _out0_ffn(ya, yb, x, mod, wo, gain, w1, w2, fgain):
    b, n, d = x.shape
    tm = ROW_TILE
    row = lambda bi, i: (bi, i, 0)
    return pl.pallas_call(
        _out0_ffn_kernel,
        grid=(b, n // tm),
        in_specs=[pl.BlockSpec((1, tm, ya.shape[2]), row),
                  pl.BlockSpec((1, tm, yb.shape[2]), row),
                  pl.BlockSpec((1, tm, d), row),
                  pl.BlockSpec((1, 6, d), lambda bi, i: (bi, 0, 0)),
                  pl.BlockSpec(wo.shape, lambda bi, i: (0, 0), pipeline_mode=pl.Buffered(1))]
        + _ffn_specs(d, w1, w2),
        out_specs=pl.BlockSpec((1, tm, d), row),
        out_shape=jax.ShapeDtypeStruct((b, n, d), F32),
        compiler_params=_cparams("parallel", "parallel"),
        name="out0_ffn",
    )(ya, yb, x, mod, wo, gain, w1, w2, fgain)


def _proj1_kernel(x_ref, mod_ref, gain_ref, w_ref, o_ref):
    h = _norm_mod(x_ref[0], gain_ref[...], mod_ref[0, 0:1, :], mod_ref[0, 1:2, :])
    o_ref[0] = _dot(h.astype(BF16), w_ref[...])


def _proj1(x, mod, gain, w):
    b, n, d = x.shape
    tm = ROW_TILE
    nw = w.shape[1]
    return pl.pallas_call(
        _proj1_kernel,
        grid=(b, n // tm),
        in_specs=[pl.BlockSpec((1, tm, d), lambda bi, i: (bi, i, 0)),
                  pl.BlockSpec((1, 6, d), lambda bi, i: (bi, 0, 0)),
                  pl.BlockSpec((1, d), lambda bi, i: (0, 0)),
                  pl.BlockSpec((d, nw), lambda bi, i: (0, 0))],
        out_specs=pl.BlockSpec((1, tm, nw), lambda bi, i: (bi, i, 0)),
        out_shape=jax.ShapeDtypeStruct((b, n, nw), F32),
        compiler_params=_cparams("parallel", "parallel"),
        name="proj1",
    )(x, mod, gain, w)


def _prep_kernel(pc_ref, prev_ref, next_ref, mu_ref, w0_ref, w2_ref, a0_ref, a2_ref, g2_ref,
                 kk_ref, ka_ref, rk_ref, seg_ref, tri_ref, *rest, latent):
    if latent:
        pw_ref, ps_ref, v_o, at_o, rt_o, bt_o, kt_o, wl_o, g_o, bonus_o, d_o, ext, tmp = rest
    else:
        v_o, at_o, rt_o, bt_o, kt_o, wl_o, ext, tmp = rest
    i = pl.program_id(1)
    nt = pl.num_programs(1)
    tm = pc_ref.shape[1]
    cw = C_WIDTH
    ext[HALO:HALO + tm, :] = pc_ref[0]
    ext[0:HALO, :] = jnp.where(i > 0, prev_ref[0], 0.0)
    ext[HALO + tm:2 * HALO + tm, :] = jnp.where(i < nt - 1, next_ref[0], 0.0)

    def mixed(lo, hi):
        nb = ext[HALO - 1:HALO - 1 + tm, lo:hi] + ext[HALO + 1:HALO + 1 + tm, lo:hi]
        return ext[HALO:HALO + tm, lo:hi] * mu_ref[0:1, lo:hi] + nb * mu_ref[1:2, lo:hi]

    lora = mixed(3 * cw, C_IN)
    tw = jnp.tanh(lora[:, 0:LANES])
    xa = lora[:, LANES:2 * LANES]
    if latent:
        g_o[0] = _dot(_sigmoid(lora[:, 2 * LANES:3 * LANES]).astype(BF16), g2_ref[...])
    tw_hi, tw_lo = _split(tw, 2)
    xa_b = xa.astype(BF16)
    for d in range(2):
        z = w0_ref[d] + _dot(tw_hi, w2_ref[0, d]) + _dot(tw_lo, w2_ref[0, d]) + _dot(tw_hi, w2_ref[1, d])
        w_log = -(jnp.maximum(-z, 0.0) + jnp.log(1.0 + jnp.exp(-jnp.abs(z)))) - 0.5
        tmp[d] = -jnp.exp(w_log)
        tmp[2 + d] = _sigmoid(a0_ref[d] + _dot(xa_b, a2_ref[d]))
    seg = seg_ref[...]
    n_chunks = tm // CHUNK
    for pb in range(cw // LANES):
        sl = slice(pb * LANES, (pb + 1) * LANES)
        r = mixed(pb * LANES, (pb + 1) * LANES)
        k = mixed(cw + pb * LANES, cw + (pb + 1) * LANES)
        v = mixed(2 * cw + pb * LANES, 2 * cw + (pb + 1) * LANES)
        kk = k * kk_ref[:, sl]
        kk = kk / jnp.maximum(jnp.sqrt(_seg_sum(kk * kk, seg)), 1e-12)
        ksum = jnp.zeros_like(k)
        for d in range(2):
            lw = tmp[d, :, sl]
            a = tmp[2 + d, :, sl]
            k_d = k * (1.0 + (a - 1.0) * ka_ref[:, sl])
            bb = kk * a
            ksum = ksum + k_d
            cs2 = _dot(tri_ref[d], jnp.concatenate(_split(lw, 2), axis=1))
            cs = cs2[:, 0:LANES] + cs2[:, LANES:2 * LANES]
            for j in range(n_chunks):
                last = j * CHUNK if d == 1 else (j + 1) * CHUNK - 1
                wl_o[d, 0, j, :, sl] = jnp.exp(cs[last:last + 1, :])
            e_up = jnp.exp(-cs)
            at_o[d, 0, :, sl] = (-kk * jnp.exp(cs - lw)).astype(BF16)
            rt_o[d, 0, :, sl] = (r * jnp.exp(cs)).astype(BF16)
            bt_o[d, 0, :, sl] = (bb * e_up).astype(BF16)
            kt_o[d, 0, :, sl] = (k_d * e_up).astype(BF16)
        v_o[0, :, sl] = v.astype(BF16)
        if latent:
            bonus_o[0, :, sl] = _seg_sum(r * rk_ref[:, sl] * ksum, seg) * v
    if latent:
        n_tok = nt * tm
        pos = i * tm + lax.broadcasted_iota(jnp.int32, (tm, LANES), 0)
        group1 = lax.broadcasted_iota(jnp.int32, (tm, LANES), 1) >= D_WIDTH // 4
        pooled = []
        for half in range(2):
            lo, hi = C_IN + half * LANES, C_IN + (half + 1) * LANES
            w_small, w_big = POOL_WINDOWS[2 * half], POOL_WINDOWS[2 * half + 1]
            s_small = None
            s_big = None
            for off in range(-(w_big // 2), w_big - w_big // 2):
                piece = ext[HALO + off:HALO + off + tm, lo:hi]
                s_big = piece if s_big is None else s_big + piece
                if -(w_small // 2) <= off < w_small - w_small // 2:
                    s_small = piece if s_small is None else s_small + piece

            def count(w):
                lo_p = jnp.clip(pos - w // 2, 0, n_tok)
                hi_p = jnp.clip(pos + (w - w // 2), 0, n_tok)
                return (hi_p - lo_p).astype(F32)

            mean = jnp.where(group1, s_big / count(w_big), s_small / count(w_small))
            pooled.append((mean - ext[HALO:HALO + tm, lo:hi]).astype(BF16))
        pm = jnp.concatenate(pooled, axis=1)
        d_o[0] = _dot(pm, pw_ref[...]) * ps_ref[...]


def _chunk_cumsum_matrices(tm):
    t = np.arange(tm)[:, None]
    u = np.arange(tm)[None, :]
    same = (t // CHUNK) == (u // CHUNK)
    return np.stack([same & (u <= t), same & (u >= t)]).astype(np.float32)


def _prep(pc, wts, latent):
    b, n, cin = pc.shape
    tm = ROW_TILE
    cw = C_WIDTH
    per = tm // HALO
    last = n // HALO - 1
    n_chunks = tm // CHUNK
    row = lambda bi, i: (bi, i, 0)
    drow = lambda bi, i: (0, bi, i, 0)
    wts = list(wts)
    wts = wts[:10] + [jnp.asarray(_chunk_cumsum_matrices(tm)).astype(BF16)] + (wts[10:] if latent else [])

    def full(a):
        return pl.BlockSpec(a.shape, lambda bi, i, nd=a.ndim: (0,) * nd)

    in_specs = [pl.BlockSpec((1, tm, cin), row),
                pl.BlockSpec((1, HALO, cin), lambda bi, i: (bi, jnp.maximum(i * per - 1, 0), 0)),
                pl.BlockSpec((1, HALO, cin), lambda bi, i: (bi, jnp.minimum((i + 1) * per, last), 0))]
    in_specs += [full(a) for a in wts]
    out_specs = [pl.BlockSpec((1, tm, cw), row)] + [pl.BlockSpec((2, 1, tm, cw), drow)] * 4
    out_shape = [jax.ShapeDtypeStruct((b, n, cw), BF16)] + [jax.ShapeDtypeStruct((2, b, n, cw), BF16)] * 4
    out_specs.append(pl.BlockSpec((2, 1, n_chunks, 1, cw), lambda bi, i: (0, bi, i, 0, 0)))
    out_shape.append(jax.ShapeDtypeStruct((2, b, n // CHUNK, 1, cw), F32))
    if latent:
        out_specs += [pl.BlockSpec((1, tm, cw), row)] * 2 + [pl.BlockSpec((1, tm, D_WIDTH), row)]
        out_shape += [jax.ShapeDtypeStruct((b, n, cw), F32)] * 2 + [jax.ShapeDtypeStruct((b, n, D_WIDTH), F32)]
    return pl.pallas_call(
        functools.partial(_prep_kernel, latent=latent),
        grid=(b, n // tm),
        in_specs=in_specs,
        out_specs=out_specs,
        out_shape=out_shape,
        scratch_shapes=[pltpu.VMEM((tm + 2 * HALO, cin), F32), pltpu.VMEM((4, tm, cw), F32)],
        compiler_params=_cparams("parallel", "parallel"),
        name="prep",
    )(pc, pc, pc, *wts)


QUAD = 4 * HEAD_DIM


def _quad_masks(rev):
    t = np.arange(CHUNK)[:, None]
    u = np.arange(CHUNK)[None, :]
    before = (u > t) if rev else (u < t)
    masks = [before, before | (u == t), u == t]
    s = 1
    while s < CHUNK:
        blk = (t // (2 * s)) == (u // (2 * s))
        t_late = (t % (2 * s) < s) if rev else (t % (2 * s) >= s)
        u_early = (u % (2 * s) >= s) if rev else (u % (2 * s) < s)
        masks.append(blk & t_late & u_early)
        s *= 2
    masks = np.stack([np.tile(m, (1, QUAD // CHUNK)) for m in masks]).astype(np.float32)
    hid = np.arange(QUAD) // HEAD_DIM
    return masks, (hid[:, None] == hid[None, :]).astype(np.float32)


def _scanq_kernel(v_ref, at_ref, rt_ref, bt_ref, kt_ref, wl_ref, h0_ref, msk_ref, bd_ref, *rest,
                  rev, with_y):
    if with_y:
        y_ref, hT_ref, g_sc = rest
    else:
        hT_ref, g_sc = rest
    tb = pl.program_id(2)
    n_tb = pl.num_programs(2)
    n_chunks = v_ref.shape[1] // CHUNK
    n_quads = v_ref.shape[2] // QUAD
    reps = QUAD // CHUNK

    @pl.when(tb == 0)
    def _():
        g_sc[...] = h0_ref[0]

    bd_mask = bd_ref[...] > 0
    strict = msk_ref[0] > 0
    incl = msk_ref[1] > 0
    eye4 = msk_ref[2]
    n_levels = msk_ref.shape[0] - 3
    nt_dims = (((1,), (1,)), ((), ()))
    tn_dims = (((0,), (0,)), ((), ()))

    def bd(x4):
        return jnp.where(bd_mask, jnp.concatenate([x4] * reps, axis=0), jnp.zeros((QUAD, QUAD), BF16))

    def fold(m):
        m = jnp.where(bd_mask, m, 0.0)
        out = m[0:CHUNK]
        for r in range(1, reps):
            out = out + m[r * CHUNK:(r + 1) * CHUNK]
        return out

    order = list(range(n_chunks - 1, -1, -1) if rev else range(n_chunks))
    chains = [(qq, c) for c in order for qq in range(n_quads)]
    ch = {}
    for key in chains:
        qq, c = key
        rows = slice(c * CHUNK, (c + 1) * CHUNK)
        lanes = slice(qq * QUAD, (qq + 1) * QUAD)
        q = ch[key] = {"rows": rows, "lanes": lanes}
        q["v"] = v_ref[0, rows, lanes]
        q["a"] = at_ref[0, 0, rows, lanes]
        q["b"] = bt_ref[0, 0, rows, lanes]
        q["k"] = kt_ref[0, 0, rows, lanes]
        q["bdv"] = bd(q["v"])
        rhs = jnp.concatenate([bd(q["b"]), bd(q["k"])], axis=0)
        if with_y:
            q["r"] = rt_ref[0, 0, rows, lanes]
            lhs = jnp.concatenate([q["a"], q["r"]], axis=0)
        else:
            lhs = q["a"]
        gram = lax.dot_general(lhs, rhs, nt_dims, preferred_element_type=F32)
        q["ab"] = jnp.where(strict, gram[0:CHUNK, 0:QUAD], 0.0)
        ak = jnp.where(strict, gram[0:CHUNK, QUAD:2 * QUAD], 0.0).astype(BF16)
        if with_y:
            q["rb"] = jnp.where(incl, gram[CHUNK:2 * CHUNK, 0:QUAD], 0.0).astype(BF16)
            rk = jnp.where(incl, gram[CHUNK:2 * CHUNK, QUAD:2 * QUAD], 0.0).astype(BF16)
            kv = _dot(jnp.concatenate([ak, rk], axis=0), q["bdv"])
            q["akv"], q["rkv"] = kv[0:CHUNK], kv[CHUNK:2 * CHUNK]
        else:
            q["akv"] = _dot(ak, q["bdv"])
        q["t"] = eye4 + q["ab"] * msk_ref[3]
    for lvl in range(1, n_levels):
        for key in chains:
            q = ch[key]
            q["x"] = _dot((q["ab"] * msk_ref[3 + lvl]).astype(BF16), bd(q["t"].astype(BF16)))
        for key in chains:
            q = ch[key]
            q["t"] = q["t"] + _dot(q["t"].astype(BF16), bd(q["x"].astype(BF16)))
    for key in chains:
        q = ch[key]
        au = _dot(q["t"].astype(BF16),
                  jnp.concatenate([bd(q["a"]), bd(q["akv"].astype(BF16))], axis=1))
        au_b = au.astype(BF16)
        wl = wl_ref[0, 0, key[1], :, q["lanes"]]
        mc = lax.dot_general(au_b, q["b"], tn_dims, preferred_element_type=F32)
        vk = lax.dot_general(q["v"], q["k"], tn_dims, preferred_element_type=F32)
        q["mt"] = jnp.where(bd_mask, mc[0:QUAD] * wl, 0.0).astype(BF16)
        q["ct"] = fold(mc[QUAD:2 * QUAD] + vk) * wl
        if with_y:
            rbau = _dot(q["rb"], jnp.concatenate([bd(au_b[:, 0:QUAD]), bd(au_b[:, QUAD:2 * QUAD])], axis=1))
            q["rt"] = (q["r"].astype(F32) + rbau[:, 0:QUAD]).astype(BF16)
            q["y0"] = rbau[:, QUAD:2 * QUAD] + q["rkv"]
    g = [g_sc[qq] for qq in range(n_quads)]
    for key in chains:
        qq, c = key
        q = ch[key]
        g_b = g[qq].astype(BF16)
        if with_y:
            y_ref[0, q["rows"], q["lanes"]] = (
                lax.dot_general(q["rt"], bd(g_b), nt_dims, preferred_element_type=F32) + q["y0"])
        g[qq] = g[qq] * wl_ref[0, 0, c, :, q["lanes"]] + _dot(g_b, q["mt"]) + q["ct"]
    for qq in range(n_quads):
        g_sc[qq] = g[qq]

    @pl.when(tb == n_tb - 1)
    def _():
        hT_ref[0] = g_sc[...]


def _scanq(prep, d, h0, rev, with_y):
    v, at, rt, bt, kt, wl = prep
    b, n, cw = v.shape
    tb = min(SCAN_BLOCK, n)
    n_tb = n // tb
    n_quads = cw // QUAD
    sq = SCAN_QUADS
    wide = sq * QUAD
    msk, bdm = _quad_masks(rev)
    tmap = (lambda t: n_tb - 1 - t) if rev else (lambda t: t)
    shared = pl.BlockSpec((1, tb, wide), lambda bi, p, t: (bi, tmap(t), p))
    perdir = pl.BlockSpec((1, 1, tb, wide), lambda bi, p, t: (d, bi, tmap(t), p))
    decay = pl.BlockSpec((1, 1, tb // CHUNK, 1, wide), lambda bi, p, t: (d, bi, tmap(t), 0, p))
    state = pl.BlockSpec((1, sq, CHUNK, QUAD), lambda bi, p, t: (bi, p, 0, 0))
    out_specs = [state]
    out_shape = [jax.ShapeDtypeStruct((b, n_quads, CHUNK, QUAD), F32)]
    if with_y:
        out_specs = [shared] + out_specs
        out_shape = [jax.ShapeDtypeStruct((b, n, cw), F32)] + out_shape
    res = pl.pallas_call(
        functools.partial(_scanq_kernel, rev=rev, with_y=with_y),
        grid=(b, n_quads // sq, n_tb),
        in_specs=[shared, perdir, perdir, perdir, perdir, decay, state,
                  pl.BlockSpec(msk.shape, lambda bi, p, t: (0, 0, 0)),
                  pl.BlockSpec(bdm.shape, lambda bi, p, t: (0, 0))],
        out_specs=out_specs,
        out_shape=out_shape,
        scratch_shapes=[pltpu.VMEM((sq, CHUNK, QUAD), F32)],
        compiler_params=_cparams("parallel", "parallel", "arbitrary"),
        name="scan_rev" if rev else "scan_fwd",
    )(v, at, rt, bt, kt, wl, h0, jnp.asarray(msk), jnp.asarray(bdm))
    return (res[0], res[1]) if with_y else (None, res[0])


def _out1_ffn_kernel(yf_ref, yb_ref, bonus_ref, g_ref, dp_ref, x_ref, mod_ref, lw_ref, lb_ref, seg_ref,
                     w_ref, gain_ref, w1_ref, w2_ref, fg_ref, o_ref):
    seg = seg_ref[...]
    cw = yf_ref.shape[2]
    acc = _dot(dp_ref[0].astype(BF16), w_ref[cw:cw + D_WIDTH, :])
    for pb in range(cw // LANES):
        sl = slice(pb * LANES, (pb + 1) * LANES)
        y = yf_ref[0, :, sl] + yb_ref[0, :, sl]
        mean = _seg_sum(y, seg) * (1.0 / HEAD_DIM)
        dev = y - mean
        var = _seg_sum(dev * dev, seg) * (1.0 / HEAD_DIM)
        yn = dev * lax.rsqrt(var + LNX_EPS) * lw_ref[:, sl] + lb_ref[:, sl]
        z = (yn + bonus_ref[0, :, sl]) * g_ref[0, :, sl]
        acc = acc + _dot(z.astype(BF16), w_ref[sl, :])
    x1 = x_ref[0] + mod_ref[0, 2:3, :] * acc
    o_ref[0] = _ffn_tile(x1, mod_ref, gain_ref, w1_ref, w2_ref, fg_ref, True)


def _out1_ffn(yf, yb, bonus, g, dp, x, mod, lnx_w, lnx_b, seg, w, gain, w1, w2, fgain):
    b, n, d = x.shape
    tm = ROW_TILE
    cw = yf.shape[2]
    row = lambda bi, i: (bi, i, 0)
    const = lambda bi, i: (0, 0)
    return pl.pallas_call(
        _out1_ffn_kernel,
        grid=(b, n // tm),
        in_specs=[pl.BlockSpec((1, tm, cw), row)] * 4
        + [pl.BlockSpec((1, tm, D_WIDTH), row),
           pl.BlockSpec((1, tm, d), row),
           pl.BlockSpec((1, 6, d), lambda bi, i: (bi, 0, 0)),
           pl.BlockSpec((1, cw), const), pl.BlockSpec((1, cw), const),
           pl.BlockSpec((LANES, LANES), const),
           pl.BlockSpec(w.shape, const, pipeline_mode=pl.Buffered(1))]
        + _ffn_specs(d, w1, w2),
        out_specs=pl.BlockSpec((1, tm, d), row),
        out_shape=jax.ShapeDtypeStruct((b, n, d), F32),
        compiler_params=_cparams("parallel", "parallel"),
        name="out1_ffn",
    )(yf, yb, bonus, g, dp, x, mod, lnx_w, lnx_b, seg, w, gain, w1, w2, fgain)


def _rope_tables(n):
    rows = n // GRID_W
    row = jnp.repeat(jnp.arange(rows, dtype=F32), GRID_W)
    col = jnp.tile(jnp.arange(GRID_W, dtype=F32), rows)
    n_freq = HEAD_DIM // 4
    inv = ROPE_THETA ** (-jnp.arange(n_freq, dtype=F32) / n_freq)
    ang = jnp.concatenate([row[:, None] * inv[None, :], col[:, None] * inv[None, :]], axis=-1)
    cos, sin = jnp.cos(ang), jnp.sin(ang)
    cos_t = jnp.tile(cos, (1, LANES // cos.shape[1]))
    sin_t = jnp.tile(jnp.concatenate([-sin, sin], axis=-1), (1, LANES // HEAD_DIM))
    return cos_t, sin_t


def _kv_dup_columns():
    cols = []
    for section in range(4):
        for head in range(2):
            base = AB_Q_COLS + section * 2 * HEAD_DIM + head * HEAD_DIM
            cols += list(range(base, base + HEAD_DIM)) * 2
    return np.concatenate([np.arange(AB_Q_COLS), np.asarray(cols)])


def kernel(x, c, ctx, c_ctx, norm_gain, ada_w, ada_b, ffn_w_in, ffn_w_out, final_gain, ab_w_in, ab_q_gain, ab_k_gain, ab_sink, ab_w_out, cd_w_in, cd_mu, cd_w0, cd_w2, cd_a0, cd_a2, cd_g2, cd_k_k, cd_k_a, cd_r_k, cd_lnx_w, cd_lnx_b, cd_pool_w, cd_pool_scale, cd_w_out):
    b, n, d = x.shape
    nc = ctx.shape[1]
    pad = (-(b + 1)) % 8
    cs = jnp.concatenate([c, c_ctx[None, :], jnp.zeros((pad, d), F32)], axis=0)
    mods = _mods(cs, ada_w, ada_b)
    seg = jnp.asarray(np.kron(np.eye(2), np.ones((HEAD_DIM, HEAD_DIM))).astype(np.float32)).astype(BF16)
    fgain = final_gain.reshape(1, d)

    def layer_mods(i):
        ml = mods[i, :b].reshape(b, 6, d)
        mc = jnp.broadcast_to(mods[i, b].reshape(1, 6, d), (b, 6, d))
        return ml, mc

    ml, mc = layer_mods(0)
    w0 = ab_w_in[0][:, _kv_dup_columns()].astype(BF16)
    gain = norm_gain[0, 0].reshape(1, d)
    qg = jnp.tile(ab_q_gain[0], 2).reshape(1, LANES)
    kg = jnp.tile(ab_k_gain[0], 2).reshape(1, LANES)
    cos_l, sin_l = _rope_tables(n)
    cos_c, sin_c = jnp.ones((nc, LANES), F32), jnp.zeros((nc, LANES), F32)
    q_l, kv_l, vt_l = _proj0(x, ml, gain, w0, cos_l, sin_l, qg, kg, seg)
    q_c, kv_c, vt_c = _proj0(ctx, mc, gain, w0, cos_c, sin_c, qg, kg, seg)
    sink = ab_sink[0]
    w_out0 = ab_w_out[0].astype(BF16)
    gain2 = norm_gain[0, 1].reshape(1, d)
    w1 = ffn_w_in[0].astype(BF16)
    w2 = ffn_w_out[0].astype(BF16)
    yb_l = _attn_b_auto(q_l, kv_c, vt_c, kv_l, vt_l, ab_q_gain[0], ab_k_gain[0])
    xl = _out0_ffn(_attn_a(q_l, kv_c, sink, kv_l), yb_l, x, ml, w_out0, gain2, w1, w2, fgain)
    xc = _out0_ffn(_attn_a(q_c, kv_c, sink), _attn_b(q_c, kv_c), ctx, mc, w_out0, gain2, w1, w2, fgain)

    ml, mc = layer_mods(1)
    gain = norm_gain[1, 0].reshape(1, d)
    w_in1 = cd_w_in[0].astype(BF16)
    pc_l = _proj1(xl, ml, gain, w_in1)
    pc_c = _proj1(xc, mc, gain, w_in1[:, :C_IN])
    zeros = jnp.zeros((DECAY_LORA_PAD, C_WIDTH), F32)
    w2x = jnp.stack([jnp.concatenate([cd_w2[0, 0], zeros]), jnp.concatenate([zeros, cd_w2[0, 1]])])
    a2x = jnp.stack([jnp.concatenate([cd_a2[0, 0], zeros]), jnp.concatenate([zeros, cd_a2[0, 1]])])
    w2x_hi = w2x.astype(BF16)
    w2x_hl = jnp.stack([w2x_hi, (w2x - w2x_hi.astype(F32)).astype(BF16)])
    wts = [jnp.stack([1.0 - cd_mu[0], 0.5 * cd_mu[0]]), cd_w0[0].reshape(2, 1, C_WIDTH), w2x_hl,
           cd_a0[0].reshape(2, 1, C_WIDTH), a2x.astype(BF16), cd_g2[0].astype(BF16),
           cd_k_k[0].reshape(1, C_WIDTH), cd_k_a[0].reshape(1, C_WIDTH),
           cd_r_k[0].reshape(1, C_WIDTH), seg]
    pool_w = jax.scipy.linalg.block_diag(*[cd_pool_w[0, g] for g in range(4)]).astype(BF16)
    pool_wts = [pool_w, cd_pool_scale[0].reshape(1, D_WIDTH)]
    prep_c = _prep(pc_c, wts, False)
    *prep_l, g_l, bonus_l, dp_l = _prep(pc_l, wts + pool_wts, True)
    h_zero = jnp.zeros((b, C_WIDTH // QUAD, CHUNK, QUAD), F32)
    _, h_f = _scanq(prep_c, 0, h_zero, False, False)
    _, h_b = _scanq(prep_c, 1, h_zero, True, False)
    y_f, _ = _scanq(prep_l, 0, h_f, False, True)
    y_b, _ = _scanq(prep_l, 1, h_b, True, True)
    return _out1_ffn(y_f, y_b, bonus_l, g_l, dp_l, xl, ml, cd_lnx_w[0].reshape(1, C_WIDTH),
                     cd_lnx_b[0].reshape(1, C_WIDTH), seg, cd_w_out[0].astype(BF16),
                     norm_gain[1, 1].reshape(1, d), ffn_w_in[1].astype(BF16),
                     ffn_w_out[1].astype(BF16), fgain)
```

```python
import functools

import numpy as np
import jax
import jax.numpy as jnp
from jax import lax
from jax.experimental import pallas as pl
from jax.experimental.pallas import tpu as pltpu

F32 = jnp.float32
BF16 = jnp.bfloat16
HIGHEST = lax.Precision.HIGHEST
LOG2_E = 1.4426950408889634
SCORE_BOUND_SAFE = 60.0

D_MODEL = 1024
GRID_W = 64
HEAD_DIM = 64
ROPE_THETA = 10000.0
NORM_EPS = 1e-6
WINDOW = 128
AB_Q_COLS = 1024
C_WIDTH = 768
C_IN = 2688
D_WIDTH = 256
CD_IN = C_IN + D_WIDTH
LNX_EPS = 64e-5
FFN_HIDDEN = 2816
POOL_WINDOWS = (2, 4, 8, 16)
DECAY_LORA_PAD = 64

LANES = 128
ROW_TILE = 256
HALO = 8
CHUNK = 64
SCAN_BLOCK = 512
SCAN_QUADS = 3
VMEM_LIMIT = 56 * 1024 * 1024


def _cparams(*sem):
    return pltpu.CompilerParams(dimension_semantics=sem, vmem_limit_bytes=VMEM_LIMIT)


def _dot(a, b):
    return jnp.dot(a, b, preferred_element_type=F32)


def _dot32(a, b):
    return jnp.dot(a, b, preferred_element_type=F32, precision=HIGHEST)


def _split(a, terms):
    pieces = []
    for _ in range(terms - 1):
        hi = a.astype(BF16)
        pieces.append(hi)
        a = a - hi.astype(F32)
    pieces.append(a.astype(BF16))
    return pieces


def _seg_sum(z, seg, terms=2):
    out = None
    for piece in _split(z, terms):
        part = _dot(piece, seg)
        out = part if out is None else out + part
    return out


def _sigmoid(x):
    return 1.0 / (1.0 + jnp.exp(-x))


def _norm_mod(x, gain, shift, scale):
    ms = jnp.mean(x * x, axis=-1, keepdims=True)
    return (x * lax.rsqrt(ms + NORM_EPS) * gain) * (1.0 + scale) + shift


def _mods_kernel(c_ref, w_ref, b_ref, o_ref):
    c = c_ref[...]
    o_ref[0] = _dot32(c * _sigmoid(c), w_ref[0]) + b_ref[0]


def _mods(cs, ada_w, ada_b):
    depth, d, n6 = ada_w.shape
    tn = 768
    rows = cs.shape[0]
    return pl.pallas_call(
        _mods_kernel,
        grid=(depth, n6 // tn),
        in_specs=[pl.BlockSpec((rows, d), lambda l, j: (0, 0)),
                  pl.BlockSpec((1, d, tn), lambda l, j: (l, 0, j)),
                  pl.BlockSpec((1, 1, tn), lambda l, j: (l, 0, j))],
        out_specs=pl.BlockSpec((1, rows, tn), lambda l, j: (l, 0, j)),
        out_shape=jax.ShapeDtypeStruct((depth, rows, n6), F32),
        compiler_params=_cparams("arbitrary", "arbitrary"),
        name="mods",
    )(cs, ada_w, ada_b.reshape(depth, 1, n6))


def _proj0_kernel(x_ref, mod_ref, gain_ref, w_ref, cos_ref, sin_ref, qg_ref, kg_ref, seg_ref,
                  q_ref, kv_ref, vt_ref):
    h = _norm_mod(x_ref[0], gain_ref[...], mod_ref[0, 0:1, :], mod_ref[0, 1:2, :])
    p = _dot(h.astype(BF16), w_ref[...])
    tm = p.shape[0]
    cos = cos_ref[...]
    sin = sin_ref[...]
    lane = lax.broadcasted_iota(jnp.int32, (tm, LANES), 1)
    first_half = (lane & (HEAD_DIM - 1)) < HEAD_DIM // 2
    seg = seg_ref[...]

    def rope(z):
        partner = jnp.where(first_half, pltpu.roll(z, LANES - HEAD_DIM // 2, 1),
                            pltpu.roll(z, HEAD_DIM // 2, 1))
        return z * cos + partner * sin

    def head_norm(z, g):
        ms = _seg_sum(z * z, seg, 1) * (1.0 / HEAD_DIM)
        return z * lax.rsqrt(ms + NORM_EPS) * g

    scale = HEAD_DIM ** -0.5 * LOG2_E
    for blk in range(8):
        z = p[:, blk * LANES:(blk + 1) * LANES]
        if blk >= 4:
            z = head_norm(z, qg_ref[...])
        q_ref[0, :, blk * LANES:(blk + 1) * LANES] = (rope(z) * scale).astype(BF16)
    for blk in range(8):
        z = p[:, AB_Q_COLS + blk * LANES:AB_Q_COLS + (blk + 1) * LANES]
        if blk in (4, 5):
            z = head_norm(z, kg_ref[...])
        if blk in (0, 1, 4, 5):
            z = rope(z)
        kv_ref[0, :, blk * LANES:(blk + 1) * LANES] = z.astype(BF16)
        if blk in (6, 7):
            zt = z.T
            top = lax.broadcasted_iota(jnp.int32, zt.shape, 0) < HEAD_DIM
            vt_ref[0, blk - 6] = jnp.where(top, zt, 1.0).astype(BF16)


def _proj0(x, mod, gain, w, cos, sin, qg, kg, seg):
    b, n, d = x.shape
    tm = ROW_TILE
    nw = w.shape[1]
    const = lambda bi, i: (0, 0)
    return pl.pallas_call(
        _proj0_kernel,
        grid=(b, n // tm),
        in_specs=[pl.BlockSpec((1, tm, d), lambda bi, i: (bi, i, 0)),
                  pl.BlockSpec((1, 6, d), lambda bi, i: (bi, 0, 0)),
                  pl.BlockSpec((1, d), const),
                  pl.BlockSpec((d, nw), const),
                  pl.BlockSpec((tm, LANES), lambda bi, i: (i, 0)),
                  pl.BlockSpec((tm, LANES), lambda bi, i: (i, 0)),
                  pl.BlockSpec((1, LANES), const),
                  pl.BlockSpec((1, LANES), const),
                  pl.BlockSpec((LANES, LANES), const)],
        out_specs=[pl.BlockSpec((1, tm, 1024), lambda bi, i: (bi, i, 0)),
                   pl.BlockSpec((1, tm, 1024), lambda bi, i: (bi, i, 0)),
                   pl.BlockSpec((1, 2, LANES, tm), lambda bi, i: (bi, 0, 0, i))],
        out_shape=[jax.ShapeDtypeStruct((b, n, 1024), BF16),
                   jax.ShapeDtypeStruct((b, n, 1024), BF16),
                   jax.ShapeDtypeStruct((b, 2, LANES, n), BF16)],
        compiler_params=_cparams("parallel", "parallel"),
        name="proj0",
    )(x, mod, gain, w, cos, sin, qg, kg, seg)


def _masked_q(q_ref, h, low):
    q2 = q_ref[0, :, (h // 2) * LANES:(h // 2 + 1) * LANES]
    keep = low if h % 2 == 0 else jnp.logical_not(low)
    return jnp.where(keep, q2, jnp.zeros_like(q2))


def _attn_b_kernel(q_ref, kc_ref, vc_ref, *rest, n_kt, has_latent):
    if has_latent:
        k_ref, v_ref, o_ref, m_sc, acc_sc = rest
    else:
        o_ref, m_sc, acc_sc = rest
    kt = pl.program_id(3)
    tq = q_ref.shape[1]
    low = lax.broadcasted_iota(jnp.int32, (tq, LANES), 1) < HEAD_DIM

    def update(k, v):
        tk = k.shape[0]
        v1 = jnp.where(lax.broadcasted_iota(jnp.int32, (tk, LANES), 1) < HEAD_DIM, v, jnp.ones_like(v))
        ss = [lax.dot_general(_masked_q(q_ref, h, low), k, (((1,), (1,)), ((), ())),
                              preferred_element_type=F32) for h in range(4)]
        for h in range(4):
            m_prev = m_sc[h]
            m_new = jnp.maximum(m_prev, jnp.max(ss[h], axis=1, keepdims=True))
            p = jnp.exp2(ss[h] - jnp.concatenate([m_new] * (tk // LANES), axis=1)).astype(BF16)
            acc_sc[h] = jnp.exp2(m_prev - m_new) * acc_sc[h] + _dot(p, v1)
            m_sc[h] = m_new

    @pl.when(kt == 0)
    def _():
        m_sc[...] = jnp.full(m_sc.shape, -jnp.inf, F32)
        acc_sc[...] = jnp.zeros(acc_sc.shape, F32)
        update(kc_ref[0], vc_ref[0])

    if has_latent:
        update(k_ref[0], v_ref[0])

    @pl.when(kt == n_kt - 1)
    def _():
        for pr in range(2):
            a0 = acc_sc[2 * pr]
            a1 = acc_sc[2 * pr + 1]
            o0 = a0 / pltpu.roll(a0, HEAD_DIM, 1)
            o1 = pltpu.roll(a1, HEAD_DIM, 1) / a1
            o_ref[0, :, pr * LANES:(pr + 1) * LANES] = jnp.where(low, o0, o1).astype(BF16)


def _attn_b(q, kv_c, kv=None, tk=1024):
    b, n, _ = q.shape
    nc = kv_c.shape[1]
    tq = ROW_TILE
    has_latent = kv is not None
    n_kt = kv.shape[1] // tk if has_latent else 1
    in_specs = [pl.BlockSpec((1, tq, 2 * LANES), lambda bi, g, i, j: (bi, i, 2 + g)),
                pl.BlockSpec((1, nc, LANES), lambda bi, g, i, j: (bi, 0, 4 + g)),
                pl.BlockSpec((1, nc, LANES), lambda bi, g, i, j: (bi, 0, 6 + g))]
    args = [q, kv_c, kv_c]
    if has_latent:
        in_specs += [pl.BlockSpec((1, tk, LANES), lambda bi, g, i, j: (bi, j, 4 + g)),
                     pl.BlockSpec((1, tk, LANES), lambda bi, g, i, j: (bi, j, 6 + g))]
        args += [kv, kv]
    return pl.pallas_call(
        functools.partial(_attn_b_kernel, n_kt=n_kt, has_latent=has_latent),
        grid=(b, 2, n // tq, n_kt),
        in_specs=in_specs,
        out_specs=pl.BlockSpec((1, tq, 2 * LANES), lambda bi, g, i, j: (bi, i, g)),
        out_shape=jax.ShapeDtypeStruct((b, n, 512), BF16),
        scratch_shapes=[pltpu.VMEM((4, tq, LANES), F32), pltpu.VMEM((4, tq, LANES), F32)],
        compiler_params=_cparams("parallel", "parallel", "parallel", "arbitrary"),
        name="attn_b",
    )(*args)


def _attn_bt_kernel(q_ref, kc_ref, vtc_ref, k_ref, vt_ref, o_ref, acc_sc, *, n_kt):
    kt = pl.program_id(3)
    tq = q_ref.shape[1]
    low = lax.broadcasted_iota(jnp.int32, (tq, LANES), 1) < HEAD_DIM

    def update(k, vt):
        sts = [lax.dot_general(k, _masked_q(q_ref, h, low), (((1,), (1,)), ((), ())),
                               preferred_element_type=F32) for h in range(4)]
        for h in range(4):
            acc_sc[h] = acc_sc[h] + _dot(vt, jnp.exp2(sts[h]).astype(BF16))

    @pl.when(kt == 0)
    def _():
        acc_sc[...] = jnp.zeros(acc_sc.shape, F32)
        update(kc_ref[0], vtc_ref[0, 0])

    update(k_ref[0], vt_ref[0, 0])

    @pl.when(kt == n_kt - 1)
    def _():
        ot = jnp.concatenate([acc_sc[h, 0:HEAD_DIM, :] / acc_sc[h, HEAD_DIM:2 * HEAD_DIM, :]
                              for h in range(4)], axis=0)
        o_ref[0] = ot.T.astype(BF16)


def _attn_bt(q, kv_c, vt_c, kv, vt, tk=4096):
    b, n, _ = q.shape
    nc = kv_c.shape[1]
    tq = ROW_TILE
    tk = min(tk, kv.shape[1])
    n_kt = kv.shape[1] // tk
    return pl.pallas_call(
        functools.partial(_attn_bt_kernel, n_kt=n_kt),
        grid=(b, 2, n // tq, n_kt),
        in_specs=[pl.BlockSpec((1, tq, 2 * LANES), lambda bi, g, i, j: (bi, i, 2 + g)),
                  pl.BlockSpec((1, nc, LANES), lambda bi, g, i, j: (bi, 0, 4 + g)),
                  pl.BlockSpec((1, 1, LANES, nc), lambda bi, g, i, j: (bi, g, 0, 0)),
                  pl.BlockSpec((1, tk, LANES), lambda bi, g, i, j: (bi, j, 4 + g)),
                  pl.BlockSpec((1, 1, LANES, tk), lambda bi, g, i, j: (bi, g, 0, j))],
        out_specs=pl.BlockSpec((1, tq, 2 * LANES), lambda bi, g, i, j: (bi, i, g)),
        out_shape=jax.ShapeDtypeStruct((b, n, 512), BF16),
        scratch_shapes=[pltpu.VMEM((4, LANES, tq), F32)],
        compiler_params=_cparams("parallel", "parallel", "parallel", "arbitrary"),
        name="attn_b_bounded",
    )(q, kv_c, vt_c, kv, vt)


def _attn_b_auto(q, kv_c, vt_c, kv, vt, q_gain, k_gain):
    bound = (1.02 * HEAD_DIM * HEAD_DIM ** -0.5 * LOG2_E) * jnp.max(jnp.abs(q_gain)) * jnp.max(jnp.abs(k_gain))
    return lax.cond(bound <= SCORE_BOUND_SAFE,
                    lambda: _attn_bt(q, kv_c, vt_c, kv, vt),
                    lambda: _attn_b(q, kv_c, kv))


def _attn_a_kernel(sink_ref, q_ref, kc_ref, vc_ref, *rest, has_local):
    if has_local:
        kp_ref, kcur_ref, kn_ref, vp_ref, vcur_ref, vn_ref, o_ref, kbuf, vbuf = rest
    else:
        o_ref, kbuf, vbuf = rest
    g = pl.program_id(1)
    i = pl.program_id(2)
    n_i = pl.num_programs(2)
    tq = q_ref.shape[1]
    nc = kc_ref.shape[1]
    kbuf[0:nc, :] = kc_ref[0]
    vbuf[0:nc, :] = vc_ref[0]
    nk = nc
    if has_local:
        kbuf[nc:nc + WINDOW, :] = kp_ref[0]
        kbuf[nc + WINDOW:nc + WINDOW + tq, :] = kcur_ref[0]
        kbuf[nc + WINDOW + tq:nc + 2 * WINDOW + tq, :] = kn_ref[0]
        vbuf[nc:nc + WINDOW, :] = vp_ref[0]
        vbuf[nc + WINDOW:nc + WINDOW + tq, :] = vcur_ref[0]
        vbuf[nc + WINDOW + tq:nc + 2 * WINDOW + tq, :] = vn_ref[0]
        nk = nc + 2 * WINDOW + tq
        row = lax.broadcasted_iota(jnp.int32, (tq, nk), 0)
        col = lax.broadcasted_iota(jnp.int32, (tq, nk), 1) - nc
        lo_c = jnp.where(i == 0, WINDOW, 0)
        hi_c = jnp.where(i == n_i - 1, WINDOW + tq, 2 * WINDOW + tq)
        valid = (col < 0) | ((col >= row) & (col <= row + 2 * WINDOW) & (col >= lo_c) & (col < hi_c))
    k = kbuf[...]
    v = vbuf[...]
    v1 = jnp.where(lax.broadcasted_iota(jnp.int32, (nk, LANES), 1) < HEAD_DIM, v, jnp.ones_like(v))
    low = lax.broadcasted_iota(jnp.int32, (tq, LANES), 1) < HEAD_DIM
    ss = [lax.dot_general(_masked_q(q_ref, h, low), k, (((1,), (1,)), ((), ())),
                          preferred_element_type=F32) for h in range(4)]
    accs = []
    for h in range(4):
        s = jnp.where(valid, ss[h], -jnp.inf) if has_local else ss[h]
        sink = sink_ref[g * 4 + h] * LOG2_E
        m = jnp.maximum(jnp.max(s, axis=1, keepdims=True), sink)
        p = jnp.exp2(s - m)
        sink_p = jnp.where(low, 0.0, jnp.exp2(sink - m))
        accs.append(_dot(p.astype(BF16), v1) + sink_p)
    for pr in range(2):
        a0, a1 = accs[2 * pr], accs[2 * pr + 1]
        o0 = a0 / pltpu.roll(a0, HEAD_DIM, 1)
        o1 = pltpu.roll(a1, HEAD_DIM, 1) / a1
        o_ref[0, :, pr * LANES:(pr + 1) * LANES] = jnp.where(low, o0, o1).astype(BF16)


def _attn_a(q, kv_c, sink, kv=None):
    b, n, _ = q.shape
    nc = kv_c.shape[1]
    tq = ROW_TILE
    has_local = kv is not None
    in_specs = [pl.BlockSpec(memory_space=pltpu.SMEM),
                pl.BlockSpec((1, tq, 2 * LANES), lambda bi, g, i: (bi, i, g)),
                pl.BlockSpec((1, nc, LANES), lambda bi, g, i: (bi, 0, g)),
                pl.BlockSpec((1, nc, LANES), lambda bi, g, i: (bi, 0, 2 + g))]
    args = [sink, q, kv_c, kv_c]
    nk = nc
    if has_local:
        per = tq // WINDOW
        last = n // WINDOW - 1
        prev_i = lambda i: jnp.maximum(i * per - 1, 0)
        next_i = lambda i: jnp.minimum((i + 1) * per, last)
        for off in (0, 2):
            in_specs += [pl.BlockSpec((1, WINDOW, LANES), lambda bi, g, i, off=off: (bi, prev_i(i), off + g)),
                         pl.BlockSpec((1, tq, LANES), lambda bi, g, i, off=off: (bi, i, off + g)),
                         pl.BlockSpec((1, WINDOW, LANES), lambda bi, g, i, off=off: (bi, next_i(i), off + g))]
            args += [kv, kv, kv]
        nk = nc + 2 * WINDOW + tq
    return pl.pallas_call(
        functools.partial(_attn_a_kernel, has_local=has_local),
        grid=(b, 2, n // tq),
        in_specs=in_specs,
        out_specs=pl.BlockSpec((1, tq, 2 * LANES), lambda bi, g, i: (bi, i, g)),
        out_shape=jax.ShapeDtypeStruct((b, n, 512), BF16),
        scratch_shapes=[pltpu.VMEM((nk, LANES), BF16), pltpu.VMEM((nk, LANES), BF16)],
        compiler_params=_cparams("parallel", "parallel", "parallel"),
        name="attn_a",
    )(*args)


def _ffn_tile(x, mod_ref, gain_ref, w1_ref, w2_ref, fg_ref, final):
    h = _norm_mod(x, gain_ref[...], mod_ref[0, 3:4, :], mod_ref[0, 4:5, :]).astype(BF16)
    hid = w2_ref.shape[0]
    ch = hid // 2
    acc = None
    for j in range(2):
        gate = _dot(h, w1_ref[:, j * ch:(j + 1) * ch])
        up = _dot(h, w1_ref[:, hid + j * ch:hid + (j + 1) * ch])
        act = (gate * _sigmoid(gate) * up).astype(BF16)
        part = _dot(act, w2_ref[j * ch:(j + 1) * ch, :])
        acc = part if acc is None else acc + part
    y = x + mod_ref[0, 5:6, :] * acc
    if final:
        ms = jnp.mean(y * y, axis=-1, keepdims=True)
        y = y * lax.rsqrt(ms + NORM_EPS) * fg_ref[...]
    return y


def _out0_ffn_kernel(ya_ref, yb_ref, x_ref, mod_ref, wo_ref, gain_ref, w1_ref, w2_ref, fg_ref, o_ref):
    half = ya_ref.shape[2]
    y = _dot(ya_ref[0], wo_ref[0:half, :]) + _dot(yb_ref[0], wo_ref[half:2 * half, :])
    x1 = x_ref[0] + mod_ref[0, 2:3, :] * y
    o_ref[0] = _ffn_tile(x1, mod_ref, gain_ref, w1_ref, w2_ref, fg_ref, False)


def _ffn_specs(d, w1, w2):
    const = lambda bi, i: (0, 0)
    return [pl.BlockSpec((1, d), const),
            pl.BlockSpec(w1.shape, const, pipeline_mode=pl.Buffered(1)),
            pl.BlockSpec(w2.shape, const, pipeline_mode=pl.Buffered(1)),
            pl.BlockSpec((1, d), const)]


def _out0_ffn(ya, yb, x, mod, wo, gain, w1, w2, fgain):
    b, n, d = x.shape
    tm = ROW_TILE
    row = lambda bi, i: (bi, i, 0)
    return pl.pallas_call(
        _out0_ffn_kernel,
        grid=(b, n // tm),
        in_specs=[pl.BlockSpec((1, tm, ya.shape[2]), row),
                  pl.BlockSpec((1, tm, yb.shape[2]), row),
                  pl.BlockSpec((1, tm, d), row),
                  pl.BlockSpec((1, 6, d), lambda bi, i: (bi, 0, 0)),
                  pl.BlockSpec(wo.shape, lambda bi, i: (0, 0), pipeline_mode=pl.Buffered(1))]
        + _ffn_specs(d, w1, w2),
        out_specs=pl.BlockSpec((1, tm, d), row),
        out_shape=jax.ShapeDtypeStruct((b, n, d), F32),
        compiler_params=_cparams("parallel", "parallel"),
        name="out0_ffn",
    )(ya, yb, x, mod, wo, gain, w1, w2, fgain)


def _proj1_kernel(x_ref, mod_ref, gain_ref, w_ref, o_ref):
    h = _norm_mod(x_ref[0], gain_ref[...], mod_ref[0, 0:1, :], mod_ref[0, 1:2, :])
    o_ref[0] = _dot(h.astype(BF16), w_ref[...])


def _proj1(x, mod, gain, w):
    b, n, d = x.shape
    tm = ROW_TILE
    nw = w.shape[1]
    return pl.pallas_call(
        _proj1_kernel,
        grid=(b, n // tm),
        in_specs=[pl.BlockSpec((1, tm, d), lambda bi, i: (bi, i, 0)),
                  pl.BlockSpec((1, 6, d), lambda bi, i: (bi, 0, 0)),
                  pl.BlockSpec((1, d), lambda bi, i: (0, 0)),
                  pl.BlockSpec((d, nw), lambda bi, i: (0, 0))],
        out_specs=pl.BlockSpec((1, tm, nw), lambda bi, i: (bi, i, 0)),
        out_shape=jax.ShapeDtypeStruct((b, n, nw), F32),
        compiler_params=_cparams("parallel", "parallel"),
        name="proj1",
    )(x, mod, gain, w)


def _prep_kernel(pc_ref, prev_ref, next_ref, mu_ref, w0_ref, w2_ref, a0_ref, a2_ref, g2_ref,
                 kk_ref, ka_ref, rk_ref, seg_ref, tri_ref, *rest, latent):
    if latent:
        pw_ref, ps_ref, v_o, at_o, rt_o, bt_o, kt_o, wl_o, g_o, bonus_o, d_o, ext, tmp = rest
    else:
        v_o, at_o, rt_o, bt_o, kt_o, wl_o, ext, tmp = rest
    i = pl.program_id(1)
    nt = pl.num_programs(1)
    tm = pc_ref.shape[1]
    cw = C_WIDTH
    ext[HALO:HALO + tm, :] = pc_ref[0]
    ext[0:HALO, :] = jnp.where(i > 0, prev_ref[0], 0.0)
    ext[HALO + tm:2 * HALO + tm, :] = jnp.where(i < nt - 1, next_ref[0], 0.0)

    def mixed(lo, hi):
        nb = ext[HALO - 1:HALO - 1 + tm, lo:hi] + ext[HALO + 1:HALO + 1 + tm, lo:hi]
        return ext[HALO:HALO + tm, lo:hi] * mu_ref[0:1, lo:hi] + nb * mu_ref[1:2, lo:hi]

    lora = mixed(3 * cw, C_IN)
    tw = jnp.tanh(lora[:, 0:LANES])
    xa = lora[:, LANES:2 * LANES]
    if latent:
        g_o[0] = _dot(_sigmoid(lora[:, 2 * LANES:3 * LANES]).astype(BF16), g2_ref[...])
    tw_hi, tw_lo = _split(tw, 2)
    xa_b = xa.astype(BF16)
    for d in range(2):
        z = w0_ref[d] + _dot(tw_hi, w2_ref[0, d]) + _dot(tw_lo, w2_ref[0, d]) + _dot(tw_hi, w2_ref[1, d])
        w_log = -(jnp.maximum(-z, 0.0) + jnp.log(1.0 + jnp.exp(-jnp.abs(z)))) - 0.5
        tmp[d] = -LOG2_E * jnp.exp(w_log)
        tmp[2 + d] = _sigmoid(a0_ref[d] + _dot(xa_b, a2_ref[d]))
    seg = seg_ref[...]
    n_chunks = tm // CHUNK
    for pb in range(cw // LANES):
        sl = slice(pb * LANES, (pb + 1) * LANES)
        r = mixed(pb * LANES, (pb + 1) * LANES)
        k = mixed(cw + pb * LANES, cw + (pb + 1) * LANES)
        v = mixed(2 * cw + pb * LANES, 2 * cw + (pb + 1) * LANES)
        kk = k * kk_ref[:, sl]
        kk = kk / jnp.maximum(jnp.sqrt(_seg_sum(kk * kk, seg, 1)), 1e-12)
        ksum = jnp.zeros_like(k)
        for d in range(2):
            lw = tmp[d, :, sl]
            a = tmp[2 + d, :, sl]
            k_d = k * (1.0 + (a - 1.0) * ka_ref[:, sl])
            bb = kk * a
            ksum = ksum + k_d
            cs2 = _dot(tri_ref[d], jnp.concatenate(_split(lw, 2), axis=1))
            cs = cs2[:, 0:LANES] + cs2[:, LANES:2 * LANES]
            for j in range(n_chunks):
                last = j * CHUNK if d == 1 else (j + 1) * CHUNK - 1
                wl_o[d, 0, j, :, sl] = jnp.exp2(cs[last:last + 1, :])
            e_up = jnp.exp2(-cs)
            at_o[d, 0, :, sl] = (-kk * jnp.exp2(cs - lw)).astype(BF16)
            rt_o[d, 0, :, sl] = (r * jnp.exp2(cs)).astype(BF16)
            bt_o[d, 0, :, sl] = (bb * e_up).astype(BF16)
            kt_o[d, 0, :, sl] = (k_d * e_up).astype(BF16)
        v_o[0, :, sl] = v.astype(BF16)
        if latent:
            bonus_o[0, :, sl] = _seg_sum(r * rk_ref[:, sl] * ksum, seg) * v
    if latent:
        n_tok = nt * tm
        pos = i * tm + lax.broadcasted_iota(jnp.int32, (tm, LANES), 0)
        group1 = lax.broadcasted_iota(jnp.int32, (tm, LANES), 1) >= D_WIDTH // 4
        pooled = []
        for half in range(2):
            lo, hi = C_IN + half * LANES, C_IN + (half + 1) * LANES
            w_small, w_big = POOL_WINDOWS[2 * half], POOL_WINDOWS[2 * half + 1]
            s_small = None
            s_big = None
            for off in range(-(w_big // 2), w_big - w_big // 2):
                piece = ext[HALO + off:HALO + off + tm, lo:hi]
                s_big = piece if s_big is None else s_big + piece
                if -(w_small // 2) <= off < w_small - w_small // 2:
                    s_small = piece if s_small is None else s_small + piece

            def count(w):
                lo_p = jnp.clip(pos - w // 2, 0, n_tok)
                hi_p = jnp.clip(pos + (w - w // 2), 0, n_tok)
                return (hi_p - lo_p).astype(F32)

            mean = jnp.where(group1, s_big / count(w_big), s_small / count(w_small))
            pooled.append((mean - ext[HALO:HALO + tm, lo:hi]).astype(BF16))
        pm = jnp.concatenate(pooled, axis=1)
        d_o[0] = _dot(pm, pw_ref[...]) * ps_ref[...]


def _chunk_cumsum_matrices(tm):
    t = np.arange(tm)[:, None]
    u = np.arange(tm)[None, :]
    same = (t // CHUNK) == (u // CHUNK)
    return np.stack([same & (u <= t), same & (u >= t)]).astype(np.float32)


def _prep(pc, wts, latent):
    b, n, cin = pc.shape
    tm = ROW_TILE
    cw = C_WIDTH
    per = tm // HALO
    last = n // HALO - 1
    n_chunks = tm // CHUNK
    row = lambda bi, i: (bi, i, 0)
    drow = lambda bi, i: (0, bi, i, 0)
    wts = list(wts)
    wts = wts[:10] + [jnp.asarray(_chunk_cumsum_matrices(tm)).astype(BF16)] + (wts[10:] if latent else [])

    def full(a):
        return pl.BlockSpec(a.shape, lambda bi, i, nd=a.ndim: (0,) * nd)

    in_specs = [pl.BlockSpec((1, tm, cin), row),
                pl.BlockSpec((1, HALO, cin), lambda bi, i: (bi, jnp.maximum(i * per - 1, 0), 0)),
                pl.BlockSpec((1, HALO, cin), lambda bi, i: (bi, jnp.minimum((i + 1) * per, last), 0))]
    in_specs += [full(a) for a in wts]
    out_specs = [pl.BlockSpec((1, tm, cw), row)] + [pl.BlockSpec((2, 1, tm, cw), drow)] * 4
    out_shape = [jax.ShapeDtypeStruct((b, n, cw), BF16)] + [jax.ShapeDtypeStruct((2, b, n, cw), BF16)] * 4
    out_specs.append(pl.BlockSpec((2, 1, n_chunks, 1, cw), lambda bi, i: (0, bi, i, 0, 0)))
    out_shape.append(jax.ShapeDtypeStruct((2, b, n // CHUNK, 1, cw), F32))
    if latent:
        out_specs += [pl.BlockSpec((1, tm, cw), row)] * 2 + [pl.BlockSpec((1, tm, D_WIDTH), row)]
        out_shape += [jax.ShapeDtypeStruct((b, n, cw), F32)] * 2 + [jax.ShapeDtypeStruct((b, n, D_WIDTH), F32)]
    return pl.pallas_call(
        functools.partial(_prep_kernel, latent=latent),
        grid=(b, n // tm),
        in_specs=in_specs,
        out_specs=out_specs,
        out_shape=out_shape,
        scratch_shapes=[pltpu.VMEM((tm + 2 * HALO, cin), F32), pltpu.VMEM((4, tm, cw), F32)],
        compiler_params=_cparams("parallel", "parallel"),
        name="prep",
    )(pc, pc, pc, *wts)


QUAD = 4 * HEAD_DIM


def _quad_masks(rev):
    t = np.arange(CHUNK)[:, None]
    u = np.arange(CHUNK)[None, :]
    before = (u > t) if rev else (u < t)
    masks = [before, before | (u == t), u == t]
    s = 1
    while s < CHUNK:
        blk = (t // (2 * s)) == (u // (2 * s))
        t_late = (t % (2 * s) < s) if rev else (t % (2 * s) >= s)
        u_early = (u % (2 * s) >= s) if rev else (u % (2 * s) < s)
        masks.append(blk & t_late & u_early)
        s *= 2
    masks = np.stack([np.tile(m, (1, QUAD // CHUNK)) for m in masks]).astype(np.float32)
    hid = np.arange(QUAD) // HEAD_DIM
    return masks, (hid[:, None] == hid[None, :]).astype(np.float32)


def _scanq_kernel(v_ref, at_ref, rt_ref, bt_ref, kt_ref, wl_ref, h0_ref, msk_ref, bd_ref, *rest,
                  rev, with_y):
    if with_y:
        y_ref, hT_ref, g_sc = rest
    else:
        hT_ref, g_sc = rest
    tb = pl.program_id(2)
    n_tb = pl.num_programs(2)
    n_chunks = v_ref.shape[1] // CHUNK
    n_quads = v_ref.shape[2] // QUAD
    reps = QUAD // CHUNK

    @pl.when(tb == 0)
    def _():
        g_sc[...] = h0_ref[0]

    bd_mask = bd_ref[...] > 0
    strict = msk_ref[0] > 0
    incl = msk_ref[1] > 0
    eye4 = msk_ref[2]
    n_levels = msk_ref.shape[0] - 3
    nt_dims = (((1,), (1,)), ((), ()))
    tn_dims = (((0,), (0,)), ((), ()))

    def bd(x4):
        return jnp.where(bd_mask, jnp.concatenate([x4] * reps, axis=0), jnp.zeros((QUAD, QUAD), BF16))

    def fold(m):
        m = jnp.where(bd_mask, m, 0.0)
        out = m[0:CHUNK]
        for r in range(1, reps):
            out = out + m[r * CHUNK:(r + 1) * CHUNK]
        return out

    order = list(range(n_chunks - 1, -1, -1) if rev else range(n_chunks))
    chains = [(qq, c) for c in order for qq in range(n_quads)]
    ch = {}
    for key in chains:
        qq, c = key
        rows = slice(c * CHUNK, (c + 1) * CHUNK)
        lanes = slice(qq * QUAD, (qq + 1) * QUAD)
        q = ch[key] = {"rows": rows, "lanes": lanes}
        q["v"] = v_ref[0, rows, lanes]
        q["a"] = at_ref[0, 0, rows, lanes]
        q["b"] = bt_ref[0, 0, rows, lanes]
        q["k"] = kt_ref[0, 0, rows, lanes]
        q["bdv"] = bd(q["v"])
        rhs = jnp.concatenate([bd(q["b"]), bd(q["k"])], axis=0)
        if with_y:
            q["r"] = rt_ref[0, 0, rows, lanes]
            lhs = jnp.concatenate([q["a"], q["r"]], axis=0)
        else:
            lhs = q["a"]
        gram = lax.dot_general(lhs, rhs, nt_dims, preferred_element_type=F32)
        q["ab"] = jnp.where(strict, gram[0:CHUNK, 0:QUAD], 0.0)
        ak = jnp.where(strict, gram[0:CHUNK, QUAD:2 * QUAD], 0.0).astype(BF16)
        if with_y:
            q["rb"] = jnp.where(incl, gram[CHUNK:2 * CHUNK, 0:QUAD], 0.0).astype(BF16)
            rk = jnp.where(incl, gram[CHUNK:2 * CHUNK, QUAD:2 * QUAD], 0.0).astype(BF16)
            kv = _dot(jnp.concatenate([ak, rk], axis=0), q["bdv"])
            q["akv"], q["rkv"] = kv[0:CHUNK], kv[CHUNK:2 * CHUNK]
        else:
            q["akv"] = _dot(ak, q["bdv"])
        q["t"] = eye4 + q["ab"] * msk_ref[3]
    for lvl in range(1, n_levels):
        for key in chains:
            q = ch[key]
            q["x"] = _dot((q["ab"] * msk_ref[3 + lvl]).astype(BF16), bd(q["t"].astype(BF16)))
        for key in chains:
            q = ch[key]
            q["t"] = q["t"] + _dot(q["t"].astype(BF16), bd(q["x"].astype(BF16)))
    for key in chains:
        q = ch[key]
        au = _dot(q["t"].astype(BF16),
                  jnp.concatenate([bd(q["a"]), bd(q["akv"].astype(BF16))], axis=1))
        au_b = au.astype(BF16)
        wl = wl_ref[0, 0, key[1], :, q["lanes"]]
        mc = lax.dot_general(au_b, q["b"], tn_dims, preferred_element_type=F32)
        vk = lax.dot_general(q["v"], q["k"], tn_dims, preferred_element_type=F32)
        q["mt"] = jnp.where(bd_mask, mc[0:QUAD] * wl, 0.0).astype(BF16)
        q["ct"] = fold(mc[QUAD:2 * QUAD] + vk) * wl
        if with_y:
            rbau = _dot(q["rb"], jnp.concatenate([bd(au_b[:, 0:QUAD]), bd(au_b[:, QUAD:2 * QUAD])], axis=1))
            q["rt"] = (q["r"].astype(F32) + rbau[:, 0:QUAD]).astype(BF16)
            q["y0"] = rbau[:, QUAD:2 * QUAD] + q["rkv"]
    g = [g_sc[qq] for qq in range(n_quads)]
    for key in chains:
        qq, c = key
        q = ch[key]
        g_b = g[qq].astype(BF16)
        if with_y:
            y_ref[0, q["rows"], q["lanes"]] = (
                lax.dot_general(q["rt"], bd(g_b), nt_dims, preferred_element_type=F32) + q["y0"])
        g[qq] = g[qq] * wl_ref[0, 0, c, :, q["lanes"]] + _dot(g_b, q["mt"]) + q["ct"]
    for qq in range(n_quads):
        g_sc[qq] = g[qq]

    @pl.when(tb == n_tb - 1)
    def _():
        hT_ref[0] = g_sc[...]


def _scanq(prep, d, h0, rev, with_y):
    v, at, rt, bt, kt, wl = prep
    b, n, cw = v.shape
    tb = min(SCAN_BLOCK, n)
    n_tb = n // tb
    n_quads = cw // QUAD
    sq = SCAN_QUADS
    wide = sq * QUAD
    msk, bdm = _quad_masks(rev)
    tmap = (lambda t: n_tb - 1 - t) if rev else (lambda t: t)
    shared = pl.BlockSpec((1, tb, wide), lambda bi, p, t: (bi, tmap(t), p))
    perdir = pl.BlockSpec((1, 1, tb, wide), lambda bi, p, t: (d, bi, tmap(t), p))
    decay = pl.BlockSpec((1, 1, tb // CHUNK, 1, wide), lambda bi, p, t: (d, bi, tmap(t), 0, p))
    state = pl.BlockSpec((1, sq, CHUNK, QUAD), lambda bi, p, t: (bi, p, 0, 0))
    out_specs = [state]
    out_shape = [jax.ShapeDtypeStruct((b, n_quads, CHUNK, QUAD), F32)]
    if with_y:
        out_specs = [shared] + out_specs
        out_shape = [jax.ShapeDtypeStruct((b, n, cw), F32)] + out_shape
    res = pl.pallas_call(
        functools.partial(_scanq_kernel, rev=rev, with_y=with_y),
        grid=(b, n_quads // sq, n_tb),
        in_specs=[shared, perdir, perdir, perdir, perdir, decay, state,
                  pl.BlockSpec(msk.shape, lambda bi, p, t: (0, 0, 0)),
                  pl.BlockSpec(bdm.shape, lambda bi, p, t: (0, 0))],
        out_specs=out_specs,
        out_shape=out_shape,
        scratch_shapes=[pltpu.VMEM((sq, CHUNK, QUAD), F32)],
        compiler_params=_cparams("parallel", "parallel", "arbitrary"),
        name="scan_rev" if rev else "scan_fwd",
    )(v, at, rt, bt, kt, wl, h0, jnp.asarray(msk), jnp.asarray(bdm))
    return (res[0], res[1]) if with_y else (None, res[0])


def _out1_ffn_kernel(yf_ref, yb_ref, bonus_ref, g_ref, dp_ref, x_ref, mod_ref, lw_ref, lb_ref, seg_ref,
                     w_ref, gain_ref, w1_ref, w2_ref, fg_ref, o_ref):
    seg = seg_ref[...]
    cw = yf_ref.shape[2]
    acc = _dot(dp_ref[0].astype(BF16), w_ref[cw:cw + D_WIDTH, :])
    wide = seg.shape[0]
    for pb in range(cw // wide):
        sl = slice(pb * wide, (pb + 1) * wide)
        y = yf_ref[0, :, sl] + yb_ref[0, :, sl]
        mean = _seg_sum(y, seg, 1) * (1.0 / HEAD_DIM)
        dev = y - mean
        var = _seg_sum(dev * dev, seg, 1) * (1.0 / HEAD_DIM)
        yn = dev * lax.rsqrt(var + LNX_EPS) * lw_ref[:, sl] + lb_ref[:, sl]
        z = (yn + bonus_ref[0, :, sl]) * g_ref[0, :, sl]
        acc = acc + _dot(z.astype(BF16), w_ref[sl, :])
    x1 = x_ref[0] + mod_ref[0, 2:3, :] * acc
    o_ref[0] = _ffn_tile(x1, mod_ref, gain_ref, w1_ref, w2_ref, fg_ref, True)


def _out1_ffn(yf, yb, bonus, g, dp, x, mod, lnx_w, lnx_b, seg, w, gain, w1, w2, fgain):
    b, n, d = x.shape
    tm = ROW_TILE
    cw = yf.shape[2]
    row = lambda bi, i: (bi, i, 0)
    const = lambda bi, i: (0, 0)
    return pl.pallas_call(
        _out1_ffn_kernel,
        grid=(b, n // tm),
        in_specs=[pl.BlockSpec((1, tm, cw), row)] * 4
        + [pl.BlockSpec((1, tm, D_WIDTH), row),
           pl.BlockSpec((1, tm, d), row),
           pl.BlockSpec((1, 6, d), lambda bi, i: (bi, 0, 0)),
           pl.BlockSpec((1, cw), const), pl.BlockSpec((1, cw), const),
           pl.BlockSpec(seg.shape, const),
           pl.BlockSpec(w.shape, const, pipeline_mode=pl.Buffered(1))]
        + _ffn_specs(d, w1, w2),
        out_specs=pl.BlockSpec((1, tm, d), row),
        out_shape=jax.ShapeDtypeStruct((b, n, d), F32),
        compiler_params=_cparams("parallel", "parallel"),
        name="out1_ffn",
    )(yf, yb, bonus, g, dp, x, mod, lnx_w, lnx_b, seg, w, gain, w1, w2, fgain)


def _rope_tables(n):
    rows = n // GRID_W
    row = jnp.repeat(jnp.arange(rows, dtype=F32), GRID_W)
    col = jnp.tile(jnp.arange(GRID_W, dtype=F32), rows)
    n_freq = HEAD_DIM // 4
    inv = ROPE_THETA ** (-jnp.arange(n_freq, dtype=F32) / n_freq)
    ang = jnp.concatenate([row[:, None] * inv[None, :], col[:, None] * inv[None, :]], axis=-1)
    cos, sin = jnp.cos(ang), jnp.sin(ang)
    cos_t = jnp.tile(cos, (1, LANES // cos.shape[1]))
    sin_t = jnp.tile(jnp.concatenate([-sin, sin], axis=-1), (1, LANES // HEAD_DIM))
    return cos_t, sin_t


def _kv_dup_columns():
    cols = []
    for section in range(4):
        for head in range(2):
            base = AB_Q_COLS + section * 2 * HEAD_DIM + head * HEAD_DIM
            cols += list(range(base, base + HEAD_DIM)) * 2
    return np.concatenate([np.arange(AB_Q_COLS), np.asarray(cols)])


def kernel(x, c, ctx, c_ctx, norm_gain, ada_w, ada_b, ffn_w_in, ffn_w_out, final_gain, ab_w_in, ab_q_gain, ab_k_gain, ab_sink, ab_w_out, cd_w_in, cd_mu, cd_w0, cd_w2, cd_a0, cd_a2, cd_g2, cd_k_k, cd_k_a, cd_r_k, cd_lnx_w, cd_lnx_b, cd_pool_w, cd_pool_scale, cd_w_out):
    b, n, d = x.shape
    nc = ctx.shape[1]
    pad = (-(b + 1)) % 8
    cs = jnp.concatenate([c, c_ctx[None, :], jnp.zeros((pad, d), F32)], axis=0)
    mods = _mods(cs, ada_w, ada_b)
    seg = jnp.asarray(np.kron(np.eye(2), np.ones((HEAD_DIM, HEAD_DIM))).astype(np.float32)).astype(BF16)
    fgain = final_gain.reshape(1, d)

    def layer_mods(i):
        ml = mods[i, :b].reshape(b, 6, d)
        mc = jnp.broadcast_to(mods[i, b].reshape(1, 6, d), (b, 6, d))
        return ml, mc

    ml, mc = layer_mods(0)
    w0 = ab_w_in[0][:, _kv_dup_columns()].astype(BF16)
    gain = norm_gain[0, 0].reshape(1, d)
    qg = jnp.tile(ab_q_gain[0], 2).reshape(1, LANES)
    kg = jnp.tile(ab_k_gain[0], 2).reshape(1, LANES)
    cos_l, sin_l = _rope_tables(n)
    cos_c, sin_c = jnp.ones((nc, LANES), F32), jnp.zeros((nc, LANES), F32)
    q_l, kv_l, vt_l = _proj0(x, ml, gain, w0, cos_l, sin_l, qg, kg, seg)
    q_c, kv_c, vt_c = _proj0(ctx, mc, gain, w0, cos_c, sin_c, qg, kg, seg)
    sink = ab_sink[0]
    w_out0 = ab_w_out[0].astype(BF16)
    gain2 = norm_gain[0, 1].reshape(1, d)
    w1 = ffn_w_in[0].astype(BF16)
    w2 = ffn_w_out[0].astype(BF16)
    yb_l = _attn_b_auto(q_l, kv_c, vt_c, kv_l, vt_l, ab_q_gain[0], ab_k_gain[0])
    xl = _out0_ffn(_attn_a(q_l, kv_c, sink, kv_l), yb_l, x, ml, w_out0, gain2, w1, w2, fgain)
    xc = _out0_ffn(_attn_a(q_c, kv_c, sink), _attn_b(q_c, kv_c), ctx, mc, w_out0, gain2, w1, w2, fgain)

    ml, mc = layer_mods(1)
    gain = norm_gain[1, 0].reshape(1, d)
    w_in1 = cd_w_in[0].astype(BF16)
    pc_l = _proj1(xl, ml, gain, w_in1)
    pc_c = _proj1(xc, mc, gain, w_in1[:, :C_IN])
    zeros = jnp.zeros((DECAY_LORA_PAD, C_WIDTH), F32)
    w2x = jnp.stack([jnp.concatenate([cd_w2[0, 0], zeros]), jnp.concatenate([zeros, cd_w2[0, 1]])])
    a2x = jnp.stack([jnp.concatenate([cd_a2[0, 0], zeros]), jnp.concatenate([zeros, cd_a2[0, 1]])])
    w2x_hi = w2x.astype(BF16)
    w2x_hl = jnp.stack([w2x_hi, (w2x - w2x_hi.astype(F32)).astype(BF16)])
    wts = [jnp.stack([1.0 - cd_mu[0], 0.5 * cd_mu[0]]), cd_w0[0].reshape(2, 1, C_WIDTH), w2x_hl,
           cd_a0[0].reshape(2, 1, C_WIDTH), a2x.astype(BF16), cd_g2[0].astype(BF16),
           cd_k_k[0].reshape(1, C_WIDTH), cd_k_a[0].reshape(1, C_WIDTH),
           cd_r_k[0].reshape(1, C_WIDTH), seg]
    pool_w = jax.scipy.linalg.block_diag(*[cd_pool_w[0, g] for g in range(4)]).astype(BF16)
    pool_wts = [pool_w, cd_pool_scale[0].reshape(1, D_WIDTH)]
    prep_c = _prep(pc_c, wts, False)
    *prep_l, g_l, bonus_l, dp_l = _prep(pc_l, wts + pool_wts, True)
    h_zero = jnp.zeros((b, C_WIDTH // QUAD, CHUNK, QUAD), F32)
    _, h_f = _scanq(prep_c, 0, h_zero, False, False)
    _, h_b = _scanq(prep_c, 1, h_zero, True, False)
    y_f, _ = _scanq(prep_l, 0, h_f, False, True)
    y_b, _ = _scanq(prep_l, 1, h_b, True, True)
    seg4 = jnp.asarray(np.kron(np.eye(4), np.ones((HEAD_DIM, HEAD_DIM))).astype(np.float32)).astype(BF16)
    return _out1_ffn(y_f, y_b, bonus_l, g_l, dp_l, xl, ml, cd_lnx_w[0].reshape(1, C_WIDTH),
                     cd_lnx_b[0].reshape(1, C_WIDTH), seg4, cd_w_out[0].astype(BF16),
                     norm_gain[1, 1].reshape(1, d), ffn_w_in[1].astype(BF16),
                     ffn_w_out[1].astype(BF16), fgain)
```

```python
import functools

import numpy as np
import jax
import jax.numpy as jnp
from jax import lax
from jax.experimental import pallas as pl
from jax.experimental.pallas import tpu as pltpu

F32 = jnp.float32
BF16 = jnp.bfloat16
HIGHEST = lax.Precision.HIGHEST
LOG2_E = 1.4426950408889634
SCORE_BOUND_SAFE = 60.0

D_MODEL = 1024
GRID_W = 64
HEAD_DIM = 64
ROPE_THETA = 10000.0
NORM_EPS = 1e-6
WINDOW = 128
AB_Q_COLS = 1024
C_WIDTH = 768
C_IN = 2688
D_WIDTH = 256
CD_IN = C_IN + D_WIDTH
LNX_EPS = 64e-5
FFN_HIDDEN = 2816
POOL_WINDOWS = (2, 4, 8, 16)
DECAY_LORA_PAD = 64

LANES = 128
ROW_TILE = 256
FFN_TILE = 512
HALO = 8
CHUNK = 64
SCAN_BLOCK = 512
SCAN_QUADS = 3
VMEM_LIMIT = 56 * 1024 * 1024


def _cparams(*sem):
    return pltpu.CompilerParams(dimension_semantics=sem, vmem_limit_bytes=VMEM_LIMIT)


def _dot(a, b):
    return jnp.dot(a, b, preferred_element_type=F32)


def _dot32(a, b):
    return jnp.dot(a, b, preferred_element_type=F32, precision=HIGHEST)


def _split(a, terms):
    pieces = []
    for _ in range(terms - 1):
        hi = a.astype(BF16)
        pieces.append(hi)
        a = a - hi.astype(F32)
    pieces.append(a.astype(BF16))
    return pieces


def _seg_sum(z, seg, terms=2):
    out = None
    for piece in _split(z, terms):
        part = _dot(piece, seg)
        out = part if out is None else out + part
    return out


def _sigmoid(x):
    return 1.0 / (1.0 + jnp.exp(-x))


def _norm_mod(x, gain, shift, scale):
    ms = jnp.mean(x * x, axis=-1, keepdims=True)
    return (x * lax.rsqrt(ms + NORM_EPS) * gain) * (1.0 + scale) + shift


def _mods_kernel(c_ref, w_ref, b_ref, o_ref):
    c = c_ref[...]
    o_ref[0] = _dot32(c * _sigmoid(c), w_ref[0]) + b_ref[0]


def _mods(cs, ada_w, ada_b):
    depth, d, n6 = ada_w.shape
    tn = 768
    rows = cs.shape[0]
    return pl.pallas_call(
        _mods_kernel,
        grid=(depth, n6 // tn),
        in_specs=[pl.BlockSpec((rows, d), lambda l, j: (0, 0)),
                  pl.BlockSpec((1, d, tn), lambda l, j: (l, 0, j)),
                  pl.BlockSpec((1, 1, tn), lambda l, j: (l, 0, j))],
        out_specs=pl.BlockSpec((1, rows, tn), lambda l, j: (l, 0, j)),
        out_shape=jax.ShapeDtypeStruct((depth, rows, n6), F32),
        compiler_params=_cparams("arbitrary", "arbitrary"),
        name="mods",
    )(cs, ada_w, ada_b.reshape(depth, 1, n6))


def _proj0_kernel(x_ref, mod_ref, gain_ref, w_ref, cos_ref, sin_ref, qg_ref, kg_ref, seg_ref,
                  q_ref, kv_ref, vt_ref):
    h = _norm_mod(x_ref[0], gain_ref[...], mod_ref[0, 0:1, :], mod_ref[0, 1:2, :])
    p = _dot(h.astype(BF16), w_ref[...])
    tm = p.shape[0]
    cos = cos_ref[...]
    sin = sin_ref[...]
    lane = lax.broadcasted_iota(jnp.int32, (tm, LANES), 1)
    first_half = (lane & (HEAD_DIM - 1)) < HEAD_DIM // 2
    seg = seg_ref[...]

    def rope(z):
        partner = jnp.where(first_half, pltpu.roll(z, LANES - HEAD_DIM // 2, 1),
                            pltpu.roll(z, HEAD_DIM // 2, 1))
        return z * cos + partner * sin

    def head_norm(z, g):
        ms = _seg_sum(z * z, seg, 1) * (1.0 / HEAD_DIM)
        return z * lax.rsqrt(ms + NORM_EPS) * g

    scale = HEAD_DIM ** -0.5 * LOG2_E
    for blk in range(8):
        z = p[:, blk * LANES:(blk + 1) * LANES]
        if blk >= 4:
            z = head_norm(z, qg_ref[...])
        q_ref[0, :, blk * LANES:(blk + 1) * LANES] = (rope(z) * scale).astype(BF16)
    for blk in range(8):
        z = p[:, AB_Q_COLS + blk * LANES:AB_Q_COLS + (blk + 1) * LANES]
        if blk in (4, 5):
            z = head_norm(z, kg_ref[...])
        if blk in (0, 1, 4, 5):
            z = rope(z)
        kv_ref[0, :, blk * LANES:(blk + 1) * LANES] = z.astype(BF16)
        if blk in (6, 7):
            zt = z.T
            top = lax.broadcasted_iota(jnp.int32, zt.shape, 0) < HEAD_DIM
            vt_ref[0, blk - 6] = jnp.where(top, zt, 1.0).astype(BF16)


def _proj0(x, mod, gain, w, cos, sin, qg, kg, seg):
    b, n, d = x.shape
    tm = ROW_TILE
    nw = w.shape[1]
    const = lambda bi, i: (0, 0)
    return pl.pallas_call(
        _proj0_kernel,
        grid=(b, n // tm),
        in_specs=[pl.BlockSpec((1, tm, d), lambda bi, i: (bi, i, 0)),
                  pl.BlockSpec((1, 6, d), lambda bi, i: (bi, 0, 0)),
                  pl.BlockSpec((1, d), const),
                  pl.BlockSpec((d, nw), const),
                  pl.BlockSpec((tm, LANES), lambda bi, i: (i, 0)),
                  pl.BlockSpec((tm, LANES), lambda bi, i: (i, 0)),
                  pl.BlockSpec((1, LANES), const),
                  pl.BlockSpec((1, LANES), const),
                  pl.BlockSpec((LANES, LANES), const)],
        out_specs=[pl.BlockSpec((1, tm, 1024), lambda bi, i: (bi, i, 0)),
                   pl.BlockSpec((1, tm, 1024), lambda bi, i: (bi, i, 0)),
                   pl.BlockSpec((1, 2, LANES, tm), lambda bi, i: (bi, 0, 0, i))],
        out_shape=[jax.ShapeDtypeStruct((b, n, 1024), BF16),
                   jax.ShapeDtypeStruct((b, n, 1024), BF16),
                   jax.ShapeDtypeStruct((b, 2, LANES, n), BF16)],
        compiler_params=_cparams("parallel", "parallel"),
        name="proj0",
    )(x, mod, gain, w, cos, sin, qg, kg, seg)


def _masked_q(q_ref, h, low):
    q2 = q_ref[0, :, (h // 2) * LANES:(h // 2 + 1) * LANES]
    keep = low if h % 2 == 0 else jnp.logical_not(low)
    return jnp.where(keep, q2, jnp.zeros_like(q2))


def _attn_b_kernel(q_ref, kc_ref, vc_ref, *rest, n_kt, has_latent):
    if has_latent:
        k_ref, v_ref, o_ref, m_sc, acc_sc = rest
    else:
        o_ref, m_sc, acc_sc = rest
    kt = pl.program_id(3)
    tq = q_ref.shape[1]
    low = lax.broadcasted_iota(jnp.int32, (tq, LANES), 1) < HEAD_DIM

    def update(k, v):
        tk = k.shape[0]
        v1 = jnp.where(lax.broadcasted_iota(jnp.int32, (tk, LANES), 1) < HEAD_DIM, v, jnp.ones_like(v))
        ss = [lax.dot_general(_masked_q(q_ref, h, low), k, (((1,), (1,)), ((), ())),
                              preferred_element_type=F32) for h in range(4)]
        for h in range(4):
            m_prev = m_sc[h]
            m_new = jnp.maximum(m_prev, jnp.max(ss[h], axis=1, keepdims=True))
            p = jnp.exp2(ss[h] - jnp.concatenate([m_new] * (tk // LANES), axis=1)).astype(BF16)
            acc_sc[h] = jnp.exp2(m_prev - m_new) * acc_sc[h] + _dot(p, v1)
            m_sc[h] = m_new

    @pl.when(kt == 0)
    def _():
        m_sc[...] = jnp.full(m_sc.shape, -jnp.inf, F32)
        acc_sc[...] = jnp.zeros(acc_sc.shape, F32)
        update(kc_ref[0], vc_ref[0])

    if has_latent:
        update(k_ref[0], v_ref[0])

    @pl.when(kt == n_kt - 1)
    def _():
        for pr in range(2):
            a0 = acc_sc[2 * pr]
            a1 = acc_sc[2 * pr + 1]
            o0 = a0 / pltpu.roll(a0, HEAD_DIM, 1)
            o1 = pltpu.roll(a1, HEAD_DIM, 1) / a1
            o_ref[0, :, pr * LANES:(pr + 1) * LANES] = jnp.where(low, o0, o1).astype(BF16)


def _attn_b(q, kv_c, kv=None, tk=1024):
    b, n, _ = q.shape
    nc = kv_c.shape[1]
    tq = ROW_TILE
    has_latent = kv is not None
    n_kt = kv.shape[1] // tk if has_latent else 1
    in_specs = [pl.BlockSpec((1, tq, 2 * LANES), lambda bi, g, i, j: (bi, i, 2 + g)),
                pl.BlockSpec((1, nc, LANES), lambda bi, g, i, j: (bi, 0, 4 + g)),
                pl.BlockSpec((1, nc, LANES), lambda bi, g, i, j: (bi, 0, 6 + g))]
    args = [q, kv_c, kv_c]
    if has_latent:
        in_specs += [pl.BlockSpec((1, tk, LANES), lambda bi, g, i, j: (bi, j, 4 + g)),
                     pl.BlockSpec((1, tk, LANES), lambda bi, g, i, j: (bi, j, 6 + g))]
        args += [kv, kv]
    return pl.pallas_call(
        functools.partial(_attn_b_kernel, n_kt=n_kt, has_latent=has_latent),
        grid=(b, 2, n // tq, n_kt),
        in_specs=in_specs,
        out_specs=pl.BlockSpec((1, tq, 2 * LANES), lambda bi, g, i, j: (bi, i, g)),
        out_shape=jax.ShapeDtypeStruct((b, n, 512), BF16),
        scratch_shapes=[pltpu.VMEM((4, tq, LANES), F32), pltpu.VMEM((4, tq, LANES), F32)],
        compiler_params=_cparams("parallel", "parallel", "parallel", "arbitrary"),
        name="attn_b",
    )(*args)


def _attn_bt_kernel(q_ref, kc_ref, vtc_ref, k_ref, vt_ref, o_ref, acc_sc, *, n_kt):
    kt = pl.program_id(3)
    tq = q_ref.shape[1]
    low = lax.broadcasted_iota(jnp.int32, (tq, LANES), 1) < HEAD_DIM

    def update(k, vt):
        sts = [lax.dot_general(k, _masked_q(q_ref, h, low), (((1,), (1,)), ((), ())),
                               preferred_element_type=F32) for h in range(4)]
        for h in range(4):
            acc_sc[h] = acc_sc[h] + _dot(vt, jnp.exp2(sts[h]).astype(BF16))

    @pl.when(kt == 0)
    def _():
        acc_sc[...] = jnp.zeros(acc_sc.shape, F32)
        update(kc_ref[0], vtc_ref[0, 0])

    update(k_ref[0], vt_ref[0, 0])

    @pl.when(kt == n_kt - 1)
    def _():
        ot = jnp.concatenate([acc_sc[h, 0:HEAD_DIM, :] / acc_sc[h, HEAD_DIM:2 * HEAD_DIM, :]
                              for h in range(4)], axis=0)
        o_ref[0] = ot.T.astype(BF16)


def _attn_bt(q, kv_c, vt_c, kv, vt, tk=4096):
    b, n, _ = q.shape
    nc = kv_c.shape[1]
    tq = ROW_TILE
    tk = min(tk, kv.shape[1])
    n_kt = kv.shape[1] // tk
    return pl.pallas_call(
        functools.partial(_attn_bt_kernel, n_kt=n_kt),
        grid=(b, 2, n // tq, n_kt),
        in_specs=[pl.BlockSpec((1, tq, 2 * LANES), lambda bi, g, i, j: (bi, i, 2 + g)),
                  pl.BlockSpec((1, nc, LANES), lambda bi, g, i, j: (bi, 0, 4 + g)),
                  pl.BlockSpec((1, 1, LANES, nc), lambda bi, g, i, j: (bi, g, 0, 0)),
                  pl.BlockSpec((1, tk, LANES), lambda bi, g, i, j: (bi, j, 4 + g)),
                  pl.BlockSpec((1, 1, LANES, tk), lambda bi, g, i, j: (bi, g, 0, j))],
        out_specs=pl.BlockSpec((1, tq, 2 * LANES), lambda bi, g, i, j: (bi, i, g)),
        out_shape=jax.ShapeDtypeStruct((b, n, 512), BF16),
        scratch_shapes=[pltpu.VMEM((4, LANES, tq), F32)],
        compiler_params=_cparams("parallel", "parallel", "parallel", "arbitrary"),
        name="attn_b_bounded",
    )(q, kv_c, vt_c, kv, vt)


def _attn_b_auto(q, kv_c, vt_c, kv, vt, q_gain, k_gain):
    bound = (1.02 * HEAD_DIM * HEAD_DIM ** -0.5 * LOG2_E) * jnp.max(jnp.abs(q_gain)) * jnp.max(jnp.abs(k_gain))
    return lax.cond(bound <= SCORE_BOUND_SAFE,
                    lambda: _attn_bt(q, kv_c, vt_c, kv, vt),
                    lambda: _attn_b(q, kv_c, kv))


def _attn_a_kernel(sink_ref, q_ref, kc_ref, vc_ref, *rest, has_local):
    if has_local:
        kp_ref, kcur_ref, kn_ref, vp_ref, vcur_ref, vn_ref, o_ref, kbuf, vbuf = rest
    else:
        o_ref, kbuf, vbuf = rest
    g = pl.program_id(1)
    i = pl.program_id(2)
    n_i = pl.num_programs(2)
    tq = q_ref.shape[1]
    nc = kc_ref.shape[1]
    kbuf[0:nc, :] = kc_ref[0]
    vbuf[0:nc, :] = vc_ref[0]
    nk = nc
    if has_local:
        kbuf[nc:nc + WINDOW, :] = kp_ref[0]
        kbuf[nc + WINDOW:nc + WINDOW + tq, :] = kcur_ref[0]
        kbuf[nc + WINDOW + tq:nc + 2 * WINDOW + tq, :] = kn_ref[0]
        vbuf[nc:nc + WINDOW, :] = vp_ref[0]
        vbuf[nc + WINDOW:nc + WINDOW + tq, :] = vcur_ref[0]
        vbuf[nc + WINDOW + tq:nc + 2 * WINDOW + tq, :] = vn_ref[0]
        nk = nc + 2 * WINDOW + tq
        row = lax.broadcasted_iota(jnp.int32, (tq, nk), 0)
        col = lax.broadcasted_iota(jnp.int32, (tq, nk), 1) - nc
        lo_c = jnp.where(i == 0, WINDOW, 0)
        hi_c = jnp.where(i == n_i - 1, WINDOW + tq, 2 * WINDOW + tq)
        valid = (col < 0) | ((col >= row) & (col <= row + 2 * WINDOW) & (col >= lo_c) & (col < hi_c))
    k = kbuf[...]
    v = vbuf[...]
    v1 = jnp.where(lax.broadcasted_iota(jnp.int32, (nk, LANES), 1) < HEAD_DIM, v, jnp.ones_like(v))
    low = lax.broadcasted_iota(jnp.int32, (tq, LANES), 1) < HEAD_DIM
    ss = [lax.dot_general(_masked_q(q_ref, h, low), k, (((1,), (1,)), ((), ())),
                          preferred_element_type=F32) for h in range(4)]
    accs = []
    for h in range(4):
        s = jnp.where(valid, ss[h], -jnp.inf) if has_local else ss[h]
        sink = sink_ref[g * 4 + h] * LOG2_E
        m = jnp.maximum(jnp.max(s, axis=1, keepdims=True), sink)
        p = jnp.exp2(s - m)
        sink_p = jnp.where(low, 0.0, jnp.exp2(sink - m))
        accs.append(_dot(p.astype(BF16), v1) + sink_p)
    for pr in range(2):
        a0, a1 = accs[2 * pr], accs[2 * pr + 1]
        o0 = a0 / pltpu.roll(a0, HEAD_DIM, 1)
        o1 = pltpu.roll(a1, HEAD_DIM, 1) / a1
        o_ref[0, :, pr * LANES:(pr + 1) * LANES] = jnp.where(low, o0, o1).astype(BF16)


def _attn_a(q, kv_c, sink, kv=None):
    b, n, _ = q.shape
    nc = kv_c.shape[1]
    tq = ROW_TILE
    has_local = kv is not None
    in_specs = [pl.BlockSpec(memory_space=pltpu.SMEM),
                pl.BlockSpec((1, tq, 2 * LANES), lambda bi, g, i: (bi, i, g)),
                pl.BlockSpec((1, nc, LANES), lambda bi, g, i: (bi, 0, g)),
                pl.BlockSpec((1, nc, LANES), lambda bi, g, i: (bi, 0, 2 + g))]
    args = [sink, q, kv_c, kv_c]
    nk = nc
    if has_local:
        per = tq // WINDOW
        last = n // WINDOW - 1
        prev_i = lambda i: jnp.maximum(i * per - 1, 0)
        next_i = lambda i: jnp.minimum((i + 1) * per, last)
        for off in (0, 2):
            in_specs += [pl.BlockSpec((1, WINDOW, LANES), lambda bi, g, i, off=off: (bi, prev_i(i), off + g)),
                         pl.BlockSpec((1, tq, LANES), lambda bi, g, i, off=off: (bi, i, off + g)),
                         pl.BlockSpec((1, WINDOW, LANES), lambda bi, g, i, off=off: (bi, next_i(i), off + g))]
            args += [kv, kv, kv]
        nk = nc + 2 * WINDOW + tq
    return pl.pallas_call(
        functools.partial(_attn_a_kernel, has_local=has_local),
        grid=(b, 2, n // tq),
        in_specs=in_specs,
        out_specs=pl.BlockSpec((1, tq, 2 * LANES), lambda bi, g, i: (bi, i, g)),
        out_shape=jax.ShapeDtypeStruct((b, n, 512), BF16),
        scratch_shapes=[pltpu.VMEM((nk, LANES), BF16), pltpu.VMEM((nk, LANES), BF16)],
        compiler_params=_cparams("parallel", "parallel", "parallel"),
        name="attn_a",
    )(*args)


def _ffn_tile(x, mod_ref, gain_ref, w1_ref, w2_ref, fg_ref, final):
    h = _norm_mod(x, gain_ref[...], mod_ref[0, 3:4, :], mod_ref[0, 4:5, :]).astype(BF16)
    hid = w2_ref.shape[0]
    ch = hid // 2
    acc = None
    for j in range(2):
        gate = _dot(h, w1_ref[:, j * ch:(j + 1) * ch])
        up = _dot(h, w1_ref[:, hid + j * ch:hid + (j + 1) * ch])
        act = (gate * _sigmoid(gate) * up).astype(BF16)
        part = _dot(act, w2_ref[j * ch:(j + 1) * ch, :])
        acc = part if acc is None else acc + part
    y = x + mod_ref[0, 5:6, :] * acc
    if final:
        ms = jnp.mean(y * y, axis=-1, keepdims=True)
        y = y * lax.rsqrt(ms + NORM_EPS) * fg_ref[...]
    return y


def _out0_ffn_kernel(ya_ref, yb_ref, x_ref, mod_ref, wo_ref, gain_ref, w1_ref, w2_ref, fg_ref, o_ref):
    half = ya_ref.shape[2]
    y = _dot(ya_ref[0], wo_ref[0:half, :]) + _dot(yb_ref[0], wo_ref[half:2 * half, :])
    x1 = x_ref[0] + mod_ref[0, 2:3, :] * y
    o_ref[0] = _ffn_tile(x1, mod_ref, gain_ref, w1_ref, w2_ref, fg_ref, False)


def _ffn_specs(d, w1, w2):
    const = lambda bi, i: (0, 0)
    return [pl.BlockSpec((1, d), const),
            pl.BlockSpec(w1.shape, const, pipeline_mode=pl.Buffered(1)),
            pl.BlockSpec(w2.shape, const, pipeline_mode=pl.Buffered(1)),
            pl.BlockSpec((1, d), const)]


def _out0_ffn(ya, yb, x, mod, wo, gain, w1, w2, fgain):
    b, n, d = x.shape
    tm = min(FFN_TILE, n)
    row = lambda bi, i: (bi, i, 0)
    return pl.pallas_call(
        _out0_ffn_kernel,
        grid=(b, n // tm),
        in_specs=[pl.BlockSpec((1, tm, ya.shape[2]), row),
                  pl.BlockSpec((1, tm, yb.shape[2]), row),
                  pl.BlockSpec((1, tm, d), row),
                  pl.BlockSpec((1, 6, d), lambda bi, i: (bi, 0, 0)),
                  pl.BlockSpec(wo.shape, lambda bi, i: (0, 0), pipeline_mode=pl.Buffered(1))]
        + _ffn_specs(d, w1, w2),
        out_specs=pl.BlockSpec((1, tm, d), row),
        out_shape=jax.ShapeDtypeStruct((b, n, d), F32),
        compiler_params=_cparams("parallel", "parallel"),
        name="out0_ffn",
    )(ya, yb, x, mod, wo, gain, w1, w2, fgain)


def _proj1_kernel(x_ref, mod_ref, gain_ref, w_ref, o_ref):
    h = _norm_mod(x_ref[0], gain_ref[...], mod_ref[0, 0:1, :], mod_ref[0, 1:2, :])
    o_ref[0] = _dot(h.astype(BF16), w_ref[...])


def _proj1(x, mod, gain, w):
    b, n, d = x.shape
    tm = min(FFN_TILE, n)
    nw = w.shape[1]
    return pl.pallas_call(
        _proj1_kernel,
        grid=(b, n // tm),
        in_specs=[pl.BlockSpec((1, tm, d), lambda bi, i: (bi, i, 0)),
                  pl.BlockSpec((1, 6, d), lambda bi, i: (bi, 0, 0)),
                  pl.BlockSpec((1, d), lambda bi, i: (0, 0)),
                  pl.BlockSpec((d, nw), lambda bi, i: (0, 0))],
        out_specs=pl.BlockSpec((1, tm, nw), lambda bi, i: (bi, i, 0)),
        out_shape=jax.ShapeDtypeStruct((b, n, nw), F32),
        compiler_params=_cparams("parallel", "parallel"),
        name="proj1",
    )(x, mod, gain, w)


def _prep_kernel(pc_ref, prev_ref, next_ref, mu_ref, w0_ref, w2_ref, a0_ref, a2_ref, g2_ref,
                 kk_ref, ka_ref, rk_ref, seg_ref, tri_ref, *rest, latent):
    if latent:
        pw_ref, ps_ref, v_o, at_o, rt_o, bt_o, kt_o, wl_o, g_o, bonus_o, d_o, ext, tmp = rest
    else:
        v_o, at_o, rt_o, bt_o, kt_o, wl_o, ext, tmp = rest
    i = pl.program_id(1)
    nt = pl.num_programs(1)
    tm = pc_ref.shape[1]
    cw = C_WIDTH
    ext[HALO:HALO + tm, :] = pc_ref[0]
    ext[0:HALO, :] = jnp.where(i > 0, prev_ref[0], 0.0)
    ext[HALO + tm:2 * HALO + tm, :] = jnp.where(i < nt - 1, next_ref[0], 0.0)

    def mixed(lo, hi):
        nb = ext[HALO - 1:HALO - 1 + tm, lo:hi] + ext[HALO + 1:HALO + 1 + tm, lo:hi]
        return ext[HALO:HALO + tm, lo:hi] * mu_ref[0:1, lo:hi] + nb * mu_ref[1:2, lo:hi]

    lora = mixed(3 * cw, C_IN)
    tw = jnp.tanh(lora[:, 0:LANES])
    xa = lora[:, LANES:2 * LANES]
    if latent:
        g_o[0] = _dot(_sigmoid(lora[:, 2 * LANES:3 * LANES]).astype(BF16), g2_ref[...])
    tw_hi, tw_lo = _split(tw, 2)
    xa_b = xa.astype(BF16)
    for d in range(2):
        z = w0_ref[d] + _dot(tw_hi, w2_ref[0, d]) + _dot(tw_lo, w2_ref[0, d]) + _dot(tw_hi, w2_ref[1, d])
        w_log = -(jnp.maximum(-z, 0.0) + jnp.log(1.0 + jnp.exp(-jnp.abs(z)))) - 0.5
        tmp[d] = -LOG2_E * jnp.exp(w_log)
        tmp[2 + d] = _sigmoid(a0_ref[d] + _dot(xa_b, a2_ref[d]))
    seg = seg_ref[...]
    n_chunks = tm // CHUNK
    for pb in range(cw // LANES):
        sl = slice(pb * LANES, (pb + 1) * LANES)
        r = mixed(pb * LANES, (pb + 1) * LANES)
        k = mixed(cw + pb * LANES, cw + (pb + 1) * LANES)
        v = mixed(2 * cw + pb * LANES, 2 * cw + (pb + 1) * LANES)
        kk = k * kk_ref[:, sl]
        kk = kk / jnp.maximum(jnp.sqrt(_seg_sum(kk * kk, seg, 1)), 1e-12)
        neg_kk = -kk
        kka = k * ka_ref[:, sl]
        ksum = jnp.zeros_like(k)
        for d in range(2):
            lw = tmp[d, :, sl]
            a = tmp[2 + d, :, sl]
            k_d = k + kka * (a - 1.0)
            bb = kk * a
            ksum = ksum + k_d
            cs2 = _dot(tri_ref[d], jnp.concatenate(_split(lw, 2), axis=1))
            cs = cs2[:, 0:LANES] + cs2[:, LANES:2 * LANES]
            for j in range(n_chunks):
                last = j * CHUNK if d == 1 else (j + 1) * CHUNK - 1
                wl_o[d, 0, j, :, sl] = jnp.exp2(cs[last:last + 1, :])
            e_up = jnp.exp2(-cs)
            at_o[d, 0, :, sl] = (neg_kk * jnp.exp2(cs - lw)).astype(BF16)
            rt_o[d, 0, :, sl] = (r * jnp.exp2(cs)).astype(BF16)
            bt_o[d, 0, :, sl] = (bb * e_up).astype(BF16)
            kt_o[d, 0, :, sl] = (k_d * e_up).astype(BF16)
        v_o[0, :, sl] = v.astype(BF16)
        if latent:
            bonus_o[0, :, sl] = _seg_sum(r * rk_ref[:, sl] * ksum, seg) * v
    if latent:
        n_tok = nt * tm
        pos = i * tm + lax.broadcasted_iota(jnp.int32, (tm, LANES), 0)
        group1 = lax.broadcasted_iota(jnp.int32, (tm, LANES), 1) >= D_WIDTH // 4
        pooled = []
        for half in range(2):
            lo, hi = C_IN + half * LANES, C_IN + (half + 1) * LANES
            w_small, w_big = POOL_WINDOWS[2 * half], POOL_WINDOWS[2 * half + 1]
            s_small = None
            s_big = None
            for off in range(-(w_big // 2), w_big - w_big // 2):
                piece = ext[HALO + off:HALO + off + tm, lo:hi]
                s_big = piece if s_big is None else s_big + piece
                if -(w_small // 2) <= off < w_small - w_small // 2:
                    s_small = piece if s_small is None else s_small + piece

            def count(w):
                lo_p = jnp.clip(pos - w // 2, 0, n_tok)
                hi_p = jnp.clip(pos + (w - w // 2), 0, n_tok)
                return (hi_p - lo_p).astype(F32)

            mean = jnp.where(group1, s_big / count(w_big), s_small / count(w_small))
            pooled.append((mean - ext[HALO:HALO + tm, lo:hi]).astype(BF16))
        pm = jnp.concatenate(pooled, axis=1)
        d_o[0] = _dot(pm, pw_ref[...]) * ps_ref[...]


def _chunk_cumsum_matrices(tm):
    t = np.arange(tm)[:, None]
    u = np.arange(tm)[None, :]
    same = (t // CHUNK) == (u // CHUNK)
    return np.stack([same & (u <= t), same & (u >= t)]).astype(np.float32)


def _prep(pc, wts, latent):
    b, n, cin = pc.shape
    tm = ROW_TILE
    cw = C_WIDTH
    per = tm // HALO
    last = n // HALO - 1
    n_chunks = tm // CHUNK
    row = lambda bi, i: (bi, i, 0)
    drow = lambda bi, i: (0, bi, i, 0)
    wts = list(wts)
    wts = wts[:10] + [jnp.asarray(_chunk_cumsum_matrices(tm)).astype(BF16)] + (wts[10:] if latent else [])

    def full(a):
        return pl.BlockSpec(a.shape, lambda bi, i, nd=a.ndim: (0,) * nd)

    in_specs = [pl.BlockSpec((1, tm, cin), row),
                pl.BlockSpec((1, HALO, cin), lambda bi, i: (bi, jnp.maximum(i * per - 1, 0), 0)),
                pl.BlockSpec((1, HALO, cin), lambda bi, i: (bi, jnp.minimum((i + 1) * per, last), 0))]
    in_specs += [full(a) for a in wts]
    out_specs = [pl.BlockSpec((1, tm, cw), row)] + [pl.BlockSpec((2, 1, tm, cw), drow)] * 4
    out_shape = [jax.ShapeDtypeStruct((b, n, cw), BF16)] + [jax.ShapeDtypeStruct((2, b, n, cw), BF16)] * 4
    out_specs.append(pl.BlockSpec((2, 1, n_chunks, 1, cw), lambda bi, i: (0, bi, i, 0, 0)))
    out_shape.append(jax.ShapeDtypeStruct((2, b, n // CHUNK, 1, cw), F32))
    if latent:
        out_specs += [pl.BlockSpec((1, tm, cw), row)] * 2 + [pl.BlockSpec((1, tm, D_WIDTH), row)]
        out_shape += [jax.ShapeDtypeStruct((b, n, cw), F32)] * 2 + [jax.ShapeDtypeStruct((b, n, D_WIDTH), F32)]
    return pl.pallas_call(
        functools.partial(_prep_kernel, latent=latent),
        grid=(b, n // tm),
        in_specs=in_specs,
        out_specs=out_specs,
        out_shape=out_shape,
        scratch_shapes=[pltpu.VMEM((tm + 2 * HALO, cin), F32), pltpu.VMEM((4, tm, cw), F32)],
        compiler_params=_cparams("parallel", "parallel"),
        name="prep",
    )(pc, pc, pc, *wts)


QUAD = 4 * HEAD_DIM


def _quad_masks(rev):
    t = np.arange(CHUNK)[:, None]
    u = np.arange(CHUNK)[None, :]
    before = (u > t) if rev else (u < t)
    masks = [before, before | (u == t), u == t]
    s = 1
    while s < CHUNK:
        blk = (t // (2 * s)) == (u // (2 * s))
        t_late = (t % (2 * s) < s) if rev else (t % (2 * s) >= s)
        u_early = (u % (2 * s) >= s) if rev else (u % (2 * s) < s)
        masks.append(blk & t_late & u_early)
        s *= 2
    masks = np.stack([np.tile(m, (1, QUAD // CHUNK)) for m in masks]).astype(np.float32)
    hid = np.arange(QUAD) // HEAD_DIM
    return masks, (hid[:, None] == hid[None, :]).astype(np.float32)


def _scanq_kernel(v_ref, at_ref, rt_ref, bt_ref, kt_ref, wl_ref, h0_ref, msk_ref, bd_ref, *rest,
                  rev, with_y):
    if with_y:
        y_ref, hT_ref, g_sc = rest
    else:
        hT_ref, g_sc = rest
    tb = pl.program_id(2)
    n_tb = pl.num_programs(2)
    n_chunks = v_ref.shape[1] // CHUNK
    n_quads = v_ref.shape[2] // QUAD
    reps = QUAD // CHUNK

    @pl.when(tb == 0)
    def _():
        g_sc[...] = h0_ref[0]

    bd_mask = bd_ref[...] > 0
    strict = msk_ref[0] > 0
    incl = msk_ref[1] > 0
    eye4 = msk_ref[2]
    n_levels = msk_ref.shape[0] - 3
    nt_dims = (((1,), (1,)), ((), ()))
    tn_dims = (((0,), (0,)), ((), ()))

    def bd(x4):
        return jnp.where(bd_mask, jnp.concatenate([x4] * reps, axis=0), jnp.zeros((QUAD, QUAD), BF16))

    def fold(m):
        m = jnp.where(bd_mask, m, 0.0)
        out = m[0:CHUNK]
        for r in range(1, reps):
            out = out + m[r * CHUNK:(r + 1) * CHUNK]
        return out

    order = list(range(n_chunks - 1, -1, -1) if rev else range(n_chunks))
    chains = [(qq, c) for c in order for qq in range(n_quads)]
    ch = {}
    for key in chains:
        qq, c = key
        rows = slice(c * CHUNK, (c + 1) * CHUNK)
        lanes = slice(qq * QUAD, (qq + 1) * QUAD)
        q = ch[key] = {"rows": rows, "lanes": lanes}
        q["v"] = v_ref[0, rows, lanes]
        q["a"] = at_ref[0, 0, rows, lanes]
        q["b"] = bt_ref[0, 0, rows, lanes]
        q["k"] = kt_ref[0, 0, rows, lanes]
        q["bdv"] = bd(q["v"])
        rhs = jnp.concatenate([bd(q["b"]), bd(q["k"])], axis=0)
        if with_y:
            q["r"] = rt_ref[0, 0, rows, lanes]
            lhs = jnp.concatenate([q["a"], q["r"]], axis=0)
        else:
            lhs = q["a"]
        gram = lax.dot_general(lhs, rhs, nt_dims, preferred_element_type=F32)
        q["ab"] = jnp.where(strict, gram[0:CHUNK, 0:QUAD], 0.0)
        ak = jnp.where(strict, gram[0:CHUNK, QUAD:2 * QUAD], 0.0).astype(BF16)
        if with_y:
            q["rb"] = jnp.where(incl, gram[CHUNK:2 * CHUNK, 0:QUAD], 0.0).astype(BF16)
            rk = jnp.where(incl, gram[CHUNK:2 * CHUNK, QUAD:2 * QUAD], 0.0).astype(BF16)
            kv = _dot(jnp.concatenate([ak, rk], axis=0), q["bdv"])
            q["akv"], q["rkv"] = kv[0:CHUNK], kv[CHUNK:2 * CHUNK]
        else:
            q["akv"] = _dot(ak, q["bdv"])
        q["t"] = eye4 + q["ab"] * msk_ref[3]
    for lvl in range(1, n_levels):
        for key in chains:
            q = ch[key]
            q["x"] = _dot((q["ab"] * msk_ref[3 + lvl]).astype(BF16), bd(q["t"].astype(BF16)))
        for key in chains:
            q = ch[key]
            q["t"] = q["t"] + _dot(q["t"].astype(BF16), bd(q["x"].astype(BF16)))
    for key in chains:
        q = ch[key]
        au = _dot(q["t"].astype(BF16),
                  jnp.concatenate([bd(q["a"]), bd(q["akv"].astype(BF16))], axis=1))
        au_b = au.astype(BF16)
        wl = wl_ref[0, 0, key[1], :, q["lanes"]]
        mc = lax.dot_general(au_b, q["b"], tn_dims, preferred_element_type=F32)
        vk = lax.dot_general(q["v"], q["k"], tn_dims, preferred_element_type=F32)
        q["mt"] = jnp.where(bd_mask, mc[0:QUAD] * wl, 0.0).astype(BF16)
        q["ct"] = fold(mc[QUAD:2 * QUAD] + vk) * wl
        if with_y:
            rbau = _dot(q["rb"], jnp.concatenate([bd(au_b[:, 0:QUAD]), bd(au_b[:, QUAD:2 * QUAD])], axis=1))
            q["rt"] = (q["r"].astype(F32) + rbau[:, 0:QUAD]).astype(BF16)
            q["y0"] = rbau[:, QUAD:2 * QUAD] + q["rkv"]
    g = [g_sc[qq] for qq in range(n_quads)]
    for key in chains:
        qq, c = key
        q = ch[key]
        g_b = g[qq].astype(BF16)
        if with_y:
            y_ref[0, q["rows"], q["lanes"]] = (
                lax.dot_general(q["rt"], bd(g_b), nt_dims, preferred_element_type=F32) + q["y0"])
        g[qq] = g[qq] * wl_ref[0, 0, c, :, q["lanes"]] + _dot(g_b, q["mt"]) + q["ct"]
    for qq in range(n_quads):
        g_sc[qq] = g[qq]

    @pl.when(tb == n_tb - 1)
    def _():
        hT_ref[0] = g_sc[...]


def _scanq(prep, d, h0, rev, with_y):
    v, at, rt, bt, kt, wl = prep
    b, n, cw = v.shape
    tb = min(SCAN_BLOCK, n)
    n_tb = n // tb
    n_quads = cw // QUAD
    sq = SCAN_QUADS
    wide = sq * QUAD
    msk, bdm = _quad_masks(rev)
    tmap = (lambda t: n_tb - 1 - t) if rev else (lambda t: t)
    shared = pl.BlockSpec((1, tb, wide), lambda bi, p, t: (bi, tmap(t), p))
    perdir = pl.BlockSpec((1, 1, tb, wide), lambda bi, p, t: (d, bi, tmap(t), p))
    decay = pl.BlockSpec((1, 1, tb // CHUNK, 1, wide), lambda bi, p, t: (d, bi, tmap(t), 0, p))
    state = pl.BlockSpec((1, sq, CHUNK, QUAD), lambda bi, p, t: (bi, p, 0, 0))
    out_specs = [state]
    out_shape = [jax.ShapeDtypeStruct((b, n_quads, CHUNK, QUAD), F32)]
    if with_y:
        out_specs = [shared] + out_specs
        out_shape = [jax.ShapeDtypeStruct((b, n, cw), F32)] + out_shape
    res = pl.pallas_call(
        functools.partial(_scanq_kernel, rev=rev, with_y=with_y),
        grid=(b, n_quads // sq, n_tb),
        in_specs=[shared, perdir, perdir, perdir, perdir, decay, state,
                  pl.BlockSpec(msk.shape, lambda bi, p, t: (0, 0, 0)),
                  pl.BlockSpec(bdm.shape, lambda bi, p, t: (0, 0))],
        out_specs=out_specs,
        out_shape=out_shape,
        scratch_shapes=[pltpu.VMEM((sq, CHUNK, QUAD), F32)],
        compiler_params=_cparams("parallel", "parallel", "arbitrary"),
        name="scan_rev" if rev else "scan_fwd",
    )(v, at, rt, bt, kt, wl, h0, jnp.asarray(msk), jnp.asarray(bdm))
    return (res[0], res[1]) if with_y else (None, res[0])


def _out1_ffn_kernel(yf_ref, yb_ref, bonus_ref, g_ref, dp_ref, x_ref, mod_ref, lw_ref, lb_ref, seg_ref,
                     w_ref, gain_ref, w1_ref, w2_ref, fg_ref, o_ref):
    seg = seg_ref[...]
    cw = yf_ref.shape[2]
    acc = _dot(dp_ref[0].astype(BF16), w_ref[cw:cw + D_WIDTH, :])
    wide = seg.shape[0]
    for pb in range(cw // wide):
        sl = slice(pb * wide, (pb + 1) * wide)
        y = yf_ref[0, :, sl] + yb_ref[0, :, sl]
        mean = _seg_sum(y, seg, 1) * (1.0 / HEAD_DIM)
        dev = y - mean
        var = _seg_sum(dev * dev, seg, 1) * (1.0 / HEAD_DIM)
        yn = dev * lax.rsqrt(var + LNX_EPS) * lw_ref[:, sl] + lb_ref[:, sl]
        z = (yn + bonus_ref[0, :, sl]) * g_ref[0, :, sl]
        acc = acc + _dot(z.astype(BF16), w_ref[sl, :])
    x1 = x_ref[0] + mod_ref[0, 2:3, :] * acc
    o_ref[0] = _ffn_tile(x1, mod_ref, gain_ref, w1_ref, w2_ref, fg_ref, True)


def _out1_ffn(yf, yb, bonus, g, dp, x, mod, lnx_w, lnx_b, seg, w, gain, w1, w2, fgain):
    b, n, d = x.shape
    tm = min(FFN_TILE, n)
    cw = yf.shape[2]
    row = lambda bi, i: (bi, i, 0)
    const = lambda bi, i: (0, 0)
    return pl.pallas_call(
        _out1_ffn_kernel,
        grid=(b, n // tm),
        in_specs=[pl.BlockSpec((1, tm, cw), row)] * 4
        + [pl.BlockSpec((1, tm, D_WIDTH), row),
           pl.BlockSpec((1, tm, d), row),
           pl.BlockSpec((1, 6, d), lambda bi, i: (bi, 0, 0)),
           pl.BlockSpec((1, cw), const), pl.BlockSpec((1, cw), const),
           pl.BlockSpec(seg.shape, const),
           pl.BlockSpec(w.shape, const, pipeline_mode=pl.Buffered(1))]
        + _ffn_specs(d, w1, w2),
        out_specs=pl.BlockSpec((1, tm, d), row),
        out_shape=jax.ShapeDtypeStruct((b, n, d), F32),
        compiler_params=_cparams("parallel", "parallel"),
        name="out1_ffn",
    )(yf, yb, bonus, g, dp, x, mod, lnx_w, lnx_b, seg, w, gain, w1, w2, fgain)


def _rope_tables(n):
    rows = n // GRID_W
    row = jnp.repeat(jnp.arange(rows, dtype=F32), GRID_W)
    col = jnp.tile(jnp.arange(GRID_W, dtype=F32), rows)
    n_freq = HEAD_DIM // 4
    inv = ROPE_THETA ** (-jnp.arange(n_freq, dtype=F32) / n_freq)
    ang = jnp.concatenate([row[:, None] * inv[None, :], col[:, None] * inv[None, :]], axis=-1)
    cos, sin = jnp.cos(ang), jnp.sin(ang)
    cos_t = jnp.tile(cos, (1, LANES // cos.shape[1]))
    sin_t = jnp.tile(jnp.concatenate([-sin, sin], axis=-1), (1, LANES // HEAD_DIM))
    return cos_t, sin_t


def _kv_dup_columns():
    cols = []
    for section in range(4):
        for head in range(2):
            base = AB_Q_COLS + section * 2 * HEAD_DIM + head * HEAD_DIM
            cols += list(range(base, base + HEAD_DIM)) * 2
    return np.concatenate([np.arange(AB_Q_COLS), np.asarray(cols)])


def kernel(x, c, ctx, c_ctx, norm_gain, ada_w, ada_b, ffn_w_in, ffn_w_out, final_gain, ab_w_in, ab_q_gain, ab_k_gain, ab_sink, ab_w_out, cd_w_in, cd_mu, cd_w0, cd_w2, cd_a0, cd_a2, cd_g2, cd_k_k, cd_k_a, cd_r_k, cd_lnx_w, cd_lnx_b, cd_pool_w, cd_pool_scale, cd_w_out):
    b, n, d = x.shape
    nc = ctx.shape[1]
    pad = (-(b + 1)) % 8
    cs = jnp.concatenate([c, c_ctx[None, :], jnp.zeros((pad, d), F32)], axis=0)
    mods = _mods(cs, ada_w, ada_b)
    seg = jnp.asarray(np.kron(np.eye(2), np.ones((HEAD_DIM, HEAD_DIM))).astype(np.float32)).astype(BF16)
    fgain = final_gain.reshape(1, d)

    def layer_mods(i):
        ml = mods[i, :b].reshape(b, 6, d)
        mc = jnp.broadcast_to(mods[i, b].reshape(1, 6, d), (b, 6, d))
        return ml, mc

    ml, mc = layer_mods(0)
    w0 = ab_w_in[0][:, _kv_dup_columns()].astype(BF16)
    gain = norm_gain[0, 0].reshape(1, d)
    qg = jnp.tile(ab_q_gain[0], 2).reshape(1, LANES)
    kg = jnp.tile(ab_k_gain[0], 2).reshape(1, LANES)
    cos_l, sin_l = _rope_tables(n)
    cos_c, sin_c = jnp.ones((nc, LANES), F32), jnp.zeros((nc, LANES), F32)
    q_l, kv_l, vt_l = _proj0(x, ml, gain, w0, cos_l, sin_l, qg, kg, seg)
    q_c, kv_c, vt_c = _proj0(ctx, mc, gain, w0, cos_c, sin_c, qg, kg, seg)
    sink = ab_sink[0]
    w_out0 = ab_w_out[0].astype(BF16)
    gain2 = norm_gain[0, 1].reshape(1, d)
    w1 = ffn_w_in[0].astype(BF16)
    w2 = ffn_w_out[0].astype(BF16)
    yb_l = _attn_b_auto(q_l, kv_c, vt_c, kv_l, vt_l, ab_q_gain[0], ab_k_gain[0])
    xl = _out0_ffn(_attn_a(q_l, kv_c, sink, kv_l), yb_l, x, ml, w_out0, gain2, w1, w2, fgain)
    xc = _out0_ffn(_attn_a(q_c, kv_c, sink), _attn_b(q_c, kv_c), ctx, mc, w_out0, gain2, w1, w2, fgain)

    ml, mc = layer_mods(1)
    gain = norm_gain[1, 0].reshape(1, d)
    w_in1 = cd_w_in[0].astype(BF16)
    pc_l = _proj1(xl, ml, gain, w_in1)
    pc_c = _proj1(xc, mc, gain, w_in1[:, :C_IN])
    zeros = jnp.zeros((DECAY_LORA_PAD, C_WIDTH), F32)
    w2x = jnp.stack([jnp.concatenate([cd_w2[0, 0], zeros]), jnp.concatenate([zeros, cd_w2[0, 1]])])
    a2x = jnp.stack([jnp.concatenate([cd_a2[0, 0], zeros]), jnp.concatenate([zeros, cd_a2[0, 1]])])
    w2x_hi = w2x.astype(BF16)
    w2x_hl = jnp.stack([w2x_hi, (w2x - w2x_hi.astype(F32)).astype(BF16)])
    wts = [jnp.stack([1.0 - cd_mu[0], 0.5 * cd_mu[0]]), cd_w0[0].reshape(2, 1, C_WIDTH), w2x_hl,
           cd_a0[0].reshape(2, 1, C_WIDTH), a2x.astype(BF16), cd_g2[0].astype(BF16),
           cd_k_k[0].reshape(1, C_WIDTH), cd_k_a[0].reshape(1, C_WIDTH),
           cd_r_k[0].reshape(1, C_WIDTH), seg]
    pool_w = jax.scipy.linalg.block_diag(*[cd_pool_w[0, g] for g in range(4)]).astype(BF16)
    pool_wts = [pool_w, cd_pool_scale[0].reshape(1, D_WIDTH)]
    prep_c = _prep(pc_c, wts, False)
    *prep_l, g_l, bonus_l, dp_l = _prep(pc_l, wts + pool_wts, True)
    h_zero = jnp.zeros((b, C_WIDTH // QUAD, CHUNK, QUAD), F32)
    _, h_f = _scanq(prep_c, 0, h_zero, False, False)
    _, h_b = _scanq(prep_c, 1, h_zero, True, False)
    y_f, _ = _scanq(prep_l, 0, h_f, False, True)
    y_b, _ = _scanq(prep_l, 1, h_b, True, True)
    seg4 = jnp.asarray(np.kron(np.eye(4), np.ones((HEAD_DIM, HEAD_DIM))).astype(np.float32)).astype(BF16)
    return _out1_ffn(y_f, y_b, bonus_l, g_l, dp_l, xl, ml, cd_lnx_w[0].reshape(1, C_WIDTH),
                     cd_lnx_b[0].reshape(1, C_WIDTH), seg4, cd_w_out[0].astype(BF16),
                     norm_gain[1, 1].reshape(1, d), ffn_w_in[1].astype(BF16),
                     ffn_w_out[1].astype(BF16), fgain)
```

```python
import functools

import numpy as np
import jax
import jax.numpy as jnp
from jax import lax
from jax.experimental import pallas as pl
from jax.experimental.pallas import tpu as pltpu

F32 = jnp.float32
BF16 = jnp.bfloat16
HIGHEST = lax.Precision.HIGHEST
LOG2_E = 1.4426950408889634
SCORE_BOUND_SAFE = 60.0

D_MODEL = 1024
GRID_W = 64
HEAD_DIM = 64
ROPE_THETA = 10000.0
NORM_EPS = 1e-6
WINDOW = 128
AB_Q_COLS = 1024
C_WIDTH = 768
C_IN = 2688
D_WIDTH = 256
CD_IN = C_IN + D_WIDTH
LNX_EPS = 64e-5
FFN_HIDDEN = 2816
POOL_WINDOWS = (2, 4, 8, 16)
DECAY_LORA_PAD = 64

LANES = 128
ROW_TILE = 256
FFN_TILE = 512
HALO = 8
CHUNK = 64
SCAN_BLOCK = 512
SCAN_QUADS = 3
VMEM_LIMIT = 56 * 1024 * 1024


def _cparams(*sem):
    return pltpu.CompilerParams(dimension_semantics=sem, vmem_limit_bytes=VMEM_LIMIT)


def _dot(a, b):
    return jnp.dot(a, b, preferred_element_type=F32)


def _dot32(a, b):
    return jnp.dot(a, b, preferred_element_type=F32, precision=HIGHEST)


def _split(a, terms):
    pieces = []
    for _ in range(terms - 1):
        hi = a.astype(BF16)
        pieces.append(hi)
        a = a - hi.astype(F32)
    pieces.append(a.astype(BF16))
    return pieces


def _seg_sum(z, seg, terms=2):
    out = None
    for piece in _split(z, terms):
        part = _dot(piece, seg)
        out = part if out is None else out + part
    return out


def _sigmoid(x):
    return 1.0 / (1.0 + jnp.exp(-x))


def _norm_mod(x, gain, shift, scale):
    ms = jnp.mean(x * x, axis=-1, keepdims=True)
    return (x * lax.rsqrt(ms + NORM_EPS) * gain) * (1.0 + scale) + shift


def _mods_kernel(c_ref, w_ref, b_ref, o_ref):
    c = c_ref[...]
    o_ref[0] = _dot32(c * _sigmoid(c), w_ref[0]) + b_ref[0]


def _mods(cs, ada_w, ada_b):
    depth, d, n6 = ada_w.shape
    tn = 768
    rows = cs.shape[0]
    return pl.pallas_call(
        _mods_kernel,
        grid=(depth, n6 // tn),
        in_specs=[pl.BlockSpec((rows, d), lambda l, j: (0, 0)),
                  pl.BlockSpec((1, d, tn), lambda l, j: (l, 0, j)),
                  pl.BlockSpec((1, 1, tn), lambda l, j: (l, 0, j))],
        out_specs=pl.BlockSpec((1, rows, tn), lambda l, j: (l, 0, j)),
        out_shape=jax.ShapeDtypeStruct((depth, rows, n6), F32),
        compiler_params=_cparams("arbitrary", "arbitrary"),
        name="mods",
    )(cs, ada_w, ada_b.reshape(depth, 1, n6))


def _proj0_kernel(x_ref, mod_ref, gain_ref, w_ref, cos_ref, sin_ref, qg_ref, kg_ref, seg_ref,
                  q_ref, kv_ref, vt_ref):
    h = _norm_mod(x_ref[0], gain_ref[...], mod_ref[0, 0:1, :], mod_ref[0, 1:2, :])
    p = _dot(h.astype(BF16), w_ref[...])
    tm = p.shape[0]
    cos = cos_ref[...]
    sin = sin_ref[...]
    lane = lax.broadcasted_iota(jnp.int32, (tm, LANES), 1)
    first_half = (lane & (HEAD_DIM - 1)) < HEAD_DIM // 2
    seg = seg_ref[...]

    def rope(z):
        partner = jnp.where(first_half, pltpu.roll(z, LANES - HEAD_DIM // 2, 1),
                            pltpu.roll(z, HEAD_DIM // 2, 1))
        return z * cos + partner * sin

    def head_norm(z, g):
        ms = _seg_sum(z * z, seg, 1) * (1.0 / HEAD_DIM)
        return z * lax.rsqrt(ms + NORM_EPS) * g

    scale = HEAD_DIM ** -0.5 * LOG2_E
    for blk in range(8):
        z = p[:, blk * LANES:(blk + 1) * LANES]
        if blk >= 4:
            z = head_norm(z, qg_ref[...])
        q_ref[0, :, blk * LANES:(blk + 1) * LANES] = (rope(z) * scale).astype(BF16)
    for blk in range(8):
        z = p[:, AB_Q_COLS + blk * LANES:AB_Q_COLS + (blk + 1) * LANES]
        if blk in (4, 5):
            z = head_norm(z, kg_ref[...])
        if blk in (0, 1, 4, 5):
            z = rope(z)
        kv_ref[0, :, blk * LANES:(blk + 1) * LANES] = z.astype(BF16)
        if blk in (2, 3, 6, 7):
            zt = z.T
            top = lax.broadcasted_iota(jnp.int32, zt.shape, 0) < HEAD_DIM
            vt_ref[0, blk - 2 if blk < 4 else blk - 4] = jnp.where(top, zt, 1.0).astype(BF16)


def _proj0(x, mod, gain, w, cos, sin, qg, kg, seg):
    b, n, d = x.shape
    tm = ROW_TILE
    nw = w.shape[1]
    const = lambda bi, i: (0, 0)
    return pl.pallas_call(
        _proj0_kernel,
        grid=(b, n // tm),
        in_specs=[pl.BlockSpec((1, tm, d), lambda bi, i: (bi, i, 0)),
                  pl.BlockSpec((1, 6, d), lambda bi, i: (bi, 0, 0)),
                  pl.BlockSpec((1, d), const),
                  pl.BlockSpec((d, nw), const),
                  pl.BlockSpec((tm, LANES), lambda bi, i: (i, 0)),
                  pl.BlockSpec((tm, LANES), lambda bi, i: (i, 0)),
                  pl.BlockSpec((1, LANES), const),
                  pl.BlockSpec((1, LANES), const),
                  pl.BlockSpec((LANES, LANES), const)],
        out_specs=[pl.BlockSpec((1, tm, 1024), lambda bi, i: (bi, i, 0)),
                   pl.BlockSpec((1, tm, 1024), lambda bi, i: (bi, i, 0)),
                   pl.BlockSpec((1, 4, LANES, tm), lambda bi, i: (bi, 0, 0, i))],
        out_shape=[jax.ShapeDtypeStruct((b, n, 1024), BF16),
                   jax.ShapeDtypeStruct((b, n, 1024), BF16),
                   jax.ShapeDtypeStruct((b, 4, LANES, n), BF16)],
        compiler_params=_cparams("parallel", "parallel"),
        name="proj0",
    )(x, mod, gain, w, cos, sin, qg, kg, seg)


def _masked_q(q_ref, h, low):
    q2 = q_ref[0, :, (h // 2) * LANES:(h // 2 + 1) * LANES]
    keep = low if h % 2 == 0 else jnp.logical_not(low)
    return jnp.where(keep, q2, jnp.zeros_like(q2))


def _attn_b_kernel(q_ref, kc_ref, vc_ref, *rest, n_kt, has_latent):
    if has_latent:
        k_ref, v_ref, o_ref, m_sc, acc_sc = rest
    else:
        o_ref, m_sc, acc_sc = rest
    kt = pl.program_id(3)
    tq = q_ref.shape[1]
    low = lax.broadcasted_iota(jnp.int32, (tq, LANES), 1) < HEAD_DIM

    def update(k, v):
        tk = k.shape[0]
        v1 = jnp.where(lax.broadcasted_iota(jnp.int32, (tk, LANES), 1) < HEAD_DIM, v, jnp.ones_like(v))
        ss = [lax.dot_general(_masked_q(q_ref, h, low), k, (((1,), (1,)), ((), ())),
                              preferred_element_type=F32) for h in range(4)]
        for h in range(4):
            m_prev = m_sc[h]
            m_new = jnp.maximum(m_prev, jnp.max(ss[h], axis=1, keepdims=True))
            p = jnp.exp2(ss[h] - jnp.concatenate([m_new] * (tk // LANES), axis=1)).astype(BF16)
            acc_sc[h] = jnp.exp2(m_prev - m_new) * acc_sc[h] + _dot(p, v1)
            m_sc[h] = m_new

    @pl.when(kt == 0)
    def _():
        m_sc[...] = jnp.full(m_sc.shape, -jnp.inf, F32)
        acc_sc[...] = jnp.zeros(acc_sc.shape, F32)
        update(kc_ref[0], vc_ref[0])

    if has_latent:
        update(k_ref[0], v_ref[0])

    @pl.when(kt == n_kt - 1)
    def _():
        for pr in range(2):
            a0 = acc_sc[2 * pr]
            a1 = acc_sc[2 * pr + 1]
            o0 = a0 / pltpu.roll(a0, HEAD_DIM, 1)
            o1 = pltpu.roll(a1, HEAD_DIM, 1) / a1
            o_ref[0, :, pr * LANES:(pr + 1) * LANES] = jnp.where(low, o0, o1).astype(BF16)


def _attn_b(q, kv_c, kv=None, tk=1024):
    b, n, _ = q.shape
    nc = kv_c.shape[1]
    tq = ROW_TILE
    has_latent = kv is not None
    n_kt = kv.shape[1] // tk if has_latent else 1
    in_specs = [pl.BlockSpec((1, tq, 2 * LANES), lambda bi, g, i, j: (bi, i, 2 + g)),
                pl.BlockSpec((1, nc, LANES), lambda bi, g, i, j: (bi, 0, 4 + g)),
                pl.BlockSpec((1, nc, LANES), lambda bi, g, i, j: (bi, 0, 6 + g))]
    args = [q, kv_c, kv_c]
    if has_latent:
        in_specs += [pl.BlockSpec((1, tk, LANES), lambda bi, g, i, j: (bi, j, 4 + g)),
                     pl.BlockSpec((1, tk, LANES), lambda bi, g, i, j: (bi, j, 6 + g))]
        args += [kv, kv]
    return pl.pallas_call(
        functools.partial(_attn_b_kernel, n_kt=n_kt, has_latent=has_latent),
        grid=(b, 2, n // tq, n_kt),
        in_specs=in_specs,
        out_specs=pl.BlockSpec((1, tq, 2 * LANES), lambda bi, g, i, j: (bi, i, g)),
        out_shape=jax.ShapeDtypeStruct((b, n, 512), BF16),
        scratch_shapes=[pltpu.VMEM((4, tq, LANES), F32), pltpu.VMEM((4, tq, LANES), F32)],
        compiler_params=_cparams("parallel", "parallel", "parallel", "arbitrary"),
        name="attn_b",
    )(*args)


def _attn_bt_kernel(q_ref, kc_ref, vtc_ref, k_ref, vt_ref, o_ref, acc_sc, *, n_kt):
    kt = pl.program_id(3)
    tq = q_ref.shape[1]
    low = lax.broadcasted_iota(jnp.int32, (tq, LANES), 1) < HEAD_DIM

    def update(k, vt):
        sts = [lax.dot_general(k, _masked_q(q_ref, h, low), (((1,), (1,)), ((), ())),
                               preferred_element_type=F32) for h in range(4)]
        for h in range(4):
            acc_sc[h] = acc_sc[h] + _dot(vt, jnp.exp2(sts[h]).astype(BF16))

    @pl.when(kt == 0)
    def _():
        acc_sc[...] = jnp.zeros(acc_sc.shape, F32)
        update(kc_ref[0], vtc_ref[0, 0])

    update(k_ref[0], vt_ref[0, 0])

    @pl.when(kt == n_kt - 1)
    def _():
        ot = jnp.concatenate([acc_sc[h, 0:HEAD_DIM, :] / acc_sc[h, HEAD_DIM:2 * HEAD_DIM, :]
                              for h in range(4)], axis=0)
        o_ref[0] = ot.T.astype(BF16)


def _attn_bt(q, kv_c, vt_c, kv, vt, tk=4096):
    b, n, _ = q.shape
    nc = kv_c.shape[1]
    tq = ROW_TILE
    tk = min(tk, kv.shape[1])
    n_kt = kv.shape[1] // tk
    return pl.pallas_call(
        functools.partial(_attn_bt_kernel, n_kt=n_kt),
        grid=(b, 2, n // tq, n_kt),
        in_specs=[pl.BlockSpec((1, tq, 2 * LANES), lambda bi, g, i, j: (bi, i, 2 + g)),
                  pl.BlockSpec((1, nc, LANES), lambda bi, g, i, j: (bi, 0, 4 + g)),
                  pl.BlockSpec((1, 1, LANES, nc), lambda bi, g, i, j: (bi, 2 + g, 0, 0)),
                  pl.BlockSpec((1, tk, LANES), lambda bi, g, i, j: (bi, j, 4 + g)),
                  pl.BlockSpec((1, 1, LANES, tk), lambda bi, g, i, j: (bi, 2 + g, 0, j))],
        out_specs=pl.BlockSpec((1, tq, 2 * LANES), lambda bi, g, i, j: (bi, i, g)),
        out_shape=jax.ShapeDtypeStruct((b, n, 512), BF16),
        scratch_shapes=[pltpu.VMEM((4, LANES, tq), F32)],
        compiler_params=_cparams("parallel", "parallel", "parallel", "arbitrary"),
        name="attn_b_bounded",
    )(q, kv_c, vt_c, kv, vt)


def _attn_b_auto(q, kv_c, vt_c, kv, vt, q_gain, k_gain):
    bound = (1.02 * HEAD_DIM * HEAD_DIM ** -0.5 * LOG2_E) * jnp.max(jnp.abs(q_gain)) * jnp.max(jnp.abs(k_gain))
    return lax.cond(bound <= SCORE_BOUND_SAFE,
                    lambda: _attn_bt(q, kv_c, vt_c, kv, vt),
                    lambda: _attn_b(q, kv_c, kv))


def _attn_a_kernel(sink_ref, q_ref, kc_ref, vtc_ref, *rest, has_local):
    if has_local:
        kp_ref, kcur_ref, kn_ref, vtp_ref, vtcur_ref, vtn_ref, o_ref, kbuf, vtbuf = rest
    else:
        o_ref, kbuf, vtbuf = rest
    g = pl.program_id(1)
    i = pl.program_id(2)
    n_i = pl.num_programs(2)
    tq = q_ref.shape[1]
    nc = kc_ref.shape[1]
    kbuf[0:nc, :] = kc_ref[0]
    vtbuf[:, 0:nc] = vtc_ref[0, 0]
    nk = nc
    if has_local:
        kbuf[nc:nc + WINDOW, :] = kp_ref[0]
        kbuf[nc + WINDOW:nc + WINDOW + tq, :] = kcur_ref[0]
        kbuf[nc + WINDOW + tq:nc + 2 * WINDOW + tq, :] = kn_ref[0]
        vtbuf[:, nc:nc + WINDOW] = vtp_ref[0, 0]
        vtbuf[:, nc + WINDOW:nc + WINDOW + tq] = vtcur_ref[0, 0]
        vtbuf[:, nc + WINDOW + tq:nc + 2 * WINDOW + tq] = vtn_ref[0, 0]
        nk = nc + 2 * WINDOW + tq
        key = lax.broadcasted_iota(jnp.int32, (nk, tq), 0) - nc
        qry = lax.broadcasted_iota(jnp.int32, (nk, tq), 1)
        lo_c = jnp.where(i == 0, WINDOW, 0)
        hi_c = jnp.where(i == n_i - 1, WINDOW + tq, 2 * WINDOW + tq)
        valid = (key < 0) | ((key >= qry) & (key <= qry + 2 * WINDOW) & (key >= lo_c) & (key < hi_c))
    k = kbuf[...]
    vt = vtbuf[...]
    low = lax.broadcasted_iota(jnp.int32, (tq, LANES), 1) < HEAD_DIM
    den_rows = lax.broadcasted_iota(jnp.int32, (LANES, tq), 0) >= HEAD_DIM
    sts = [lax.dot_general(k, _masked_q(q_ref, h, low), (((1,), (1,)), ((), ())),
                           preferred_element_type=F32) for h in range(4)]
    outs = []
    for h in range(4):
        s = jnp.where(valid, sts[h], -jnp.inf) if has_local else sts[h]
        sink = sink_ref[g * 4 + h] * LOG2_E
        m = jnp.maximum(jnp.max(s, axis=0, keepdims=True), sink)
        acc = _dot(vt, jnp.exp2(s - m).astype(BF16)) + jnp.where(den_rows, jnp.exp2(sink - m), 0.0)
        outs.append(acc[0:HEAD_DIM, :] / acc[HEAD_DIM:2 * HEAD_DIM, :])
    o_ref[0] = jnp.concatenate(outs, axis=0).T.astype(BF16)


def _attn_a(q, kv_c, vt_c, sink, kv=None, vt=None):
    b, n, _ = q.shape
    nc = kv_c.shape[1]
    tq = ROW_TILE
    has_local = kv is not None
    in_specs = [pl.BlockSpec(memory_space=pltpu.SMEM),
                pl.BlockSpec((1, tq, 2 * LANES), lambda bi, g, i: (bi, i, g)),
                pl.BlockSpec((1, nc, LANES), lambda bi, g, i: (bi, 0, g)),
                pl.BlockSpec((1, 1, LANES, nc), lambda bi, g, i: (bi, g, 0, 0))]
    args = [sink, q, kv_c, vt_c]
    nk = nc
    if has_local:
        per = tq // WINDOW
        last = n // WINDOW - 1
        prev_i = lambda i: jnp.maximum(i * per - 1, 0)
        next_i = lambda i: jnp.minimum((i + 1) * per, last)
        in_specs += [pl.BlockSpec((1, WINDOW, LANES), lambda bi, g, i: (bi, prev_i(i), g)),
                     pl.BlockSpec((1, tq, LANES), lambda bi, g, i: (bi, i, g)),
                     pl.BlockSpec((1, WINDOW, LANES), lambda bi, g, i: (bi, next_i(i), g)),
                     pl.BlockSpec((1, 1, LANES, WINDOW), lambda bi, g, i: (bi, g, 0, prev_i(i))),
                     pl.BlockSpec((1, 1, LANES, tq), lambda bi, g, i: (bi, g, 0, i)),
                     pl.BlockSpec((1, 1, LANES, WINDOW), lambda bi, g, i: (bi, g, 0, next_i(i)))]
        args += [kv, kv, kv, vt, vt, vt]
        nk = nc + 2 * WINDOW + tq
    return pl.pallas_call(
        functools.partial(_attn_a_kernel, has_local=has_local),
        grid=(b, 2, n // tq),
        in_specs=in_specs,
        out_specs=pl.BlockSpec((1, tq, 2 * LANES), lambda bi, g, i: (bi, i, g)),
        out_shape=jax.ShapeDtypeStruct((b, n, 512), BF16),
        scratch_shapes=[pltpu.VMEM((nk, LANES), BF16), pltpu.VMEM((LANES, nk), BF16)],
        compiler_params=_cparams("parallel", "parallel", "parallel"),
        name="attn_a",
    )(*args)


def _ffn_tile(x, mod_ref, gain_ref, w1_ref, w2_ref, fg_ref, final):
    h = _norm_mod(x, gain_ref[...], mod_ref[0, 3:4, :], mod_ref[0, 4:5, :]).astype(BF16)
    hid = w2_ref.shape[0]
    ch = hid // 2
    acc = None
    for j in range(2):
        gate = _dot(h, w1_ref[:, j * ch:(j + 1) * ch])
        up = _dot(h, w1_ref[:, hid + j * ch:hid + (j + 1) * ch])
        act = (gate * _sigmoid(gate) * up).astype(BF16)
        part = _dot(act, w2_ref[j * ch:(j + 1) * ch, :])
        acc = part if acc is None else acc + part
    y = x + mod_ref[0, 5:6, :] * acc
    if final:
        ms = jnp.mean(y * y, axis=-1, keepdims=True)
        y = y * lax.rsqrt(ms + NORM_EPS) * fg_ref[...]
    return y


def _out0_ffn_kernel(ya_ref, yb_ref, x_ref, mod_ref, wo_ref, gain_ref, w1_ref, w2_ref, fg_ref, o_ref):
    half = ya_ref.shape[2]
    y = _dot(ya_ref[0], wo_ref[0:half, :]) + _dot(yb_ref[0], wo_ref[half:2 * half, :])
    x1 = x_ref[0] + mod_ref[0, 2:3, :] * y
    o_ref[0] = _ffn_tile(x1, mod_ref, gain_ref, w1_ref, w2_ref, fg_ref, False)


def _ffn_specs(d, w1, w2):
    const = lambda bi, i: (0, 0)
    return [pl.BlockSpec((1, d), const),
            pl.BlockSpec(w1.shape, const, pipeline_mode=pl.Buffered(1)),
            pl.BlockSpec(w2.shape, const, pipeline_mode=pl.Buffered(1)),
            pl.BlockSpec((1, d), const)]


def _out0_ffn(ya, yb, x, mod, wo, gain, w1, w2, fgain):
    b, n, d = x.shape
    tm = min(FFN_TILE, n)
    row = lambda bi, i: (bi, i, 0)
    return pl.pallas_call(
        _out0_ffn_kernel,
        grid=(b, n // tm),
        in_specs=[pl.BlockSpec((1, tm, ya.shape[2]), row),
                  pl.BlockSpec((1, tm, yb.shape[2]), row),
                  pl.BlockSpec((1, tm, d), row),
                  pl.BlockSpec((1, 6, d), lambda bi, i: (bi, 0, 0)),
                  pl.BlockSpec(wo.shape, lambda bi, i: (0, 0), pipeline_mode=pl.Buffered(1))]
        + _ffn_specs(d, w1, w2),
        out_specs=pl.BlockSpec((1, tm, d), row),
        out_shape=jax.ShapeDtypeStruct((b, n, d), F32),
        compiler_params=_cparams("parallel", "parallel"),
        name="out0_ffn",
    )(ya, yb, x, mod, wo, gain, w1, w2, fgain)


def _proj1_kernel(x_ref, mod_ref, gain_ref, w_ref, o_ref):
    h = _norm_mod(x_ref[0], gain_ref[...], mod_ref[0, 0:1, :], mod_ref[0, 1:2, :])
    o_ref[0] = _dot(h.astype(BF16), w_ref[...])


def _proj1(x, mod, gain, w):
    b, n, d = x.shape
    tm = min(FFN_TILE, n)
    nw = w.shape[1]
    return pl.pallas_call(
        _proj1_kernel,
        grid=(b, n // tm),
        in_specs=[pl.BlockSpec((1, tm, d), lambda bi, i: (bi, i, 0)),
                  pl.BlockSpec((1, 6, d), lambda bi, i: (bi, 0, 0)),
                  pl.BlockSpec((1, d), lambda bi, i: (0, 0)),
                  pl.BlockSpec((d, nw), lambda bi, i: (0, 0))],
        out_specs=pl.BlockSpec((1, tm, nw), lambda bi, i: (bi, i, 0)),
        out_shape=jax.ShapeDtypeStruct((b, n, nw), F32),
        compiler_params=_cparams("parallel", "parallel"),
        name="proj1",
    )(x, mod, gain, w)


def _prep_kernel(pc_ref, prev_ref, next_ref, mu_ref, w0_ref, w2_ref, a0_ref, a2_ref, g2_ref,
                 kk_ref, ka_ref, rk_ref, seg_ref, tri_ref, *rest, latent):
    if latent:
        pw_ref, ps_ref, v_o, at_o, rt_o, bt_o, kt_o, wl_o, g_o, bonus_o, d_o, ext, tmp = rest
    else:
        v_o, at_o, rt_o, bt_o, kt_o, wl_o, ext, tmp = rest
    i = pl.program_id(1)
    nt = pl.num_programs(1)
    tm = pc_ref.shape[1]
    cw = C_WIDTH
    ext[HALO:HALO + tm, :] = pc_ref[0]
    ext[0:HALO, :] = jnp.where(i > 0, prev_ref[0], 0.0)
    ext[HALO + tm:2 * HALO + tm, :] = jnp.where(i < nt - 1, next_ref[0], 0.0)

    def mixed(lo, hi):
        nb = ext[HALO - 1:HALO - 1 + tm, lo:hi] + ext[HALO + 1:HALO + 1 + tm, lo:hi]
        return ext[HALO:HALO + tm, lo:hi] * mu_ref[0:1, lo:hi] + nb * mu_ref[1:2, lo:hi]

    lora = mixed(3 * cw, C_IN)
    tw = jnp.tanh(lora[:, 0:LANES])
    xa = lora[:, LANES:2 * LANES]
    if latent:
        g_o[0] = _dot(_sigmoid(lora[:, 2 * LANES:3 * LANES]).astype(BF16), g2_ref[...])
    tw_hi, tw_lo = _split(tw, 2)
    xa_b = xa.astype(BF16)
    for d in range(2):
        z = w0_ref[d] + _dot(tw_hi, w2_ref[0, d]) + _dot(tw_lo, w2_ref[0, d]) + _dot(tw_hi, w2_ref[1, d])
        w_log = -(jnp.maximum(-z, 0.0) + jnp.log(1.0 + jnp.exp(-jnp.abs(z)))) - 0.5
        tmp[d] = -LOG2_E * jnp.exp(w_log)
        tmp[2 + d] = _sigmoid(a0_ref[d] + _dot(xa_b, a2_ref[d]))
    seg = seg_ref[...]
    n_chunks = tm // CHUNK
    for pb in range(cw // LANES):
        sl = slice(pb * LANES, (pb + 1) * LANES)
        r = mixed(pb * LANES, (pb + 1) * LANES)
        k = mixed(cw + pb * LANES, cw + (pb + 1) * LANES)
        v = mixed(2 * cw + pb * LANES, 2 * cw + (pb + 1) * LANES)
        kk = k * kk_ref[:, sl]
        kk = kk / jnp.maximum(jnp.sqrt(_seg_sum(kk * kk, seg, 1)), 1e-12)
        neg_kk = -kk
        kka = k * ka_ref[:, sl]
        ksum = jnp.zeros_like(k)
        for d in range(2):
            lw = tmp[d, :, sl]
            a = tmp[2 + d, :, sl]
            k_d = k + kka * (a - 1.0)
            bb = kk * a
            ksum = ksum + k_d
            cs2 = _dot(tri_ref[d], jnp.concatenate(_split(lw, 2), axis=1))
            cs = cs2[:, 0:LANES] + cs2[:, LANES:2 * LANES]
            for j in range(n_chunks):
                last = j * CHUNK if d == 1 else (j + 1) * CHUNK - 1
                wl_o[d, 0, j, :, sl] = jnp.exp2(cs[last:last + 1, :])
            e_up = jnp.exp2(-cs)
            at_o[d, 0, :, sl] = (neg_kk * jnp.exp2(cs - lw)).astype(BF16)
            rt_o[d, 0, :, sl] = (r * jnp.exp2(cs)).astype(BF16)
            bt_o[d, 0, :, sl] = (bb * e_up).astype(BF16)
            kt_o[d, 0, :, sl] = (k_d * e_up).astype(BF16)
        v_o[0, :, sl] = v.astype(BF16)
        if latent:
            bonus_o[0, :, sl] = _seg_sum(r * rk_ref[:, sl] * ksum, seg) * v
    if latent:
        n_tok = nt * tm
        pos = i * tm + lax.broadcasted_iota(jnp.int32, (tm, LANES), 0)
        group1 = lax.broadcasted_iota(jnp.int32, (tm, LANES), 1) >= D_WIDTH // 4
        pooled = []
        for half in range(2):
            lo, hi = C_IN + half * LANES, C_IN + (half + 1) * LANES
            w_small, w_big = POOL_WINDOWS[2 * half], POOL_WINDOWS[2 * half + 1]
            s_small = None
            s_big = None
            for off in range(-(w_big // 2), w_big - w_big // 2):
                piece = ext[HALO + off:HALO + off + tm, lo:hi]
                s_big = piece if s_big is None else s_big + piece
                if -(w_small // 2) <= off < w_small - w_small // 2:
                    s_small = piece if s_small is None else s_small + piece

            def count(w):
                lo_p = jnp.clip(pos - w // 2, 0, n_tok)
                hi_p = jnp.clip(pos + (w - w // 2), 0, n_tok)
                return (hi_p - lo_p).astype(F32)

            mean = jnp.where(group1, s_big / count(w_big), s_small / count(w_small))
            pooled.append((mean - ext[HALO:HALO + tm, lo:hi]).astype(BF16))
        pm = jnp.concatenate(pooled, axis=1)
        d_o[0] = _dot(pm, pw_ref[...]) * ps_ref[...]


def _chunk_cumsum_matrices(tm):
    t = np.arange(tm)[:, None]
    u = np.arange(tm)[None, :]
    same = (t // CHUNK) == (u // CHUNK)
    return np.stack([same & (u <= t), same & (u >= t)]).astype(np.float32)


def _prep(pc, wts, latent):
    b, n, cin = pc.shape
    tm = ROW_TILE
    cw = C_WIDTH
    per = tm // HALO
    last = n // HALO - 1
    n_chunks = tm // CHUNK
    row = lambda bi, i: (bi, i, 0)
    drow = lambda bi, i: (0, bi, i, 0)
    wts = list(wts)
    wts = wts[:10] + [jnp.asarray(_chunk_cumsum_matrices(tm)).astype(BF16)] + (wts[10:] if latent else [])

    def full(a):
        return pl.BlockSpec(a.shape, lambda bi, i, nd=a.ndim: (0,) * nd)

    in_specs = [pl.BlockSpec((1, tm, cin), row),
                pl.BlockSpec((1, HALO, cin), lambda bi, i: (bi, jnp.maximum(i * per - 1, 0), 0)),
                pl.BlockSpec((1, HALO, cin), lambda bi, i: (bi, jnp.minimum((i + 1) * per, last), 0))]
    in_specs += [full(a) for a in wts]
    out_specs = [pl.BlockSpec((1, tm, cw), row)] + [pl.BlockSpec((2, 1, tm, cw), drow)] * 4
    out_shape = [jax.ShapeDtypeStruct((b, n, cw), BF16)] + [jax.ShapeDtypeStruct((2, b, n, cw), BF16)] * 4
    out_specs.append(pl.BlockSpec((2, 1, n_chunks, 1, cw), lambda bi, i: (0, bi, i, 0, 0)))
    out_shape.append(jax.ShapeDtypeStruct((2, b, n // CHUNK, 1, cw), F32))
    if latent:
        out_specs += [pl.BlockSpec((1, tm, cw), row)] * 2 + [pl.BlockSpec((1, tm, D_WIDTH), row)]
        out_shape += [jax.ShapeDtypeStruct((b, n, cw), F32)] * 2 + [jax.ShapeDtypeStruct((b, n, D_WIDTH), F32)]
    return pl.pallas_call(
        functools.partial(_prep_kernel, latent=latent),
        grid=(b, n // tm),
        in_specs=in_specs,
        out_specs=out_specs,
        out_shape=out_shape,
        scratch_shapes=[pltpu.VMEM((tm + 2 * HALO, cin), F32), pltpu.VMEM((4, tm, cw), F32)],
        compiler_params=_cparams("parallel", "parallel"),
        name="prep",
    )(pc, pc, pc, *wts)


QUAD = 4 * HEAD_DIM


def _quad_masks(rev):
    t = np.arange(CHUNK)[:, None]
    u = np.arange(CHUNK)[None, :]
    before = (u > t) if rev else (u < t)
    masks = [before, before | (u == t), u == t]
    s = 1
    while s < CHUNK:
        blk = (t // (2 * s)) == (u // (2 * s))
        t_late = (t % (2 * s) < s) if rev else (t % (2 * s) >= s)
        u_early = (u % (2 * s) >= s) if rev else (u % (2 * s) < s)
        masks.append(blk & t_late & u_early)
        s *= 2
    masks = np.stack([np.tile(m, (1, QUAD // CHUNK)) for m in masks]).astype(np.float32)
    hid = np.arange(QUAD) // HEAD_DIM
    return masks, (hid[:, None] == hid[None, :]).astype(np.float32)


def _scanq_kernel(v_ref, at_ref, rt_ref, bt_ref, kt_ref, wl_ref, h0_ref, msk_ref, bd_ref, *rest,
                  rev, with_y):
    if with_y:
        y_ref, hT_ref, g_sc = rest
    else:
        hT_ref, g_sc = rest
    tb = pl.program_id(2)
    n_tb = pl.num_programs(2)
    n_chunks = v_ref.shape[1] // CHUNK
    n_quads = v_ref.shape[2] // QUAD
    reps = QUAD // CHUNK

    @pl.when(tb == 0)
    def _():
        g_sc[...] = h0_ref[0]

    bd_mask = bd_ref[...] > 0
    strict = msk_ref[0] > 0
    incl = msk_ref[1] > 0
    eye4 = msk_ref[2]
    n_levels = msk_ref.shape[0] - 3
    nt_dims = (((1,), (1,)), ((), ()))
    tn_dims = (((0,), (0,)), ((), ()))

    def bd(x4):
        return jnp.where(bd_mask, jnp.concatenate([x4] * reps, axis=0), jnp.zeros((QUAD, QUAD), BF16))

    def fold(m):
        m = jnp.where(bd_mask, m, 0.0)
        out = m[0:CHUNK]
        for r in range(1, reps):
            out = out + m[r * CHUNK:(r + 1) * CHUNK]
        return out

    order = list(range(n_chunks - 1, -1, -1) if rev else range(n_chunks))
    chains = [(qq, c) for c in order for qq in range(n_quads)]
    ch = {}
    for key in chains:
        qq, c = key
        rows = slice(c * CHUNK, (c + 1) * CHUNK)
        lanes = slice(qq * QUAD, (qq + 1) * QUAD)
        q = ch[key] = {"rows": rows, "lanes": lanes}
        q["v"] = v_ref[0, rows, lanes]
        q["a"] = at_ref[0, 0, rows, lanes]
        q["b"] = bt_ref[0, 0, rows, lanes]
        q["k"] = kt_ref[0, 0, rows, lanes]
        q["bdv"] = bd(q["v"])
        rhs = jnp.concatenate([bd(q["b"]), bd(q["k"])], axis=0)
        if with_y:
            q["r"] = rt_ref[0, 0, rows, lanes]
            lhs = jnp.concatenate([q["a"], q["r"]], axis=0)
        else:
            lhs = q["a"]
        gram = lax.dot_general(lhs, rhs, nt_dims, preferred_element_type=F32)
        q["ab"] = jnp.where(strict, gram[0:CHUNK, 0:QUAD], 0.0)
        ak = jnp.where(strict, gram[0:CHUNK, QUAD:2 * QUAD], 0.0).astype(BF16)
        if with_y:
            q["rb"] = jnp.where(incl, gram[CHUNK:2 * CHUNK, 0:QUAD], 0.0).astype(BF16)
            rk = jnp.where(incl, gram[CHUNK:2 * CHUNK, QUAD:2 * QUAD], 0.0).astype(BF16)
            kv = _dot(jnp.concatenate([ak, rk], axis=0), q["bdv"])
            q["akv"], q["rkv"] = kv[0:CHUNK], kv[CHUNK:2 * CHUNK]
        else:
            q["akv"] = _dot(ak, q["bdv"])
        q["t"] = eye4 + q["ab"] * msk_ref[3]
    for lvl in range(1, n_levels):
        for key in chains:
            q = ch[key]
            q["x"] = _dot((q["ab"] * msk_ref[3 + lvl]).astype(BF16), bd(q["t"].astype(BF16)))
        for key in chains:
            q = ch[key]
            q["t"] = q["t"] + _dot(q["t"].astype(BF16), bd(q["x"].astype(BF16)))
    for key in chains:
        q = ch[key]
        au = _dot(q["t"].astype(BF16),
                  jnp.concatenate([bd(q["a"]), bd(q["akv"].astype(BF16))], axis=1))
        au_b = au.astype(BF16)
        wl = wl_ref[0, 0, key[1], :, q["lanes"]]
        mc = lax.dot_general(au_b, q["b"], tn_dims, preferred_element_type=F32)
        vk = lax.dot_general(q["v"], q["k"], tn_dims, preferred_element_type=F32)
        q["mt"] = jnp.where(bd_mask, mc[0:QUAD] * wl, 0.0).astype(BF16)
        q["ct"] = fold(mc[QUAD:2 * QUAD] + vk) * wl
        if with_y:
            rbau = _dot(q["rb"], jnp.concatenate([bd(au_b[:, 0:QUAD]), bd(au_b[:, QUAD:2 * QUAD])], axis=1))
            q["rt"] = (q["r"].astype(F32) + rbau[:, 0:QUAD]).astype(BF16)
            q["y0"] = rbau[:, QUAD:2 * QUAD] + q["rkv"]
    g = [g_sc[qq] for qq in range(n_quads)]
    for key in chains:
        qq, c = key
        q = ch[key]
        g_b = g[qq].astype(BF16)
        if with_y:
            y_ref[0, q["rows"], q["lanes"]] = (
                lax.dot_general(q["rt"], bd(g_b), nt_dims, preferred_element_type=F32) + q["y0"])
        g[qq] = g[qq] * wl_ref[0, 0, c, :, q["lanes"]] + _dot(g_b, q["mt"]) + q["ct"]
    for qq in range(n_quads):
        g_sc[qq] = g[qq]

    @pl.when(tb == n_tb - 1)
    def _():
        hT_ref[0] = g_sc[...]


def _scanq(prep, d, h0, rev, with_y):
    v, at, rt, bt, kt, wl = prep
    b, n, cw = v.shape
    tb = min(SCAN_BLOCK, n)
    n_tb = n // tb
    n_quads = cw // QUAD
    sq = SCAN_QUADS
    wide = sq * QUAD
    msk, bdm = _quad_masks(rev)
    tmap = (lambda t: n_tb - 1 - t) if rev else (lambda t: t)
    shared = pl.BlockSpec((1, tb, wide), lambda bi, p, t: (bi, tmap(t), p))
    perdir = pl.BlockSpec((1, 1, tb, wide), lambda bi, p, t: (d, bi, tmap(t), p))
    decay = pl.BlockSpec((1, 1, tb // CHUNK, 1, wide), lambda bi, p, t: (d, bi, tmap(t), 0, p))
    state = pl.BlockSpec((1, sq, CHUNK, QUAD), lambda bi, p, t: (bi, p, 0, 0))
    out_specs = [state]
    out_shape = [jax.ShapeDtypeStruct((b, n_quads, CHUNK, QUAD), F32)]
    if with_y:
        out_specs = [shared] + out_specs
        out_shape = [jax.ShapeDtypeStruct((b, n, cw), F32)] + out_shape
    res = pl.pallas_call(
        functools.partial(_scanq_kernel, rev=rev, with_y=with_y),
        grid=(b, n_quads // sq, n_tb),
        in_specs=[shared, perdir, perdir, perdir, perdir, decay, state,
                  pl.BlockSpec(msk.shape, lambda bi, p, t: (0, 0, 0)),
                  pl.BlockSpec(bdm.shape, lambda bi, p, t: (0, 0))],
        out_specs=out_specs,
        out_shape=out_shape,
        scratch_shapes=[pltpu.VMEM((sq, CHUNK, QUAD), F32)],
        compiler_params=_cparams("parallel", "parallel", "arbitrary"),
        name="scan_rev" if rev else "scan_fwd",
    )(v, at, rt, bt, kt, wl, h0, jnp.asarray(msk), jnp.asarray(bdm))
    return (res[0], res[1]) if with_y else (None, res[0])


def _out1_ffn_kernel(yf_ref, yb_ref, bonus_ref, g_ref, dp_ref, x_ref, mod_ref, lw_ref, lb_ref, seg_ref,
                     w_ref, gain_ref, w1_ref, w2_ref, fg_ref, o_ref):
    seg = seg_ref[...]
    cw = yf_ref.shape[2]
    acc = _dot(dp_ref[0].astype(BF16), w_ref[cw:cw + D_WIDTH, :])
    wide = seg.shape[0]
    for pb in range(cw // wide):
        sl = slice(pb * wide, (pb + 1) * wide)
        y = yf_ref[0, :, sl] + yb_ref[0, :, sl]
        mean = _seg_sum(y, seg, 1) * (1.0 / HEAD_DIM)
        dev = y - mean
        var = _seg_sum(dev * dev, seg, 1) * (1.0 / HEAD_DIM)
        yn = dev * lax.rsqrt(var + LNX_EPS) * lw_ref[:, sl] + lb_ref[:, sl]
        z = (yn + bonus_ref[0, :, sl]) * g_ref[0, :, sl]
        acc = acc + _dot(z.astype(BF16), w_ref[sl, :])
    x1 = x_ref[0] + mod_ref[0, 2:3, :] * acc
    o_ref[0] = _ffn_tile(x1, mod_ref, gain_ref, w1_ref, w2_ref, fg_ref, True)


def _out1_ffn(yf, yb, bonus, g, dp, x, mod, lnx_w, lnx_b, seg, w, gain, w1, w2, fgain):
    b, n, d = x.shape
    tm = min(FFN_TILE, n)
    cw = yf.shape[2]
    row = lambda bi, i: (bi, i, 0)
    const = lambda bi, i: (0, 0)
    return pl.pallas_call(
        _out1_ffn_kernel,
        grid=(b, n // tm),
        in_specs=[pl.BlockSpec((1, tm, cw), row)] * 4
        + [pl.BlockSpec((1, tm, D_WIDTH), row),
           pl.BlockSpec((1, tm, d), row),
           pl.BlockSpec((1, 6, d), lambda bi, i: (bi, 0, 0)),
           pl.BlockSpec((1, cw), const), pl.BlockSpec((1, cw), const),
           pl.BlockSpec(seg.shape, const),
           pl.BlockSpec(w.shape, const, pipeline_mode=pl.Buffered(1))]
        + _ffn_specs(d, w1, w2),
        out_specs=pl.BlockSpec((1, tm, d), row),
        out_shape=jax.ShapeDtypeStruct((b, n, d), F32),
        compiler_params=_cparams("parallel", "parallel"),
        name="out1_ffn",
    )(yf, yb, bonus, g, dp, x, mod, lnx_w, lnx_b, seg, w, gain, w1, w2, fgain)


def _rope_tables(n):
    rows = n // GRID_W
    row = jnp.repeat(jnp.arange(rows, dtype=F32), GRID_W)
    col = jnp.tile(jnp.arange(GRID_W, dtype=F32), rows)
    n_freq = HEAD_DIM // 4
    inv = ROPE_THETA ** (-jnp.arange(n_freq, dtype=F32) / n_freq)
    ang = jnp.concatenate([row[:, None] * inv[None, :], col[:, None] * inv[None, :]], axis=-1)
    cos, sin = jnp.cos(ang), jnp.sin(ang)
    cos_t = jnp.tile(cos, (1, LANES // cos.shape[1]))
    sin_t = jnp.tile(jnp.concatenate([-sin, sin], axis=-1), (1, LANES // HEAD_DIM))
    return cos_t, sin_t


def _kv_dup_columns():
    cols = []
    for section in range(4):
        for head in range(2):
            base = AB_Q_COLS + section * 2 * HEAD_DIM + head * HEAD_DIM
            cols += list(range(base, base + HEAD_DIM)) * 2
    return np.concatenate([np.arange(AB_Q_COLS), np.asarray(cols)])


def kernel(x, c, ctx, c_ctx, norm_gain, ada_w, ada_b, ffn_w_in, ffn_w_out, final_gain, ab_w_in, ab_q_gain, ab_k_gain, ab_sink, ab_w_out, cd_w_in, cd_mu, cd_w0, cd_w2, cd_a0, cd_a2, cd_g2, cd_k_k, cd_k_a, cd_r_k, cd_lnx_w, cd_lnx_b, cd_pool_w, cd_pool_scale, cd_w_out):
    b, n, d = x.shape
    nc = ctx.shape[1]
    pad = (-(b + 1)) % 8
    cs = jnp.concatenate([c, c_ctx[None, :], jnp.zeros((pad, d), F32)], axis=0)
    mods = _mods(cs, ada_w, ada_b)
    seg = jnp.asarray(np.kron(np.eye(2), np.ones((HEAD_DIM, HEAD_DIM))).astype(np.float32)).astype(BF16)
    fgain = final_gain.reshape(1, d)

    def layer_mods(i):
        ml = mods[i, :b].reshape(b, 6, d)
        mc = jnp.broadcast_to(mods[i, b].reshape(1, 6, d), (b, 6, d))
        return ml, mc

    ml, mc = layer_mods(0)
    w0 = ab_w_in[0][:, _kv_dup_columns()].astype(BF16)
    gain = norm_gain[0, 0].reshape(1, d)
    qg = jnp.tile(ab_q_gain[0], 2).reshape(1, LANES)
    kg = jnp.tile(ab_k_gain[0], 2).reshape(1, LANES)
    cos_l, sin_l = _rope_tables(n)
    cos_c, sin_c = jnp.ones((nc, LANES), F32), jnp.zeros((nc, LANES), F32)
    q_l, kv_l, vt_l = _proj0(x, ml, gain, w0, cos_l, sin_l, qg, kg, seg)
    q_c, kv_c, vt_c = _proj0(ctx, mc, gain, w0, cos_c, sin_c, qg, kg, seg)
    sink = ab_sink[0]
    w_out0 = ab_w_out[0].astype(BF16)
    gain2 = norm_gain[0, 1].reshape(1, d)
    w1 = ffn_w_in[0].astype(BF16)
    w2 = ffn_w_out[0].astype(BF16)
    yb_l = _attn_b_auto(q_l, kv_c, vt_c, kv_l, vt_l, ab_q_gain[0], ab_k_gain[0])
    xl = _out0_ffn(_attn_a(q_l, kv_c, vt_c, sink, kv_l, vt_l), yb_l, x, ml, w_out0, gain2, w1, w2, fgain)
    xc = _out0_ffn(_attn_a(q_c, kv_c, vt_c, sink), _attn_b(q_c, kv_c), ctx, mc, w_out0, gain2, w1, w2, fgain)

    ml, mc = layer_mods(1)
    gain = norm_gain[1, 0].reshape(1, d)
    w_in1 = cd_w_in[0].astype(BF16)
    pc_l = _proj1(xl, ml, gain, w_in1)
    pc_c = _proj1(xc, mc, gain, w_in1[:, :C_IN])
    zeros = jnp.zeros((DECAY_LORA_PAD, C_WIDTH), F32)
    w2x = jnp.stack([jnp.concatenate([cd_w2[0, 0], zeros]), jnp.concatenate([zeros, cd_w2[0, 1]])])
    a2x = jnp.stack([jnp.concatenate([cd_a2[0, 0], zeros]), jnp.concatenate([zeros, cd_a2[0, 1]])])
    w2x_hi = w2x.astype(BF16)
    w2x_hl = jnp.stack([w2x_hi, (w2x - w2x_hi.astype(F32)).astype(BF16)])
    wts = [jnp.stack([1.0 - cd_mu[0], 0.5 * cd_mu[0]]), cd_w0[0].reshape(2, 1, C_WIDTH), w2x_hl,
           cd_a0[0].reshape(2, 1, C_WIDTH), a2x.astype(BF16), cd_g2[0].astype(BF16),
           cd_k_k[0].reshape(1, C_WIDTH), cd_k_a[0].reshape(1, C_WIDTH),
           cd_r_k[0].reshape(1, C_WIDTH), seg]
    pool_w = jax.scipy.linalg.block_diag(*[cd_pool_w[0, g] for g in range(4)]).astype(BF16)
    pool_wts = [pool_w, cd_pool_scale[0].reshape(1, D_WIDTH)]
    prep_c = _prep(pc_c, wts, False)
    *prep_l, g_l, bonus_l, dp_l = _prep(pc_l, wts + pool_wts, True)
    h_zero = jnp.zeros((b, C_WIDTH // QUAD, CHUNK, QUAD), F32)
    _, h_f = _scanq(prep_c, 0, h_zero, False, False)
    _, h_b = _scanq(prep_c, 1, h_zero, True, False)
    y_f, _ = _scanq(prep_l, 0, h_f, False, True)
    y_b, _ = _scanq(prep_l, 1, h_b, True, True)
    seg4 = jnp.asarray(np.kron(np.eye(4), np.ones((HEAD_DIM, HEAD_DIM))).astype(np.float32)).astype(BF16)
    return _out1_ffn(y_f, y_b, bonus_l, g_l, dp_l, xl, ml, cd_lnx_w[0].reshape(1, C_WIDTH),
                     cd_lnx_b[0].reshape(1, C_WIDTH), seg4, cd_w_out[0].astype(BF16),
                     norm_gain[1, 1].reshape(1, d), ffn_w_in[1].astype(BF16),
                     ffn_w_out[1].astype(BF16), fgain)
```

```python
import functools

import numpy as np
import jax
import jax.numpy as jnp
from jax import lax
from jax.experimental import pallas as pl
from jax.experimental.pallas import tpu as pltpu

F32 = jnp.float32
BF16 = jnp.bfloat16
HIGHEST = lax.Precision.HIGHEST
LOG2_E = 1.4426950408889634
SCORE_BOUND_SAFE = 60.0

D_MODEL = 1024
GRID_W = 64
HEAD_DIM = 64
ROPE_THETA = 10000.0
NORM_EPS = 1e-6
WINDOW = 128
AB_Q_COLS = 1024
C_WIDTH = 768
C_IN = 2688
D_WIDTH = 256
CD_IN = C_IN + D_WIDTH
LNX_EPS = 64e-5
FFN_HIDDEN = 2816
POOL_WINDOWS = (2, 4, 8, 16)
DECAY_LORA_PAD = 64

LANES = 128
ROW_TILE = 256
FFN_TILE = 512
FFN_CHUNKS = 11
HALO = 8
CHUNK = 64
SCAN_BLOCK = 512
SCAN_QUADS = 3
VMEM_LIMIT = 56 * 1024 * 1024


def _cparams(*sem):
    return pltpu.CompilerParams(dimension_semantics=sem, vmem_limit_bytes=VMEM_LIMIT)


def _dot(a, b):
    return jnp.dot(a, b, preferred_element_type=F32)


def _dot32(a, b):
    return jnp.dot(a, b, preferred_element_type=F32, precision=HIGHEST)


def _split(a, terms):
    pieces = []
    for _ in range(terms - 1):
        hi = a.astype(BF16)
        pieces.append(hi)
        a = a - hi.astype(F32)
    pieces.append(a.astype(BF16))
    return pieces


def _seg_sum(z, seg, terms=2):
    out = None
    for piece in _split(z, terms):
        part = _dot(piece, seg)
        out = part if out is None else out + part
    return out


def _sigmoid(x):
    return 1.0 / (1.0 + jnp.exp(-x))


def _norm_mod(x, gain, shift, scale):
    ms = jnp.mean(x * x, axis=-1, keepdims=True)
    return (x * lax.rsqrt(ms + NORM_EPS) * gain) * (1.0 + scale) + shift


def _mods_kernel(c_ref, w_ref, b_ref, o_ref):
    c = c_ref[...]
    o_ref[0] = _dot32(c * _sigmoid(c), w_ref[0]) + b_ref[0]


def _mods(cs, ada_w, ada_b):
    depth, d, n6 = ada_w.shape
    tn = 768
    rows = cs.shape[0]
    return pl.pallas_call(
        _mods_kernel,
        grid=(depth, n6 // tn),
        in_specs=[pl.BlockSpec((rows, d), lambda l, j: (0, 0)),
                  pl.BlockSpec((1, d, tn), lambda l, j: (l, 0, j)),
                  pl.BlockSpec((1, 1, tn), lambda l, j: (l, 0, j))],
        out_specs=pl.BlockSpec((1, rows, tn), lambda l, j: (l, 0, j)),
        out_shape=jax.ShapeDtypeStruct((depth, rows, n6), F32),
        compiler_params=_cparams("arbitrary", "arbitrary"),
        name="mods",
    )(cs, ada_w, ada_b.reshape(depth, 1, n6))


def _proj0_kernel(x_ref, mod_ref, gain_ref, w_ref, cos_ref, sin_ref, qg_ref, kg_ref, seg_ref,
                  q_ref, kv_ref, vt_ref):
    h = _norm_mod(x_ref[0], gain_ref[...], mod_ref[0, 0:1, :], mod_ref[0, 1:2, :])
    p = _dot(h.astype(BF16), w_ref[...])
    tm = p.shape[0]
    cos = cos_ref[...]
    sin = sin_ref[...]
    lane = lax.broadcasted_iota(jnp.int32, (tm, LANES), 1)
    first_half = (lane & (HEAD_DIM - 1)) < HEAD_DIM // 2
    seg = seg_ref[...]

    def rope(z):
        partner = jnp.where(first_half, pltpu.roll(z, LANES - HEAD_DIM // 2, 1),
                            pltpu.roll(z, HEAD_DIM // 2, 1))
        return z * cos + partner * sin

    def head_norm(z, g):
        ms = _seg_sum(z * z, seg, 1) * (1.0 / HEAD_DIM)
        return z * lax.rsqrt(ms + NORM_EPS) * g

    scale = HEAD_DIM ** -0.5 * LOG2_E
    for blk in range(8):
        z = p[:, blk * LANES:(blk + 1) * LANES]
        if blk >= 4:
            z = head_norm(z, qg_ref[...])
        q_ref[0, :, blk * LANES:(blk + 1) * LANES] = (rope(z) * scale).astype(BF16)
    for blk in range(8):
        z = p[:, AB_Q_COLS + blk * LANES:AB_Q_COLS + (blk + 1) * LANES]
        if blk in (4, 5):
            z = head_norm(z, kg_ref[...])
        if blk in (0, 1, 4, 5):
            z = rope(z)
        kv_ref[0, :, blk * LANES:(blk + 1) * LANES] = z.astype(BF16)
        if blk in (6, 7):
            zt = z.T
            top = lax.broadcasted_iota(jnp.int32, zt.shape, 0) < HEAD_DIM
            vt_ref[0, blk - 6] = jnp.where(top, zt, 1.0).astype(BF16)


def _proj0(x, mod, gain, w, cos, sin, qg, kg, seg):
    b, n, d = x.shape
    tm = ROW_TILE
    nw = w.shape[1]
    const = lambda bi, i: (0, 0)
    return pl.pallas_call(
        _proj0_kernel,
        grid=(b, n // tm),
        in_specs=[pl.BlockSpec((1, tm, d), lambda bi, i: (bi, i, 0)),
                  pl.BlockSpec((1, 6, d), lambda bi, i: (bi, 0, 0)),
                  pl.BlockSpec((1, d), const),
                  pl.BlockSpec((d, nw), const),
                  pl.BlockSpec((tm, LANES), lambda bi, i: (i, 0)),
                  pl.BlockSpec((tm, LANES), lambda bi, i: (i, 0)),
                  pl.BlockSpec((1, LANES), const),
                  pl.BlockSpec((1, LANES), const),
                  pl.BlockSpec((LANES, LANES), const)],
        out_specs=[pl.BlockSpec((1, tm, 1024), lambda bi, i: (bi, i, 0)),
                   pl.BlockSpec((1, tm, 1024), lambda bi, i: (bi, i, 0)),
                   pl.BlockSpec((1, 2, LANES, tm), lambda bi, i: (bi, 0, 0, i))],
        out_shape=[jax.ShapeDtypeStruct((b, n, 1024), BF16),
                   jax.ShapeDtypeStruct((b, n, 1024), BF16),
                   jax.ShapeDtypeStruct((b, 2, LANES, n), BF16)],
        compiler_params=_cparams("parallel", "parallel"),
        name="proj0",
    )(x, mod, gain, w, cos, sin, qg, kg, seg)


def _masked_q(q_ref, h, low):
    q2 = q_ref[0, :, (h // 2) * LANES:(h // 2 + 1) * LANES]
    keep = low if h % 2 == 0 else jnp.logical_not(low)
    return jnp.where(keep, q2, jnp.zeros_like(q2))


def _attn_b_kernel(q_ref, kc_ref, vc_ref, *rest, n_kt, has_latent):
    if has_latent:
        k_ref, v_ref, o_ref, m_sc, acc_sc = rest
    else:
        o_ref, m_sc, acc_sc = rest
    kt = pl.program_id(3)
    tq = q_ref.shape[1]
    low = lax.broadcasted_iota(jnp.int32, (tq, LANES), 1) < HEAD_DIM

    def update(k, v):
        tk = k.shape[0]
        v1 = jnp.where(lax.broadcasted_iota(jnp.int32, (tk, LANES), 1) < HEAD_DIM, v, jnp.ones_like(v))
        ss = [lax.dot_general(_masked_q(q_ref, h, low), k, (((1,), (1,)), ((), ())),
                              preferred_element_type=F32) for h in range(4)]
        for h in range(4):
            m_prev = m_sc[h]
            m_new = jnp.maximum(m_prev, jnp.max(ss[h], axis=1, keepdims=True))
            p = jnp.exp2(ss[h] - jnp.concatenate([m_new] * (tk // LANES), axis=1)).astype(BF16)
            acc_sc[h] = jnp.exp2(m_prev - m_new) * acc_sc[h] + _dot(p, v1)
            m_sc[h] = m_new

    @pl.when(kt == 0)
    def _():
        m_sc[...] = jnp.full(m_sc.shape, -jnp.inf, F32)
        acc_sc[...] = jnp.zeros(acc_sc.shape, F32)
        update(kc_ref[0], vc_ref[0])

    if has_latent:
        update(k_ref[0], v_ref[0])

    @pl.when(kt == n_kt - 1)
    def _():
        for pr in range(2):
            a0 = acc_sc[2 * pr]
            a1 = acc_sc[2 * pr + 1]
            o0 = a0 / pltpu.roll(a0, HEAD_DIM, 1)
            o1 = pltpu.roll(a1, HEAD_DIM, 1) / a1
            o_ref[0, :, pr * LANES:(pr + 1) * LANES] = jnp.where(low, o0, o1).astype(BF16)


def _attn_b(q, kv_c, kv=None, tk=1024):
    b, n, _ = q.shape
    nc = kv_c.shape[1]
    tq = ROW_TILE
    has_latent = kv is not None
    n_kt = kv.shape[1] // tk if has_latent else 1
    in_specs = [pl.BlockSpec((1, tq, 2 * LANES), lambda bi, g, i, j: (bi, i, 2 + g)),
                pl.BlockSpec((1, nc, LANES), lambda bi, g, i, j: (bi, 0, 4 + g)),
                pl.BlockSpec((1, nc, LANES), lambda bi, g, i, j: (bi, 0, 6 + g))]
    args = [q, kv_c, kv_c]
    if has_latent:
        in_specs += [pl.BlockSpec((1, tk, LANES), lambda bi, g, i, j: (bi, j, 4 + g)),
                     pl.BlockSpec((1, tk, LANES), lambda bi, g, i, j: (bi, j, 6 + g))]
        args += [kv, kv]
    return pl.pallas_call(
        functools.partial(_attn_b_kernel, n_kt=n_kt, has_latent=has_latent),
        grid=(b, 2, n // tq, n_kt),
        in_specs=in_specs,
        out_specs=pl.BlockSpec((1, tq, 2 * LANES), lambda bi, g, i, j: (bi, i, g)),
        out_shape=jax.ShapeDtypeStruct((b, n, 512), BF16),
        scratch_shapes=[pltpu.VMEM((4, tq, LANES), F32), pltpu.VMEM((4, tq, LANES), F32)],
        compiler_params=_cparams("parallel", "parallel", "parallel", "arbitrary"),
        name="attn_b",
    )(*args)


def _attn_bt_kernel(q_ref, kc_ref, vtc_ref, k_ref, vt_ref, o_ref, acc_sc, *, n_kt):
    kt = pl.program_id(3)
    tq = q_ref.shape[1]
    low = lax.broadcasted_iota(jnp.int32, (tq, LANES), 1) < HEAD_DIM

    def update(k, vt):
        sts = [lax.dot_general(k, _masked_q(q_ref, h, low), (((1,), (1,)), ((), ())),
                               preferred_element_type=F32) for h in range(4)]
        for h in range(4):
            acc_sc[h] = acc_sc[h] + _dot(vt, jnp.exp2(sts[h]).astype(BF16))

    @pl.when(kt == 0)
    def _():
        acc_sc[...] = jnp.zeros(acc_sc.shape, F32)
        update(kc_ref[0], vtc_ref[0, 0])

    update(k_ref[0], vt_ref[0, 0])

    @pl.when(kt == n_kt - 1)
    def _():
        ot = jnp.concatenate([acc_sc[h, 0:HEAD_DIM, :] / acc_sc[h, HEAD_DIM:2 * HEAD_DIM, :]
                              for h in range(4)], axis=0)
        o_ref[0] = ot.T.astype(BF16)


def _attn_bt(q, kv_c, vt_c, kv, vt, tk=4096):
    b, n, _ = q.shape
    nc = kv_c.shape[1]
    tq = ROW_TILE
    tk = min(tk, kv.shape[1])
    n_kt = kv.shape[1] // tk
    return pl.pallas_call(
        functools.partial(_attn_bt_kernel, n_kt=n_kt),
        grid=(b, 2, n // tq, n_kt),
        in_specs=[pl.BlockSpec((1, tq, 2 * LANES), lambda bi, g, i, j: (bi, i, 2 + g)),
                  pl.BlockSpec((1, nc, LANES), lambda bi, g, i, j: (bi, 0, 4 + g)),
                  pl.BlockSpec((1, 1, LANES, nc), lambda bi, g, i, j: (bi, g, 0, 0)),
                  pl.BlockSpec((1, tk, LANES), lambda bi, g, i, j: (bi, j, 4 + g)),
                  pl.BlockSpec((1, 1, LANES, tk), lambda bi, g, i, j: (bi, g, 0, j))],
        out_specs=pl.BlockSpec((1, tq, 2 * LANES), lambda bi, g, i, j: (bi, i, g)),
        out_shape=jax.ShapeDtypeStruct((b, n, 512), BF16),
        scratch_shapes=[pltpu.VMEM((4, LANES, tq), F32)],
        compiler_params=_cparams("parallel", "parallel", "parallel", "arbitrary"),
        name="attn_b_bounded",
    )(q, kv_c, vt_c, kv, vt)


def _attn_b_auto(q, kv_c, vt_c, kv, vt, q_gain, k_gain):
    bound = (1.02 * HEAD_DIM * HEAD_DIM ** -0.5 * LOG2_E) * jnp.max(jnp.abs(q_gain)) * jnp.max(jnp.abs(k_gain))
    return lax.cond(bound <= SCORE_BOUND_SAFE,
                    lambda: _attn_bt(q, kv_c, vt_c, kv, vt),
                    lambda: _attn_b(q, kv_c, kv))


def _attn_a_kernel(sink_ref, q_ref, kc_ref, vc_ref, *rest, has_local):
    if has_local:
        kp_ref, kcur_ref, kn_ref, vp_ref, vcur_ref, vn_ref, o_ref, kbuf, vbuf = rest
    else:
        o_ref, kbuf, vbuf = rest
    g = pl.program_id(1)
    i = pl.program_id(2)
    n_i = pl.num_programs(2)
    tq = q_ref.shape[1]
    nc = kc_ref.shape[1]
    kbuf[0:nc, :] = kc_ref[0]
    vbuf[0:nc, :] = vc_ref[0]
    nk = nc
    if has_local:
        kbuf[nc:nc + WINDOW, :] = kp_ref[0]
        kbuf[nc + WINDOW:nc + WINDOW + tq, :] = kcur_ref[0]
        kbuf[nc + WINDOW + tq:nc + 2 * WINDOW + tq, :] = kn_ref[0]
        vbuf[nc:nc + WINDOW, :] = vp_ref[0]
        vbuf[nc + WINDOW:nc + WINDOW + tq, :] = vcur_ref[0]
        vbuf[nc + WINDOW + tq:nc + 2 * WINDOW + tq, :] = vn_ref[0]
        nk = nc + 2 * WINDOW + tq
        row = lax.broadcasted_iota(jnp.int32, (tq, nk), 0)
        col = lax.broadcasted_iota(jnp.int32, (tq, nk), 1) - nc
        lo_c = jnp.where(i == 0, WINDOW, 0)
        hi_c = jnp.where(i == n_i - 1, WINDOW + tq, 2 * WINDOW + tq)
        valid = (col < 0) | ((col >= row) & (col <= row + 2 * WINDOW) & (col >= lo_c) & (col < hi_c))
    k = kbuf[...]
    v = vbuf[...]
    v1 = jnp.where(lax.broadcasted_iota(jnp.int32, (nk, LANES), 1) < HEAD_DIM, v, jnp.ones_like(v))
    low = lax.broadcasted_iota(jnp.int32, (tq, LANES), 1) < HEAD_DIM
    ss = [lax.dot_general(_masked_q(q_ref, h, low), k, (((1,), (1,)), ((), ())),
                          preferred_element_type=F32) for h in range(4)]
    accs = []
    for h in range(4):
        s = jnp.where(valid, ss[h], -jnp.inf) if has_local else ss[h]
        sink = sink_ref[g * 4 + h] * LOG2_E
        m = jnp.maximum(jnp.max(s, axis=1, keepdims=True), sink)
        p = jnp.exp2(s - m)
        sink_p = jnp.where(low, 0.0, jnp.exp2(sink - m))
        accs.append(_dot(p.astype(BF16), v1) + sink_p)
    for pr in range(2):
        a0, a1 = accs[2 * pr], accs[2 * pr + 1]
        o0 = a0 / pltpu.roll(a0, HEAD_DIM, 1)
        o1 = pltpu.roll(a1, HEAD_DIM, 1) / a1
        o_ref[0, :, pr * LANES:(pr + 1) * LANES] = jnp.where(low, o0, o1).astype(BF16)


def _attn_a(q, kv_c, sink, kv=None):
    b, n, _ = q.shape
    nc = kv_c.shape[1]
    tq = ROW_TILE
    has_local = kv is not None
    in_specs = [pl.BlockSpec(memory_space=pltpu.SMEM),
                pl.BlockSpec((1, tq, 2 * LANES), lambda bi, g, i: (bi, i, g)),
                pl.BlockSpec((1, nc, LANES), lambda bi, g, i: (bi, 0, g)),
                pl.BlockSpec((1, nc, LANES), lambda bi, g, i: (bi, 0, 2 + g))]
    args = [sink, q, kv_c, kv_c]
    nk = nc
    if has_local:
        per = tq // WINDOW
        last = n // WINDOW - 1
        prev_i = lambda i: jnp.maximum(i * per - 1, 0)
        next_i = lambda i: jnp.minimum((i + 1) * per, last)
        for off in (0, 2):
            in_specs += [pl.BlockSpec((1, WINDOW, LANES), lambda bi, g, i, off=off: (bi, prev_i(i), off + g)),
                         pl.BlockSpec((1, tq, LANES), lambda bi, g, i, off=off: (bi, i, off + g)),
                         pl.BlockSpec((1, WINDOW, LANES), lambda bi, g, i, off=off: (bi, next_i(i), off + g))]
            args += [kv, kv, kv]
        nk = nc + 2 * WINDOW + tq
    return pl.pallas_call(
        functools.partial(_attn_a_kernel, has_local=has_local),
        grid=(b, 2, n // tq),
        in_specs=in_specs,
        out_specs=pl.BlockSpec((1, tq, 2 * LANES), lambda bi, g, i: (bi, i, g)),
        out_shape=jax.ShapeDtypeStruct((b, n, 512), BF16),
        scratch_shapes=[pltpu.VMEM((nk, LANES), BF16), pltpu.VMEM((nk, LANES), BF16)],
        compiler_params=_cparams("parallel", "parallel", "parallel"),
        name="attn_a",
    )(*args)


def _ffn_tile(x, mod_ref, gain_ref, w1_ref, w2_ref, fg_ref, final):
    h = _norm_mod(x, gain_ref[...], mod_ref[0, 3:4, :], mod_ref[0, 4:5, :]).astype(BF16)
    hid = w2_ref.shape[0]
    ch = hid // FFN_CHUNKS
    acc = None
    for j in range(FFN_CHUNKS):
        gate = _dot(h, w1_ref[:, j * ch:(j + 1) * ch])
        up = _dot(h, w1_ref[:, hid + j * ch:hid + (j + 1) * ch])
        act = (gate * _sigmoid(gate) * up).astype(BF16)
        part = _dot(act, w2_ref[j * ch:(j + 1) * ch, :])
        acc = part if acc is None else acc + part
    y = x + mod_ref[0, 5:6, :] * acc
    if final:
        ms = jnp.mean(y * y, axis=-1, keepdims=True)
        y = y * lax.rsqrt(ms + NORM_EPS) * fg_ref[...]
    return y


def _out0_ffn_kernel(ya_ref, yb_ref, x_ref, mod_ref, wo_ref, gain_ref, w1_ref, w2_ref, fg_ref, o_ref):
    half = ya_ref.shape[2]
    y = _dot(ya_ref[0], wo_ref[0:half, :]) + _dot(yb_ref[0], wo_ref[half:2 * half, :])
    x1 = x_ref[0] + mod_ref[0, 2:3, :] * y
    o_ref[0] = _ffn_tile(x1, mod_ref, gain_ref, w1_ref, w2_ref, fg_ref, False)


def _ffn_specs(d, w1, w2):
    const = lambda bi, i: (0, 0)
    return [pl.BlockSpec((1, d), const),
            pl.BlockSpec(w1.shape, const, pipeline_mode=pl.Buffered(1)),
            pl.BlockSpec(w2.shape, const, pipeline_mode=pl.Buffered(1)),
            pl.BlockSpec((1, d), const)]


def _out0_ffn(ya, yb, x, mod, wo, gain, w1, w2, fgain):
    b, n, d = x.shape
    tm = min(FFN_TILE, n)
    row = lambda bi, i: (bi, i, 0)
    return pl.pallas_call(
        _out0_ffn_kernel,
        grid=(b, n // tm),
        in_specs=[pl.BlockSpec((1, tm, ya.shape[2]), row),
                  pl.BlockSpec((1, tm, yb.shape[2]), row),
                  pl.BlockSpec((1, tm, d), row),
                  pl.BlockSpec((1, 6, d), lambda bi, i: (bi, 0, 0)),
                  pl.BlockSpec(wo.shape, lambda bi, i: (0, 0), pipeline_mode=pl.Buffered(1))]
        + _ffn_specs(d, w1, w2),
        out_specs=pl.BlockSpec((1, tm, d), row),
        out_shape=jax.ShapeDtypeStruct((b, n, d), F32),
        compiler_params=_cparams("parallel", "parallel"),
        name="out0_ffn",
    )(ya, yb, x, mod, wo, gain, w1, w2, fgain)


def _proj1_kernel(x_ref, mod_ref, gain_ref, w_ref, o_ref):
    h = _norm_mod(x_ref[0], gain_ref[...], mod_ref[0, 0:1, :], mod_ref[0, 1:2, :])
    o_ref[0] = _dot(h.astype(BF16), w_ref[...])


def _proj1(x, mod, gain, w):
    b, n, d = x.shape
    tm = min(FFN_TILE, n)
    nw = w.shape[1]
    return pl.pallas_call(
        _proj1_kernel,
        grid=(b, n // tm),
        in_specs=[pl.BlockSpec((1, tm, d), lambda bi, i: (bi, i, 0)),
                  pl.BlockSpec((1, 6, d), lambda bi, i: (bi, 0, 0)),
                  pl.BlockSpec((1, d), lambda bi, i: (0, 0)),
                  pl.BlockSpec((d, nw), lambda bi, i: (0, 0))],
        out_specs=pl.BlockSpec((1, tm, nw), lambda bi, i: (bi, i, 0)),
        out_shape=jax.ShapeDtypeStruct((b, n, nw), F32),
        compiler_params=_cparams("parallel", "parallel"),
        name="proj1",
    )(x, mod, gain, w)


def _prep_kernel(pc_ref, prev_ref, next_ref, mu_ref, w0_ref, w2_ref, a0_ref, a2_ref, g2_ref,
                 kk_ref, ka_ref, rk_ref, seg_ref, tri_ref, *rest, latent):
    if latent:
        pw_ref, ps_ref, v_o, at_o, rt_o, bt_o, kt_o, wl_o, g_o, bonus_o, d_o, ext, tmp = rest
    else:
        v_o, at_o, rt_o, bt_o, kt_o, wl_o, ext, tmp = rest
    i = pl.program_id(1)
    nt = pl.num_programs(1)
    tm = pc_ref.shape[1]
    cw = C_WIDTH
    ext[HALO:HALO + tm, :] = pc_ref[0]
    ext[0:HALO, :] = jnp.where(i > 0, prev_ref[0], 0.0)
    ext[HALO + tm:2 * HALO + tm, :] = jnp.where(i < nt - 1, next_ref[0], 0.0)

    def mixed(lo, hi):
        nb = ext[HALO - 1:HALO - 1 + tm, lo:hi] + ext[HALO + 1:HALO + 1 + tm, lo:hi]
        return ext[HALO:HALO + tm, lo:hi] * mu_ref[0:1, lo:hi] + nb * mu_ref[1:2, lo:hi]

    lora = mixed(3 * cw, C_IN)
    tw = jnp.tanh(lora[:, 0:LANES])
    xa = lora[:, LANES:2 * LANES]
    if latent:
        g_o[0] = _dot(_sigmoid(lora[:, 2 * LANES:3 * LANES]).astype(BF16), g2_ref[...])
    tw_hi, tw_lo = _split(tw, 2)
    xa_b = xa.astype(BF16)
    for d in range(2):
        z = w0_ref[d] + _dot(tw_hi, w2_ref[0, d]) + _dot(tw_lo, w2_ref[0, d]) + _dot(tw_hi, w2_ref[1, d])
        w_log = -(jnp.maximum(-z, 0.0) + jnp.log(1.0 + jnp.exp(-jnp.abs(z)))) - 0.5
        tmp[d] = -LOG2_E * jnp.exp(w_log)
        tmp[2 + d] = _sigmoid(a0_ref[d] + _dot(xa_b, a2_ref[d]))
    seg = seg_ref[...]
    n_chunks = tm // CHUNK
    for pb in range(cw // LANES):
        sl = slice(pb * LANES, (pb + 1) * LANES)
        r = mixed(pb * LANES, (pb + 1) * LANES)
        k = mixed(cw + pb * LANES, cw + (pb + 1) * LANES)
        v = mixed(2 * cw + pb * LANES, 2 * cw + (pb + 1) * LANES)
        kk = k * kk_ref[:, sl]
        kk = kk / jnp.maximum(jnp.sqrt(_seg_sum(kk * kk, seg, 1)), 1e-12)
        neg_kk = -kk
        kka = k * ka_ref[:, sl]
        ksum = jnp.zeros_like(k)
        for d in range(2):
            lw = tmp[d, :, sl]
            a = tmp[2 + d, :, sl]
            k_d = k + kka * (a - 1.0)
            bb = kk * a
            ksum = ksum + k_d
            cs2 = _dot(tri_ref[d], jnp.concatenate(_split(lw, 2), axis=1))
            cs = cs2[:, 0:LANES] + cs2[:, LANES:2 * LANES]
            for j in range(n_chunks):
                last = j * CHUNK if d == 1 else (j + 1) * CHUNK - 1
                wl_o[d, 0, j, :, sl] = jnp.exp2(cs[last:last + 1, :])
            e_up = jnp.exp2(-cs)
            at_o[d, 0, :, sl] = (neg_kk * jnp.exp2(cs - lw)).astype(BF16)
            rt_o[d, 0, :, sl] = (r * jnp.exp2(cs)).astype(BF16)
            bt_o[d, 0, :, sl] = (bb * e_up).astype(BF16)
            kt_o[d, 0, :, sl] = (k_d * e_up).astype(BF16)
        v_o[0, :, sl] = v.astype(BF16)
        if latent:
            bonus_o[0, :, sl] = _seg_sum(r * rk_ref[:, sl] * ksum, seg) * v
    if latent:
        n_tok = nt * tm
        pos = i * tm + lax.broadcasted_iota(jnp.int32, (tm, LANES), 0)
        group1 = lax.broadcasted_iota(jnp.int32, (tm, LANES), 1) >= D_WIDTH // 4
        pooled = []
        for half in range(2):
            lo, hi = C_IN + half * LANES, C_IN + (half + 1) * LANES
            w_small, w_big = POOL_WINDOWS[2 * half], POOL_WINDOWS[2 * half + 1]
            s_small = None
            s_big = None
            for off in range(-(w_big // 2), w_big - w_big // 2):
                piece = ext[HALO + off:HALO + off + tm, lo:hi]
                s_big = piece if s_big is None else s_big + piece
                if -(w_small // 2) <= off < w_small - w_small // 2:
                    s_small = piece if s_small is None else s_small + piece

            def count(w):
                lo_p = jnp.clip(pos - w // 2, 0, n_tok)
                hi_p = jnp.clip(pos + (w - w // 2), 0, n_tok)
                return (hi_p - lo_p).astype(F32)

            mean = jnp.where(group1, s_big / count(w_big), s_small / count(w_small))
            pooled.append((mean - ext[HALO:HALO + tm, lo:hi]).astype(BF16))
        pm = jnp.concatenate(pooled, axis=1)
        d_o[0] = _dot(pm, pw_ref[...]) * ps_ref[...]


def _chunk_cumsum_matrices(tm):
    t = np.arange(tm)[:, None]
    u = np.arange(tm)[None, :]
    same = (t // CHUNK) == (u // CHUNK)
    return np.stack([same & (u <= t), same & (u >= t)]).astype(np.float32)


def _prep(pc, wts, latent):
    b, n, cin = pc.shape
    tm = ROW_TILE
    cw = C_WIDTH
    per = tm // HALO
    last = n // HALO - 1
    n_chunks = tm // CHUNK
    row = lambda bi, i: (bi, i, 0)
    drow = lambda bi, i: (0, bi, i, 0)
    wts = list(wts)
    wts = wts[:10] + [jnp.asarray(_chunk_cumsum_matrices(tm)).astype(BF16)] + (wts[10:] if latent else [])

    def full(a):
        return pl.BlockSpec(a.shape, lambda bi, i, nd=a.ndim: (0,) * nd)

    in_specs = [pl.BlockSpec((1, tm, cin), row),
                pl.BlockSpec((1, HALO, cin), lambda bi, i: (bi, jnp.maximum(i * per - 1, 0), 0)),
                pl.BlockSpec((1, HALO, cin), lambda bi, i: (bi, jnp.minimum((i + 1) * per, last), 0))]
    in_specs += [full(a) for a in wts]
    out_specs = [pl.BlockSpec((1, tm, cw), row)] + [pl.BlockSpec((2, 1, tm, cw), drow)] * 4
    out_shape = [jax.ShapeDtypeStruct((b, n, cw), BF16)] + [jax.ShapeDtypeStruct((2, b, n, cw), BF16)] * 4
    out_specs.append(pl.BlockSpec((2, 1, n_chunks, 1, cw), lambda bi, i: (0, bi, i, 0, 0)))
    out_shape.append(jax.ShapeDtypeStruct((2, b, n // CHUNK, 1, cw), F32))
    if latent:
        out_specs += [pl.BlockSpec((1, tm, cw), row)] * 2 + [pl.BlockSpec((1, tm, D_WIDTH), row)]
        out_shape += [jax.ShapeDtypeStruct((b, n, cw), F32)] * 2 + [jax.ShapeDtypeStruct((b, n, D_WIDTH), F32)]
    return pl.pallas_call(
        functools.partial(_prep_kernel, latent=latent),
        grid=(b, n // tm),
        in_specs=in_specs,
        out_specs=out_specs,
        out_shape=out_shape,
        scratch_shapes=[pltpu.VMEM((tm + 2 * HALO, cin), F32), pltpu.VMEM((4, tm, cw), F32)],
        compiler_params=_cparams("parallel", "parallel"),
        name="prep",
    )(pc, pc, pc, *wts)


QUAD = 4 * HEAD_DIM


def _quad_masks(rev):
    t = np.arange(CHUNK)[:, None]
    u = np.arange(CHUNK)[None, :]
    before = (u > t) if rev else (u < t)
    masks = [before, before | (u == t), u == t]
    s = 1
    while s < CHUNK:
        blk = (t // (2 * s)) == (u // (2 * s))
        t_late = (t % (2 * s) < s) if rev else (t % (2 * s) >= s)
        u_early = (u % (2 * s) >= s) if rev else (u % (2 * s) < s)
        masks.append(blk & t_late & u_early)
        s *= 2
    masks = np.stack([np.tile(m, (1, QUAD // CHUNK)) for m in masks]).astype(np.float32)
    hid = np.arange(QUAD) // HEAD_DIM
    return masks, (hid[:, None] == hid[None, :]).astype(np.float32)


def _scanq_kernel(v_ref, at_ref, rt_ref, bt_ref, kt_ref, wl_ref, h0_ref, msk_ref, bd_ref, *rest,
                  rev, with_y):
    if with_y:
        y_ref, hT_ref, g_sc = rest
    else:
        hT_ref, g_sc = rest
    tb = pl.program_id(2)
    n_tb = pl.num_programs(2)
    n_chunks = v_ref.shape[1] // CHUNK
    n_quads = v_ref.shape[2] // QUAD
    reps = QUAD // CHUNK

    @pl.when(tb == 0)
    def _():
        g_sc[...] = h0_ref[0]

    bd_mask = bd_ref[...] > 0
    strict = msk_ref[0] > 0
    incl = msk_ref[1] > 0
    eye4 = msk_ref[2]
    n_levels = msk_ref.shape[0] - 3
    nt_dims = (((1,), (1,)), ((), ()))
    tn_dims = (((0,), (0,)), ((), ()))

    def bd(x4):
        return jnp.where(bd_mask, jnp.concatenate([x4] * reps, axis=0), jnp.zeros((QUAD, QUAD), BF16))

    def fold(m):
        m = jnp.where(bd_mask, m, 0.0)
        out = m[0:CHUNK]
        for r in range(1, reps):
            out = out + m[r * CHUNK:(r + 1) * CHUNK]
        return out

    order = list(range(n_chunks - 1, -1, -1) if rev else range(n_chunks))
    chains = [(qq, c) for c in order for qq in range(n_quads)]
    ch = {}
    for key in chains:
        qq, c = key
        rows = slice(c * CHUNK, (c + 1) * CHUNK)
        lanes = slice(qq * QUAD, (qq + 1) * QUAD)
        q = ch[key] = {"rows": rows, "lanes": lanes}
        q["v"] = v_ref[0, rows, lanes]
        q["a"] = at_ref[0, 0, rows, lanes]
        q["b"] = bt_ref[0, 0, rows, lanes]
        q["k"] = kt_ref[0, 0, rows, lanes]
        q["bdv"] = bd(q["v"])
        rhs = jnp.concatenate([bd(q["b"]), bd(q["k"])], axis=0)
        if with_y:
            q["r"] = rt_ref[0, 0, rows, lanes]
            lhs = jnp.concatenate([q["a"], q["r"]], axis=0)
        else:
            lhs = q["a"]
        gram = lax.dot_general(lhs, rhs, nt_dims, preferred_element_type=F32)
        q["ab"] = jnp.where(strict, gram[0:CHUNK, 0:QUAD], 0.0)
        ak = jnp.where(strict, gram[0:CHUNK, QUAD:2 * QUAD], 0.0).astype(BF16)
        if with_y:
            q["rb"] = jnp.where(incl, gram[CHUNK:2 * CHUNK, 0:QUAD], 0.0).astype(BF16)
            rk = jnp.where(incl, gram[CHUNK:2 * CHUNK, QUAD:2 * QUAD], 0.0).astype(BF16)
            kv = _dot(jnp.concatenate([ak, rk], axis=0), q["bdv"])
            q["akv"], q["rkv"] = kv[0:CHUNK], kv[CHUNK:2 * CHUNK]
        else:
            q["akv"] = _dot(ak, q["bdv"])
        q["t"] = eye4 + q["ab"] * msk_ref[3]
    for lvl in range(1, n_levels):
        for key in chains:
            q = ch[key]
            q["x"] = _dot((q["ab"] * msk_ref[3 + lvl]).astype(BF16), bd(q["t"].astype(BF16)))
        for key in chains:
            q = ch[key]
            q["t"] = q["t"] + _dot(q["t"].astype(BF16), bd(q["x"].astype(BF16)))
    for key in chains:
        q = ch[key]
        au = _dot(q["t"].astype(BF16),
                  jnp.concatenate([bd(q["a"]), bd(q["akv"].astype(BF16))], axis=1))
        au_b = au.astype(BF16)
        wl = wl_ref[0, 0, key[1], :, q["lanes"]]
        mc = lax.dot_general(au_b, q["b"], tn_dims, preferred_element_type=F32)
        vk = lax.dot_general(q["v"], q["k"], tn_dims, preferred_element_type=F32)
        q["mt"] = jnp.where(bd_mask, mc[0:QUAD] * wl, 0.0).astype(BF16)
        q["ct"] = fold(mc[QUAD:2 * QUAD] + vk) * wl
        if with_y:
            rbau = _dot(q["rb"], jnp.concatenate([bd(au_b[:, 0:QUAD]), bd(au_b[:, QUAD:2 * QUAD])], axis=1))
            q["rt"] = (q["r"].astype(F32) + rbau[:, 0:QUAD]).astype(BF16)
            q["y0"] = rbau[:, QUAD:2 * QUAD] + q["rkv"]
    g = [g_sc[qq] for qq in range(n_quads)]
    for key in chains:
        qq, c = key
        q = ch[key]
        g_b = g[qq].astype(BF16)
        if with_y:
            y_ref[0, q["rows"], q["lanes"]] = (
                lax.dot_general(q["rt"], bd(g_b), nt_dims, preferred_element_type=F32) + q["y0"])
        g[qq] = g[qq] * wl_ref[0, 0, c, :, q["lanes"]] + _dot(g_b, q["mt"]) + q["ct"]
    for qq in range(n_quads):
        g_sc[qq] = g[qq]

    @pl.when(tb == n_tb - 1)
    def _():
        hT_ref[0] = g_sc[...]


def _scanq(prep, d, h0, rev, with_y):
    v, at, rt, bt, kt, wl = prep
    b, n, cw = v.shape
    tb = min(SCAN_BLOCK, n)
    n_tb = n // tb
    n_quads = cw // QUAD
    sq = SCAN_QUADS
    wide = sq * QUAD
    msk, bdm = _quad_masks(rev)
    tmap = (lambda t: n_tb - 1 - t) if rev else (lambda t: t)
    shared = pl.BlockSpec((1, tb, wide), lambda bi, p, t: (bi, tmap(t), p))
    perdir = pl.BlockSpec((1, 1, tb, wide), lambda bi, p, t: (d, bi, tmap(t), p))
    decay = pl.BlockSpec((1, 1, tb // CHUNK, 1, wide), lambda bi, p, t: (d, bi, tmap(t), 0, p))
    state = pl.BlockSpec((1, sq, CHUNK, QUAD), lambda bi, p, t: (bi, p, 0, 0))
    out_specs = [state]
    out_shape = [jax.ShapeDtypeStruct((b, n_quads, CHUNK, QUAD), F32)]
    if with_y:
        out_specs = [shared] + out_specs
        out_shape = [jax.ShapeDtypeStruct((b, n, cw), F32)] + out_shape
    res = pl.pallas_call(
        functools.partial(_scanq_kernel, rev=rev, with_y=with_y),
        grid=(b, n_quads // sq, n_tb),
        in_specs=[shared, perdir, perdir, perdir, perdir, decay, state,
                  pl.BlockSpec(msk.shape, lambda bi, p, t: (0, 0, 0)),
                  pl.BlockSpec(bdm.shape, lambda bi, p, t: (0, 0))],
        out_specs=out_specs,
        out_shape=out_shape,
        scratch_shapes=[pltpu.VMEM((sq, CHUNK, QUAD), F32)],
        compiler_params=_cparams("parallel", "parallel", "arbitrary"),
        name="scan_rev" if rev else "scan_fwd",
    )(v, at, rt, bt, kt, wl, h0, jnp.asarray(msk), jnp.asarray(bdm))
    return (res[0], res[1]) if with_y else (None, res[0])


def _out1_ffn_kernel(yf_ref, yb_ref, bonus_ref, g_ref, dp_ref, x_ref, mod_ref, lw_ref, lb_ref, seg_ref,
                     w_ref, gain_ref, w1_ref, w2_ref, fg_ref, o_ref):
    seg = seg_ref[...]
    cw = yf_ref.shape[2]
    acc = _dot(dp_ref[0].astype(BF16), w_ref[cw:cw + D_WIDTH, :])
    wide = seg.shape[0]
    for pb in range(cw // wide):
        sl = slice(pb * wide, (pb + 1) * wide)
        y = yf_ref[0, :, sl] + yb_ref[0, :, sl]
        mean = _seg_sum(y, seg, 1) * (1.0 / HEAD_DIM)
        dev = y - mean
        var = _seg_sum(dev * dev, seg, 1) * (1.0 / HEAD_DIM)
        yn = dev * lax.rsqrt(var + LNX_EPS) * lw_ref[:, sl] + lb_ref[:, sl]
        z = (yn + bonus_ref[0, :, sl]) * g_ref[0, :, sl]
        acc = acc + _dot(z.astype(BF16), w_ref[sl, :])
    x1 = x_ref[0] + mod_ref[0, 2:3, :] * acc
    o_ref[0] = _ffn_tile(x1, mod_ref, gain_ref, w1_ref, w2_ref, fg_ref, True)


def _out1_ffn(yf, yb, bonus, g, dp, x, mod, lnx_w, lnx_b, seg, w, gain, w1, w2, fgain):
    b, n, d = x.shape
    tm = min(FFN_TILE, n)
    cw = yf.shape[2]
    row = lambda bi, i: (bi, i, 0)
    const = lambda bi, i: (0, 0)
    return pl.pallas_call(
        _out1_ffn_kernel,
        grid=(b, n // tm),
        in_specs=[pl.BlockSpec((1, tm, cw), row)] * 4
        + [pl.BlockSpec((1, tm, D_WIDTH), row),
           pl.BlockSpec((1, tm, d), row),
           pl.BlockSpec((1, 6, d), lambda bi, i: (bi, 0, 0)),
           pl.BlockSpec((1, cw), const), pl.BlockSpec((1, cw), const),
           pl.BlockSpec(seg.shape, const),
           pl.BlockSpec(w.shape, const, pipeline_mode=pl.Buffered(1))]
        + _ffn_specs(d, w1, w2),
        out_specs=pl.BlockSpec((1, tm, d), row),
        out_shape=jax.ShapeDtypeStruct((b, n, d), F32),
        compiler_params=_cparams("parallel", "parallel"),
        name="out1_ffn",
    )(yf, yb, bonus, g, dp, x, mod, lnx_w, lnx_b, seg, w, gain, w1, w2, fgain)


def _rope_tables(n):
    rows = n // GRID_W
    row = jnp.repeat(jnp.arange(rows, dtype=F32), GRID_W)
    col = jnp.tile(jnp.arange(GRID_W, dtype=F32), rows)
    n_freq = HEAD_DIM // 4
    inv = ROPE_THETA ** (-jnp.arange(n_freq, dtype=F32) / n_freq)
    ang = jnp.concatenate([row[:, None] * inv[None, :], col[:, None] * inv[None, :]], axis=-1)
    cos, sin = jnp.cos(ang), jnp.sin(ang)
    cos_t = jnp.tile(cos, (1, LANES // cos.shape[1]))
    sin_t = jnp.tile(jnp.concatenate([-sin, sin], axis=-1), (1, LANES // HEAD_DIM))
    return cos_t, sin_t


def _kv_dup_columns():
    cols = []
    for section in range(4):
        for head in range(2):
            base = AB_Q_COLS + section * 2 * HEAD_DIM + head * HEAD_DIM
            cols += list(range(base, base + HEAD_DIM)) * 2
    return np.concatenate([np.arange(AB_Q_COLS), np.asarray(cols)])


def kernel(x, c, ctx, c_ctx, norm_gain, ada_w, ada_b, ffn_w_in, ffn_w_out, final_gain, ab_w_in, ab_q_gain, ab_k_gain, ab_sink, ab_w_out, cd_w_in, cd_mu, cd_w0, cd_w2, cd_a0, cd_a2, cd_g2, cd_k_k, cd_k_a, cd_r_k, cd_lnx_w, cd_lnx_b, cd_pool_w, cd_pool_scale, cd_w_out):
    b, n, d = x.shape
    nc = ctx.shape[1]
    pad = (-(b + 1)) % 8
    cs = jnp.concatenate([c, c_ctx[None, :], jnp.zeros((pad, d), F32)], axis=0)
    mods = _mods(cs, ada_w, ada_b)
    seg = jnp.asarray(np.kron(np.eye(2), np.ones((HEAD_DIM, HEAD_DIM))).astype(np.float32)).astype(BF16)
    fgain = final_gain.reshape(1, d)

    def layer_mods(i):
        ml = mods[i, :b].reshape(b, 6, d)
        mc = jnp.broadcast_to(mods[i, b].reshape(1, 6, d), (b, 6, d))
        return ml, mc

    ml, mc = layer_mods(0)
    w0 = ab_w_in[0][:, _kv_dup_columns()].astype(BF16)
    gain = norm_gain[0, 0].reshape(1, d)
    qg = jnp.tile(ab_q_gain[0], 2).reshape(1, LANES)
    kg = jnp.tile(ab_k_gain[0], 2).reshape(1, LANES)
    cos_l, sin_l = _rope_tables(n)
    cos_c, sin_c = jnp.ones((nc, LANES), F32), jnp.zeros((nc, LANES), F32)
    q_l, kv_l, vt_l = _proj0(x, ml, gain, w0, cos_l, sin_l, qg, kg, seg)
    q_c, kv_c, vt_c = _proj0(ctx, mc, gain, w0, cos_c, sin_c, qg, kg, seg)
    sink = ab_sink[0]
    w_out0 = ab_w_out[0].astype(BF16)
    gain2 = norm_gain[0, 1].reshape(1, d)
    w1 = ffn_w_in[0].astype(BF16)
    w2 = ffn_w_out[0].astype(BF16)
    yb_l = _attn_b_auto(q_l, kv_c, vt_c, kv_l, vt_l, ab_q_gain[0], ab_k_gain[0])
    xl = _out0_ffn(_attn_a(q_l, kv_c, sink, kv_l), yb_l, x, ml, w_out0, gain2, w1, w2, fgain)
    xc = _out0_ffn(_attn_a(q_c, kv_c, sink), _attn_b(q_c, kv_c), ctx, mc, w_out0, gain2, w1, w2, fgain)

    ml, mc = layer_mods(1)
    gain = norm_gain[1, 0].reshape(1, d)
    w_in1 = cd_w_in[0].astype(BF16)
    pc_l = _proj1(xl, ml, gain, w_in1)
    pc_c = _proj1(xc, mc, gain, w_in1[:, :C_IN])
    zeros = jnp.zeros((DECAY_LORA_PAD, C_WIDTH), F32)
    w2x = jnp.stack([jnp.concatenate([cd_w2[0, 0], zeros]), jnp.concatenate([zeros, cd_w2[0, 1]])])
    a2x = jnp.stack([jnp.concatenate([cd_a2[0, 0], zeros]), jnp.concatenate([zeros, cd_a2[0, 1]])])
    w2x_hi = w2x.astype(BF16)
    w2x_hl = jnp.stack([w2x_hi, (w2x - w2x_hi.astype(F32)).astype(BF16)])
    wts = [jnp.stack([1.0 - cd_mu[0], 0.5 * cd_mu[0]]), cd_w0[0].reshape(2, 1, C_WIDTH), w2x_hl,
           cd_a0[0].reshape(2, 1, C_WIDTH), a2x.astype(BF16), cd_g2[0].astype(BF16),
           cd_k_k[0].reshape(1, C_WIDTH), cd_k_a[0].reshape(1, C_WIDTH),
           cd_r_k[0].reshape(1, C_WIDTH), seg]
    pool_w = jax.scipy.linalg.block_diag(*[cd_pool_w[0, g] for g in range(4)]).astype(BF16)
    pool_wts = [pool_w, cd_pool_scale[0].reshape(1, D_WIDTH)]
    prep_c = _prep(pc_c, wts, False)
    *prep_l, g_l, bonus_l, dp_l = _prep(pc_l, wts + pool_wts, True)
    h_zero = jnp.zeros((b, C_WIDTH // QUAD, CHUNK, QUAD), F32)
    _, h_f = _scanq(prep_c, 0, h_zero, False, False)
    _, h_b = _scanq(prep_c, 1, h_zero, True, False)
    y_f, _ = _scanq(prep_l, 0, h_f, False, True)
    y_b, _ = _scanq(prep_l, 1, h_b, True, True)
    seg4 = jnp.asarray(np.kron(np.eye(4), np.ones((HEAD_DIM, HEAD_DIM))).astype(np.float32)).astype(BF16)
    return _out1_ffn(y_f, y_b, bonus_l, g_l, dp_l, xl, ml, cd_lnx_w[0].reshape(1, C_WIDTH),
                     cd_lnx_b[0].reshape(1, C_WIDTH), seg4, cd_w_out[0].astype(BF16),
                     norm_gain[1, 1].reshape(1, d), ffn_w_in[1].astype(BF16),
                     ffn_w_out[1].astype(BF16), fgain)
```

```python
import functools

import numpy as np
import jax
import jax.numpy as jnp
from jax import lax
from jax.experimental import pallas as pl
from jax.experimental.pallas import tpu as pltpu

F32 = jnp.float32
BF16 = jnp.bfloat16
HIGHEST = lax.Precision.HIGHEST
LOG2_E = 1.4426950408889634
SCORE_BOUND_SAFE = 60.0

D_MODEL = 1024
GRID_W = 64
HEAD_DIM = 64
ROPE_THETA = 10000.0
NORM_EPS = 1e-6
WINDOW = 128
AB_Q_COLS = 1024
C_WIDTH = 768
C_IN = 2688
D_WIDTH = 256
CD_IN = C_IN + D_WIDTH
LNX_EPS = 64e-5
FFN_HIDDEN = 2816
POOL_WINDOWS = (2, 4, 8, 16)
DECAY_LORA_PAD = 64

LANES = 128
ROW_TILE = 256
FFN_TILE = 512
FFN_CHUNKS = 11
HALO = 8
CHUNK = 64
SCAN_BLOCK = 512
SCAN_QUADS = 3
VMEM_LIMIT = 56 * 1024 * 1024


def _cparams(*sem):
    return pltpu.CompilerParams(dimension_semantics=sem, vmem_limit_bytes=VMEM_LIMIT)


def _dot(a, b):
    return jnp.dot(a, b, preferred_element_type=F32)


def _dot32(a, b):
    return jnp.dot(a, b, preferred_element_type=F32, precision=HIGHEST)


def _split(a, terms):
    pieces = []
    for _ in range(terms - 1):
        hi = a.astype(BF16)
        pieces.append(hi)
        a = a - hi.astype(F32)
    pieces.append(a.astype(BF16))
    return pieces


def _seg_sum(z, seg, terms=2):
    out = None
    for piece in _split(z, terms):
        part = _dot(piece, seg)
        out = part if out is None else out + part
    return out


def _sigmoid(x):
    return 1.0 / (1.0 + jnp.exp(-x))


def _norm_mod(x, gain, shift, scale):
    ms = jnp.mean(x * x, axis=-1, keepdims=True)
    return (x * lax.rsqrt(ms + NORM_EPS) * gain) * (1.0 + scale) + shift


def _mods_kernel(c_ref, w_ref, b_ref, o_ref):
    c = c_ref[...]
    o_ref[0] = _dot32(c * _sigmoid(c), w_ref[0]) + b_ref[0]


def _mods(cs, ada_w, ada_b):
    depth, d, n6 = ada_w.shape
    tn = 768
    rows = cs.shape[0]
    return pl.pallas_call(
        _mods_kernel,
        grid=(depth, n6 // tn),
        in_specs=[pl.BlockSpec((rows, d), lambda l, j: (0, 0)),
                  pl.BlockSpec((1, d, tn), lambda l, j: (l, 0, j)),
                  pl.BlockSpec((1, 1, tn), lambda l, j: (l, 0, j))],
        out_specs=pl.BlockSpec((1, rows, tn), lambda l, j: (l, 0, j)),
        out_shape=jax.ShapeDtypeStruct((depth, rows, n6), F32),
        compiler_params=_cparams("arbitrary", "arbitrary"),
        name="mods",
    )(cs, ada_w, ada_b.reshape(depth, 1, n6))


def _proj0_kernel(x_ref, mod_ref, gain_ref, w_ref, cos_ref, sin_ref, qg_ref, kg_ref, seg_ref,
                  q_ref, kv_ref, vt_ref):
    h = _norm_mod(x_ref[0], gain_ref[...], mod_ref[0, 0:1, :], mod_ref[0, 1:2, :])
    p = _dot(h.astype(BF16), w_ref[...])
    tm = p.shape[0]
    cos = cos_ref[...]
    sin = sin_ref[...]
    lane = lax.broadcasted_iota(jnp.int32, (tm, LANES), 1)
    first_half = (lane & (HEAD_DIM - 1)) < HEAD_DIM // 2
    seg = seg_ref[...]

    def rope(z):
        partner = jnp.where(first_half, pltpu.roll(z, LANES - HEAD_DIM // 2, 1),
                            pltpu.roll(z, HEAD_DIM // 2, 1))
        return z * cos + partner * sin

    def head_norm(z, g):
        ms = _seg_sum(z * z, seg, 1) * (1.0 / HEAD_DIM)
        return z * lax.rsqrt(ms + NORM_EPS) * g

    scale = HEAD_DIM ** -0.5 * LOG2_E
    for blk in range(8):
        z = p[:, blk * LANES:(blk + 1) * LANES]
        if blk >= 4:
            z = head_norm(z, qg_ref[...])
        q_ref[0, :, blk * LANES:(blk + 1) * LANES] = (rope(z) * scale).astype(BF16)
    for blk in range(8):
        z = p[:, AB_Q_COLS + blk * LANES:AB_Q_COLS + (blk + 1) * LANES]
        if blk in (4, 5):
            z = head_norm(z, kg_ref[...])
        if blk in (0, 1, 4, 5):
            z = rope(z)
        kv_ref[0, :, blk * LANES:(blk + 1) * LANES] = z.astype(BF16)
        if blk in (6, 7):
            zt = z.T
            top = lax.broadcasted_iota(jnp.int32, zt.shape, 0) < HEAD_DIM
            vt_ref[0, blk - 6] = jnp.where(top, zt, 1.0).astype(BF16)


def _proj0(x, mod, gain, w, cos, sin, qg, kg, seg):
    b, n, d = x.shape
    tm = min(FFN_TILE, n)
    nw = w.shape[1]
    const = lambda bi, i: (0, 0)
    return pl.pallas_call(
        _proj0_kernel,
        grid=(b, n // tm),
        in_specs=[pl.BlockSpec((1, tm, d), lambda bi, i: (bi, i, 0)),
                  pl.BlockSpec((1, 6, d), lambda bi, i: (bi, 0, 0)),
                  pl.BlockSpec((1, d), const),
                  pl.BlockSpec((d, nw), const),
                  pl.BlockSpec((tm, LANES), lambda bi, i: (i, 0)),
                  pl.BlockSpec((tm, LANES), lambda bi, i: (i, 0)),
                  pl.BlockSpec((1, LANES), const),
                  pl.BlockSpec((1, LANES), const),
                  pl.BlockSpec((LANES, LANES), const)],
        out_specs=[pl.BlockSpec((1, tm, 1024), lambda bi, i: (bi, i, 0)),
                   pl.BlockSpec((1, tm, 1024), lambda bi, i: (bi, i, 0)),
                   pl.BlockSpec((1, 2, LANES, tm), lambda bi, i: (bi, 0, 0, i))],
        out_shape=[jax.ShapeDtypeStruct((b, n, 1024), BF16),
                   jax.ShapeDtypeStruct((b, n, 1024), BF16),
                   jax.ShapeDtypeStruct((b, 2, LANES, n), BF16)],
        compiler_params=_cparams("parallel", "parallel"),
        name="proj0",
    )(x, mod, gain, w, cos, sin, qg, kg, seg)


def _masked_q(q_ref, h, low):
    q2 = q_ref[0, :, (h // 2) * LANES:(h // 2 + 1) * LANES]
    keep = low if h % 2 == 0 else jnp.logical_not(low)
    return jnp.where(keep, q2, jnp.zeros_like(q2))


def _attn_b_kernel(q_ref, kc_ref, vc_ref, *rest, n_kt, has_latent):
    if has_latent:
        k_ref, v_ref, o_ref, m_sc, acc_sc = rest
    else:
        o_ref, m_sc, acc_sc = rest
    kt = pl.program_id(3)
    tq = q_ref.shape[1]
    low = lax.broadcasted_iota(jnp.int32, (tq, LANES), 1) < HEAD_DIM

    def update(k, v):
        tk = k.shape[0]
        v1 = jnp.where(lax.broadcasted_iota(jnp.int32, (tk, LANES), 1) < HEAD_DIM, v, jnp.ones_like(v))
        ss = [lax.dot_general(_masked_q(q_ref, h, low), k, (((1,), (1,)), ((), ())),
                              preferred_element_type=F32) for h in range(4)]
        for h in range(4):
            m_prev = m_sc[h]
            m_new = jnp.maximum(m_prev, jnp.max(ss[h], axis=1, keepdims=True))
            p = jnp.exp2(ss[h] - jnp.concatenate([m_new] * (tk // LANES), axis=1)).astype(BF16)
            acc_sc[h] = jnp.exp2(m_prev - m_new) * acc_sc[h] + _dot(p, v1)
            m_sc[h] = m_new

    @pl.when(kt == 0)
    def _():
        m_sc[...] = jnp.full(m_sc.shape, -jnp.inf, F32)
        acc_sc[...] = jnp.zeros(acc_sc.shape, F32)
        update(kc_ref[0], vc_ref[0])

    if has_latent:
        update(k_ref[0], v_ref[0])

    @pl.when(kt == n_kt - 1)
    def _():
        for pr in range(2):
            a0 = acc_sc[2 * pr]
            a1 = acc_sc[2 * pr + 1]
            o0 = a0 / pltpu.roll(a0, HEAD_DIM, 1)
            o1 = pltpu.roll(a1, HEAD_DIM, 1) / a1
            o_ref[0, :, pr * LANES:(pr + 1) * LANES] = jnp.where(low, o0, o1).astype(BF16)


def _attn_b(q, kv_c, kv=None, tk=1024):
    b, n, _ = q.shape
    nc = kv_c.shape[1]
    tq = ROW_TILE
    has_latent = kv is not None
    n_kt = kv.shape[1] // tk if has_latent else 1
    in_specs = [pl.BlockSpec((1, tq, 2 * LANES), lambda bi, g, i, j: (bi, i, 2 + g)),
                pl.BlockSpec((1, nc, LANES), lambda bi, g, i, j: (bi, 0, 4 + g)),
                pl.BlockSpec((1, nc, LANES), lambda bi, g, i, j: (bi, 0, 6 + g))]
    args = [q, kv_c, kv_c]
    if has_latent:
        in_specs += [pl.BlockSpec((1, tk, LANES), lambda bi, g, i, j: (bi, j, 4 + g)),
                     pl.BlockSpec((1, tk, LANES), lambda bi, g, i, j: (bi, j, 6 + g))]
        args += [kv, kv]
    return pl.pallas_call(
        functools.partial(_attn_b_kernel, n_kt=n_kt, has_latent=has_latent),
        grid=(b, 2, n // tq, n_kt),
        in_specs=in_specs,
        out_specs=pl.BlockSpec((1, tq, 2 * LANES), lambda bi, g, i, j: (bi, i, g)),
        out_shape=jax.ShapeDtypeStruct((b, n, 512), BF16),
        scratch_shapes=[pltpu.VMEM((4, tq, LANES), F32), pltpu.VMEM((4, tq, LANES), F32)],
        compiler_params=_cparams("parallel", "parallel", "parallel", "arbitrary"),
        name="attn_b",
    )(*args)


def _attn_bt_kernel(q_ref, kc_ref, vtc_ref, k_ref, vt_ref, o_ref, acc_sc, *, n_kt):
    kt = pl.program_id(3)
    tq = q_ref.shape[1]
    low = lax.broadcasted_iota(jnp.int32, (tq, LANES), 1) < HEAD_DIM

    def update(k, vt):
        sts = [lax.dot_general(k, _masked_q(q_ref, h, low), (((1,), (1,)), ((), ())),
                               preferred_element_type=F32) for h in range(4)]
        for h in range(4):
            acc_sc[h] = acc_sc[h] + _dot(vt, jnp.exp2(sts[h]).astype(BF16))

    @pl.when(kt == 0)
    def _():
        acc_sc[...] = jnp.zeros(acc_sc.shape, F32)
        update(kc_ref[0], vtc_ref[0, 0])

    update(k_ref[0], vt_ref[0, 0])

    @pl.when(kt == n_kt - 1)
    def _():
        ot = jnp.concatenate([acc_sc[h, 0:HEAD_DIM, :] / acc_sc[h, HEAD_DIM:2 * HEAD_DIM, :]
                              for h in range(4)], axis=0)
        o_ref[0] = ot.T.astype(BF16)


def _attn_bt(q, kv_c, vt_c, kv, vt, tk=4096):
    b, n, _ = q.shape
    nc = kv_c.shape[1]
    tq = ROW_TILE
    tk = min(tk, kv.shape[1])
    n_kt = kv.shape[1] // tk
    return pl.pallas_call(
        functools.partial(_attn_bt_kernel, n_kt=n_kt),
        grid=(b, 2, n // tq, n_kt),
        in_specs=[pl.BlockSpec((1, tq, 2 * LANES), lambda bi, g, i, j: (bi, i, 2 + g)),
                  pl.BlockSpec((1, nc, LANES), lambda bi, g, i, j: (bi, 0, 4 + g)),
                  pl.BlockSpec((1, 1, LANES, nc), lambda bi, g, i, j: (bi, g, 0, 0)),
                  pl.BlockSpec((1, tk, LANES), lambda bi, g, i, j: (bi, j, 4 + g)),
                  pl.BlockSpec((1, 1, LANES, tk), lambda bi, g, i, j: (bi, g, 0, j))],
        out_specs=pl.BlockSpec((1, tq, 2 * LANES), lambda bi, g, i, j: (bi, i, g)),
        out_shape=jax.ShapeDtypeStruct((b, n, 512), BF16),
        scratch_shapes=[pltpu.VMEM((4, LANES, tq), F32)],
        compiler_params=_cparams("parallel", "parallel", "parallel", "arbitrary"),
        name="attn_b_bounded",
    )(q, kv_c, vt_c, kv, vt)


def _attn_b_auto(q, kv_c, vt_c, kv, vt, q_gain, k_gain):
    bound = (1.02 * HEAD_DIM * HEAD_DIM ** -0.5 * LOG2_E) * jnp.max(jnp.abs(q_gain)) * jnp.max(jnp.abs(k_gain))
    return lax.cond(bound <= SCORE_BOUND_SAFE,
                    lambda: _attn_bt(q, kv_c, vt_c, kv, vt),
                    lambda: _attn_b(q, kv_c, kv))


def _attn_a_kernel(sink_ref, q_ref, kc_ref, vc_ref, *rest, has_local):
    if has_local:
        kp_ref, kcur_ref, kn_ref, vp_ref, vcur_ref, vn_ref, o_ref, kbuf, vbuf = rest
    else:
        o_ref, kbuf, vbuf = rest
    g = pl.program_id(1)
    i = pl.program_id(2)
    n_i = pl.num_programs(2)
    tq = q_ref.shape[1]
    nc = kc_ref.shape[1]
    kbuf[0:nc, :] = kc_ref[0]
    vbuf[0:nc, :] = vc_ref[0]
    nk = nc
    if has_local:
        kbuf[nc:nc + WINDOW, :] = kp_ref[0]
        kbuf[nc + WINDOW:nc + WINDOW + tq, :] = kcur_ref[0]
        kbuf[nc + WINDOW + tq:nc + 2 * WINDOW + tq, :] = kn_ref[0]
        vbuf[nc:nc + WINDOW, :] = vp_ref[0]
        vbuf[nc + WINDOW:nc + WINDOW + tq, :] = vcur_ref[0]
        vbuf[nc + WINDOW + tq:nc + 2 * WINDOW + tq, :] = vn_ref[0]
        nk = nc + 2 * WINDOW + tq
        row = lax.broadcasted_iota(jnp.int32, (tq, nk), 0)
        col = lax.broadcasted_iota(jnp.int32, (tq, nk), 1) - nc
        lo_c = jnp.where(i == 0, WINDOW, 0)
        hi_c = jnp.where(i == n_i - 1, WINDOW + tq, 2 * WINDOW + tq)
        valid = (col < 0) | ((col >= row) & (col <= row + 2 * WINDOW) & (col >= lo_c) & (col < hi_c))
    k = kbuf[...]
    v = vbuf[...]
    v1 = jnp.where(lax.broadcasted_iota(jnp.int32, (nk, LANES), 1) < HEAD_DIM, v, jnp.ones_like(v))
    low = lax.broadcasted_iota(jnp.int32, (tq, LANES), 1) < HEAD_DIM
    ss = [lax.dot_general(_masked_q(q_ref, h, low), k, (((1,), (1,)), ((), ())),
                          preferred_element_type=F32) for h in range(4)]
    accs = []
    for h in range(4):
        s = jnp.where(valid, ss[h], -jnp.inf) if has_local else ss[h]
        sink = sink_ref[g * 4 + h] * LOG2_E
        m = jnp.maximum(jnp.max(s, axis=1, keepdims=True), sink)
        p = jnp.exp2(s - m)
        sink_p = jnp.where(low, 0.0, jnp.exp2(sink - m))
        accs.append(_dot(p.astype(BF16), v1) + sink_p)
    for pr in range(2):
        a0, a1 = accs[2 * pr], accs[2 * pr + 1]
        o0 = a0 / pltpu.roll(a0, HEAD_DIM, 1)
        o1 = pltpu.roll(a1, HEAD_DIM, 1) / a1
        o_ref[0, :, pr * LANES:(pr + 1) * LANES] = jnp.where(low, o0, o1).astype(BF16)


def _attn_a(q, kv_c, sink, kv=None):
    b, n, _ = q.shape
    nc = kv_c.shape[1]
    tq = ROW_TILE
    has_local = kv is not None
    in_specs = [pl.BlockSpec(memory_space=pltpu.SMEM),
                pl.BlockSpec((1, tq, 2 * LANES), lambda bi, g, i: (bi, i, g)),
                pl.BlockSpec((1, nc, LANES), lambda bi, g, i: (bi, 0, g)),
                pl.BlockSpec((1, nc, LANES), lambda bi, g, i: (bi, 0, 2 + g))]
    args = [sink, q, kv_c, kv_c]
    nk = nc
    if has_local:
        per = tq // WINDOW
        last = n // WINDOW - 1
        prev_i = lambda i: jnp.maximum(i * per - 1, 0)
        next_i = lambda i: jnp.minimum((i + 1) * per, last)
        for off in (0, 2):
            in_specs += [pl.BlockSpec((1, WINDOW, LANES), lambda bi, g, i, off=off: (bi, prev_i(i), off + g)),
                         pl.BlockSpec((1, tq, LANES), lambda bi, g, i, off=off: (bi, i, off + g)),
                         pl.BlockSpec((1, WINDOW, LANES), lambda bi, g, i, off=off: (bi, next_i(i), off + g))]
            args += [kv, kv, kv]
        nk = nc + 2 * WINDOW + tq
    return pl.pallas_call(
        functools.partial(_attn_a_kernel, has_local=has_local),
        grid=(b, 2, n // tq),
        in_specs=in_specs,
        out_specs=pl.BlockSpec((1, tq, 2 * LANES), lambda bi, g, i: (bi, i, g)),
        out_shape=jax.ShapeDtypeStruct((b, n, 512), BF16),
        scratch_shapes=[pltpu.VMEM((nk, LANES), BF16), pltpu.VMEM((nk, LANES), BF16)],
        compiler_params=_cparams("parallel", "parallel", "parallel"),
        name="attn_a",
    )(*args)


def _ffn_tile(x, mod_ref, gain_ref, w1_ref, w2_ref, fg_ref, final):
    h = _norm_mod(x, gain_ref[...], mod_ref[0, 3:4, :], mod_ref[0, 4:5, :]).astype(BF16)
    hid = w2_ref.shape[0]
    ch = hid // FFN_CHUNKS
    acc = None
    for j in range(FFN_CHUNKS):
        gate = _dot(h, w1_ref[:, j * ch:(j + 1) * ch])
        up = _dot(h, w1_ref[:, hid + j * ch:hid + (j + 1) * ch])
        act = (gate * _sigmoid(gate) * up).astype(BF16)
        part = _dot(act, w2_ref[j * ch:(j + 1) * ch, :])
        acc = part if acc is None else acc + part
    y = x + mod_ref[0, 5:6, :] * acc
    if final:
        ms = jnp.mean(y * y, axis=-1, keepdims=True)
        y = y * lax.rsqrt(ms + NORM_EPS) * fg_ref[...]
    return y


def _out0_ffn_kernel(ya_ref, yb_ref, x_ref, mod_ref, wo_ref, gain_ref, w1_ref, w2_ref, fg_ref, o_ref):
    half = ya_ref.shape[2]
    y = _dot(ya_ref[0], wo_ref[0:half, :]) + _dot(yb_ref[0], wo_ref[half:2 * half, :])
    x1 = x_ref[0] + mod_ref[0, 2:3, :] * y
    o_ref[0] = _ffn_tile(x1, mod_ref, gain_ref, w1_ref, w2_ref, fg_ref, False)


def _ffn_specs(d, w1, w2):
    const = lambda bi, i: (0, 0)
    return [pl.BlockSpec((1, d), const),
            pl.BlockSpec(w1.shape, const, pipeline_mode=pl.Buffered(1)),
            pl.BlockSpec(w2.shape, const, pipeline_mode=pl.Buffered(1)),
            pl.BlockSpec((1, d), const)]


def _out0_ffn(ya, yb, x, mod, wo, gain, w1, w2, fgain):
    b, n, d = x.shape
    tm = min(FFN_TILE, n)
    row = lambda bi, i: (bi, i, 0)
    return pl.pallas_call(
        _out0_ffn_kernel,
        grid=(b, n // tm),
        in_specs=[pl.BlockSpec((1, tm, ya.shape[2]), row),
                  pl.BlockSpec((1, tm, yb.shape[2]), row),
                  pl.BlockSpec((1, tm, d), row),
                  pl.BlockSpec((1, 6, d), lambda bi, i: (bi, 0, 0)),
                  pl.BlockSpec(wo.shape, lambda bi, i: (0, 0), pipeline_mode=pl.Buffered(1))]
        + _ffn_specs(d, w1, w2),
        out_specs=pl.BlockSpec((1, tm, d), row),
        out_shape=jax.ShapeDtypeStruct((b, n, d), F32),
        compiler_params=_cparams("parallel", "parallel"),
        name="out0_ffn",
    )(ya, yb, x, mod, wo, gain, w1, w2, fgain)


def _proj1_kernel(x_ref, mod_ref, gain_ref, w_ref, o_ref):
    h = _norm_mod(x_ref[0], gain_ref[...], mod_ref[0, 0:1, :], mod_ref[0, 1:2, :])
    o_ref[0] = _dot(h.astype(BF16), w_ref[...])


def _proj1(x, mod, gain, w):
    b, n, d = x.shape
    tm = min(FFN_TILE, n)
    nw = w.shape[1]
    return pl.pallas_call(
        _proj1_kernel,
        grid=(b, n // tm),
        in_specs=[pl.BlockSpec((1, tm, d), lambda bi, i: (bi, i, 0)),
                  pl.BlockSpec((1, 6, d), lambda bi, i: (bi, 0, 0)),
                  pl.BlockSpec((1, d), lambda bi, i: (0, 0)),
                  pl.BlockSpec((d, nw), lambda bi, i: (0, 0))],
        out_specs=pl.BlockSpec((1, tm, nw), lambda bi, i: (bi, i, 0)),
        out_shape=jax.ShapeDtypeStruct((b, n, nw), F32),
        compiler_params=_cparams("parallel", "parallel"),
        name="proj1",
    )(x, mod, gain, w)


def _prep_kernel(pc_ref, prev_ref, next_ref, mu_ref, w0_ref, w2_ref, a0_ref, a2_ref, g2_ref,
                 kk_ref, ka_ref, rk_ref, seg_ref, tri_ref, *rest, latent):
    if latent:
        pw_ref, ps_ref, v_o, at_o, rt_o, bt_o, kt_o, wl_o, g_o, bonus_o, d_o, ext, tmp = rest
    else:
        v_o, at_o, rt_o, bt_o, kt_o, wl_o, ext, tmp = rest
    i = pl.program_id(1)
    nt = pl.num_programs(1)
    tm = pc_ref.shape[1]
    cw = C_WIDTH
    ext[HALO:HALO + tm, :] = pc_ref[0]
    ext[0:HALO, :] = jnp.where(i > 0, prev_ref[0], 0.0)
    ext[HALO + tm:2 * HALO + tm, :] = jnp.where(i < nt - 1, next_ref[0], 0.0)

    def mixed(lo, hi):
        nb = ext[HALO - 1:HALO - 1 + tm, lo:hi] + ext[HALO + 1:HALO + 1 + tm, lo:hi]
        return ext[HALO:HALO + tm, lo:hi] * mu_ref[0:1, lo:hi] + nb * mu_ref[1:2, lo:hi]

    lora = mixed(3 * cw, C_IN)
    tw = jnp.tanh(lora[:, 0:LANES])
    xa = lora[:, LANES:2 * LANES]
    if latent:
        g_o[0] = _dot(_sigmoid(lora[:, 2 * LANES:3 * LANES]).astype(BF16), g2_ref[...])
    tw_hi, tw_lo = _split(tw, 2)
    xa_b = xa.astype(BF16)
    for d in range(2):
        z = w0_ref[d] + _dot(tw_hi, w2_ref[0, d]) + _dot(tw_lo, w2_ref[0, d]) + _dot(tw_hi, w2_ref[1, d])
        w_log = -(jnp.maximum(-z, 0.0) + jnp.log(1.0 + jnp.exp(-jnp.abs(z)))) - 0.5
        tmp[d] = -LOG2_E * jnp.exp(w_log)
        tmp[2 + d] = _sigmoid(a0_ref[d] + _dot(xa_b, a2_ref[d]))
    seg = seg_ref[...]
    n_chunks = tm // CHUNK
    for pb in range(cw // LANES):
        sl = slice(pb * LANES, (pb + 1) * LANES)
        r = mixed(pb * LANES, (pb + 1) * LANES)
        k = mixed(cw + pb * LANES, cw + (pb + 1) * LANES)
        v = mixed(2 * cw + pb * LANES, 2 * cw + (pb + 1) * LANES)
        kk = k * kk_ref[:, sl]
        kk = kk / jnp.maximum(jnp.sqrt(_seg_sum(kk * kk, seg, 1)), 1e-12)
        neg_kk = -kk
        kka = k * ka_ref[:, sl]
        ksum = jnp.zeros_like(k)
        for d in range(2):
            lw = tmp[d, :, sl]
            a = tmp[2 + d, :, sl]
            k_d = k + kka * (a - 1.0)
            bb = kk * a
            ksum = ksum + k_d
            cs2 = _dot(tri_ref[d], jnp.concatenate(_split(lw, 2), axis=1))
            cs = cs2[:, 0:LANES] + cs2[:, LANES:2 * LANES]
            for j in range(n_chunks):
                last = j * CHUNK if d == 1 else (j + 1) * CHUNK - 1
                wl_o[d, 0, j, :, sl] = jnp.exp2(cs[last:last + 1, :])
            e_up = jnp.exp2(-cs)
            at_o[d, 0, :, sl] = (neg_kk * jnp.exp2(cs - lw)).astype(BF16)
            rt_o[d, 0, :, sl] = (r * jnp.exp2(cs)).astype(BF16)
            bt_o[d, 0, :, sl] = (bb * e_up).astype(BF16)
            kt_o[d, 0, :, sl] = (k_d * e_up).astype(BF16)
        v_o[0, :, sl] = v.astype(BF16)
        if latent:
            bonus_o[0, :, sl] = _seg_sum(r * rk_ref[:, sl] * ksum, seg) * v
    if latent:
        n_tok = nt * tm
        pos = i * tm + lax.broadcasted_iota(jnp.int32, (tm, LANES), 0)
        group1 = lax.broadcasted_iota(jnp.int32, (tm, LANES), 1) >= D_WIDTH // 4
        pooled = []
        for half in range(2):
            lo, hi = C_IN + half * LANES, C_IN + (half + 1) * LANES
            w_small, w_big = POOL_WINDOWS[2 * half], POOL_WINDOWS[2 * half + 1]
            s_small = None
            s_big = None
            for off in range(-(w_big // 2), w_big - w_big // 2):
                piece = ext[HALO + off:HALO + off + tm, lo:hi]
                s_big = piece if s_big is None else s_big + piece
                if -(w_small // 2) <= off < w_small - w_small // 2:
                    s_small = piece if s_small is None else s_small + piece

            def count(w):
                lo_p = jnp.clip(pos - w // 2, 0, n_tok)
                hi_p = jnp.clip(pos + (w - w // 2), 0, n_tok)
                return (hi_p - lo_p).astype(F32)

            mean = jnp.where(group1, s_big / count(w_big), s_small / count(w_small))
            pooled.append((mean - ext[HALO:HALO + tm, lo:hi]).astype(BF16))
        pm = jnp.concatenate(pooled, axis=1)
        d_o[0] = _dot(pm, pw_ref[...]) * ps_ref[...]


def _chunk_cumsum_matrices(tm):
    t = np.arange(tm)[:, None]
    u = np.arange(tm)[None, :]
    same = (t // CHUNK) == (u // CHUNK)
    return np.stack([same & (u <= t), same & (u >= t)]).astype(np.float32)


def _prep(pc, wts, latent):
    b, n, cin = pc.shape
    tm = ROW_TILE
    cw = C_WIDTH
    per = tm // HALO
    last = n // HALO - 1
    n_chunks = tm // CHUNK
    row = lambda bi, i: (bi, i, 0)
    drow = lambda bi, i: (0, bi, i, 0)
    wts = list(wts)
    wts = wts[:10] + [jnp.asarray(_chunk_cumsum_matrices(tm)).astype(BF16)] + (wts[10:] if latent else [])

    def full(a):
        return pl.BlockSpec(a.shape, lambda bi, i, nd=a.ndim: (0,) * nd)

    in_specs = [pl.BlockSpec((1, tm, cin), row),
                pl.BlockSpec((1, HALO, cin), lambda bi, i: (bi, jnp.maximum(i * per - 1, 0), 0)),
                pl.BlockSpec((1, HALO, cin), lambda bi, i: (bi, jnp.minimum((i + 1) * per, last), 0))]
    in_specs += [full(a) for a in wts]
    out_specs = [pl.BlockSpec((1, tm, cw), row)] + [pl.BlockSpec((2, 1, tm, cw), drow)] * 4
    out_shape = [jax.ShapeDtypeStruct((b, n, cw), BF16)] + [jax.ShapeDtypeStruct((2, b, n, cw), BF16)] * 4
    out_specs.append(pl.BlockSpec((2, 1, n_chunks, 1, cw), lambda bi, i: (0, bi, i, 0, 0)))
    out_shape.append(jax.ShapeDtypeStruct((2, b, n // CHUNK, 1, cw), F32))
    if latent:
        out_specs += [pl.BlockSpec((1, tm, cw), row)] * 2 + [pl.BlockSpec((1, tm, D_WIDTH), row)]
        out_shape += [jax.ShapeDtypeStruct((b, n, cw), F32)] * 2 + [jax.ShapeDtypeStruct((b, n, D_WIDTH), F32)]
    return pl.pallas_call(
        functools.partial(_prep_kernel, latent=latent),
        grid=(b, n // tm),
        in_specs=in_specs,
        out_specs=out_specs,
        out_shape=out_shape,
        scratch_shapes=[pltpu.VMEM((tm + 2 * HALO, cin), F32), pltpu.VMEM((4, tm, cw), F32)],
        compiler_params=_cparams("parallel", "parallel"),
        name="prep",
    )(pc, pc, pc, *wts)


QUAD = 4 * HEAD_DIM


def _quad_masks(rev):
    t = np.arange(CHUNK)[:, None]
    u = np.arange(CHUNK)[None, :]
    before = (u > t) if rev else (u < t)
    masks = [before, before | (u == t), u == t]
    s = 1
    while s < CHUNK:
        blk = (t // (2 * s)) == (u // (2 * s))
        t_late = (t % (2 * s) < s) if rev else (t % (2 * s) >= s)
        u_early = (u % (2 * s) >= s) if rev else (u % (2 * s) < s)
        masks.append(blk & t_late & u_early)
        s *= 2
    masks = np.stack([np.tile(m, (1, QUAD // CHUNK)) for m in masks]).astype(np.float32)
    hid = np.arange(QUAD) // HEAD_DIM
    return masks, (hid[:, None] == hid[None, :]).astype(np.float32)


def _scanq_kernel(v_ref, at_ref, rt_ref, bt_ref, kt_ref, wl_ref, h0_ref, msk_ref, bd_ref, *rest,
                  rev, with_y):
    if with_y:
        y_ref, hT_ref, g_sc = rest
    else:
        hT_ref, g_sc = rest
    tb = pl.program_id(2)
    n_tb = pl.num_programs(2)
    n_chunks = v_ref.shape[1] // CHUNK
    n_quads = v_ref.shape[2] // QUAD
    reps = QUAD // CHUNK

    @pl.when(tb == 0)
    def _():
        g_sc[...] = h0_ref[0]

    bd_mask = bd_ref[...] > 0
    strict = msk_ref[0] > 0
    incl = msk_ref[1] > 0
    eye4 = msk_ref[2]
    n_levels = msk_ref.shape[0] - 3
    nt_dims = (((1,), (1,)), ((), ()))
    tn_dims = (((0,), (0,)), ((), ()))

    def bd(x4):
        return jnp.where(bd_mask, jnp.concatenate([x4] * reps, axis=0), jnp.zeros((QUAD, QUAD), BF16))

    def fold(m):
        m = jnp.where(bd_mask, m, 0.0)
        out = m[0:CHUNK]
        for r in range(1, reps):
            out = out + m[r * CHUNK:(r + 1) * CHUNK]
        return out

    order = list(range(n_chunks - 1, -1, -1) if rev else range(n_chunks))
    chains = [(qq, c) for c in order for qq in range(n_quads)]
    ch = {}
    for key in chains:
        qq, c = key
        rows = slice(c * CHUNK, (c + 1) * CHUNK)
        lanes = slice(qq * QUAD, (qq + 1) * QUAD)
        q = ch[key] = {"rows": rows, "lanes": lanes}
        q["v"] = v_ref[0, rows, lanes]
        q["a"] = at_ref[0, 0, rows, lanes]
        q["b"] = bt_ref[0, 0, rows, lanes]
        q["k"] = kt_ref[0, 0, rows, lanes]
        q["bdv"] = bd(q["v"])
        rhs = jnp.concatenate([bd(q["b"]), bd(q["k"])], axis=0)
        if with_y:
            q["r"] = rt_ref[0, 0, rows, lanes]
            lhs = jnp.concatenate([q["a"], q["r"]], axis=0)
        else:
            lhs = q["a"]
        gram = lax.dot_general(lhs, rhs, nt_dims, preferred_element_type=F32)
        q["ab"] = jnp.where(strict, gram[0:CHUNK, 0:QUAD], 0.0)
        ak = jnp.where(strict, gram[0:CHUNK, QUAD:2 * QUAD], 0.0).astype(BF16)
        if with_y:
            q["rb"] = jnp.where(incl, gram[CHUNK:2 * CHUNK, 0:QUAD], 0.0).astype(BF16)
            rk = jnp.where(incl, gram[CHUNK:2 * CHUNK, QUAD:2 * QUAD], 0.0).astype(BF16)
            kv = _dot(jnp.concatenate([ak, rk], axis=0), q["bdv"])
            q["akv"], q["rkv"] = kv[0:CHUNK], kv[CHUNK:2 * CHUNK]
        else:
            q["akv"] = _dot(ak, q["bdv"])
        q["t"] = eye4 + q["ab"] * msk_ref[3]
    for lvl in range(1, n_levels):
        for key in chains:
            q = ch[key]
            q["x"] = _dot((q["ab"] * msk_ref[3 + lvl]).astype(BF16), bd(q["t"].astype(BF16)))
        for key in chains:
            q = ch[key]
            q["t"] = q["t"] + _dot(q["t"].astype(BF16), bd(q["x"].astype(BF16)))
    for key in chains:
        q = ch[key]
        au = _dot(q["t"].astype(BF16),
                  jnp.concatenate([bd(q["a"]), bd(q["akv"].astype(BF16))], axis=1))
        au_b = au.astype(BF16)
        wl = wl_ref[0, 0, key[1], :, q["lanes"]]
        mc = lax.dot_general(au_b, q["b"], tn_dims, preferred_element_type=F32)
        vk = lax.dot_general(q["v"], q["k"], tn_dims, preferred_element_type=F32)
        q["mt"] = jnp.where(bd_mask, mc[0:QUAD] * wl, 0.0).astype(BF16)
        q["ct"] = fold(mc[QUAD:2 * QUAD] + vk) * wl
        if with_y:
            rbau = _dot(q["rb"], jnp.concatenate([bd(au_b[:, 0:QUAD]), bd(au_b[:, QUAD:2 * QUAD])], axis=1))
            q["rt"] = (q["r"].astype(F32) + rbau[:, 0:QUAD]).astype(BF16)
            q["y0"] = rbau[:, QUAD:2 * QUAD] + q["rkv"]
    g = [g_sc[qq] for qq in range(n_quads)]
    for key in chains:
        qq, c = key
        q = ch[key]
        g_b = g[qq].astype(BF16)
        if with_y:
            y_ref[0, q["rows"], q["lanes"]] = (
                lax.dot_general(q["rt"], bd(g_b), nt_dims, preferred_element_type=F32) + q["y0"])
        g[qq] = g[qq] * wl_ref[0, 0, c, :, q["lanes"]] + _dot(g_b, q["mt"]) + q["ct"]
    for qq in range(n_quads):
        g_sc[qq] = g[qq]

    @pl.when(tb == n_tb - 1)
    def _():
        hT_ref[0] = g_sc[...]


def _scanq(prep, d, h0, rev, with_y):
    v, at, rt, bt, kt, wl = prep
    b, n, cw = v.shape
    tb = min(SCAN_BLOCK, n)
    n_tb = n // tb
    n_quads = cw // QUAD
    sq = SCAN_QUADS
    wide = sq * QUAD
    msk, bdm = _quad_masks(rev)
    tmap = (lambda t: n_tb - 1 - t) if rev else (lambda t: t)
    shared = pl.BlockSpec((1, tb, wide), lambda bi, p, t: (bi, tmap(t), p))
    perdir = pl.BlockSpec((1, 1, tb, wide), lambda bi, p, t: (d, bi, tmap(t), p))
    decay = pl.BlockSpec((1, 1, tb // CHUNK, 1, wide), lambda bi, p, t: (d, bi, tmap(t), 0, p))
    state = pl.BlockSpec((1, sq, CHUNK, QUAD), lambda bi, p, t: (bi, p, 0, 0))
    out_specs = [state]
    out_shape = [jax.ShapeDtypeStruct((b, n_quads, CHUNK, QUAD), F32)]
    if with_y:
        out_specs = [shared] + out_specs
        out_shape = [jax.ShapeDtypeStruct((b, n, cw), F32)] + out_shape
    res = pl.pallas_call(
        functools.partial(_scanq_kernel, rev=rev, with_y=with_y),
        grid=(b, n_quads // sq, n_tb),
        in_specs=[shared, perdir, perdir, perdir, perdir, decay, state,
                  pl.BlockSpec(msk.shape, lambda bi, p, t: (0, 0, 0)),
                  pl.BlockSpec(bdm.shape, lambda bi, p, t: (0, 0))],
        out_specs=out_specs,
        out_shape=out_shape,
        scratch_shapes=[pltpu.VMEM((sq, CHUNK, QUAD), F32)],
        compiler_params=_cparams("parallel", "parallel", "arbitrary"),
        name="scan_rev" if rev else "scan_fwd",
    )(v, at, rt, bt, kt, wl, h0, jnp.asarray(msk), jnp.asarray(bdm))
    return (res[0], res[1]) if with_y else (None, res[0])


def _out1_ffn_kernel(yf_ref, yb_ref, bonus_ref, g_ref, dp_ref, x_ref, mod_ref, lw_ref, lb_ref, seg_ref,
                     w_ref, gain_ref, w1_ref, w2_ref, fg_ref, o_ref):
    seg = seg_ref[...]
    cw = yf_ref.shape[2]
    acc = _dot(dp_ref[0].astype(BF16), w_ref[cw:cw + D_WIDTH, :])
    wide = seg.shape[0]
    for pb in range(cw // wide):
        sl = slice(pb * wide, (pb + 1) * wide)
        y = yf_ref[0, :, sl] + yb_ref[0, :, sl]
        mean = _seg_sum(y, seg, 1) * (1.0 / HEAD_DIM)
        dev = y - mean
        var = _seg_sum(dev * dev, seg, 1) * (1.0 / HEAD_DIM)
        yn = dev * lax.rsqrt(var + LNX_EPS) * lw_ref[:, sl] + lb_ref[:, sl]
        z = (yn + bonus_ref[0, :, sl]) * g_ref[0, :, sl]
        acc = acc + _dot(z.astype(BF16), w_ref[sl, :])
    x1 = x_ref[0] + mod_ref[0, 2:3, :] * acc
    o_ref[0] = _ffn_tile(x1, mod_ref, gain_ref, w1_ref, w2_ref, fg_ref, True)


def _out1_ffn(yf, yb, bonus, g, dp, x, mod, lnx_w, lnx_b, seg, w, gain, w1, w2, fgain):
    b, n, d = x.shape
    tm = min(FFN_TILE, n)
    cw = yf.shape[2]
    row = lambda bi, i: (bi, i, 0)
    const = lambda bi, i: (0, 0)
    return pl.pallas_call(
        _out1_ffn_kernel,
        grid=(b, n // tm),
        in_specs=[pl.BlockSpec((1, tm, cw), row)] * 4
        + [pl.BlockSpec((1, tm, D_WIDTH), row),
           pl.BlockSpec((1, tm, d), row),
           pl.BlockSpec((1, 6, d), lambda bi, i: (bi, 0, 0)),
           pl.BlockSpec((1, cw), const), pl.BlockSpec((1, cw), const),
           pl.BlockSpec(seg.shape, const),
           pl.BlockSpec(w.shape, const, pipeline_mode=pl.Buffered(1))]
        + _ffn_specs(d, w1, w2),
        out_specs=pl.BlockSpec((1, tm, d), row),
        out_shape=jax.ShapeDtypeStruct((b, n, d), F32),
        compiler_params=_cparams("parallel", "parallel"),
        name="out1_ffn",
    )(yf, yb, bonus, g, dp, x, mod, lnx_w, lnx_b, seg, w, gain, w1, w2, fgain)


def _rope_tables(n):
    rows = n // GRID_W
    row = jnp.repeat(jnp.arange(rows, dtype=F32), GRID_W)
    col = jnp.tile(jnp.arange(GRID_W, dtype=F32), rows)
    n_freq = HEAD_DIM // 4
    inv = ROPE_THETA ** (-jnp.arange(n_freq, dtype=F32) / n_freq)
    ang = jnp.concatenate([row[:, None] * inv[None, :], col[:, None] * inv[None, :]], axis=-1)
    cos, sin = jnp.cos(ang), jnp.sin(ang)
    cos_t = jnp.tile(cos, (1, LANES // cos.shape[1]))
    sin_t = jnp.tile(jnp.concatenate([-sin, sin], axis=-1), (1, LANES // HEAD_DIM))
    return cos_t, sin_t


def _kv_dup_columns():
    cols = []
    for section in range(4):
        for head in range(2):
            base = AB_Q_COLS + section * 2 * HEAD_DIM + head * HEAD_DIM
            cols += list(range(base, base + HEAD_DIM)) * 2
    return np.concatenate([np.arange(AB_Q_COLS), np.asarray(cols)])


def kernel(x, c, ctx, c_ctx, norm_gain, ada_w, ada_b, ffn_w_in, ffn_w_out, final_gain, ab_w_in, ab_q_gain, ab_k_gain, ab_sink, ab_w_out, cd_w_in, cd_mu, cd_w0, cd_w2, cd_a0, cd_a2, cd_g2, cd_k_k, cd_k_a, cd_r_k, cd_lnx_w, cd_lnx_b, cd_pool_w, cd_pool_scale, cd_w_out):
    b, n, d = x.shape
    nc = ctx.shape[1]
    pad = (-(b + 1)) % 8
    cs = jnp.concatenate([c, c_ctx[None, :], jnp.zeros((pad, d), F32)], axis=0)
    mods = _mods(cs, ada_w, ada_b)
    seg = jnp.asarray(np.kron(np.eye(2), np.ones((HEAD_DIM, HEAD_DIM))).astype(np.float32)).astype(BF16)
    fgain = final_gain.reshape(1, d)

    def layer_mods(i):
        ml = mods[i, :b].reshape(b, 6, d)
        mc = jnp.broadcast_to(mods[i, b].reshape(1, 6, d), (b, 6, d))
        return ml, mc

    ml, mc = layer_mods(0)
    w0 = ab_w_in[0][:, _kv_dup_columns()].astype(BF16)
    gain = norm_gain[0, 0].reshape(1, d)
    qg = jnp.tile(ab_q_gain[0], 2).reshape(1, LANES)
    kg = jnp.tile(ab_k_gain[0], 2).reshape(1, LANES)
    cos_l, sin_l = _rope_tables(n)
    cos_c, sin_c = jnp.ones((nc, LANES), F32), jnp.zeros((nc, LANES), F32)
    q_l, kv_l, vt_l = _proj0(x, ml, gain, w0, cos_l, sin_l, qg, kg, seg)
    q_c, kv_c, vt_c = _proj0(ctx, mc, gain, w0, cos_c, sin_c, qg, kg, seg)
    sink = ab_sink[0]
    w_out0 = ab_w_out[0].astype(BF16)
    gain2 = norm_gain[0, 1].reshape(1, d)
    w1 = ffn_w_in[0].astype(BF16)
    w2 = ffn_w_out[0].astype(BF16)
    yb_l = _attn_b_auto(q_l, kv_c, vt_c, kv_l, vt_l, ab_q_gain[0], ab_k_gain[0])
    xl = _out0_ffn(_attn_a(q_l, kv_c, sink, kv_l), yb_l, x, ml, w_out0, gain2, w1, w2, fgain)
    xc = _out0_ffn(_attn_a(q_c, kv_c, sink), _attn_b(q_c, kv_c), ctx, mc, w_out0, gain2, w1, w2, fgain)

    ml, mc = layer_mods(1)
    gain = norm_gain[1, 0].reshape(1, d)
    w_in1 = cd_w_in[0].astype(BF16)
    pc_l = _proj1(xl, ml, gain, w_in1)
    pc_c = _proj1(xc, mc, gain, w_in1[:, :C_IN])
    zeros = jnp.zeros((DECAY_LORA_PAD, C_WIDTH), F32)
    w2x = jnp.stack([jnp.concatenate([cd_w2[0, 0], zeros]), jnp.concatenate([zeros, cd_w2[0, 1]])])
    a2x = jnp.stack([jnp.concatenate([cd_a2[0, 0], zeros]), jnp.concatenate([zeros, cd_a2[0, 1]])])
    w2x_hi = w2x.astype(BF16)
    w2x_hl = jnp.stack([w2x_hi, (w2x - w2x_hi.astype(F32)).astype(BF16)])
    wts = [jnp.stack([1.0 - cd_mu[0], 0.5 * cd_mu[0]]), cd_w0[0].reshape(2, 1, C_WIDTH), w2x_hl,
           cd_a0[0].reshape(2, 1, C_WIDTH), a2x.astype(BF16), cd_g2[0].astype(BF16),
           cd_k_k[0].reshape(1, C_WIDTH), cd_k_a[0].reshape(1, C_WIDTH),
           cd_r_k[0].reshape(1, C_WIDTH), seg]
    pool_w = jax.scipy.linalg.block_diag(*[cd_pool_w[0, g] for g in range(4)]).astype(BF16)
    pool_wts = [pool_w, cd_pool_scale[0].reshape(1, D_WIDTH)]
    prep_c = _prep(pc_c, wts, False)
    *prep_l, g_l, bonus_l, dp_l = _prep(pc_l, wts + pool_wts, True)
    h_zero = jnp.zeros((b, C_WIDTH // QUAD, CHUNK, QUAD), F32)
    _, h_f = _scanq(prep_c, 0, h_zero, False, False)
    _, h_b = _scanq(prep_c, 1, h_zero, True, False)
    y_f, _ = _scanq(prep_l, 0, h_f, False, True)
    y_b, _ = _scanq(prep_l, 1, h_b, True, True)
    seg4 = jnp.asarray(np.kron(np.eye(4), np.ones((HEAD_DIM, HEAD_DIM))).astype(np.float32)).astype(BF16)
    return _out1_ffn(y_f, y_b, bonus_l, g_l, dp_l, xl, ml, cd_lnx_w[0].reshape(1, C_WIDTH),
                     cd_lnx_b[0].reshape(1, C_WIDTH), seg4, cd_w_out[0].astype(BF16),
                     norm_gain[1, 1].reshape(1, d), ffn_w_in[1].astype(BF16),
                     ffn_w_out[1].astype(BF16), fgain)
```
